```python
import jax, jax.numpy as jnp
from jax import lax
import numpy as np

D_MODEL = 1024
BATCH = 16
SEQ = 4096
DEPTH = 1
DEC_BATCH = 8
DEC_SEQ = 64
PAST_LEN = 2048

CHUNK = 64
QBLOCK = 128
EPS = 1e-6
MLA_HEADS = 8
Q_LORA = 256
KV_LORA = 128
NOPE_DIM = 64
ROPE_DIM = 32
V_DIM = 64
QK_DIM = NOPE_DIM + ROPE_DIM
ATT_WIDTH = MLA_HEADS * V_DIM
ROPE_BASE = 10000.0
GM_HEADS = 8
GM_HEAD_DIM = 64
GM_WIDTH = GM_HEADS * GM_HEAD_DIM
GM_CHUNK = 128
MIX_WIDTH = ATT_WIDTH + GM_WIDTH
IN_COLS = Q_LORA + KV_LORA + ROPE_DIM + 2 * GM_WIDTH
N_GROUPS = 4
EXPERTS_PER_GROUP = 8
N_EXPERTS = N_GROUPS * EXPERTS_PER_GROUP
TOP_K = 2
D_EXPERT = 256
MOE_BLOCK = 128

kernel_name = "hymba_mla_gmlp_hmoe_stream_step"


def rms_norm(x, g):
    xf = x.astype(jnp.float32)
    y = xf * lax.rsqrt(jnp.mean(xf * xf, axis=-1, keepdims=True) + EPS)
    return (y * g.astype(jnp.float32)).astype(x.dtype)


def layer_norm(x, g, b):
    xf = x.astype(jnp.float32)
    mu = jnp.mean(xf, axis=-1, keepdims=True)
    xc = xf - mu
    var = jnp.mean(xc * xc, axis=-1, keepdims=True)
    return (xc * lax.rsqrt(var + EPS) * g.astype(jnp.float32) + b.astype(jnp.float32)).astype(x.dtype)


def rope(x, pos):
    half = ROPE_DIM // 2
    inv = 1.0 / (ROPE_BASE ** (jnp.arange(half, dtype=jnp.float32) / half))
    ang = pos.astype(jnp.float32)[:, None] * inv[None, :]
    cos = jnp.cos(ang)[None, :, None, :]
    sin = jnp.sin(ang)[None, :, None, :]
    xf = x.astype(jnp.float32)
    x1, x2 = xf[..., :half], xf[..., half:]
    return jnp.concatenate([x1 * cos - x2 * sin, x2 * cos + x1 * sin], axis=-1).astype(x.dtype)


def mla_queries(q_lat, pos, lw):
    B, S, _ = q_lat.shape
    q = (rms_norm(q_lat, lw["g_q_lat"]) @ lw["w_uq"]).reshape(B, S, MLA_HEADS, QK_DIM)
    q = rms_norm(q, lw["g_qnorm"])
    return jnp.concatenate([q[..., :NOPE_DIM], rope(q[..., NOPE_DIM:], pos)], axis=-1)


def mla_keys_values(ckv, krope, pos, lw):
    B, K, _ = ckv.shape
    kv = (rms_norm(ckv, lw["g_kv_lat"]) @ lw["w_ukv"]).reshape(B, K, MLA_HEADS, NOPE_DIM + V_DIM)
    k_nope, v = kv[..., :NOPE_DIM], kv[..., NOPE_DIM:]
    k = jnp.concatenate([k_nope, jnp.broadcast_to(krope[:, :, None, :], (B, K, MLA_HEADS, ROPE_DIM))], axis=-1)
    k = rms_norm(k, lw["g_knorm"])
    k = jnp.concatenate([k[..., :NOPE_DIM], rope(k[..., NOPE_DIM:], pos)], axis=-1)
    return k, v


def chunk_causal_attend(q, q_pos, k, v, k_pos):
    s = jnp.einsum("bqhd,bkhd->bhqk", q.astype(jnp.float32), k.astype(jnp.float32)) * (QK_DIM ** -0.5)
    mask = (k_pos[None, :] // CHUNK) <= (q_pos[:, None] // CHUNK)
    s = jnp.where(mask[None, None], s, -jnp.inf)
    p = jax.nn.softmax(s, axis=-1)
    return jnp.einsum("bhqk,bkhd->bqhd", p, v.astype(jnp.float32)).astype(v.dtype)


def mla_attend(q, q_pos, k, v, k_pos):
    B, Q = q.shape[:2]
    if Q <= QBLOCK:
        return chunk_causal_attend(q, q_pos, k, v, k_pos)
    nb = Q // QBLOCK
    qb = jnp.moveaxis(q.reshape(B, nb, QBLOCK, MLA_HEADS, QK_DIM), 1, 0)
    ob = lax.map(lambda a: chunk_causal_attend(a[0], a[1], k, v, k_pos), (qb, q_pos.reshape(nb, QBLOCK)))
    return jnp.moveaxis(ob, 0, 1).reshape(B, Q, MLA_HEADS, V_DIM)


def spatial_gating(u, v, lw):
    B, S, _ = v.shape
    L = min(S, GM_CHUNK)
    n = S // L
    t = jnp.arange(L)
    mask = (t[None, :] // CHUNK) <= (t[:, None] // CHUNK)
    w = jnp.where(mask[None], lw["w_spatial"][:, :L, :L], 0)
    vh = v.reshape(B, n, L, GM_HEADS, GM_HEAD_DIM)
    bias = jnp.transpose(lw["b_spatial"][:, :L])[None, None, :, :, None]
    mixed = jnp.einsum("hts,bnshd->bnthd", w, vh) + bias
    return u * mixed.reshape(B, S, GM_WIDTH)


def token_mixers(h, past_ckv, past_krope, q_pos, k_pos, lw):
    B, S, _ = h.shape
    proj = h @ lw["w_in"]
    o1 = Q_LORA
    o2 = o1 + KV_LORA
    o3 = o2 + ROPE_DIM
    o4 = o3 + GM_WIDTH
    q_lat, ckv_new, krope_new = proj[..., :o1], proj[..., o1:o2], proj[..., o2:o3]
    g_u, g_v = proj[..., o3:o4], proj[..., o4:]
    if past_ckv is None:
        ckv_all, krope_all = ckv_new, krope_new
    else:
        ckv_all = jnp.concatenate([past_ckv.astype(ckv_new.dtype), ckv_new], axis=1)
        krope_all = jnp.concatenate([past_krope.astype(krope_new.dtype), krope_new], axis=1)
    q = mla_queries(q_lat, q_pos, lw)
    k, v = mla_keys_values(ckv_all, krope_all, k_pos, lw)
    att = mla_attend(q, q_pos, k, v, k_pos).reshape(B, S, ATT_WIDTH)
    u = jax.nn.gelu(g_u)
    v_rows = layer_norm(jax.nn.gelu(g_v), lw["g_ln_v"], lw["b_ln_v"])
    gm = spatial_gating(u, v_rows, lw)
    merged = jnp.concatenate([rms_norm(att, lw["g_out_attn"]), rms_norm(gm, lw["g_out_gmlp"])], axis=-1)
    return merged @ lw["w_out"], ckv_new, krope_new, v_rows


def grouped_experts(t, slot_expert, slot_weight, w_gate, w_up, w_down):
    T, D = t.shape
    A = slot_expert.shape[0]
    order = jnp.argsort(slot_expert)
    sorted_e = slot_expert[order]
    counts = jnp.bincount(slot_expert, length=N_EXPERTS)
    padded = (counts + MOE_BLOCK - 1) // MOE_BLOCK * MOE_BLOCK
    pad_end = jnp.cumsum(padded)
    dest = (pad_end - padded)[sorted_e] + jnp.arange(A) - (jnp.cumsum(counts) - counts)[sorted_e]
    n_blocks = -(-A // MOE_BLOCK) + N_EXPERTS
    P = n_blocks * MOE_BLOCK
    row_token = jnp.zeros((P,), jnp.int32).at[dest].set((order // TOP_K).astype(jnp.int32))
    row_weight = jnp.zeros((P,), jnp.float32).at[dest].set(slot_weight[order])
    block_expert = jnp.minimum(jnp.searchsorted(pad_end, jnp.arange(n_blocks) * MOE_BLOCK, side="right"), N_EXPERTS - 1)
    xb = t[row_token].reshape(n_blocks, MOE_BLOCK, D)

    def expert_block(args):
        xblk, e = args
        hid = jax.nn.silu(xblk @ w_gate[e]) * (xblk @ w_up[e])
        return hid @ w_down[e]

    yb = lax.map(expert_block, (xb, block_expert)).reshape(P, D)
    y = jnp.zeros((T, D), jnp.float32).at[row_token].add(yb.astype(jnp.float32) * row_weight[:, None])
    return y.astype(t.dtype)


def hier_moe(h, lw):
    B, S, D = h.shape
    t = h.reshape(B * S, D)
    T = t.shape[0]
    p_group = jax.nn.softmax((t @ lw["w_router_group"] + lw["b_router_group"]).astype(jnp.float32), axis=-1)
    g_prob, g_idx = lax.top_k(p_group, 1)
    logit_e = (t @ lw["w_router_expert"] + lw["b_router_expert"]).astype(jnp.float32)
    logit_e = logit_e.reshape(T, N_GROUPS, EXPERTS_PER_GROUP)
    logit_e = jnp.take_along_axis(logit_e, g_idx[:, :, None], axis=1)[:, 0, :]
    e_prob, e_idx = lax.top_k(jax.nn.softmax(logit_e, axis=-1), TOP_K)
    weight = g_prob * e_prob / jnp.sum(e_prob, axis=-1, keepdims=True)
    expert_id = (g_idx * EXPERTS_PER_GROUP + e_idx).astype(jnp.int32)
    y = grouped_experts(t, expert_id.reshape(-1), weight.reshape(-1), lw["w_gate_e"], lw["w_up_e"], lw["w_down_e"])
    return y.reshape(B, S, D)


def layer(x, c, past_ckv, past_krope, q_pos, k_pos, lw):
    mod = jax.nn.silu(c) @ lw["w_ada"] + lw["b_ada"]
    sh_a, sc_a, gt_a, sh_m, sc_m, gt_m = jnp.split(mod[:, None, :], 6, axis=-1)
    h = rms_norm(x, lw["g_norm_mix"]) * (1 + sc_a) + sh_a
    mix, ckv_new, krope_new, v_rows = token_mixers(h, past_ckv, past_krope, q_pos, k_pos, lw)
    x = x + gt_a * mix
    h = rms_norm(x, lw["g_norm_ffn"]) * (1 + sc_m) + sh_m
    x = x + gt_m * hier_moe(h, lw)
    return x, ckv_new, krope_new, v_rows


def setup_inputs(seed: int = 0) -> dict:
    key = jax.random.key(seed)
    ks = iter(jax.random.split(key, 40))

    def nrm(shape, scale):
        return jax.random.normal(next(ks), shape, jnp.float32) * scale

    def gain(shape):
        return 1.0 + nrm(shape, 0.02)

    L, D = DEPTH, D_MODEL
    return {
        "x_prompt": nrm((BATCH, SEQ, D), 1.0),
        "x_sample": nrm((DEC_BATCH, DEC_SEQ, D), 1.0),
        "cache_ckv": nrm((L, DEC_BATCH, PAST_LEN, KV_LORA), 1.0),
        "cache_krope": nrm((L, DEC_BATCH, PAST_LEN, ROPE_DIM), 1.0),
        "c_prompt": nrm((BATCH, D), 1.0),
        "c_sample": nrm((DEC_BATCH, D), 1.0),
        "w_ada": nrm((L, D, 6 * D), 0.3 * D ** -0.5),
        "b_ada": nrm((L, 6 * D), 0.01),
        "g_norm_mix": gain((L, D)),
        "w_in": nrm((L, D, IN_COLS), D ** -0.5),
        "g_q_lat": gain((L, Q_LORA)),
        "w_uq": nrm((L, Q_LORA, MLA_HEADS * QK_DIM), Q_LORA ** -0.5),
        "g_kv_lat": gain((L, KV_LORA)),
        "w_ukv": nrm((L, KV_LORA, MLA_HEADS * (NOPE_DIM + V_DIM)), KV_LORA ** -0.5),
        "g_qnorm": gain((L, QK_DIM)),
        "g_knorm": gain((L, QK_DIM)),
        "g_ln_v": gain((L, GM_WIDTH)),
        "b_ln_v": nrm((L, GM_WIDTH), 0.02),
        "w_spatial": nrm((L, GM_HEADS, GM_CHUNK, GM_CHUNK), GM_CHUNK ** -0.5),
        "b_spatial": gain((L, GM_HEADS, GM_CHUNK)),
        "g_out_attn": gain((L, ATT_WIDTH)),
        "g_out_gmlp": gain((L, GM_WIDTH)),
        "w_out": nrm((L, MIX_WIDTH, D), MIX_WIDTH ** -0.5),
        "g_norm_ffn": gain((L, D)),
        "w_router_group": nrm((L, D, N_GROUPS), D ** -0.5),
        "b_router_group": nrm((L, N_GROUPS), 0.01),
        "w_router_expert": nrm((L, D, N_EXPERTS), D ** -0.5),
        "b_router_expert": nrm((L, N_EXPERTS), 0.01),
        "w_gate_e": nrm((L, N_EXPERTS, D, D_EXPERT), D ** -0.5),
        "w_up_e": nrm((L, N_EXPERTS, D, D_EXPERT), D ** -0.5),
        "w_down_e": nrm((L, N_EXPERTS, D_EXPERT, D), D_EXPERT ** -0.5),
    }


def reference(x_prompt, x_sample, cache_ckv, cache_krope, c_prompt, c_sample,
              w_ada, b_ada, g_norm_mix, w_in, g_q_lat, w_uq, g_kv_lat, w_ukv, g_qnorm, g_knorm,
              g_ln_v, b_ln_v, w_spatial, b_spatial, g_out_attn, g_out_gmlp, w_out, g_norm_ffn,
              w_router_group, b_router_group, w_router_expert, b_router_expert, w_gate_e, w_up_e, w_down_e):
    seq = x_prompt.shape[1]
    dec_seq = x_sample.shape[1]
    past = cache_ckv.shape[2]
    pos_prompt = jnp.arange(seq)
    pos_sample_q = past + jnp.arange(dec_seq)
    pos_sample_k = jnp.arange(past + dec_seq)

    y_p, y_s = x_prompt, x_sample
    ckv_p_list, krope_p_list, ckv_s_list, krope_s_list, v_s_list = [], [], [], [], []
    for l in range(DEPTH):
        lw = {
            "w_ada": w_ada[l], "b_ada": b_ada[l], "g_norm_mix": g_norm_mix[l], "w_in": w_in[l],
            "g_q_lat": g_q_lat[l], "w_uq": w_uq[l], "g_kv_lat": g_kv_lat[l], "w_ukv": w_ukv[l],
            "g_qnorm": g_qnorm[l], "g_knorm": g_knorm[l], "g_ln_v": g_ln_v[l], "b_ln_v": b_ln_v[l],
            "w_spatial": w_spatial[l], "b_spatial": b_spatial[l], "g_out_attn": g_out_attn[l],
            "g_out_gmlp": g_out_gmlp[l], "w_out": w_out[l], "g_norm_ffn": g_norm_ffn[l],
            "w_router_group": w_router_group[l], "b_router_group": b_router_group[l],
            "w_router_expert": w_router_expert[l], "b_router_expert": b_router_expert[l],
            "w_gate_e": w_gate_e[l], "w_up_e": w_up_e[l], "w_down_e": w_down_e[l],
        }
        y_p, ckv_p, krope_p, _ = layer(y_p, c_prompt, None, None, pos_prompt, pos_prompt, lw)
        y_s, ckv_s, krope_s, v_s = layer(y_s, c_sample, cache_ckv[l], cache_krope[l], pos_sample_q, pos_sample_k, lw)
        ckv_p_list.append(ckv_p)
        krope_p_list.append(krope_p)
        ckv_s_list.append(ckv_s)
        krope_s_list.append(krope_s)
        v_s_list.append(v_s)

    new_ckv_prompt = jnp.stack(ckv_p_list)
    new_krope_prompt = jnp.stack(krope_p_list)
    new_ckv_sample = jnp.stack(ckv_s_list)
    new_krope_sample = jnp.stack(krope_s_list)
    new_gmlp_v_sample = jnp.stack(v_s_list)
    return (y_p, y_s, new_ckv_prompt, new_krope_prompt, new_ckv_sample, new_krope_sample, new_gmlp_v_sample)
```

```python
import functools

import numpy as np
import jax
import jax.numpy as jnp
from jax import lax
from jax.experimental import pallas as pl
from jax.experimental.pallas import tpu as pltpu

F32 = jnp.float32
BF16 = jnp.bfloat16
I32 = jnp.int32

CHUNK = 64
CHUNK_SHIFT = 6
EPS = 1e-6
MLA_HEADS = 8
Q_LORA = 256
KV_LORA = 128
NOPE_DIM = 64
ROPE_DIM = 32
ROPE_HALF = ROPE_DIM // 2
V_DIM = 64
QK_DIM = NOPE_DIM + ROPE_DIM
ATT_WIDTH = MLA_HEADS * V_DIM
ROPE_BASE = 10000.0
GM_HEADS = 8
GM_HEAD_DIM = 64
GM_WIDTH = GM_HEADS * GM_HEAD_DIM
GM_CHUNK = 128
N_GROUPS = 4
EXPERTS_PER_GROUP = 8
N_EXPERTS = N_GROUPS * EXPERTS_PER_GROUP
D_EXPERT = 256
PAIRS_PER_GROUP = EXPERTS_PER_GROUP * (EXPERTS_PER_GROUP - 1) // 2
N_CLASSES = N_GROUPS * PAIRS_PER_GROUP

LANES = 128
SUBLANES = 8
HEAD_PAD = LANES
PROJ_COLS = 1536
ROUTER_COLS = LANES
ROUTER_EXPERT_LANE0 = SUBLANES
CLASS_ROWS = LANES
MOE_ROWS = 128
MOD_BATCH_PAD = 16
VMEM_LIMIT = 48 * 1024 * 1024
NEG_BIG = -1e30

assert CHUNK == 1 << CHUNK_SHIFT


def _params(n_axes, vmem=VMEM_LIMIT):
    return pltpu.CompilerParams(dimension_semantics=("arbitrary",) * n_axes, vmem_limit_bytes=vmem)


def _full(shape):
    nd = len(shape)
    return pl.BlockSpec(shape, lambda *_: (0,) * nd)


def _split_bf16(x):
    hi = x.astype(BF16)
    lo = (x - hi.astype(F32)).astype(BF16)
    return hi, lo


def _dot(a, b):
    return jnp.dot(a, b, preferred_element_type=F32)


def _ada_kernel(c_ref, w_ref, b_ref, o_ref):
    a_hi, a_lo = _split_bf16(jax.nn.silu(c_ref[...]))
    w_hi, w_lo = _split_bf16(w_ref[...])
    o_ref[...] = _dot(a_hi, w_hi) + _dot(a_lo, w_hi) + _dot(a_hi, w_lo) + b_ref[...]


def _ada_mod(c, w_ada, b_ada):
    n, d = c.shape
    cols = w_ada.shape[1]
    tn = 1536
    return pl.pallas_call(
        _ada_kernel,
        grid=(cols // tn,),
        in_specs=[_full((n, d)), pl.BlockSpec((d, tn), lambda j: (0, j)), pl.BlockSpec((1, tn), lambda j: (0, j))],
        out_specs=pl.BlockSpec((n, tn), lambda j: (0, j)),
        out_shape=jax.ShapeDtypeStruct((n, cols), F32),
        compiler_params=_params(1),
    )(c, w_ada, b_ada.reshape(1, cols))


def _rms(x, g):
    return x * lax.rsqrt(jnp.mean(x * x, axis=-1, keepdims=True) + EPS) * g


def _head_norm_rope(xh, g, c, s_up, s_dn):
    ms = jnp.sum(xh * xh, axis=-1, keepdims=True) * (1.0 / QK_DIM)
    xh = xh * lax.rsqrt(ms + EPS) * g
    return xh * c + pltpu.roll(xh, ROPE_HALF, 1) * s_up + pltpu.roll(xh, HEAD_PAD - ROPE_HALF, 1) * s_dn


def _keys_values(ckv, kr_slot, gkv, wk, wv, gkn, c, s_up, s_dn, k_ref, v_ref):
    cb = _rms(ckv, gkv).astype(BF16)
    kall = _dot(cb, wk)
    for h in range(MLA_HEADS):
        kh = kall[:, h * HEAD_PAD:(h + 1) * HEAD_PAD] + kr_slot
        k_ref[0, h] = _head_norm_rope(kh, gkn, c, s_up, s_dn).astype(BF16)
    v_ref[0] = _dot(cb, wv).astype(BF16)


def _mix_in_kernel(x_ref, mod_ref, gmix_ref, win_ref, gql_ref, wuq_ref, gkv_ref, wk_ref, wv_ref, gqn_ref, gkn_ref,
                   cq_ref, squ_ref, sqd_ref, ck_ref, sku_ref, skd_ref, glnv_ref, blnv_ref, wsp_ref, bsp_ref,
                   ggm_ref, *rest, chunk_len, emit_kv, emit_vrows):
    outs = list(rest[:-1])
    mixed_scr = rest[-1]
    ckv_ref, kr_ref, q_ref, gm_ref = outs[:4]
    outs = outs[4:]
    if emit_kv:
        k_ref, v_ref = outs[:2]
        outs = outs[2:]
    if emit_vrows:
        vrows_ref = outs[0]

    x = x_ref[0]
    tm = x.shape[0]
    shift, scale = mod_ref[0, 0:1, :], mod_ref[0, 1:2, :]
    h = _rms(x, gmix_ref[...]) * (1.0 + scale) + shift
    proj = _dot(h.astype(BF16), win_ref[...])

    q_lat = proj[:, 0:Q_LORA]
    ckv = proj[:, Q_LORA:Q_LORA + KV_LORA]
    kr_blk = proj[:, Q_LORA + KV_LORA:Q_LORA + KV_LORA + LANES]
    ckv_ref[0] = ckv
    kr_ref[0] = kr_blk[:, 0:ROPE_DIM]

    q = _dot(_rms(q_lat, gql_ref[...]).astype(BF16), wuq_ref[...])
    cq, squ, sqd = cq_ref[...], squ_ref[...], sqd_ref[...]
    for hd in range(MLA_HEADS):
        qh = q[:, hd * HEAD_PAD:(hd + 1) * HEAD_PAD]
        q_ref[0, hd] = _head_norm_rope(qh, gqn_ref[...], cq, squ, sqd).astype(BF16)

    if emit_kv:
        lane = lax.broadcasted_iota(I32, (1, LANES), 1)
        kr_slot = jnp.where(lane >= NOPE_DIM, kr_blk, 0.0)
        _keys_values(ckv, kr_slot, gkv_ref[...], wk_ref[...], wv_ref[...], gkn_ref[...],
                     ck_ref[...], sku_ref[...], skd_ref[...], k_ref, v_ref)

    g_u = proj[:, 512:512 + GM_WIDTH]
    g_v = proj[:, 1024:1024 + GM_WIDTH]
    u = jax.nn.gelu(g_u)
    gv = jax.nn.gelu(g_v)
    mu = jnp.mean(gv, axis=-1, keepdims=True)
    xc = gv - mu
    var = jnp.mean(xc * xc, axis=-1, keepdims=True)
    v_rows = xc * lax.rsqrt(var + EPS) * glnv_ref[...] + blnv_ref[...]
    if emit_vrows:
        vrows_ref[0] = v_rows
    vb = v_rows.astype(BF16)

    L = chunk_len
    t = lax.broadcasted_iota(I32, (2 * L, L), 0)
    s = lax.broadcasted_iota(I32, (2 * L, L), 1)
    t = jnp.where(t >= L, t - L, t)
    allowed = (s >> CHUNK_SHIFT) <= (t >> CHUNK_SHIFT)
    lane = lax.broadcasted_iota(I32, (1, LANES), 1)
    first_head = lane < GM_HEAD_DIM
    for p in range(GM_HEADS // 2):
        w_pair = jnp.where(allowed, wsp_ref[p], jnp.zeros((), BF16))
        for c in range(tm // L):
            vp = vb[c * L:(c + 1) * L, p * LANES:(p + 1) * LANES]
            r = _dot(w_pair, vp)
            mixed = jnp.where(first_head, r[:L], r[L:])
            mixed_scr[c * L:(c + 1) * L, p * LANES:(p + 1) * LANES] = mixed + bsp_ref[:, p * LANES:(p + 1) * LANES]
    gm = u * mixed_scr[...]
    gm_ref[0] = _rms(gm, ggm_ref[...]).astype(BF16)


def _mix_in(x, mod, lw, q_tabs, k_tabs, *, tm, chunk_len, emit_kv, emit_vrows):
    B, S, D = x.shape
    nt = S // tm
    H = MLA_HEADS
    row = lambda b, i: (b, i, 0)
    tab = pl.BlockSpec((tm, LANES), lambda b, i: (i, 0))
    in_specs = [
        pl.BlockSpec((1, tm, D), row),
        pl.BlockSpec((1, 6, D), lambda b, i: (b, 0, 0)),
        _full((1, D)), _full((D, PROJ_COLS)), _full((1, Q_LORA)), _full((Q_LORA, H * HEAD_PAD)),
        _full((1, KV_LORA)), _full((KV_LORA, H * HEAD_PAD)), _full((KV_LORA, ATT_WIDTH)),
        _full((1, LANES)), _full((1, LANES)),
        tab, tab, tab, tab, tab, tab,
        _full((1, GM_WIDTH)), _full((1, GM_WIDTH)),
        _full((GM_HEADS // 2, 2 * chunk_len, chunk_len)), _full((chunk_len, GM_WIDTH)), _full((1, GM_WIDTH)),
    ]
    out_shape = [
        jax.ShapeDtypeStruct((B, S, KV_LORA), F32),
        jax.ShapeDtypeStruct((B, S, ROPE_DIM), F32),
        jax.ShapeDtypeStruct((B, H, S, HEAD_PAD), BF16),
        jax.ShapeDtypeStruct((B, S, GM_WIDTH), BF16),
    ]
    head_blk = pl.BlockSpec((1, H, tm, HEAD_PAD), lambda b, i: (b, 0, i, 0))
    out_specs = [
        pl.BlockSpec((1, tm, KV_LORA), row),
        pl.BlockSpec((1, tm, ROPE_DIM), row),
        head_blk,
        pl.BlockSpec((1, tm, GM_WIDTH), row),
    ]
    if emit_kv:
        out_shape += [jax.ShapeDtypeStruct((B, H, S, HEAD_PAD), BF16), jax.ShapeDtypeStruct((B, S, ATT_WIDTH), BF16)]
        out_specs += [head_blk, pl.BlockSpec((1, tm, ATT_WIDTH), row)]
    if emit_vrows:
        out_shape += [jax.ShapeDtypeStruct((B, S, GM_WIDTH), F32)]
        out_specs += [pl.BlockSpec((1, tm, GM_WIDTH), row)]
    kern = functools.partial(_mix_in_kernel, chunk_len=chunk_len, emit_kv=emit_kv, emit_vrows=emit_vrows)
    return pl.pallas_call(
        kern,
        grid=(B, nt),
        in_specs=in_specs,
        out_specs=out_specs,
        out_shape=out_shape,
        scratch_shapes=[pltpu.VMEM((tm, GM_WIDTH), F32)],
        compiler_params=_params(2),
    )(x, mod, lw["g_norm_mix"], lw["w_in_p"], lw["g_q_lat"], lw["w_uq_p"], lw["g_kv_lat"], lw["w_k_p"], lw["w_v_p"],
      lw["g_qnorm_p"], lw["g_knorm_p"], *q_tabs, *k_tabs, lw["g_ln_v"], lw["b_ln_v"],
      lw["w_sp_pairs_%d" % chunk_len], lw["b_sp_rows_%d" % chunk_len], lw["g_out_gmlp"])


def _kv_latent_kernel(ckv_ref, kr_ref, gkv_ref, wk_ref, wv_ref, gkn_ref, c_ref, su_ref, sd_ref, k_ref, v_ref):
    _keys_values(ckv_ref[0], kr_ref[0], gkv_ref[...], wk_ref[...], wv_ref[...], gkn_ref[...],
                 c_ref[...], su_ref[...], sd_ref[...], k_ref, v_ref)


def _kv_latent(ckv_all, kr_slot_all, lw, k_tabs, *, tr):
    B, K, _ = ckv_all.shape
    H = MLA_HEADS
    row = lambda b, i: (b, i, 0)
    tab = pl.BlockSpec((tr, LANES), lambda b, i: (i, 0))
    return pl.pallas_call(
        _kv_latent_kernel,
        grid=(B, K // tr),
        in_specs=[pl.BlockSpec((1, tr, KV_LORA), row), pl.BlockSpec((1, tr, LANES), row),
                  _full((1, KV_LORA)), _full((KV_LORA, H * HEAD_PAD)), _full((KV_LORA, ATT_WIDTH)), _full((1, LANES)),
                  tab, tab, tab],
        out_specs=[pl.BlockSpec((1, H, tr, HEAD_PAD), lambda b, i: (b, 0, i, 0)), pl.BlockSpec((1, tr, ATT_WIDTH), row)],
        out_shape=[jax.ShapeDtypeStruct((B, H, K, HEAD_PAD), BF16), jax.ShapeDtypeStruct((B, K, ATT_WIDTH), BF16)],
        compiler_params=_params(2),
    )(ckv_all, kr_slot_all, lw["g_kv_lat"], lw["w_k_p"], lw["w_v_p"], lw["g_knorm_p"], *k_tabs)


def _attn_kernel(q_ref, k_ref, v_ref, o_ref, *, tq, tk, n_q, q_off, kv_valid):
    i = pl.program_id(2) if n_q > 1 else 0
    q_first = q_off + i * tq
    vis_first = jnp.minimum(((q_first >> CHUNK_SHIFT) + 1) << CHUNK_SHIFT, kv_valid)
    vis_last = jnp.minimum((((q_first + tq - 1) >> CHUNK_SHIFT) + 1) << CHUNK_SHIFT, kv_valid)
    n_unmasked = vis_first // tk
    n_total = (vis_last + tk - 1) // tk

    q_pos = q_first + lax.broadcasted_iota(I32, (tq, 1), 0)
    limit = jnp.minimum(((q_pos >> CHUNK_SHIFT) + 1) << CHUNK_SHIFT, kv_valid)
    lane = lax.broadcasted_iota(I32, (1, LANES), 1)
    first_head = lane < V_DIM

    def block(kb, carry, masked):
        start = pl.multiple_of(kb * tk, tk)
        v_blk = v_ref[0, pl.ds(start, tk), :]
        if masked:
            k_pos = start + lax.broadcasted_iota(I32, (1, tk), 1)
            allowed = k_pos < limit
        new = []
        for j in range(2):
            m, l, acc = carry[3 * j:3 * j + 3]
            s = lax.dot_general(q_ref[0, j], k_ref[0, j, pl.ds(start, tk), :], (((1,), (1,)), ((), ())),
                                preferred_element_type=F32)
            if masked:
                s = jnp.where(allowed, s, NEG_BIG)
            m_new = jnp.maximum(m, jnp.max(s, axis=-1, keepdims=True))
            alpha = jnp.exp(m - m_new)
            p = jnp.exp(s - m_new)
            l = alpha * l + jnp.sum(p, axis=-1, keepdims=True)
            acc = alpha * acc + _dot(p.astype(BF16), v_blk)
            new += [m_new, l, acc]
        return tuple(new)

    init = []
    for _ in range(2):
        init += [jnp.full((tq, 1), NEG_BIG, F32), jnp.zeros((tq, 1), F32), jnp.zeros((tq, LANES), F32)]
    carry = lax.fori_loop(0, n_unmasked, lambda kb, c: block(kb, c, False), tuple(init))
    carry = lax.fori_loop(n_unmasked, n_total, lambda kb, c: block(kb, c, True), carry)
    o0 = carry[2] / carry[1]
    o1 = carry[5] / carry[4]
    o_ref[0] = jnp.where(first_head, o0, o1).astype(BF16)


def _attention(q, k, v, *, tq, tk, q_off, kv_valid):
    B, H, Sq, _ = q.shape
    Sk = k.shape[2]
    kern = functools.partial(_attn_kernel, tq=tq, tk=tk, n_q=Sq // tq, q_off=q_off, kv_valid=kv_valid)
    return pl.pallas_call(
        kern,
        grid=(B, H // 2, Sq // tq),
        in_specs=[pl.BlockSpec((1, 2, tq, HEAD_PAD), lambda b, hp, i: (b, hp, i, 0)),
                  pl.BlockSpec((1, 2, Sk, HEAD_PAD), lambda b, hp, i: (b, hp, 0, 0)),
                  pl.BlockSpec((1, Sk, LANES), lambda b, hp, i: (b, 0, hp))],
        out_specs=pl.BlockSpec((1, tq, LANES), lambda b, hp, i: (b, i, hp)),
        out_shape=jax.ShapeDtypeStruct((B, Sq, ATT_WIDTH), BF16),
        compiler_params=_params(3),
    )(q, k, v)


def _out_route_kernel(x_ref, att_ref, gm_ref, mod_ref, goa_ref, woa_ref, wog_ref, gffn_ref, wra_ref, wrb_ref, br_ref,
                      x1_ref, ri_ref, rf_ref, cnt_ref, *rest, emit_h2):
    carry_scr = rest[-1]
    first_step = (pl.program_id(0) == 0) & (pl.program_id(1) == 0)

    @pl.when(first_step)
    def _():
        carry_scr[...] = jnp.zeros_like(carry_scr)

    x = x_ref[0]
    tm = x.shape[0]
    gate_a = mod_ref[0, 2:3, :]
    shift_m, scale_m = mod_ref[0, 3:4, :], mod_ref[0, 4:5, :]
    att_n = _rms(att_ref[0].astype(F32), goa_ref[...]).astype(BF16)
    mix = _dot(att_n, woa_ref[...]) + _dot(gm_ref[0], wog_ref[...])
    x1 = x + gate_a * mix
    x1_ref[0] = x1
    h2 = _rms(x1, gffn_ref[...]) * (1.0 + scale_m) + shift_m
    if emit_h2:
        rest[0][0] = h2.astype(BF16)

    h_hi, h_lo = _split_bf16(h2)
    la = _dot(h_hi, wra_ref[...])
    logits = la[:, :ROUTER_COLS] + la[:, ROUTER_COLS:] + _dot(h_lo, wrb_ref[...]) + br_ref[...]
    lt = logits.T

    g = [lt[r:r + 1] for r in range(N_GROUPS)]
    gmax = jnp.maximum(jnp.maximum(g[0], g[1]), jnp.maximum(g[2], g[3]))
    gsum = sum(jnp.exp(gr - gmax) for gr in g)
    g_prob = 1.0 / gsum
    g_idx = jnp.where(g[0] == gmax, 0.0, jnp.where(g[1] == gmax, 1.0, jnp.where(g[2] == gmax, 2.0, 3.0)))

    e0 = ROUTER_EXPERT_LANE0
    grp = [lt[e0 + EXPERTS_PER_GROUP * r:e0 + EXPERTS_PER_GROUP * (r + 1)] for r in range(N_GROUPS)]
    sel = jnp.where(g_idx == 0.0, grp[0], jnp.where(g_idx == 1.0, grp[1], jnp.where(g_idx == 2.0, grp[2], grp[3])))
    sub = lax.broadcasted_iota(I32, (EXPERTS_PER_GROUP, tm), 0).astype(F32)
    m1 = jnp.max(sel, axis=0, keepdims=True)
    i1 = jnp.min(jnp.where(sel == m1, sub, float(EXPERTS_PER_GROUP)), axis=0, keepdims=True)
    sel2 = jnp.where(sub == i1, -jnp.inf, sel)
    m2 = jnp.max(sel2, axis=0, keepdims=True)
    i2 = jnp.min(jnp.where(sel2 == m2, sub, float(EXPERTS_PER_GROUP)), axis=0, keepdims=True)
    d = jnp.exp(m2 - m1)
    w1 = g_prob / (1.0 + d)
    w2 = g_prob * d / (1.0 + d)
    first_lower = i1 < i2
    lo = jnp.minimum(i1, i2)
    hi = jnp.maximum(i1, i2)
    w_lo = jnp.where(first_lower, w1, w2)
    w_hi = jnp.where(first_lower, w2, w1)
    pair = lo * EXPERTS_PER_GROUP - lo * (lo + 1.0) * 0.5 + hi - lo - 1.0
    cls = g_idx * PAIRS_PER_GROUP + pair

    crow = lax.broadcasted_iota(I32, (CLASS_ROWS, tm), 0).astype(F32)
    onehot = jnp.where(crow == cls, 1.0, 0.0)
    ta = lax.broadcasted_iota(I32, (tm, tm), 0)
    tb = lax.broadcasted_iota(I32, (tm, tm), 1)
    earlier = jnp.where(ta < tb, 1.0, 0.0).astype(BF16)
    before = _dot(onehot.astype(BF16), earlier)
    carry = carry_scr[...]
    rank = jnp.sum(onehot * (before + carry[:, 0:1]), axis=0, keepdims=True)
    carry = carry + jnp.sum(onehot, axis=1, keepdims=True)
    carry_scr[...] = carry
    cnt_ref[...] = carry

    ri_ref[...] = jnp.zeros_like(ri_ref)
    ri_ref[0, 0:1, :] = cls.astype(I32)
    ri_ref[0, 1:2, :] = rank.astype(I32)
    rf_ref[...] = jnp.zeros_like(rf_ref)
    rf_ref[0, 0:1, :] = w_lo
    rf_ref[0, 1:2, :] = w_hi


def _out_route(x, att, gm, mod, lw, *, tm, emit_h2):
    B, S, D = x.shape
    nt = S // tm
    row = lambda b, i: (b, i, 0)
    tile = lambda b, i: (b * nt + i, 0, 0)
    out_shape = [jax.ShapeDtypeStruct((B, S, D), F32),
                 jax.ShapeDtypeStruct((B * nt, SUBLANES, tm), I32),
                 jax.ShapeDtypeStruct((B * nt, SUBLANES, tm), F32),
                 jax.ShapeDtypeStruct((CLASS_ROWS, LANES), F32)]
    out_specs = [pl.BlockSpec((1, tm, D), row),
                 pl.BlockSpec((1, SUBLANES, tm), tile),
                 pl.BlockSpec((1, SUBLANES, tm), tile),
                 _full((CLASS_ROWS, LANES))]
    if emit_h2:
        out_shape += [jax.ShapeDtypeStruct((B, S, D), BF16)]
        out_specs += [pl.BlockSpec((1, tm, D), row)]
    return pl.pallas_call(
        functools.partial(_out_route_kernel, emit_h2=emit_h2),
        grid=(B, nt),
        in_specs=[pl.BlockSpec((1, tm, D), row), pl.BlockSpec((1, tm, ATT_WIDTH), row), pl.BlockSpec((1, tm, GM_WIDTH), row),
                  pl.BlockSpec((1, 6, D), lambda b, i: (b, 0, 0)),
                  _full((1, ATT_WIDTH)), _full((ATT_WIDTH, D)), _full((GM_WIDTH, D)), _full((1, D)),
                  _full((D, 2 * ROUTER_COLS)), _full((D, ROUTER_COLS)), _full((1, ROUTER_COLS))],
        out_specs=out_specs,
        out_shape=out_shape,
        scratch_shapes=[pltpu.VMEM((CLASS_ROWS, LANES), F32)],
        compiler_params=_params(2),
    )(x, att, gm, mod, lw["g_out_attn"], lw["w_out_a"], lw["w_out_g"], lw["g_norm_ffn"],
      lw["w_r_a"], lw["w_r_b"], lw["b_r"])


def _swiglu(hb, wgu, wd):
    hid = _dot(hb, wgu)
    act = jax.nn.silu(hid[:, :D_EXPERT]) * hid[:, D_EXPERT:]
    return _dot(act.astype(BF16), wd)


def _moe_pairs_kernel(elo_ref, ehi_ref, nv_ref, tok_ref, meta_ref, x1_hbm, mod_hi_ref, mod_lo_ref, gffn_ref,
                      wgu_lo_ref, wd_lo_ref, wgu_hi_ref, wd_hi_ref, y_hbm, xbuf, ybuf, gsem, ssem):
    i = pl.program_id(0)
    n = pl.num_programs(0)
    slot = i % 2
    D = xbuf.shape[-1]

    def gather_copy(t, s, r):
        return pltpu.make_async_copy(x1_hbm.at[pl.ds(t, 1)], xbuf.at[s, pl.ds(r, 1)], gsem.at[s])

    def scatter_copy(t, s, r):
        return pltpu.make_async_copy(ybuf.at[s, pl.ds(r, 1)], y_hbm.at[pl.ds(t, 1)], ssem.at[s])

    def for_rows(blk, fn):
        def body(r, c):
            fn(tok_ref[blk * MOE_ROWS + r], r)
            return c
        lax.fori_loop(0, nv_ref[blk], body, 0)

    @pl.when(i == 0)
    def _():
        xbuf[...] = jnp.zeros_like(xbuf)
        for_rows(0, lambda t, r: gather_copy(t, 0, r).start())

    @pl.when(i + 1 < n)
    def _():
        for_rows(i + 1, lambda t, r: gather_copy(t, 1 - slot, r).start())

    for_rows(i, lambda t, r: gather_copy(t, slot, r).wait())

    @pl.when(i >= 2)
    def _():
        for_rows(i - 2, lambda t, r: scatter_copy(t, slot, r).wait())

    @pl.when(nv_ref[i] > 0)
    def _():
        x = xbuf[slot]
        meta = meta_ref[0]
        ra = lax.broadcasted_iota(I32, (MOE_ROWS, LANES), 0)
        rb = lax.broadcasted_iota(I32, (MOE_ROWS, LANES), 1)
        eye = ra == rb
        col = lambda r: jnp.sum(jnp.where(eye, meta[r:r + 1, :], 0.0), axis=1, keepdims=True)
        w_lo, w_hi, bidx = col(0), col(1), col(2)
        blane = lax.broadcasted_iota(I32, (1, MOD_BATCH_PAD), 1).astype(F32)
        onehot = jnp.where(bidx == blane, 1.0, 0.0).astype(BF16)
        modr = _dot(onehot, mod_hi_ref[...]) + _dot(onehot, mod_lo_ref[...])
        shift_m, scale_m, gate_m = modr[:, :D], modr[:, D:2 * D], modr[:, 2 * D:]
        hb = (_rms(x, gffn_ref[...]) * (1.0 + scale_m) + shift_m).astype(BF16)
        moe = w_lo * _swiglu(hb, wgu_lo_ref[0], wd_lo_ref[0]) + w_hi * _swiglu(hb, wgu_hi_ref[0], wd_hi_ref[0])
        ybuf[slot] = x + gate_m * moe
        for_rows(i, lambda t, r: scatter_copy(t, slot, r).start())

    @pl.when(i == n - 1)
    def _():
        @pl.when(i >= 1)
        def _():
            for_rows(i - 1, lambda t, r: scatter_copy(t, 1 - slot, r).wait())
        for_rows(i, lambda t, r: scatter_copy(t, slot, r).wait())


def _moe_pairs(x1, blk_elo, blk_ehi, blk_nv, row_tok, meta, mod_hi, mod_lo, lw):
    T, D = x1.shape
    nb = blk_nv.shape[0]
    wgu, wd = lw["w_gu_e"], lw["w_d_e"]
    grid_spec = pltpu.PrefetchScalarGridSpec(
        num_scalar_prefetch=4,
        grid=(nb,),
        in_specs=[
            pl.BlockSpec((1, SUBLANES, LANES), lambda i, *_: (i, 0, 0)),
            pl.BlockSpec(memory_space=pl.ANY),
            pl.BlockSpec(mod_hi.shape, lambda i, *_: (0, 0)),
            pl.BlockSpec(mod_lo.shape, lambda i, *_: (0, 0)),
            pl.BlockSpec((1, D), lambda i, *_: (0, 0)),
            pl.BlockSpec((1, D, 2 * D_EXPERT), lambda i, elo, ehi, nv, tok: (elo[i], 0, 0)),
            pl.BlockSpec((1, D_EXPERT, D), lambda i, elo, ehi, nv, tok: (elo[i], 0, 0)),
            pl.BlockSpec((1, D, 2 * D_EXPERT), lambda i, elo, ehi, nv, tok: (ehi[i], 0, 0)),
            pl.BlockSpec((1, D_EXPERT, D), lambda i, elo, ehi, nv, tok: (ehi[i], 0, 0)),
        ],
        out_specs=pl.BlockSpec(memory_space=pl.ANY),
        scratch_shapes=[pltpu.VMEM((2, MOE_ROWS, D), F32), pltpu.VMEM((2, MOE_ROWS, D), F32),
                        pltpu.SemaphoreType.DMA((2,)), pltpu.SemaphoreType.DMA((2,))],
    )
    return pl.pallas_call(
        _moe_pairs_kernel,
        grid_spec=grid_spec,
        out_shape=jax.ShapeDtypeStruct((T, D), F32),
        compiler_params=_params(1),
    )(blk_elo, blk_ehi, blk_nv, row_tok, meta, x1, mod_hi, mod_lo, lw["g_norm_ffn"], wgu, wd, wgu, wd)


def _pair_tables():
    lo, hi = [], []
    for g in range(N_GROUPS):
        for a in range(EXPERTS_PER_GROUP):
            for b in range(a + 1, EXPERTS_PER_GROUP):
                lo.append(g * EXPERTS_PER_GROUP + a)
                hi.append(g * EXPERTS_PER_GROUP + b)
    return np.asarray(lo, np.int32), np.asarray(hi, np.int32)


def _moe_prompt(x1, cls, rank, w_lo, w_hi, counts, mod, lw):
    B, S, D = x1.shape
    T = B * S
    nb = T // MOE_ROWS + N_CLASSES
    nblk = (counts + MOE_ROWS - 1) // MOE_ROWS
    blk_end = jnp.cumsum(nblk)
    blk_start = blk_end - nblk
    dest = blk_start[cls] * MOE_ROWS + rank
    tok = jnp.arange(T, dtype=I32)
    row_tok = jnp.zeros((nb * MOE_ROWS,), I32).at[dest].set(tok)
    ids = jnp.arange(nb, dtype=I32)
    used = blk_end[-1]
    last_cls = jnp.searchsorted(blk_end, used - 1, side="right").astype(I32)
    blk_cls = jnp.minimum(jnp.searchsorted(blk_end, ids, side="right").astype(I32), N_CLASSES - 1)
    blk_cls = jnp.where(ids < used, blk_cls, last_cls)
    blk_nv = jnp.where(ids < used, jnp.clip(counts[blk_cls] - (ids - blk_start[blk_cls]) * MOE_ROWS, 0, MOE_ROWS), 0)
    lo_tab, hi_tab = _pair_tables()
    blk_elo = jnp.asarray(lo_tab)[blk_cls]
    blk_ehi = jnp.asarray(hi_tab)[blk_cls]
    valid = (jnp.arange(nb * MOE_ROWS, dtype=I32) % MOE_ROWS) < jnp.repeat(blk_nv, MOE_ROWS)
    meta_rows = [jnp.where(valid, w_lo[row_tok], 0.0), jnp.where(valid, w_hi[row_tok], 0.0),
                 (row_tok // S).astype(F32)]
    meta = jnp.stack([r.reshape(nb, MOE_ROWS) for r in meta_rows], axis=1)
    meta = jnp.pad(meta, ((0, 0), (0, SUBLANES - len(meta_rows)), (0, 0)))
    mod_tab = jnp.pad(mod[:, 3:6, :].reshape(B, 3 * D), ((0, MOD_BATCH_PAD - B), (0, 0)))
    mod_hi, mod_lo = _split_bf16(mod_tab)
    y = _moe_pairs(x1.reshape(T, D), blk_elo, blk_ehi, blk_nv.astype(I32), row_tok, meta, mod_hi, mod_lo, lw)
    return y.reshape(B, S, D)


def _moe_dense_kernel(h_ref, x1_ref, gate_ref, w_ref, sel_ref, wgu_ref, wd_ref, y_ref):
    e = pl.program_id(0)

    @pl.when(e == 0)
    def _():
        y_ref[...] = jnp.zeros_like(y_ref)

    ye = _swiglu(h_ref[...], wgu_ref[0], wd_ref[0])
    y_ref[...] += jnp.where(sel_ref[0] > 0.5, w_ref[0] * ye, 0.0)

    @pl.when(e == pl.num_programs(0) - 1)
    def _():
        y_ref[...] = x1_ref[...] + gate_ref[...] * y_ref[...]


def _moe_dense(h2, x1, gate_rows, w_sel, sel, lw):
    T, D = x1.shape
    per_e = lambda e: (e, 0, 0)
    return pl.pallas_call(
        _moe_dense_kernel,
        grid=(N_EXPERTS,),
        in_specs=[_full((T, D)), _full((T, D)), _full((T, D)),
                  pl.BlockSpec((1, T, 1), per_e), pl.BlockSpec((1, T, 1), per_e),
                  pl.BlockSpec((1, D, 2 * D_EXPERT), per_e), pl.BlockSpec((1, D_EXPERT, D), per_e)],
        out_specs=_full((T, D)),
        out_shape=jax.ShapeDtypeStruct((T, D), F32),
        compiler_params=_params(1),
    )(h2, x1, gate_rows, w_sel, sel, lw["w_gu_e"], lw["w_d_e"])


def _rope_tables(pos, scale):
    inv = 1.0 / (ROPE_BASE ** (jnp.arange(ROPE_HALF, dtype=F32) / ROPE_HALF))
    ang = pos.astype(F32)[:, None] * inv[None, :]
    cos, sin = jnp.cos(ang) * scale, jnp.sin(ang) * scale
    n = pos.shape[0]
    z = lambda w: jnp.zeros((n, w), F32)
    pad = HEAD_PAD - QK_DIM
    c = jnp.concatenate([jnp.full((n, NOPE_DIM), scale, F32), cos, cos, z(pad)], axis=1)
    s_up = jnp.concatenate([z(NOPE_DIM + ROPE_HALF), sin, z(pad)], axis=1)
    s_dn = jnp.concatenate([z(NOPE_DIM), -sin, z(ROPE_HALF + pad)], axis=1)
    return c, s_up, s_dn


def _prep_layer(w, l, chunk_lens):
    D = w["w_in"].shape[1]
    H = MLA_HEADS
    lw = {}
    row = lambda name: w[name][l].reshape(1, -1)
    for name in ("g_norm_mix", "g_q_lat", "g_kv_lat", "g_ln_v", "b_ln_v", "g_out_attn", "g_out_gmlp", "g_norm_ffn"):
        lw[name] = row(name)
    w_in = w["w_in"][l]
    o1, o2, o3 = Q_LORA, Q_LORA + KV_LORA, Q_LORA + KV_LORA + ROPE_DIM
    o4 = o3 + GM_WIDTH
    z32 = jnp.zeros((D, ROPE_DIM), F32)
    wr = w_in[:, o2:o3]
    lw["w_in_p"] = jnp.concatenate([w_in[:, :o2], wr, z32, wr, z32, w_in[:, o3:o4], w_in[:, o4:]], axis=1).astype(BF16)
    assert lw["w_in_p"].shape[1] == PROJ_COLS
    pad = HEAD_PAD - QK_DIM
    lw["w_uq_p"] = jnp.pad(w["w_uq"][l].reshape(Q_LORA, H, QK_DIM), ((0, 0), (0, 0), (0, pad))).reshape(Q_LORA, H * HEAD_PAD).astype(BF16)
    w_ukv = w["w_ukv"][l].reshape(KV_LORA, H, NOPE_DIM + V_DIM)
    lw["w_k_p"] = jnp.pad(w_ukv[:, :, :NOPE_DIM], ((0, 0), (0, 0), (0, HEAD_PAD - NOPE_DIM))).reshape(KV_LORA, H * HEAD_PAD).astype(BF16)
    lw["w_v_p"] = w_ukv[:, :, NOPE_DIM:].reshape(KV_LORA, ATT_WIDTH).astype(BF16)
    lw["g_qnorm_p"] = jnp.pad(w["g_qnorm"][l], (0, pad)).reshape(1, HEAD_PAD)
    lw["g_knorm_p"] = jnp.pad(w["g_knorm"][l], (0, pad)).reshape(1, HEAD_PAD)
    for L in chunk_lens:
        wsp = w["w_spatial"][l][:, :L, :L]
        lw["w_sp_pairs_%d" % L] = wsp.reshape(GM_HEADS // 2, 2 * L, L).astype(BF16)
        lw["b_sp_rows_%d" % L] = jnp.repeat(jnp.transpose(w["b_spatial"][l][:, :L]), GM_HEAD_DIM, axis=1)
    w_out = w["w_out"][l].astype(BF16)
    lw["w_out_a"], lw["w_out_g"] = w_out[:ATT_WIDTH], w_out[ATT_WIDTH:]
    wr_full = jnp.zeros((D, ROUTER_COLS), F32)
    wr_full = wr_full.at[:, :N_GROUPS].set(w["w_router_group"][l])
    wr_full = wr_full.at[:, ROUTER_EXPERT_LANE0:ROUTER_EXPERT_LANE0 + N_EXPERTS].set(w["w_router_expert"][l])
    r_hi, r_lo = _split_bf16(wr_full)
    lw["w_r_a"] = jnp.concatenate([r_hi, r_lo], axis=1)
    lw["w_r_b"] = r_hi
    br = jnp.zeros((1, ROUTER_COLS), F32)
    br = br.at[0, :N_GROUPS].set(w["b_router_group"][l])
    lw["b_r"] = br.at[0, ROUTER_EXPERT_LANE0:ROUTER_EXPERT_LANE0 + N_EXPERTS].set(w["b_router_expert"][l])
    lw["w_gu_e"] = jnp.concatenate([w["w_gate_e"][l], w["w_up_e"][l]], axis=-1).astype(BF16)
    lw["w_d_e"] = w["w_down_e"][l].astype(BF16)
    return lw


def _tiles(seq):
    tm = min(seq, 512)
    return tm, min(seq, 512)


def _route_rows(ri, rf, n_tok):
    cls = ri[:, 0, :].reshape(n_tok)
    rank = ri[:, 1, :].reshape(n_tok)
    return cls, rank, rf[:, 0, :].reshape(n_tok), rf[:, 1, :].reshape(n_tok)


def _layer_prompt(x, mod, lw):
    B, S, D = x.shape
    tm, tq = _tiles(S)
    pos = jnp.arange(S)
    q_tabs = _rope_tables(pos, QK_DIM ** -0.5)
    k_tabs = _rope_tables(pos, 1.0)
    ckv, krope, q, gm, k, v = _mix_in(x, mod, lw, q_tabs, k_tabs, tm=tm, chunk_len=GM_CHUNK, emit_kv=True, emit_vrows=False)
    att = _attention(q, k, v, tq=tq, tk=tq, q_off=0, kv_valid=S)
    x1, ri, rf, cnt = _out_route(x, att, gm, mod, lw, tm=tm, emit_h2=False)
    cls, rank, w_lo, w_hi = _route_rows(ri, rf, B * S)
    counts = cnt[:N_CLASSES, 0].astype(I32)
    y = _moe_prompt(x1, cls, rank, w_lo, w_hi, counts, mod, lw)
    return y, ckv, krope


def _layer_sample(x, mod, past_ckv, past_krope, lw):
    B, S, D = x.shape
    past = past_ckv.shape[1]
    q_tabs = _rope_tables(past + jnp.arange(S), QK_DIM ** -0.5)
    ckv, krope, q, gm, v_rows = _mix_in(x, mod, lw, q_tabs, q_tabs, tm=S, chunk_len=S, emit_kv=False, emit_vrows=True)
    kv_valid = past + S
    kv_pad = -(-kv_valid // LANES) * LANES
    extra = kv_pad - kv_valid
    ckv_all = jnp.concatenate([past_ckv, ckv, jnp.zeros((B, extra, KV_LORA), F32)], axis=1)
    kr_all = jnp.concatenate([past_krope, krope, jnp.zeros((B, extra, ROPE_DIM), F32)], axis=1)
    kr_slot = jnp.pad(kr_all, ((0, 0), (0, 0), (NOPE_DIM, HEAD_PAD - QK_DIM)))
    k_tabs = _rope_tables(jnp.arange(kv_pad), 1.0)
    k, v = _kv_latent(ckv_all, kr_slot, lw, k_tabs, tr=kv_pad)
    att = _attention(q, k, v, tq=S, tk=kv_pad, q_off=past, kv_valid=kv_valid)
    x1, ri, rf, _, h2 = _out_route(x, att, gm, mod, lw, tm=S, emit_h2=True)
    T = B * S
    cls, _, w_lo, w_hi = _route_rows(ri, rf, T)
    lo_tab, hi_tab = _pair_tables()
    e_lo, e_hi = jnp.asarray(lo_tab)[cls], jnp.asarray(hi_tab)[cls]
    eids = jnp.arange(N_EXPERTS, dtype=I32)[:, None]
    is_lo, is_hi = eids == e_lo[None, :], eids == e_hi[None, :]
    w_sel = (jnp.where(is_lo, w_lo[None, :], 0.0) + jnp.where(is_hi, w_hi[None, :], 0.0))[:, :, None]
    sel = (is_lo | is_hi).astype(F32)[:, :, None]
    gate_rows = jnp.repeat(mod[:, 5, :], S, axis=0)
    y = _moe_dense(h2.reshape(T, D), x1.reshape(T, D), gate_rows, w_sel, sel, lw)
    return y.reshape(B, S, D), ckv, krope, v_rows


def kernel(x_prompt, x_sample, cache_ckv, cache_krope, c_prompt, c_sample, w_ada, b_ada, g_norm_mix, w_in, g_q_lat, w_uq, g_kv_lat, w_ukv, g_qnorm, g_knorm, g_ln_v, b_ln_v, w_spatial, b_spatial, g_out_attn, g_out_gmlp, w_out, g_norm_ffn, w_router_group, b_router_group, w_router_expert, b_router_expert, w_gate_e, w_up_e, w_down_e):
    w = dict(w_in=w_in, g_norm_mix=g_norm_mix, g_q_lat=g_q_lat, w_uq=w_uq, g_kv_lat=g_kv_lat, w_ukv=w_ukv,
             g_qnorm=g_qnorm, g_knorm=g_knorm, g_ln_v=g_ln_v, b_ln_v=b_ln_v, w_spatial=w_spatial, b_spatial=b_spatial,
             g_out_attn=g_out_attn, g_out_gmlp=g_out_gmlp, w_out=w_out, g_norm_ffn=g_norm_ffn,
             w_router_group=w_router_group, b_router_group=b_router_group, w_router_expert=w_router_expert,
             b_router_expert=b_router_expert, w_gate_e=w_gate_e, w_up_e=w_up_e, w_down_e=w_down_e)
    depth = w_ada.shape[0]
    Bp, Sp, D = x_prompt.shape
    Bs, Ss, _ = x_sample.shape
    assert Sp % GM_CHUNK == 0 and Ss <= GM_CHUNK and Ss % CHUNK == 0 and Bp <= MOD_BATCH_PAD
    c_all = jnp.concatenate([c_prompt, c_sample], axis=0)
    y_p, y_s = x_prompt, x_sample
    outs = [[] for _ in range(5)]
    for l in range(depth):
        lw = _prep_layer(w, l, (GM_CHUNK, Ss))
        mod = _ada_mod(c_all, w_ada[l], b_ada[l]).reshape(Bp + Bs, 6, D)
        y_p, ckv_p, kr_p = _layer_prompt(y_p, mod[:Bp], lw)
        y_s, ckv_s, kr_s, v_s = _layer_sample(y_s, mod[Bp:], cache_ckv[l], cache_krope[l], lw)
        for lst, val in zip(outs, (ckv_p, kr_p, ckv_s, kr_s, v_s)):
            lst.append(val)
    return (y_p, y_s) + tuple(jnp.stack(lst) for lst in outs)
```

```python
import functools

import numpy as np
import jax
import jax.numpy as jnp
from jax import lax
from jax.experimental import pallas as pl
from jax.experimental.pallas import tpu as pltpu

F32 = jnp.float32
BF16 = jnp.bfloat16
I32 = jnp.int32

CHUNK = 64
CHUNK_SHIFT = 6
EPS = 1e-6
MLA_HEADS = 8
Q_LORA = 256
KV_LORA = 128
NOPE_DIM = 64
ROPE_DIM = 32
ROPE_HALF = ROPE_DIM // 2
V_DIM = 64
QK_DIM = NOPE_DIM + ROPE_DIM
ATT_WIDTH = MLA_HEADS * V_DIM
ROPE_BASE = 10000.0
GM_HEADS = 8
GM_HEAD_DIM = 64
GM_WIDTH = GM_HEADS * GM_HEAD_DIM
GM_CHUNK = 128
N_GROUPS = 4
EXPERTS_PER_GROUP = 8
N_EXPERTS = N_GROUPS * EXPERTS_PER_GROUP
D_EXPERT = 256
PAIRS_PER_GROUP = EXPERTS_PER_GROUP * (EXPERTS_PER_GROUP - 1) // 2
N_CLASSES = N_GROUPS * PAIRS_PER_GROUP

LANES = 128
SUBLANES = 8
HEAD_PAD = LANES
PROJ_COLS = 1536
ROUTER_COLS = LANES
ROUTER_EXPERT_LANE0 = SUBLANES
CLASS_ROWS = LANES
MOE_ROWS = 128
MOD_BATCH_PAD = 16
VMEM_LIMIT = 48 * 1024 * 1024
NEG_BIG = -1e30
LOG2E = 1.4426950408889634
BF16_SLACK = 1.02
SCORE_BOUND = 90.0

assert CHUNK == 1 << CHUNK_SHIFT


def _params(n_axes, vmem=VMEM_LIMIT):
    return pltpu.CompilerParams(dimension_semantics=("arbitrary",) * n_axes, vmem_limit_bytes=vmem)


def _full(shape):
    nd = len(shape)
    return pl.BlockSpec(shape, lambda *_: (0,) * nd)


def _split_bf16(x):
    hi = x.astype(BF16)
    lo = (x - hi.astype(F32)).astype(BF16)
    return hi, lo


def _dot(a, b):
    return jnp.dot(a, b, preferred_element_type=F32)


def _ada_kernel(c_ref, w_ref, b_ref, o_ref):
    a_hi, a_lo = _split_bf16(jax.nn.silu(c_ref[...]))
    w_hi, w_lo = _split_bf16(w_ref[...])
    o_ref[...] = _dot(a_hi, w_hi) + _dot(a_lo, w_hi) + _dot(a_hi, w_lo) + b_ref[...]


def _ada_mod(c, w_ada, b_ada):
    n, d = c.shape
    cols = w_ada.shape[1]
    tn = 1536
    return pl.pallas_call(
        _ada_kernel,
        grid=(cols // tn,),
        in_specs=[_full((n, d)), pl.BlockSpec((d, tn), lambda j: (0, j)), pl.BlockSpec((1, tn), lambda j: (0, j))],
        out_specs=pl.BlockSpec((n, tn), lambda j: (0, j)),
        out_shape=jax.ShapeDtypeStruct((n, cols), F32),
        compiler_params=_params(1),
    )(c, w_ada, b_ada.reshape(1, cols))


def _rms(x, g):
    return x * lax.rsqrt(jnp.mean(x * x, axis=-1, keepdims=True) + EPS) * g


def _head_norm_rope(xh, g, c, s_up, s_dn):
    ms = jnp.sum(xh * xh, axis=-1, keepdims=True) * (1.0 / QK_DIM)
    xh = xh * lax.rsqrt(ms + EPS) * g
    return xh * c + pltpu.roll(xh, ROPE_HALF, 1) * s_up + pltpu.roll(xh, HEAD_PAD - ROPE_HALF, 1) * s_dn


def _keys_values(ckv, kr_slot, gkv, wk, wv, gkn, c, s_up, s_dn, k_ref, v_ref):
    cb = _rms(ckv, gkv).astype(BF16)
    kall = _dot(cb, wk)
    vall = _dot(cb, wv)
    lane = lax.broadcasted_iota(I32, (1, LANES), 1)
    one_col = jnp.where(lane == V_DIM, 1.0, 0.0)
    for h in range(MLA_HEADS):
        kh = kall[:, h * HEAD_PAD:(h + 1) * HEAD_PAD] + kr_slot
        k_ref[0, h] = _head_norm_rope(kh, gkn, c, s_up, s_dn).astype(BF16)
        v_ref[0, h] = (vall[:, h * HEAD_PAD:(h + 1) * HEAD_PAD] + one_col).astype(BF16)


def _mix_in_kernel(x_ref, mod_ref, gmix_ref, win_ref, gql_ref, wuq_ref, gkv_ref, wk_ref, wv_ref, gqn_ref, gkn_ref,
                   cq_ref, squ_ref, sqd_ref, ck_ref, sku_ref, skd_ref, glnv_ref, blnv_ref, wsp_ref, bsp_ref,
                   ggm_ref, *rest, chunk_len, emit_kv, emit_vrows):
    outs = list(rest[:-1])
    mixed_scr = rest[-1]
    ckv_ref, kr_ref, q_ref, gm_ref = outs[:4]
    outs = outs[4:]
    if emit_kv:
        k_ref, v_ref = outs[:2]
        outs = outs[2:]
    if emit_vrows:
        vrows_ref = outs[0]

    x = x_ref[0]
    tm = x.shape[0]
    shift, scale = mod_ref[0, 0:1, :], mod_ref[0, 1:2, :]
    h = _rms(x, gmix_ref[...]) * (1.0 + scale) + shift
    proj = _dot(h.astype(BF16), win_ref[...])

    q_lat = proj[:, 0:Q_LORA]
    ckv = proj[:, Q_LORA:Q_LORA + KV_LORA]
    kr_blk = proj[:, Q_LORA + KV_LORA:Q_LORA + KV_LORA + LANES]
    ckv_ref[0] = ckv
    kr_ref[0] = kr_blk[:, 0:ROPE_DIM]

    q = _dot(_rms(q_lat, gql_ref[...]).astype(BF16), wuq_ref[...])
    cq, squ, sqd = cq_ref[...], squ_ref[...], sqd_ref[...]
    for hd in range(MLA_HEADS):
        qh = q[:, hd * HEAD_PAD:(hd + 1) * HEAD_PAD]
        q_ref[0, hd] = _head_norm_rope(qh, gqn_ref[...], cq, squ, sqd).astype(BF16)

    if emit_kv:
        lane = lax.broadcasted_iota(I32, (1, LANES), 1)
        kr_slot = jnp.where(lane >= NOPE_DIM, kr_blk, 0.0)
        _keys_values(ckv, kr_slot, gkv_ref[...], wk_ref[...], wv_ref[...], gkn_ref[...],
                     ck_ref[...], sku_ref[...], skd_ref[...], k_ref, v_ref)

    g_u = proj[:, 512:512 + GM_WIDTH]
    g_v = proj[:, 1024:1024 + GM_WIDTH]
    u = jax.nn.gelu(g_u)
    gv = jax.nn.gelu(g_v)
    mu = jnp.mean(gv, axis=-1, keepdims=True)
    xc = gv - mu
    var = jnp.mean(xc * xc, axis=-1, keepdims=True)
    v_rows = xc * lax.rsqrt(var + EPS) * glnv_ref[...] + blnv_ref[...]
    if emit_vrows:
        vrows_ref[0] = v_rows
    vb = v_rows.astype(BF16)

    L = chunk_len
    t = lax.broadcasted_iota(I32, (2 * L, L), 0)
    s = lax.broadcasted_iota(I32, (2 * L, L), 1)
    t = jnp.where(t >= L, t - L, t)
    allowed = (s >> CHUNK_SHIFT) <= (t >> CHUNK_SHIFT)
    lane = lax.broadcasted_iota(I32, (1, LANES), 1)
    first_head = lane < GM_HEAD_DIM
    for p in range(GM_HEADS // 2):
        w_pair = jnp.where(allowed, wsp_ref[p], jnp.zeros((), BF16))
        for c in range(tm // L):
            vp = vb[c * L:(c + 1) * L, p * LANES:(p + 1) * LANES]
            r = _dot(w_pair, vp)
            mixed = jnp.where(first_head, r[:L], r[L:])
            mixed_scr[c * L:(c + 1) * L, p * LANES:(p + 1) * LANES] = mixed + bsp_ref[:, p * LANES:(p + 1) * LANES]
    gm = u * mixed_scr[...]
    gm_ref[0] = _rms(gm, ggm_ref[...]).astype(BF16)


def _mix_in(x, mod, lw, q_tabs, k_tabs, *, tm, chunk_len, emit_kv, emit_vrows):
    B, S, D = x.shape
    nt = S // tm
    H = MLA_HEADS
    row = lambda b, i: (b, i, 0)
    tab = pl.BlockSpec((tm, LANES), lambda b, i: (i, 0))
    in_specs = [
        pl.BlockSpec((1, tm, D), row),
        pl.BlockSpec((1, 6, D), lambda b, i: (b, 0, 0)),
        _full((1, D)), _full((D, PROJ_COLS)), _full((1, Q_LORA)), _full((Q_LORA, H * HEAD_PAD)),
        _full((1, KV_LORA)), _full((KV_LORA, H * HEAD_PAD)), _full((KV_LORA, H * HEAD_PAD)),
        _full((1, LANES)), _full((1, LANES)),
        tab, tab, tab, tab, tab, tab,
        _full((1, GM_WIDTH)), _full((1, GM_WIDTH)),
        _full((GM_HEADS // 2, 2 * chunk_len, chunk_len)), _full((chunk_len, GM_WIDTH)), _full((1, GM_WIDTH)),
    ]
    out_shape = [
        jax.ShapeDtypeStruct((B, S, KV_LORA), F32),
        jax.ShapeDtypeStruct((B, S, ROPE_DIM), F32),
        jax.ShapeDtypeStruct((B, H, S, HEAD_PAD), BF16),
        jax.ShapeDtypeStruct((B, S, GM_WIDTH), BF16),
    ]
    head_blk = pl.BlockSpec((1, H, tm, HEAD_PAD), lambda b, i: (b, 0, i, 0))
    out_specs = [
        pl.BlockSpec((1, tm, KV_LORA), row),
        pl.BlockSpec((1, tm, ROPE_DIM), row),
        head_blk,
        pl.BlockSpec((1, tm, GM_WIDTH), row),
    ]
    if emit_kv:
        out_shape += [jax.ShapeDtypeStruct((B, H, S, HEAD_PAD), BF16), jax.ShapeDtypeStruct((B, H, S, HEAD_PAD), BF16)]
        out_specs += [head_blk, head_blk]
    if emit_vrows:
        out_shape += [jax.ShapeDtypeStruct((B, S, GM_WIDTH), F32)]
        out_specs += [pl.BlockSpec((1, tm, GM_WIDTH), row)]
    kern = functools.partial(_mix_in_kernel, chunk_len=chunk_len, emit_kv=emit_kv, emit_vrows=emit_vrows)
    return pl.pallas_call(
        kern,
        grid=(B, nt),
        in_specs=in_specs,
        out_specs=out_specs,
        out_shape=out_shape,
        scratch_shapes=[pltpu.VMEM((tm, GM_WIDTH), F32)],
        compiler_params=_params(2),
    )(x, mod, lw["g_norm_mix"], lw["w_in_p"], lw["g_q_lat"], lw["w_uq_p"], lw["g_kv_lat"], lw["w_k_p"], lw["w_v_p"],
      lw["g_qnorm_p"], lw["g_knorm_p"], *q_tabs, *k_tabs, lw["g_ln_v"], lw["b_ln_v"],
      lw["w_sp_pairs_%d" % chunk_len], lw["b_sp_rows_%d" % chunk_len], lw["g_out_gmlp"])


def _kv_latent_kernel(ckv_ref, kr_ref, gkv_ref, wk_ref, wv_ref, gkn_ref, c_ref, su_ref, sd_ref, k_ref, v_ref):
    _keys_values(ckv_ref[0], kr_ref[0], gkv_ref[...], wk_ref[...], wv_ref[...], gkn_ref[...],
                 c_ref[...], su_ref[...], sd_ref[...], k_ref, v_ref)


def _kv_latent(ckv_all, kr_slot_all, lw, k_tabs, *, tr):
    B, K, _ = ckv_all.shape
    H = MLA_HEADS
    row = lambda b, i: (b, i, 0)
    tab = pl.BlockSpec((tr, LANES), lambda b, i: (i, 0))
    return pl.pallas_call(
        _kv_latent_kernel,
        grid=(B, K // tr),
        in_specs=[pl.BlockSpec((1, tr, KV_LORA), row), pl.BlockSpec((1, tr, LANES), row),
                  _full((1, KV_LORA)), _full((KV_LORA, H * HEAD_PAD)), _full((KV_LORA, H * HEAD_PAD)), _full((1, LANES)),
                  tab, tab, tab],
        out_specs=[pl.BlockSpec((1, H, tr, HEAD_PAD), lambda b, i: (b, 0, i, 0))] * 2,
        out_shape=[jax.ShapeDtypeStruct((B, H, K, HEAD_PAD), BF16)] * 2,
        compiler_params=_params(2),
    )(ckv_all, kr_slot_all, lw["g_kv_lat"], lw["w_k_p"], lw["w_v_p"], lw["g_knorm_p"], *k_tabs)


def _attn_kernel(bounded_ref, q_ref, k_ref, v_ref, o_ref, *, tq, tk, n_q, q_off, kv_valid):
    i = pl.program_id(2) if n_q > 1 else 0
    q_first = q_off + i * tq
    vis_first = jnp.minimum(((q_first >> CHUNK_SHIFT) + 1) << CHUNK_SHIFT, kv_valid)
    vis_last = jnp.minimum((((q_first + tq - 1) >> CHUNK_SHIFT) + 1) << CHUNK_SHIFT, kv_valid)
    n_unmasked = vis_first // tk
    n_total = (vis_last + tk - 1) // tk

    q_pos = q_first + lax.broadcasted_iota(I32, (tq, 1), 0)
    limit = jnp.minimum(((q_pos >> CHUNK_SHIFT) + 1) << CHUNK_SHIFT, kv_valid)
    lane = lax.broadcasted_iota(I32, (1, LANES), 1)

    def scores(j, start, masked):
        s = lax.dot_general(q_ref[0, j], k_ref[0, j, pl.ds(start, tk), :], (((1,), (1,)), ((), ())),
                            preferred_element_type=F32)
        if masked:
            k_pos = start + lax.broadcasted_iota(I32, (1, tk), 1)
            s = jnp.where(k_pos < limit, s, NEG_BIG)
        return s

    def plain_block(kb, carry, masked):
        start = pl.multiple_of(kb * tk, tk)
        new = []
        for j in range(2):
            p = jnp.exp2(scores(j, start, masked)).astype(BF16)
            new.append(carry[j] + _dot(p, v_ref[0, j, pl.ds(start, tk), :]))
        return tuple(new)

    def online_block(kb, carry, masked):
        start = pl.multiple_of(kb * tk, tk)
        new = []
        for j in range(2):
            m, acc = carry[2 * j:2 * j + 2]
            s = scores(j, start, masked)
            m_new = jnp.maximum(m, jnp.max(s, axis=-1, keepdims=True))
            p = jnp.exp2(s - m_new).astype(BF16)
            acc = jnp.exp2(m - m_new) * acc + _dot(p, v_ref[0, j, pl.ds(start, tk), :])
            new += [m_new, acc]
        return tuple(new)

    def run(block, init):
        carry = lax.fori_loop(0, n_unmasked, lambda kb, c: block(kb, c, False), tuple(init))
        return lax.fori_loop(n_unmasked, n_total, lambda kb, c: block(kb, c, True), carry)

    def finish(acc0, acc1):
        outs = []
        for acc in (acc0, acc1):
            denom = jnp.sum(jnp.where(lane == V_DIM, acc, 0.0), axis=-1, keepdims=True)
            outs.append(acc / denom)
        o_ref[0] = jnp.where(lane < V_DIM, outs[0], pltpu.roll(outs[1], V_DIM, 1)).astype(BF16)

    zeros = jnp.zeros((tq, LANES), F32)

    @pl.when(bounded_ref[0] == 1)
    def _():
        finish(*run(plain_block, [zeros, zeros]))

    @pl.when(bounded_ref[0] != 1)
    def _():
        m0 = jnp.full((tq, 1), NEG_BIG, F32)
        c = run(online_block, [m0, zeros, m0, zeros])
        finish(c[1], c[3])


def _attention(bounded, q, k, v, *, tq, tk, q_off, kv_valid):
    B, H, Sq, _ = q.shape
    Sk = k.shape[2]
    kern = functools.partial(_attn_kernel, tq=tq, tk=tk, n_q=Sq // tq, q_off=q_off, kv_valid=kv_valid)
    grid_spec = pltpu.PrefetchScalarGridSpec(
        num_scalar_prefetch=1,
        grid=(B, H // 2, Sq // tq),
        in_specs=[pl.BlockSpec((1, 2, tq, HEAD_PAD), lambda b, hp, i, f: (b, hp, i, 0)),
                  pl.BlockSpec((1, 2, Sk, HEAD_PAD), lambda b, hp, i, f: (b, hp, 0, 0)),
                  pl.BlockSpec((1, 2, Sk, HEAD_PAD), lambda b, hp, i, f: (b, hp, 0, 0))],
        out_specs=pl.BlockSpec((1, tq, LANES), lambda b, hp, i, f: (b, i, hp)),
    )
    return pl.pallas_call(
        kern,
        grid_spec=grid_spec,
        out_shape=jax.ShapeDtypeStruct((B, Sq, ATT_WIDTH), BF16),
        compiler_params=_params(3),
    )(bounded, q, k, v)


def _scores_bounded(w, l):
    gq = jnp.max(jnp.abs(w["g_qnorm"][l]))
    gk = jnp.max(jnp.abs(w["g_knorm"][l]))
    bound = (QK_DIM ** 0.5) * LOG2E * BF16_SLACK * gq * gk
    return (bound <= SCORE_BOUND).astype(I32).reshape(1)


def _out_route_kernel(x_ref, att_ref, gm_ref, mod_ref, goa_ref, woa_ref, wog_ref, gffn_ref, wra_ref, wrb_ref, br_ref,
                      x1_ref, ri_ref, rf_ref, cnt_ref, *rest, emit_h2):
    carry_scr = rest[-1]
    first_step = (pl.program_id(0) == 0) & (pl.program_id(1) == 0)

    @pl.when(first_step)
    def _():
        carry_scr[...] = jnp.zeros_like(carry_scr)

    x = x_ref[0]
    tm = x.shape[0]
    gate_a = mod_ref[0, 2:3, :]
    shift_m, scale_m = mod_ref[0, 3:4, :], mod_ref[0, 4:5, :]
    att_n = _rms(att_ref[0].astype(F32), goa_ref[...]).astype(BF16)
    mix = _dot(att_n, woa_ref[...]) + _dot(gm_ref[0], wog_ref[...])
    x1 = x + gate_a * mix
    x1_ref[0] = x1
    h2 = _rms(x1, gffn_ref[...]) * (1.0 + scale_m) + shift_m
    if emit_h2:
        rest[0][0] = h2.astype(BF16)

    h_hi, h_lo = _split_bf16(h2)
    la = _dot(h_hi, wra_ref[...])
    logits = la[:, :ROUTER_COLS] + la[:, ROUTER_COLS:] + _dot(h_lo, wrb_ref[...]) + br_ref[...]
    lt = logits.T

    g = [lt[r:r + 1] for r in range(N_GROUPS)]
    gmax = jnp.maximum(jnp.maximum(g[0], g[1]), jnp.maximum(g[2], g[3]))
    gsum = sum(jnp.exp(gr - gmax) for gr in g)
    g_prob = 1.0 / gsum
    g_idx = jnp.where(g[0] == gmax, 0.0, jnp.where(g[1] == gmax, 1.0, jnp.where(g[2] == gmax, 2.0, 3.0)))

    e0 = ROUTER_EXPERT_LANE0
    grp = [lt[e0 + EXPERTS_PER_GROUP * r:e0 + EXPERTS_PER_GROUP * (r + 1)] for r in range(N_GROUPS)]
    sel = jnp.where(g_idx == 0.0, grp[0], jnp.where(g_idx == 1.0, grp[1], jnp.where(g_idx == 2.0, grp[2], grp[3])))
    sub = lax.broadcasted_iota(I32, (EXPERTS_PER_GROUP, tm), 0).astype(F32)
    m1 = jnp.max(sel, axis=0, keepdims=True)
    i1 = jnp.min(jnp.where(sel == m1, sub, float(EXPERTS_PER_GROUP)), axis=0, keepdims=True)
    sel2 = jnp.where(sub == i1, -jnp.inf, sel)
    m2 = jnp.max(sel2, axis=0, keepdims=True)
    i2 = jnp.min(jnp.where(sel2 == m2, sub, float(EXPERTS_PER_GROUP)), axis=0, keepdims=True)
    d = jnp.exp(m2 - m1)
    w1 = g_prob / (1.0 + d)
    w2 = g_prob * d / (1.0 + d)
    first_lower = i1 < i2
    lo = jnp.minimum(i1, i2)
    hi = jnp.maximum(i1, i2)
    w_lo = jnp.where(first_lower, w1, w2)
    w_hi = jnp.where(first_lower, w2, w1)
    pair = lo * EXPERTS_PER_GROUP - lo * (lo + 1.0) * 0.5 + hi - lo - 1.0
    cls = g_idx * PAIRS_PER_GROUP + pair

    crow = lax.broadcasted_iota(I32, (CLASS_ROWS, tm), 0).astype(F32)
    onehot = jnp.where(crow == cls, 1.0, 0.0)
    ta = lax.broadcasted_iota(I32, (tm, tm), 0)
    tb = lax.broadcasted_iota(I32, (tm, tm), 1)
    earlier = jnp.where(ta < tb, 1.0, 0.0).astype(BF16)
    before = _dot(onehot.astype(BF16), earlier)
    carry = carry_scr[...]
    rank = jnp.sum(onehot * (before + carry[:, 0:1]), axis=0, keepdims=True)
    carry = carry + jnp.sum(onehot, axis=1, keepdims=True)
    carry_scr[...] = carry
    cnt_ref[...] = carry

    ri_ref[...] = jnp.zeros_like(ri_ref)
    ri_ref[0, 0:1, :] = cls.astype(I32)
    ri_ref[0, 1:2, :] = rank.astype(I32)
    rf_ref[...] = jnp.zeros_like(rf_ref)
    rf_ref[0, 0:1, :] = w_lo
    rf_ref[0, 1:2, :] = w_hi


def _out_route(x, att, gm, mod, lw, *, tm, emit_h2):
    B, S, D = x.shape
    nt = S // tm
    row = lambda b, i: (b, i, 0)
    tile = lambda b, i: (b * nt + i, 0, 0)
    out_shape = [jax.ShapeDtypeStruct((B, S, D), F32),
                 jax.ShapeDtypeStruct((B * nt, SUBLANES, tm), I32),
                 jax.ShapeDtypeStruct((B * nt, SUBLANES, tm), F32),
                 jax.ShapeDtypeStruct((CLASS_ROWS, LANES), F32)]
    out_specs = [pl.BlockSpec((1, tm, D), row),
                 pl.BlockSpec((1, SUBLANES, tm), tile),
                 pl.BlockSpec((1, SUBLANES, tm), tile),
                 _full((CLASS_ROWS, LANES))]
    if emit_h2:
        out_shape += [jax.ShapeDtypeStruct((B, S, D), BF16)]
        out_specs += [pl.BlockSpec((1, tm, D), row)]
    return pl.pallas_call(
        functools.partial(_out_route_kernel, emit_h2=emit_h2),
        grid=(B, nt),
        in_specs=[pl.BlockSpec((1, tm, D), row), pl.BlockSpec((1, tm, ATT_WIDTH), row), pl.BlockSpec((1, tm, GM_WIDTH), row),
                  pl.BlockSpec((1, 6, D), lambda b, i: (b, 0, 0)),
                  _full((1, ATT_WIDTH)), _full((ATT_WIDTH, D)), _full((GM_WIDTH, D)), _full((1, D)),
                  _full((D, 2 * ROUTER_COLS)), _full((D, ROUTER_COLS)), _full((1, ROUTER_COLS))],
        out_specs=out_specs,
        out_shape=out_shape,
        scratch_shapes=[pltpu.VMEM((CLASS_ROWS, LANES), F32)],
        compiler_params=_params(2),
    )(x, att, gm, mod, lw["g_out_attn"], lw["w_out_a"], lw["w_out_g"], lw["g_norm_ffn"],
      lw["w_r_a"], lw["w_r_b"], lw["b_r"])


def _swiglu(hb, wgu, wd):
    hid = _dot(hb, wgu)
    act = jax.nn.silu(hid[:, :D_EXPERT]) * hid[:, D_EXPERT:]
    return _dot(act.astype(BF16), wd)


def _moe_pairs_kernel(elo_ref, ehi_ref, nv_ref, tok_ref, meta_ref, x1_hbm, mod_hi_ref, mod_lo_ref, gffn_ref,
                      wgu_lo_ref, wd_lo_ref, wgu_hi_ref, wd_hi_ref, y_hbm, xbuf, ybuf, gsem, ssem):
    i = pl.program_id(0)
    n = pl.num_programs(0)
    slot = i % 2
    D = xbuf.shape[-1]

    def gather_copy(t, s, r):
        return pltpu.make_async_copy(x1_hbm.at[pl.ds(t, 1)], xbuf.at[s, pl.ds(r, 1)], gsem.at[s])

    def scatter_copy(t, s, r):
        return pltpu.make_async_copy(ybuf.at[s, pl.ds(r, 1)], y_hbm.at[pl.ds(t, 1)], ssem.at[s])

    def for_rows(blk, fn):
        def body(r, c):
            fn(tok_ref[blk * MOE_ROWS + r], r)
            return c
        lax.fori_loop(0, nv_ref[blk], body, 0)

    @pl.when(i == 0)
    def _():
        xbuf[...] = jnp.zeros_like(xbuf)
        for_rows(0, lambda t, r: gather_copy(t, 0, r).start())

    @pl.when(i + 1 < n)
    def _():
        for_rows(i + 1, lambda t, r: gather_copy(t, 1 - slot, r).start())

    for_rows(i, lambda t, r: gather_copy(t, slot, r).wait())

    @pl.when(i >= 2)
    def _():
        for_rows(i - 2, lambda t, r: scatter_copy(t, slot, r).wait())

    @pl.when(nv_ref[i] > 0)
    def _():
        x = xbuf[slot]
        meta = meta_ref[0]
        ra = lax.broadcasted_iota(I32, (MOE_ROWS, LANES), 0)
        rb = lax.broadcasted_iota(I32, (MOE_ROWS, LANES), 1)
        eye = ra == rb
        col = lambda r: jnp.sum(jnp.where(eye, meta[r:r + 1, :], 0.0), axis=1, keepdims=True)
        w_lo, w_hi, bidx = col(0), col(1), col(2)
        blane = lax.broadcasted_iota(I32, (1, MOD_BATCH_PAD), 1).astype(F32)
        onehot = jnp.where(bidx == blane, 1.0, 0.0).astype(BF16)
        modr = _dot(onehot, mod_hi_ref[...]) + _dot(onehot, mod_lo_ref[...])
        shift_m, scale_m, gate_m = modr[:, :D], modr[:, D:2 * D], modr[:, 2 * D:]
        hb = (_rms(x, gffn_ref[...]) * (1.0 + scale_m) + shift_m).astype(BF16)
        moe = w_lo * _swiglu(hb, wgu_lo_ref[0], wd_lo_ref[0]) + w_hi * _swiglu(hb, wgu_hi_ref[0], wd_hi_ref[0])
        ybuf[slot] = x + gate_m * moe
        for_rows(i, lambda t, r: scatter_copy(t, slot, r).start())

    @pl.when(i == n - 1)
    def _():
        @pl.when(i >= 1)
        def _():
            for_rows(i - 1, lambda t, r: scatter_copy(t, 1 - slot, r).wait())
        for_rows(i, lambda t, r: scatter_copy(t, slot, r).wait())


def _moe_pairs(x1, blk_elo, blk_ehi, blk_nv, row_tok, meta, mod_hi, mod_lo, lw):
    T, D = x1.shape
    nb = blk_nv.shape[0]
    wgu, wd = lw["w_gu_e"], lw["w_d_e"]
    grid_spec = pltpu.PrefetchScalarGridSpec(
        num_scalar_prefetch=4,
        grid=(nb,),
        in_specs=[
            pl.BlockSpec((1, SUBLANES, LANES), lambda i, *_: (i, 0, 0)),
            pl.BlockSpec(memory_space=pl.ANY),
            pl.BlockSpec(mod_hi.shape, lambda i, *_: (0, 0)),
            pl.BlockSpec(mod_lo.shape, lambda i, *_: (0, 0)),
            pl.BlockSpec((1, D), lambda i, *_: (0, 0)),
            pl.BlockSpec((1, D, 2 * D_EXPERT), lambda i, elo, ehi, nv, tok: (elo[i], 0, 0)),
            pl.BlockSpec((1, D_EXPERT, D), lambda i, elo, ehi, nv, tok: (elo[i], 0, 0)),
            pl.BlockSpec((1, D, 2 * D_EXPERT), lambda i, elo, ehi, nv, tok: (ehi[i], 0, 0)),
            pl.BlockSpec((1, D_EXPERT, D), lambda i, elo, ehi, nv, tok: (ehi[i], 0, 0)),
        ],
        out_specs=pl.BlockSpec(memory_space=pl.ANY),
        scratch_shapes=[pltpu.VMEM((2, MOE_ROWS, D), F32), pltpu.VMEM((2, MOE_ROWS, D), F32),
                        pltpu.SemaphoreType.DMA((2,)), pltpu.SemaphoreType.DMA((2,))],
    )
    return pl.pallas_call(
        _moe_pairs_kernel,
        grid_spec=grid_spec,
        out_shape=jax.ShapeDtypeStruct((T, D), F32),
        compiler_params=_params(1),
    )(blk_elo, blk_ehi, blk_nv, row_tok, meta, x1, mod_hi, mod_lo, lw["g_norm_ffn"], wgu, wd, wgu, wd)


def _pair_tables():
    lo, hi = [], []
    for g in range(N_GROUPS):
        for a in range(EXPERTS_PER_GROUP):
            for b in range(a + 1, EXPERTS_PER_GROUP):
                lo.append(g * EXPERTS_PER_GROUP + a)
                hi.append(g * EXPERTS_PER_GROUP + b)
    return np.asarray(lo, np.int32), np.asarray(hi, np.int32)


def _moe_prompt(x1, cls, rank, w_lo, w_hi, counts, mod, lw):
    B, S, D = x1.shape
    T = B * S
    nb = T // MOE_ROWS + N_CLASSES
    nblk = (counts + MOE_ROWS - 1) // MOE_ROWS
    blk_end = jnp.cumsum(nblk)
    blk_start = blk_end - nblk
    dest = blk_start[cls] * MOE_ROWS + rank
    tok = jnp.arange(T, dtype=I32)
    row_tok = jnp.zeros((nb * MOE_ROWS,), I32).at[dest].set(tok)
    ids = jnp.arange(nb, dtype=I32)
    used = blk_end[-1]
    last_cls = jnp.searchsorted(blk_end, used - 1, side="right").astype(I32)
    blk_cls = jnp.minimum(jnp.searchsorted(blk_end, ids, side="right").astype(I32), N_CLASSES - 1)
    blk_cls = jnp.where(ids < used, blk_cls, last_cls)
    blk_nv = jnp.where(ids < used, jnp.clip(counts[blk_cls] - (ids - blk_start[blk_cls]) * MOE_ROWS, 0, MOE_ROWS), 0)
    lo_tab, hi_tab = _pair_tables()
    blk_elo = jnp.asarray(lo_tab)[blk_cls]
    blk_ehi = jnp.asarray(hi_tab)[blk_cls]
    valid = (jnp.arange(nb * MOE_ROWS, dtype=I32) % MOE_ROWS) < jnp.repeat(blk_nv, MOE_ROWS)
    meta_rows = [jnp.where(valid, w_lo[row_tok], 0.0), jnp.where(valid, w_hi[row_tok], 0.0),
                 (row_tok // S).astype(F32)]
    meta = jnp.stack([r.reshape(nb, MOE_ROWS) for r in meta_rows], axis=1)
    meta = jnp.pad(meta, ((0, 0), (0, SUBLANES - len(meta_rows)), (0, 0)))
    mod_tab = jnp.pad(mod[:, 3:6, :].reshape(B, 3 * D), ((0, MOD_BATCH_PAD - B), (0, 0)))
    mod_hi, mod_lo = _split_bf16(mod_tab)
    y = _moe_pairs(x1.reshape(T, D), blk_elo, blk_ehi, blk_nv.astype(I32), row_tok, meta, mod_hi, mod_lo, lw)
    return y.reshape(B, S, D)


def _moe_dense_kernel(h_ref, x1_ref, gate_ref, w_ref, sel_ref, wgu_ref, wd_ref, y_ref):
    e = pl.program_id(0)

    @pl.when(e == 0)
    def _():
        y_ref[...] = jnp.zeros_like(y_ref)

    ye = _swiglu(h_ref[...], wgu_ref[0], wd_ref[0])
    y_ref[...] += jnp.where(sel_ref[0] > 0.5, w_ref[0] * ye, 0.0)

    @pl.when(e == pl.num_programs(0) - 1)
    def _():
        y_ref[...] = x1_ref[...] + gate_ref[...] * y_ref[...]


def _moe_dense(h2, x1, gate_rows, w_sel, sel, lw):
    T, D = x1.shape
    per_e = lambda e: (e, 0, 0)
    return pl.pallas_call(
        _moe_dense_kernel,
        grid=(N_EXPERTS,),
        in_specs=[_full((T, D)), _full((T, D)), _full((T, D)),
                  pl.BlockSpec((1, T, 1), per_e), pl.BlockSpec((1, T, 1), per_e),
                  pl.BlockSpec((1, D, 2 * D_EXPERT), per_e), pl.BlockSpec((1, D_EXPERT, D), per_e)],
        out_specs=_full((T, D)),
        out_shape=jax.ShapeDtypeStruct((T, D), F32),
        compiler_params=_params(1),
    )(h2, x1, gate_rows, w_sel, sel, lw["w_gu_e"], lw["w_d_e"])


def _rope_tables(pos, scale):
    inv = 1.0 / (ROPE_BASE ** (jnp.arange(ROPE_HALF, dtype=F32) / ROPE_HALF))
    ang = pos.astype(F32)[:, None] * inv[None, :]
    cos, sin = jnp.cos(ang) * scale, jnp.sin(ang) * scale
    n = pos.shape[0]
    z = lambda w: jnp.zeros((n, w), F32)
    pad = HEAD_PAD - QK_DIM
    c = jnp.concatenate([jnp.full((n, NOPE_DIM), scale, F32), cos, cos, z(pad)], axis=1)
    s_up = jnp.concatenate([z(NOPE_DIM + ROPE_HALF), sin, z(pad)], axis=1)
    s_dn = jnp.concatenate([z(NOPE_DIM), -sin, z(ROPE_HALF + pad)], axis=1)
    return c, s_up, s_dn


def _prep_layer(w, l, chunk_lens):
    D = w["w_in"].shape[1]
    H = MLA_HEADS
    lw = {}
    row = lambda name: w[name][l].reshape(1, -1)
    for name in ("g_norm_mix", "g_q_lat", "g_kv_lat", "g_ln_v", "b_ln_v", "g_out_attn", "g_out_gmlp", "g_norm_ffn"):
        lw[name] = row(name)
    w_in = w["w_in"][l]
    o1, o2, o3 = Q_LORA, Q_LORA + KV_LORA, Q_LORA + KV_LORA + ROPE_DIM
    o4 = o3 + GM_WIDTH
    z32 = jnp.zeros((D, ROPE_DIM), F32)
    wr = w_in[:, o2:o3]
    lw["w_in_p"] = jnp.concatenate([w_in[:, :o2], wr, z32, wr, z32, w_in[:, o3:o4], w_in[:, o4:]], axis=1).astype(BF16)
    assert lw["w_in_p"].shape[1] == PROJ_COLS
    pad = HEAD_PAD - QK_DIM
    lw["w_uq_p"] = jnp.pad(w["w_uq"][l].reshape(Q_LORA, H, QK_DIM), ((0, 0), (0, 0), (0, pad))).reshape(Q_LORA, H * HEAD_PAD).astype(BF16)
    w_ukv = w["w_ukv"][l].reshape(KV_LORA, H, NOPE_DIM + V_DIM)
    lw["w_k_p"] = jnp.pad(w_ukv[:, :, :NOPE_DIM], ((0, 0), (0, 0), (0, HEAD_PAD - NOPE_DIM))).reshape(KV_LORA, H * HEAD_PAD).astype(BF16)
    lw["w_v_p"] = jnp.pad(w_ukv[:, :, NOPE_DIM:], ((0, 0), (0, 0), (0, HEAD_PAD - V_DIM))).reshape(KV_LORA, H * HEAD_PAD).astype(BF16)
    lw["scores_bounded"] = _scores_bounded(w, l)
    lw["g_qnorm_p"] = jnp.pad(w["g_qnorm"][l], (0, pad)).reshape(1, HEAD_PAD)
    lw["g_knorm_p"] = jnp.pad(w["g_knorm"][l], (0, pad)).reshape(1, HEAD_PAD)
    for L in chunk_lens:
        wsp = w["w_spatial"][l][:, :L, :L]
        lw["w_sp_pairs_%d" % L] = wsp.reshape(GM_HEADS // 2, 2 * L, L).astype(BF16)
        lw["b_sp_rows_%d" % L] = jnp.repeat(jnp.transpose(w["b_spatial"][l][:, :L]), GM_HEAD_DIM, axis=1)
    w_out = w["w_out"][l].astype(BF16)
    lw["w_out_a"], lw["w_out_g"] = w_out[:ATT_WIDTH], w_out[ATT_WIDTH:]
    wr_full = jnp.zeros((D, ROUTER_COLS), F32)
    wr_full = wr_full.at[:, :N_GROUPS].set(w["w_router_group"][l])
    wr_full = wr_full.at[:, ROUTER_EXPERT_LANE0:ROUTER_EXPERT_LANE0 + N_EXPERTS].set(w["w_router_expert"][l])
    r_hi, r_lo = _split_bf16(wr_full)
    lw["w_r_a"] = jnp.concatenate([r_hi, r_lo], axis=1)
    lw["w_r_b"] = r_hi
    br = jnp.zeros((1, ROUTER_COLS), F32)
    br = br.at[0, :N_GROUPS].set(w["b_router_group"][l])
    lw["b_r"] = br.at[0, ROUTER_EXPERT_LANE0:ROUTER_EXPERT_LANE0 + N_EXPERTS].set(w["b_router_expert"][l])
    lw["w_gu_e"] = jnp.concatenate([w["w_gate_e"][l], w["w_up_e"][l]], axis=-1).astype(BF16)
    lw["w_d_e"] = w["w_down_e"][l].astype(BF16)
    return lw


def _tiles(seq):
    tm = min(seq, 512)
    return tm, min(seq, 512)


def _route_rows(ri, rf, n_tok):
    cls = ri[:, 0, :].reshape(n_tok)
    rank = ri[:, 1, :].reshape(n_tok)
    return cls, rank, rf[:, 0, :].reshape(n_tok), rf[:, 1, :].reshape(n_tok)


def _layer_prompt(x, mod, lw):
    B, S, D = x.shape
    tm, tq = _tiles(S)
    pos = jnp.arange(S)
    q_tabs = _rope_tables(pos, QK_DIM ** -0.5 * LOG2E)
    k_tabs = _rope_tables(pos, 1.0)
    ckv, krope, q, gm, k, v = _mix_in(x, mod, lw, q_tabs, k_tabs, tm=tm, chunk_len=GM_CHUNK, emit_kv=True, emit_vrows=False)
    att = _attention(lw["scores_bounded"], q, k, v, tq=tq, tk=tq, q_off=0, kv_valid=S)
    x1, ri, rf, cnt = _out_route(x, att, gm, mod, lw, tm=tm, emit_h2=False)
    cls, rank, w_lo, w_hi = _route_rows(ri, rf, B * S)
    counts = cnt[:N_CLASSES, 0].astype(I32)
    y = _moe_prompt(x1, cls, rank, w_lo, w_hi, counts, mod, lw)
    return y, ckv, krope


def _layer_sample(x, mod, past_ckv, past_krope, lw):
    B, S, D = x.shape
    past = past_ckv.shape[1]
    q_tabs = _rope_tables(past + jnp.arange(S), QK_DIM ** -0.5 * LOG2E)
    ckv, krope, q, gm, v_rows = _mix_in(x, mod, lw, q_tabs, q_tabs, tm=S, chunk_len=S, emit_kv=False, emit_vrows=True)
    kv_valid = past + S
    kv_pad = -(-kv_valid // LANES) * LANES
    extra = kv_pad - kv_valid
    ckv_all = jnp.concatenate([past_ckv, ckv, jnp.zeros((B, extra, KV_LORA), F32)], axis=1)
    kr_all = jnp.concatenate([past_krope, krope, jnp.zeros((B, extra, ROPE_DIM), F32)], axis=1)
    kr_slot = jnp.pad(kr_all, ((0, 0), (0, 0), (NOPE_DIM, HEAD_PAD - QK_DIM)))
    k_tabs = _rope_tables(jnp.arange(kv_pad), 1.0)
    k, v = _kv_latent(ckv_all, kr_slot, lw, k_tabs, tr=kv_pad)
    att = _attention(lw["scores_bounded"], q, k, v, tq=S, tk=kv_pad, q_off=past, kv_valid=kv_valid)
    x1, ri, rf, _, h2 = _out_route(x, att, gm, mod, lw, tm=S, emit_h2=True)
    T = B * S
    cls, _, w_lo, w_hi = _route_rows(ri, rf, T)
    lo_tab, hi_tab = _pair_tables()
    e_lo, e_hi = jnp.asarray(lo_tab)[cls], jnp.asarray(hi_tab)[cls]
    eids = jnp.arange(N_EXPERTS, dtype=I32)[:, None]
    is_lo, is_hi = eids == e_lo[None, :], eids == e_hi[None, :]
    w_sel = (jnp.where(is_lo, w_lo[None, :], 0.0) + jnp.where(is_hi, w_hi[None, :], 0.0))[:, :, None]
    sel = (is_lo | is_hi).astype(F32)[:, :, None]
    gate_rows = jnp.repeat(mod[:, 5, :], S, axis=0)
    y = _moe_dense(h2.reshape(T, D), x1.reshape(T, D), gate_rows, w_sel, sel, lw)
    return y.reshape(B, S, D), ckv, krope, v_rows


def kernel(x_prompt, x_sample, cache_ckv, cache_krope, c_prompt, c_sample, w_ada, b_ada, g_norm_mix, w_in, g_q_lat, w_uq, g_kv_lat, w_ukv, g_qnorm, g_knorm, g_ln_v, b_ln_v, w_spatial, b_spatial, g_out_attn, g_out_gmlp, w_out, g_norm_ffn, w_router_group, b_router_group, w_router_expert, b_router_expert, w_gate_e, w_up_e, w_down_e):
    w = dict(w_in=w_in, g_norm_mix=g_norm_mix, g_q_lat=g_q_lat, w_uq=w_uq, g_kv_lat=g_kv_lat, w_ukv=w_ukv,
             g_qnorm=g_qnorm, g_knorm=g_knorm, g_ln_v=g_ln_v, b_ln_v=b_ln_v, w_spatial=w_spatial, b_spatial=b_spatial,
             g_out_attn=g_out_attn, g_out_gmlp=g_out_gmlp, w_out=w_out, g_norm_ffn=g_norm_ffn,
             w_router_group=w_router_group, b_router_group=b_router_group, w_router_expert=w_router_expert,
             b_router_expert=b_router_expert, w_gate_e=w_gate_e, w_up_e=w_up_e, w_down_e=w_down_e)
    depth = w_ada.shape[0]
    Bp, Sp, D = x_prompt.shape
    Bs, Ss, _ = x_sample.shape
    assert Sp % GM_CHUNK == 0 and Ss <= GM_CHUNK and Ss % CHUNK == 0 and Bp <= MOD_BATCH_PAD
    c_all = jnp.concatenate([c_prompt, c_sample], axis=0)
    y_p, y_s = x_prompt, x_sample
    outs = [[] for _ in range(5)]
    for l in range(depth):
        lw = _prep_layer(w, l, (GM_CHUNK, Ss))
        mod = _ada_mod(c_all, w_ada[l], b_ada[l]).reshape(Bp + Bs, 6, D)
        y_p, ckv_p, kr_p = _layer_prompt(y_p, mod[:Bp], lw)
        y_s, ckv_s, kr_s, v_s = _layer_sample(y_s, mod[Bp:], cache_ckv[l], cache_krope[l], lw)
        for lst, val in zip(outs, (ckv_p, kr_p, ckv_s, kr_s, v_s)):
            lst.append(val)
    return (y_p, y_s) + tuple(jnp.stack(lst) for lst in outs)
```

```python
import functools

import numpy as np
import jax
import jax.numpy as jnp
from jax import lax
from jax.experimental import pallas as pl
from jax.experimental.pallas import tpu as pltpu

F32 = jnp.float32
BF16 = jnp.bfloat16
I32 = jnp.int32

CHUNK = 64
CHUNK_SHIFT = 6
EPS = 1e-6
MLA_HEADS = 8
Q_LORA = 256
KV_LORA = 128
NOPE_DIM = 64
ROPE_DIM = 32
ROPE_HALF = ROPE_DIM // 2
V_DIM = 64
QK_DIM = NOPE_DIM + ROPE_DIM
ATT_WIDTH = MLA_HEADS * V_DIM
ROPE_BASE = 10000.0
GM_HEADS = 8
GM_HEAD_DIM = 64
GM_WIDTH = GM_HEADS * GM_HEAD_DIM
GM_CHUNK = 128
N_GROUPS = 4
EXPERTS_PER_GROUP = 8
N_EXPERTS = N_GROUPS * EXPERTS_PER_GROUP
D_EXPERT = 256
PAIRS_PER_GROUP = EXPERTS_PER_GROUP * (EXPERTS_PER_GROUP - 1) // 2
N_CLASSES = N_GROUPS * PAIRS_PER_GROUP

LANES = 128
SUBLANES = 8
HEAD_PAD = LANES
PROJ_COLS = 1536
ROUTER_COLS = LANES
ROUTER_EXPERT_LANE0 = SUBLANES
CLASS_ROWS = LANES
MOE_ROWS = 128
MOD_BATCH_PAD = 16
VMEM_LIMIT = 48 * 1024 * 1024
NEG_BIG = -1e30
LOG2E = 1.4426950408889634
BF16_SLACK = 1.02
SCORE_BOUND = 90.0

assert CHUNK == 1 << CHUNK_SHIFT


def _params(n_axes, vmem=VMEM_LIMIT):
    return pltpu.CompilerParams(dimension_semantics=("arbitrary",) * n_axes, vmem_limit_bytes=vmem)


def _full(shape):
    nd = len(shape)
    return pl.BlockSpec(shape, lambda *_: (0,) * nd)


def _split_bf16(x):
    hi = x.astype(BF16)
    lo = (x - hi.astype(F32)).astype(BF16)
    return hi, lo


def _dot(a, b):
    return jnp.dot(a, b, preferred_element_type=F32)


def _ada_kernel(c_ref, w_ref, b_ref, o_ref):
    a_hi, a_lo = _split_bf16(jax.nn.silu(c_ref[...]))
    w_hi, w_lo = _split_bf16(w_ref[...])
    o_ref[...] = _dot(a_hi, w_hi) + _dot(a_lo, w_hi) + _dot(a_hi, w_lo) + b_ref[...]


def _ada_mod(c, w_ada, b_ada):
    n, d = c.shape
    cols = w_ada.shape[1]
    tn = 1536
    return pl.pallas_call(
        _ada_kernel,
        grid=(cols // tn,),
        in_specs=[_full((n, d)), pl.BlockSpec((d, tn), lambda j: (0, j)), pl.BlockSpec((1, tn), lambda j: (0, j))],
        out_specs=pl.BlockSpec((n, tn), lambda j: (0, j)),
        out_shape=jax.ShapeDtypeStruct((n, cols), F32),
        compiler_params=_params(1),
    )(c, w_ada, b_ada.reshape(1, cols))


def _rms(x, g):
    return x * lax.rsqrt(jnp.mean(x * x, axis=-1, keepdims=True) + EPS) * g


def _head_norm_rope(xh, g, c, s_up, s_dn):
    ms = jnp.sum(xh * xh, axis=-1, keepdims=True) * (1.0 / QK_DIM)
    xh = xh * lax.rsqrt(ms + EPS) * g
    return xh * c + pltpu.roll(xh, ROPE_HALF, 1) * s_up + pltpu.roll(xh, HEAD_PAD - ROPE_HALF, 1) * s_dn


def _keys_values(ckv, kr_slot, gkv, wk, wv, gkn, c, s_up, s_dn, k_ref, v_ref):
    cb = _rms(ckv, gkv).astype(BF16)
    kall = _dot(cb, wk)
    vall = _dot(cb, wv)
    lane = lax.broadcasted_iota(I32, (1, LANES), 1)
    one_col = jnp.where(lane == V_DIM, 1.0, 0.0)
    for h in range(MLA_HEADS):
        kh = kall[:, h * HEAD_PAD:(h + 1) * HEAD_PAD] + kr_slot
        k_ref[0, h] = _head_norm_rope(kh, gkn, c, s_up, s_dn).astype(BF16)
        v_ref[0, h] = (vall[:, h * HEAD_PAD:(h + 1) * HEAD_PAD] + one_col).astype(BF16)


def _mix_in_kernel(x_ref, mod_ref, gmix_ref, win_ref, gql_ref, wuq_ref, gkv_ref, wk_ref, wv_ref, gqn_ref, gkn_ref,
                   cq_ref, squ_ref, sqd_ref, ck_ref, sku_ref, skd_ref, glnv_ref, blnv_ref, wsp_ref, bsp_ref,
                   ggm_ref, *rest, chunk_len, emit_kv, emit_vrows):
    outs = list(rest[:-1])
    mixed_scr = rest[-1]
    ckv_ref, kr_ref, q_ref, gm_ref = outs[:4]
    outs = outs[4:]
    if emit_kv:
        k_ref, v_ref = outs[:2]
        outs = outs[2:]
    if emit_vrows:
        vrows_ref = outs[0]

    x = x_ref[0]
    tm = x.shape[0]
    shift, scale = mod_ref[0, 0:1, :], mod_ref[0, 1:2, :]
    h = _rms(x, gmix_ref[...]) * (1.0 + scale) + shift
    proj = _dot(h.astype(BF16), win_ref[...])

    q_lat = proj[:, 0:Q_LORA]
    ckv = proj[:, Q_LORA:Q_LORA + KV_LORA]
    kr_blk = proj[:, Q_LORA + KV_LORA:Q_LORA + KV_LORA + LANES]
    ckv_ref[0] = ckv
    kr_ref[0] = kr_blk[:, 0:ROPE_DIM]

    q = _dot(_rms(q_lat, gql_ref[...]).astype(BF16), wuq_ref[...])
    cq, squ, sqd = cq_ref[...], squ_ref[...], sqd_ref[...]
    for hd in range(MLA_HEADS):
        qh = q[:, hd * HEAD_PAD:(hd + 1) * HEAD_PAD]
        q_ref[0, hd] = _head_norm_rope(qh, gqn_ref[...], cq, squ, sqd).astype(BF16)

    if emit_kv:
        lane = lax.broadcasted_iota(I32, (1, LANES), 1)
        kr_slot = jnp.where(lane >= NOPE_DIM, kr_blk, 0.0)
        _keys_values(ckv, kr_slot, gkv_ref[...], wk_ref[...], wv_ref[...], gkn_ref[...],
                     ck_ref[...], sku_ref[...], skd_ref[...], k_ref, v_ref)

    g_u = proj[:, 512:512 + GM_WIDTH]
    g_v = proj[:, 1024:1024 + GM_WIDTH]
    u = jax.nn.gelu(g_u)
    gv = jax.nn.gelu(g_v)
    mu = jnp.mean(gv, axis=-1, keepdims=True)
    xc = gv - mu
    var = jnp.mean(xc * xc, axis=-1, keepdims=True)
    v_rows = xc * lax.rsqrt(var + EPS) * glnv_ref[...] + blnv_ref[...]
    if emit_vrows:
        vrows_ref[0] = v_rows
    vb = v_rows.astype(BF16)

    L = chunk_len
    t = lax.broadcasted_iota(I32, (2 * L, L), 0)
    s = lax.broadcasted_iota(I32, (2 * L, L), 1)
    t = jnp.where(t >= L, t - L, t)
    allowed = (s >> CHUNK_SHIFT) <= (t >> CHUNK_SHIFT)
    lane = lax.broadcasted_iota(I32, (1, LANES), 1)
    first_head = lane < GM_HEAD_DIM
    for p in range(GM_HEADS // 2):
        w_pair = jnp.where(allowed, wsp_ref[p], jnp.zeros((), BF16))
        for c in range(tm // L):
            vp = vb[c * L:(c + 1) * L, p * LANES:(p + 1) * LANES]
            r = _dot(w_pair, vp)
            mixed = jnp.where(first_head, r[:L], r[L:])
            mixed_scr[c * L:(c + 1) * L, p * LANES:(p + 1) * LANES] = mixed + bsp_ref[:, p * LANES:(p + 1) * LANES]
    gm = u * mixed_scr[...]
    gm_ref[0] = _rms(gm, ggm_ref[...]).astype(BF16)


def _mix_in(x, mod, lw, q_tabs, k_tabs, *, tm, chunk_len, emit_kv, emit_vrows):
    B, S, D = x.shape
    nt = S // tm
    H = MLA_HEADS
    row = lambda b, i: (b, i, 0)
    tab = pl.BlockSpec((tm, LANES), lambda b, i: (i, 0))
    in_specs = [
        pl.BlockSpec((1, tm, D), row),
        pl.BlockSpec((1, 6, D), lambda b, i: (b, 0, 0)),
        _full((1, D)), _full((D, PROJ_COLS)), _full((1, Q_LORA)), _full((Q_LORA, H * HEAD_PAD)),
        _full((1, KV_LORA)), _full((KV_LORA, H * HEAD_PAD)), _full((KV_LORA, H * HEAD_PAD)),
        _full((1, LANES)), _full((1, LANES)),
        tab, tab, tab, tab, tab, tab,
        _full((1, GM_WIDTH)), _full((1, GM_WIDTH)),
        _full((GM_HEADS // 2, 2 * chunk_len, chunk_len)), _full((chunk_len, GM_WIDTH)), _full((1, GM_WIDTH)),
    ]
    out_shape = [
        jax.ShapeDtypeStruct((B, S, KV_LORA), F32),
        jax.ShapeDtypeStruct((B, S, ROPE_DIM), F32),
        jax.ShapeDtypeStruct((B, H, S, HEAD_PAD), BF16),
        jax.ShapeDtypeStruct((B, S, GM_WIDTH), BF16),
    ]
    head_blk = pl.BlockSpec((1, H, tm, HEAD_PAD), lambda b, i: (b, 0, i, 0))
    out_specs = [
        pl.BlockSpec((1, tm, KV_LORA), row),
        pl.BlockSpec((1, tm, ROPE_DIM), row),
        head_blk,
        pl.BlockSpec((1, tm, GM_WIDTH), row),
    ]
    if emit_kv:
        out_shape += [jax.ShapeDtypeStruct((B, H, S, HEAD_PAD), BF16), jax.ShapeDtypeStruct((B, H, S, HEAD_PAD), BF16)]
        out_specs += [head_blk, head_blk]
    if emit_vrows:
        out_shape += [jax.ShapeDtypeStruct((B, S, GM_WIDTH), F32)]
        out_specs += [pl.BlockSpec((1, tm, GM_WIDTH), row)]
    kern = functools.partial(_mix_in_kernel, chunk_len=chunk_len, emit_kv=emit_kv, emit_vrows=emit_vrows)
    return pl.pallas_call(
        kern,
        grid=(B, nt),
        in_specs=in_specs,
        out_specs=out_specs,
        out_shape=out_shape,
        scratch_shapes=[pltpu.VMEM((tm, GM_WIDTH), F32)],
        compiler_params=_params(2),
    )(x, mod, lw["g_norm_mix"], lw["w_in_p"], lw["g_q_lat"], lw["w_uq_p"], lw["g_kv_lat"], lw["w_k_p"], lw["w_v_p"],
      lw["g_qnorm_p"], lw["g_knorm_p"], *q_tabs, *k_tabs, lw["g_ln_v"], lw["b_ln_v"],
      lw["w_sp_pairs_%d" % chunk_len], lw["b_sp_rows_%d" % chunk_len], lw["g_out_gmlp"])


def _kv_latent_kernel(ckv_ref, kr_ref, gkv_ref, wk_ref, wv_ref, gkn_ref, c_ref, su_ref, sd_ref, k_ref, v_ref):
    _keys_values(ckv_ref[0], kr_ref[0], gkv_ref[...], wk_ref[...], wv_ref[...], gkn_ref[...],
                 c_ref[...], su_ref[...], sd_ref[...], k_ref, v_ref)


def _kv_latent(ckv_all, kr_slot_all, lw, k_tabs, *, tr):
    B, K, _ = ckv_all.shape
    H = MLA_HEADS
    row = lambda b, i: (b, i, 0)
    tab = pl.BlockSpec((tr, LANES), lambda b, i: (i, 0))
    return pl.pallas_call(
        _kv_latent_kernel,
        grid=(B, K // tr),
        in_specs=[pl.BlockSpec((1, tr, KV_LORA), row), pl.BlockSpec((1, tr, LANES), row),
                  _full((1, KV_LORA)), _full((KV_LORA, H * HEAD_PAD)), _full((KV_LORA, H * HEAD_PAD)), _full((1, LANES)),
                  tab, tab, tab],
        out_specs=[pl.BlockSpec((1, H, tr, HEAD_PAD), lambda b, i: (b, 0, i, 0))] * 2,
        out_shape=[jax.ShapeDtypeStruct((B, H, K, HEAD_PAD), BF16)] * 2,
        compiler_params=_params(2),
    )(ckv_all, kr_slot_all, lw["g_kv_lat"], lw["w_k_p"], lw["w_v_p"], lw["g_knorm_p"], *k_tabs)


def _attn_kernel(bounded_ref, q_ref, k_ref, v_ref, o_ref, *, tq, tk, n_q, q_off, kv_valid):
    i = pl.program_id(2) if n_q > 1 else 0
    q_first = q_off + i * tq
    vis_first = jnp.minimum(((q_first >> CHUNK_SHIFT) + 1) << CHUNK_SHIFT, kv_valid)
    vis_last = jnp.minimum((((q_first + tq - 1) >> CHUNK_SHIFT) + 1) << CHUNK_SHIFT, kv_valid)
    n_unmasked = vis_first // tk
    n_total = (vis_last + tk - 1) // tk

    q_pos = q_first + lax.broadcasted_iota(I32, (tq, 1), 0)
    limit = jnp.minimum(((q_pos >> CHUNK_SHIFT) + 1) << CHUNK_SHIFT, kv_valid)
    lane = lax.broadcasted_iota(I32, (1, LANES), 1)

    def scores(j, start, masked):
        s = lax.dot_general(q_ref[0, j], k_ref[0, j, pl.ds(start, tk), :], (((1,), (1,)), ((), ())),
                            preferred_element_type=F32)
        if masked:
            k_pos = start + lax.broadcasted_iota(I32, (1, tk), 1)
            s = jnp.where(k_pos < limit, s, NEG_BIG)
        return s

    def plain_block(kb, carry, masked):
        start = pl.multiple_of(kb * tk, tk)
        new = []
        for j in range(2):
            p = jnp.exp2(scores(j, start, masked)).astype(BF16)
            new.append(carry[j] + _dot(p, v_ref[0, j, pl.ds(start, tk), :]))
        return tuple(new)

    def online_block(kb, carry, masked):
        start = pl.multiple_of(kb * tk, tk)
        new = []
        for j in range(2):
            m, acc = carry[2 * j:2 * j + 2]
            s = scores(j, start, masked)
            m_new = jnp.maximum(m, jnp.max(s, axis=-1, keepdims=True))
            p = jnp.exp2(s - m_new).astype(BF16)
            acc = jnp.exp2(m - m_new) * acc + _dot(p, v_ref[0, j, pl.ds(start, tk), :])
            new += [m_new, acc]
        return tuple(new)

    def run(block, init):
        carry = lax.fori_loop(0, n_unmasked, lambda kb, c: block(kb, c, False), tuple(init))
        return lax.fori_loop(n_unmasked, n_total, lambda kb, c: block(kb, c, True), carry)

    def finish(acc0, acc1):
        outs = []
        for acc in (acc0, acc1):
            denom = jnp.sum(jnp.where(lane == V_DIM, acc, 0.0), axis=-1, keepdims=True)
            outs.append(acc / denom)
        o_ref[0] = jnp.where(lane < V_DIM, outs[0], pltpu.roll(outs[1], V_DIM, 1)).astype(BF16)

    zeros = jnp.zeros((tq, LANES), F32)

    @pl.when(bounded_ref[0] == 1)
    def _():
        finish(*run(plain_block, [zeros, zeros]))

    @pl.when(bounded_ref[0] != 1)
    def _():
        m0 = jnp.full((tq, 1), NEG_BIG, F32)
        c = run(online_block, [m0, zeros, m0, zeros])
        finish(c[1], c[3])


def _attention(bounded, q, k, v, *, tq, tk, q_off, kv_valid):
    B, H, Sq, _ = q.shape
    Sk = k.shape[2]
    kern = functools.partial(_attn_kernel, tq=tq, tk=tk, n_q=Sq // tq, q_off=q_off, kv_valid=kv_valid)
    grid_spec = pltpu.PrefetchScalarGridSpec(
        num_scalar_prefetch=1,
        grid=(B, H // 2, Sq // tq),
        in_specs=[pl.BlockSpec((1, 2, tq, HEAD_PAD), lambda b, hp, i, f: (b, hp, i, 0)),
                  pl.BlockSpec((1, 2, Sk, HEAD_PAD), lambda b, hp, i, f: (b, hp, 0, 0)),
                  pl.BlockSpec((1, 2, Sk, HEAD_PAD), lambda b, hp, i, f: (b, hp, 0, 0))],
        out_specs=pl.BlockSpec((1, tq, LANES), lambda b, hp, i, f: (b, i, hp)),
    )
    return pl.pallas_call(
        kern,
        grid_spec=grid_spec,
        out_shape=jax.ShapeDtypeStruct((B, Sq, ATT_WIDTH), BF16),
        compiler_params=_params(3),
    )(bounded, q, k, v)


def _scores_bounded(w, l):
    gq = jnp.max(jnp.abs(w["g_qnorm"][l]))
    gk = jnp.max(jnp.abs(w["g_knorm"][l]))
    bound = (QK_DIM ** 0.5) * LOG2E * BF16_SLACK * gq * gk
    return (bound <= SCORE_BOUND).astype(I32).reshape(1)


def _pack_bf16_pairs(h):
    n = h.shape[1] // 2
    hi = pltpu.bitcast(h[:, :n].astype(BF16).astype(F32), jnp.uint32)
    lo = pltpu.bitcast(h[:, n:].astype(BF16).astype(F32), jnp.uint32)
    return pltpu.bitcast(hi | (lo >> 16), F32)


def _unpack_bf16_pairs(words):
    w = pltpu.bitcast(words, jnp.uint32)
    hi = pltpu.bitcast(w & jnp.uint32(0xFFFF0000), F32)
    lo = pltpu.bitcast(w << 16, F32)
    return jnp.concatenate([hi, lo], axis=1).astype(BF16)


def _out_route_kernel(x_ref, att_ref, gm_ref, mod_ref, goa_ref, woa_ref, wog_ref, gffn_ref, wra_ref, wrb_ref, br_ref,
                      *rest, pack_rows):
    carry_scr = rest[-1]
    if pack_rows:
        rec_ref, ri_ref, cnt_ref = rest[:3]
    else:
        x1_ref, h2_ref, ri_ref, rf_ref, cnt_ref = rest[:5]
    first_step = (pl.program_id(0) == 0) & (pl.program_id(1) == 0)

    @pl.when(first_step)
    def _():
        carry_scr[...] = jnp.zeros_like(carry_scr)

    x = x_ref[0]
    tm, D = x.shape
    gate_a = mod_ref[0, 2:3, :]
    shift_m, scale_m = mod_ref[0, 3:4, :], mod_ref[0, 4:5, :]
    att_n = _rms(att_ref[0].astype(F32), goa_ref[...]).astype(BF16)
    mix = _dot(att_n, woa_ref[...]) + _dot(gm_ref[0], wog_ref[...])
    x1 = x + gate_a * mix
    h2 = _rms(x1, gffn_ref[...]) * (1.0 + scale_m) + shift_m
    if pack_rows:
        rec_ref[0, :, 0:D] = x1
        rec_ref[0, :, D:D + D // 2] = _pack_bf16_pairs(h2)
    else:
        x1_ref[0] = x1
        h2_ref[0] = h2.astype(BF16)

    h_hi, h_lo = _split_bf16(h2)
    la = _dot(h_hi, wra_ref[...])
    logits = la[:, :ROUTER_COLS] + la[:, ROUTER_COLS:] + _dot(h_lo, wrb_ref[...]) + br_ref[...]
    lt = logits.T

    g = [lt[r:r + 1] for r in range(N_GROUPS)]
    gmax = jnp.maximum(jnp.maximum(g[0], g[1]), jnp.maximum(g[2], g[3]))
    gsum = sum(jnp.exp(gr - gmax) for gr in g)
    g_prob = 1.0 / gsum
    g_idx = jnp.where(g[0] == gmax, 0.0, jnp.where(g[1] == gmax, 1.0, jnp.where(g[2] == gmax, 2.0, 3.0)))

    e0 = ROUTER_EXPERT_LANE0
    grp = [lt[e0 + EXPERTS_PER_GROUP * r:e0 + EXPERTS_PER_GROUP * (r + 1)] for r in range(N_GROUPS)]
    sel = jnp.where(g_idx == 0.0, grp[0], jnp.where(g_idx == 1.0, grp[1], jnp.where(g_idx == 2.0, grp[2], grp[3])))
    sub = lax.broadcasted_iota(I32, (EXPERTS_PER_GROUP, tm), 0).astype(F32)
    m1 = jnp.max(sel, axis=0, keepdims=True)
    i1 = jnp.min(jnp.where(sel == m1, sub, float(EXPERTS_PER_GROUP)), axis=0, keepdims=True)
    sel2 = jnp.where(sub == i1, -jnp.inf, sel)
    m2 = jnp.max(sel2, axis=0, keepdims=True)
    i2 = jnp.min(jnp.where(sel2 == m2, sub, float(EXPERTS_PER_GROUP)), axis=0, keepdims=True)
    d = jnp.exp(m2 - m1)
    w1 = g_prob / (1.0 + d)
    w2 = g_prob * d / (1.0 + d)
    first_lower = i1 < i2
    lo = jnp.minimum(i1, i2)
    hi = jnp.maximum(i1, i2)
    w_lo = jnp.where(first_lower, w1, w2)
    w_hi = jnp.where(first_lower, w2, w1)
    pair = lo * EXPERTS_PER_GROUP - lo * (lo + 1.0) * 0.5 + hi - lo - 1.0
    cls = g_idx * PAIRS_PER_GROUP + pair

    crow = lax.broadcasted_iota(I32, (CLASS_ROWS, tm), 0).astype(F32)
    onehot = jnp.where(crow == cls, 1.0, 0.0)
    ta = lax.broadcasted_iota(I32, (tm, tm), 0)
    tb = lax.broadcasted_iota(I32, (tm, tm), 1)
    earlier = jnp.where(ta < tb, 1.0, 0.0).astype(BF16)
    before = _dot(onehot.astype(BF16), earlier)
    carry = carry_scr[...]
    rank = jnp.sum(onehot * (before + carry[:, 0:1]), axis=0, keepdims=True)
    carry = carry + jnp.sum(onehot, axis=1, keepdims=True)
    carry_scr[...] = carry
    cnt_ref[...] = carry

    ri_ref[...] = jnp.zeros_like(ri_ref)
    ri_ref[0, 0:1, :] = cls.astype(I32)
    ri_ref[0, 1:2, :] = rank.astype(I32)
    if pack_rows:
        mrow = lax.broadcasted_iota(I32, (LANES, tm), 0)
        batch = pl.program_id(0).astype(F32)
        meta_t = jnp.where(mrow == 0, w_lo, jnp.where(mrow == 1, w_hi, jnp.where(mrow == 2, batch, 0.0)))
        rec_ref[0, :, D + D // 2:] = meta_t.T
    else:
        rf_ref[...] = jnp.zeros_like(rf_ref)
        rf_ref[0, 0:1, :] = w_lo
        rf_ref[0, 1:2, :] = w_hi


def _out_route(x, att, gm, mod, lw, *, tm, pack_rows):
    B, S, D = x.shape
    nt = S // tm
    row = lambda b, i: (b, i, 0)
    tile = lambda b, i: (b * nt + i, 0, 0)
    route_i = (jax.ShapeDtypeStruct((B * nt, SUBLANES, tm), I32), pl.BlockSpec((1, SUBLANES, tm), tile))
    route_f = (jax.ShapeDtypeStruct((B * nt, SUBLANES, tm), F32), pl.BlockSpec((1, SUBLANES, tm), tile))
    counts = (jax.ShapeDtypeStruct((CLASS_ROWS, LANES), F32), _full((CLASS_ROWS, LANES)))
    if pack_rows:
        rec_w = D + D // 2 + LANES
        outs = [(jax.ShapeDtypeStruct((B, S, rec_w), F32), pl.BlockSpec((1, tm, rec_w), row)), route_i, counts]
    else:
        outs = [(jax.ShapeDtypeStruct((B, S, D), F32), pl.BlockSpec((1, tm, D), row)),
                (jax.ShapeDtypeStruct((B, S, D), BF16), pl.BlockSpec((1, tm, D), row)), route_i, route_f, counts]
    out_shape = [o[0] for o in outs]
    out_specs = [o[1] for o in outs]
    return pl.pallas_call(
        functools.partial(_out_route_kernel, pack_rows=pack_rows),
        grid=(B, nt),
        in_specs=[pl.BlockSpec((1, tm, D), row), pl.BlockSpec((1, tm, ATT_WIDTH), row), pl.BlockSpec((1, tm, GM_WIDTH), row),
                  pl.BlockSpec((1, 6, D), lambda b, i: (b, 0, 0)),
                  _full((1, ATT_WIDTH)), _full((ATT_WIDTH, D)), _full((GM_WIDTH, D)), _full((1, D)),
                  _full((D, 2 * ROUTER_COLS)), _full((D, ROUTER_COLS)), _full((1, ROUTER_COLS))],
        out_specs=out_specs,
        out_shape=out_shape,
        scratch_shapes=[pltpu.VMEM((CLASS_ROWS, LANES), F32)],
        compiler_params=_params(2),
    )(x, att, gm, mod, lw["g_out_attn"], lw["w_out_a"], lw["w_out_g"], lw["g_norm_ffn"],
      lw["w_r_a"], lw["w_r_b"], lw["b_r"])


def _swiglu(hb, wgu, wd):
    hid = _dot(hb, wgu)
    act = jax.nn.silu(hid[:, :D_EXPERT]) * hid[:, D_EXPERT:]
    return _dot(act.astype(BF16), wd)


def _moe_pairs_kernel(elo_ref, ehi_ref, nv_ref, tok_ref, rec_hbm, gate_ref,
                      wgu_lo_ref, wd_lo_ref, wgu_hi_ref, wd_hi_ref, y_hbm, xbuf, ybuf, gsem, ssem):
    i = pl.program_id(0)
    n = pl.num_programs(0)
    slot = i % 2
    D = ybuf.shape[-1]

    def gather_start(t, s, g, u):
        pltpu.make_async_copy(rec_hbm.at[pl.ds(t, 1)], xbuf.at[s, g, pl.ds(u, 1)], gsem.at[s]).start()

    def scatter_start(t, s, g, u):
        pltpu.make_async_copy(ybuf.at[s, g, pl.ds(u, 1)], y_hbm.at[pl.ds(t, 1)], ssem.at[s]).start()

    def for_rows(blk, fn):
        nv = nv_ref[blk]
        base = blk * MOE_ROWS
        n_groups = nv // SUBLANES

        def group(g, c):
            for u in range(SUBLANES):
                fn(tok_ref[base + g * SUBLANES + u], g, u)
            return c

        def single(r, c):
            fn(tok_ref[base + r], n_groups, r - n_groups * SUBLANES)
            return c

        lax.fori_loop(0, n_groups, group, 0)
        lax.fori_loop(n_groups * SUBLANES, nv, single, 0)

    def wait_rows(blk, buf, sem):
        nv = nv_ref[blk]
        n_groups = nv // SUBLANES

        @pl.when(n_groups > 0)
        def _():
            rows = buf.at[pl.ds(0, n_groups)]
            pltpu.make_async_copy(rows, rows, sem).wait()

        def single(r, c):
            row = buf.at[0, pl.ds(0, 1)]
            pltpu.make_async_copy(row, row, sem).wait()
            return c

        lax.fori_loop(n_groups * SUBLANES, nv, single, 0)

    def gather_wait(blk, s):
        wait_rows(blk, xbuf.at[s], gsem.at[s])

    def scatter_wait(blk, s):
        wait_rows(blk, ybuf.at[s], ssem.at[s])

    @pl.when(i == 0)
    def _():
        xbuf[...] = jnp.zeros_like(xbuf)
        for_rows(0, lambda t, g, u: gather_start(t, 0, g, u))

    @pl.when(i + 1 < n)
    def _():
        for_rows(i + 1, lambda t, g, u: gather_start(t, 1 - slot, g, u))

    gather_wait(i, slot)

    @pl.when(i >= 2)
    def _():
        scatter_wait(i - 2, slot)

    @pl.when(nv_ref[i] > 0)
    def _():
        x = xbuf[slot, :, :, 0:D].reshape(MOE_ROWS, D)
        hb = _unpack_bf16_pairs(xbuf[slot, :, :, D:D + D // 2].reshape(MOE_ROWS, D // 2))
        meta = xbuf[slot, :, :, D + D // 2:].reshape(MOE_ROWS, LANES)
        w_lo, w_hi, bidx = meta[:, 0:1], meta[:, 1:2], meta[:, 2:3]
        blane = lax.broadcasted_iota(I32, (1, 2 * MOD_BATCH_PAD), 1)
        blane = jnp.where(blane >= MOD_BATCH_PAD, blane - MOD_BATCH_PAD, blane).astype(F32)
        onehot = jnp.where(bidx == blane, 1.0, 0.0).astype(BF16)
        gate_m = _dot(onehot, gate_ref[...])
        moe = w_lo * _swiglu(hb, wgu_lo_ref[0], wd_lo_ref[0]) + w_hi * _swiglu(hb, wgu_hi_ref[0], wd_hi_ref[0])
        ybuf[slot] = (x + gate_m * moe).reshape(MOE_ROWS // SUBLANES, SUBLANES, D)
        for_rows(i, lambda t, g, u: scatter_start(t, slot, g, u))

    @pl.when(i == n - 1)
    def _():
        @pl.when(i >= 1)
        def _():
            scatter_wait(i - 1, 1 - slot)
        scatter_wait(i, slot)


def _moe_pairs(rec, blk_elo, blk_ehi, blk_nv, row_tok, gate_tab, lw):
    T, rec_w = rec.shape
    D = gate_tab.shape[1]
    nb = blk_nv.shape[0]
    wgu, wd = lw["w_gu_e"], lw["w_d_e"]
    grid_spec = pltpu.PrefetchScalarGridSpec(
        num_scalar_prefetch=4,
        grid=(nb,),
        in_specs=[
            pl.BlockSpec(memory_space=pl.ANY),
            pl.BlockSpec(gate_tab.shape, lambda i, *_: (0, 0)),
            pl.BlockSpec((1, D, 2 * D_EXPERT), lambda i, elo, ehi, nv, tok: (elo[i], 0, 0)),
            pl.BlockSpec((1, D_EXPERT, D), lambda i, elo, ehi, nv, tok: (elo[i], 0, 0)),
            pl.BlockSpec((1, D, 2 * D_EXPERT), lambda i, elo, ehi, nv, tok: (ehi[i], 0, 0)),
            pl.BlockSpec((1, D_EXPERT, D), lambda i, elo, ehi, nv, tok: (ehi[i], 0, 0)),
        ],
        out_specs=pl.BlockSpec(memory_space=pl.ANY),
        scratch_shapes=[pltpu.VMEM((2, MOE_ROWS // SUBLANES, SUBLANES, rec_w), F32),
                        pltpu.VMEM((2, MOE_ROWS // SUBLANES, SUBLANES, D), F32),
                        pltpu.SemaphoreType.DMA((2,)), pltpu.SemaphoreType.DMA((2,))],
    )
    return pl.pallas_call(
        _moe_pairs_kernel,
        grid_spec=grid_spec,
        out_shape=jax.ShapeDtypeStruct((T, D), F32),
        compiler_params=_params(1),
    )(blk_elo, blk_ehi, blk_nv, row_tok, rec, gate_tab, wgu, wd, wgu, wd)


def _pair_tables():
    lo, hi = [], []
    for g in range(N_GROUPS):
        for a in range(EXPERTS_PER_GROUP):
            for b in range(a + 1, EXPERTS_PER_GROUP):
                lo.append(g * EXPERTS_PER_GROUP + a)
                hi.append(g * EXPERTS_PER_GROUP + b)
    return np.asarray(lo, np.int32), np.asarray(hi, np.int32)


def _moe_prompt(rec, cls, rank, counts, mod, lw):
    B, S, rec_w = rec.shape
    D = mod.shape[-1]
    T = B * S
    nb = T // MOE_ROWS + N_CLASSES
    nblk = (counts + MOE_ROWS - 1) // MOE_ROWS
    blk_end = jnp.cumsum(nblk)
    blk_start = blk_end - nblk
    dest = blk_start[cls] * MOE_ROWS + rank
    tok = jnp.arange(T, dtype=I32)
    row_tok = jnp.zeros((nb * MOE_ROWS,), I32).at[dest].set(tok)
    ids = jnp.arange(nb, dtype=I32)
    used = blk_end[-1]
    last_cls = jnp.searchsorted(blk_end, used - 1, side="right").astype(I32)
    blk_cls = jnp.minimum(jnp.searchsorted(blk_end, ids, side="right").astype(I32), N_CLASSES - 1)
    blk_cls = jnp.where(ids < used, blk_cls, last_cls)
    blk_nv = jnp.where(ids < used, jnp.clip(counts[blk_cls] - (ids - blk_start[blk_cls]) * MOE_ROWS, 0, MOE_ROWS), 0)
    lo_tab, hi_tab = _pair_tables()
    blk_elo = jnp.asarray(lo_tab)[blk_cls]
    blk_ehi = jnp.asarray(hi_tab)[blk_cls]
    gate_hi, gate_lo = _split_bf16(jnp.pad(mod[:, 5, :], ((0, MOD_BATCH_PAD - B), (0, 0))))
    gate_tab = jnp.concatenate([gate_hi, gate_lo], axis=0)
    y = _moe_pairs(rec.reshape(T, rec_w), blk_elo, blk_ehi, blk_nv.astype(I32), row_tok, gate_tab, lw)
    return y.reshape(B, S, D)


def _moe_dense_kernel(h_ref, x1_ref, gate_ref, w_ref, sel_ref, wgu_ref, wd_ref, y_ref):
    e = pl.program_id(0)

    @pl.when(e == 0)
    def _():
        y_ref[...] = jnp.zeros_like(y_ref)

    ye = _swiglu(h_ref[...], wgu_ref[0], wd_ref[0])
    y_ref[...] += jnp.where(sel_ref[0] > 0.5, w_ref[0] * ye, 0.0)

    @pl.when(e == pl.num_programs(0) - 1)
    def _():
        y_ref[...] = x1_ref[...] + gate_ref[...] * y_ref[...]


def _moe_dense(h2, x1, gate_rows, w_sel, sel, lw):
    T, D = x1.shape
    per_e = lambda e: (e, 0, 0)
    return pl.pallas_call(
        _moe_dense_kernel,
        grid=(N_EXPERTS,),
        in_specs=[_full((T, D)), _full((T, D)), _full((T, D)),
                  pl.BlockSpec((1, T, 1), per_e), pl.BlockSpec((1, T, 1), per_e),
                  pl.BlockSpec((1, D, 2 * D_EXPERT), per_e), pl.BlockSpec((1, D_EXPERT, D), per_e)],
        out_specs=_full((T, D)),
        out_shape=jax.ShapeDtypeStruct((T, D), F32),
        compiler_params=_params(1),
    )(h2, x1, gate_rows, w_sel, sel, lw["w_gu_e"], lw["w_d_e"])


def _rope_tables(pos, scale):
    inv = 1.0 / (ROPE_BASE ** (jnp.arange(ROPE_HALF, dtype=F32) / ROPE_HALF))
    ang = pos.astype(F32)[:, None] * inv[None, :]
    cos, sin = jnp.cos(ang) * scale, jnp.sin(ang) * scale
    n = pos.shape[0]
    z = lambda w: jnp.zeros((n, w), F32)
    pad = HEAD_PAD - QK_DIM
    c = jnp.concatenate([jnp.full((n, NOPE_DIM), scale, F32), cos, cos, z(pad)], axis=1)
    s_up = jnp.concatenate([z(NOPE_DIM + ROPE_HALF), sin, z(pad)], axis=1)
    s_dn = jnp.concatenate([z(NOPE_DIM), -sin, z(ROPE_HALF + pad)], axis=1)
    return c, s_up, s_dn


def _prep_layer(w, l, chunk_lens):
    D = w["w_in"].shape[1]
    H = MLA_HEADS
    lw = {}
    row = lambda name: w[name][l].reshape(1, -1)
    for name in ("g_norm_mix", "g_q_lat", "g_kv_lat", "g_ln_v", "b_ln_v", "g_out_attn", "g_out_gmlp", "g_norm_ffn"):
        lw[name] = row(name)
    w_in = w["w_in"][l]
    o1, o2, o3 = Q_LORA, Q_LORA + KV_LORA, Q_LORA + KV_LORA + ROPE_DIM
    o4 = o3 + GM_WIDTH
    z32 = jnp.zeros((D, ROPE_DIM), F32)
    wr = w_in[:, o2:o3]
    lw["w_in_p"] = jnp.concatenate([w_in[:, :o2], wr, z32, wr, z32, w_in[:, o3:o4], w_in[:, o4:]], axis=1).astype(BF16)
    assert lw["w_in_p"].shape[1] == PROJ_COLS
    pad = HEAD_PAD - QK_DIM
    lw["w_uq_p"] = jnp.pad(w["w_uq"][l].reshape(Q_LORA, H, QK_DIM), ((0, 0), (0, 0), (0, pad))).reshape(Q_LORA, H * HEAD_PAD).astype(BF16)
    w_ukv = w["w_ukv"][l].reshape(KV_LORA, H, NOPE_DIM + V_DIM)
    lw["w_k_p"] = jnp.pad(w_ukv[:, :, :NOPE_DIM], ((0, 0), (0, 0), (0, HEAD_PAD - NOPE_DIM))).reshape(KV_LORA, H * HEAD_PAD).astype(BF16)
    lw["w_v_p"] = jnp.pad(w_ukv[:, :, NOPE_DIM:], ((0, 0), (0, 0), (0, HEAD_PAD - V_DIM))).reshape(KV_LORA, H * HEAD_PAD).astype(BF16)
    lw["scores_bounded"] = _scores_bounded(w, l)
    lw["g_qnorm_p"] = jnp.pad(w["g_qnorm"][l], (0, pad)).reshape(1, HEAD_PAD)
    lw["g_knorm_p"] = jnp.pad(w["g_knorm"][l], (0, pad)).reshape(1, HEAD_PAD)
    for L in chunk_lens:
        wsp = w["w_spatial"][l][:, :L, :L]
        lw["w_sp_pairs_%d" % L] = wsp.reshape(GM_HEADS // 2, 2 * L, L).astype(BF16)
        lw["b_sp_rows_%d" % L] = jnp.repeat(jnp.transpose(w["b_spatial"][l][:, :L]), GM_HEAD_DIM, axis=1)
    w_out = w["w_out"][l].astype(BF16)
    lw["w_out_a"], lw["w_out_g"] = w_out[:ATT_WIDTH], w_out[ATT_WIDTH:]
    wr_full = jnp.zeros((D, ROUTER_COLS), F32)
    wr_full = wr_full.at[:, :N_GROUPS].set(w["w_router_group"][l])
    wr_full = wr_full.at[:, ROUTER_EXPERT_LANE0:ROUTER_EXPERT_LANE0 + N_EXPERTS].set(w["w_router_expert"][l])
    r_hi, r_lo = _split_bf16(wr_full)
    lw["w_r_a"] = jnp.concatenate([r_hi, r_lo], axis=1)
    lw["w_r_b"] = r_hi
    br = jnp.zeros((1, ROUTER_COLS), F32)
    br = br.at[0, :N_GROUPS].set(w["b_router_group"][l])
    lw["b_r"] = br.at[0, ROUTER_EXPERT_LANE0:ROUTER_EXPERT_LANE0 + N_EXPERTS].set(w["b_router_expert"][l])
    lw["w_gu_e"] = jnp.concatenate([w["w_gate_e"][l], w["w_up_e"][l]], axis=-1).astype(BF16)
    lw["w_d_e"] = w["w_down_e"][l].astype(BF16)
    return lw


def _tiles(seq):
    tm = min(seq, 512)
    return tm, min(seq, 512)


def _layer_prompt(x, mod, lw):
    B, S, D = x.shape
    tm, tq = _tiles(S)
    pos = jnp.arange(S)
    q_tabs = _rope_tables(pos, QK_DIM ** -0.5 * LOG2E)
    k_tabs = _rope_tables(pos, 1.0)
    ckv, krope, q, gm, k, v = _mix_in(x, mod, lw, q_tabs, k_tabs, tm=tm, chunk_len=GM_CHUNK, emit_kv=True, emit_vrows=False)
    att = _attention(lw["scores_bounded"], q, k, v, tq=tq, tk=tq, q_off=0, kv_valid=S)
    rec, ri, cnt = _out_route(x, att, gm, mod, lw, tm=tm, pack_rows=True)
    cls = ri[:, 0, :].reshape(B * S)
    rank = ri[:, 1, :].reshape(B * S)
    counts = cnt[:N_CLASSES, 0].astype(I32)
    y = _moe_prompt(rec, cls, rank, counts, mod, lw)
    return y, ckv, krope


def _layer_sample(x, mod, past_ckv, past_krope, lw):
    B, S, D = x.shape
    past = past_ckv.shape[1]
    q_tabs = _rope_tables(past + jnp.arange(S), QK_DIM ** -0.5 * LOG2E)
    ckv, krope, q, gm, v_rows = _mix_in(x, mod, lw, q_tabs, q_tabs, tm=S, chunk_len=S, emit_kv=False, emit_vrows=True)
    kv_valid = past + S
    kv_pad = -(-kv_valid // LANES) * LANES
    extra = kv_pad - kv_valid
    ckv_all = jnp.concatenate([past_ckv, ckv, jnp.zeros((B, extra, KV_LORA), F32)], axis=1)
    kr_all = jnp.concatenate([past_krope, krope, jnp.zeros((B, extra, ROPE_DIM), F32)], axis=1)
    kr_slot = jnp.pad(kr_all, ((0, 0), (0, 0), (NOPE_DIM, HEAD_PAD - QK_DIM)))
    k_tabs = _rope_tables(jnp.arange(kv_pad), 1.0)
    k, v = _kv_latent(ckv_all, kr_slot, lw, k_tabs, tr=kv_pad)
    att = _attention(lw["scores_bounded"], q, k, v, tq=S, tk=kv_pad, q_off=past, kv_valid=kv_valid)
    x1, h2, ri, rf, _ = _out_route(x, att, gm, mod, lw, tm=S, pack_rows=False)
    T = B * S
    cls, w_lo, w_hi = ri[:, 0, :].reshape(T), rf[:, 0, :].reshape(T), rf[:, 1, :].reshape(T)
    lo_tab, hi_tab = _pair_tables()
    e_lo, e_hi = jnp.asarray(lo_tab)[cls], jnp.asarray(hi_tab)[cls]
    eids = jnp.arange(N_EXPERTS, dtype=I32)[:, None]
    is_lo, is_hi = eids == e_lo[None, :], eids == e_hi[None, :]
    w_sel = (jnp.where(is_lo, w_lo[None, :], 0.0) + jnp.where(is_hi, w_hi[None, :], 0.0))[:, :, None]
    sel = (is_lo | is_hi).astype(F32)[:, :, None]
    gate_rows = jnp.repeat(mod[:, 5, :], S, axis=0)
    y = _moe_dense(h2.reshape(T, D), x1.reshape(T, D), gate_rows, w_sel, sel, lw)
    return y.reshape(B, S, D), ckv, krope, v_rows


def kernel(x_prompt, x_sample, cache_ckv, cache_krope, c_prompt, c_sample, w_ada, b_ada, g_norm_mix, w_in, g_q_lat, w_uq, g_kv_lat, w_ukv, g_qnorm, g_knorm, g_ln_v, b_ln_v, w_spatial, b_spatial, g_out_attn, g_out_gmlp, w_out, g_norm_ffn, w_router_group, b_router_group, w_router_expert, b_router_expert, w_gate_e, w_up_e, w_down_e):
    w = dict(w_in=w_in, g_norm_mix=g_norm_mix, g_q_lat=g_q_lat, w_uq=w_uq, g_kv_lat=g_kv_lat, w_ukv=w_ukv,
             g_qnorm=g_qnorm, g_knorm=g_knorm, g_ln_v=g_ln_v, b_ln_v=b_ln_v, w_spatial=w_spatial, b_spatial=b_spatial,
             g_out_attn=g_out_attn, g_out_gmlp=g_out_gmlp, w_out=w_out, g_norm_ffn=g_norm_ffn,
             w_router_group=w_router_group, b_router_group=b_router_group, w_router_expert=w_router_expert,
             b_router_expert=b_router_expert, w_gate_e=w_gate_e, w_up_e=w_up_e, w_down_e=w_down_e)
    depth = w_ada.shape[0]
    Bp, Sp, D = x_prompt.shape
    Bs, Ss, _ = x_sample.shape
    assert Sp % GM_CHUNK == 0 and Ss <= GM_CHUNK and Ss % CHUNK == 0 and Bp <= MOD_BATCH_PAD
    c_all = jnp.concatenate([c_prompt, c_sample], axis=0)
    y_p, y_s = x_prompt, x_sample
    outs = [[] for _ in range(5)]
    for l in range(depth):
        lw = _prep_layer(w, l, (GM_CHUNK, Ss))
        mod = _ada_mod(c_all, w_ada[l], b_ada[l]).reshape(Bp + Bs, 6, D)
        y_p, ckv_p, kr_p = _layer_prompt(y_p, mod[:Bp], lw)
        y_s, ckv_s, kr_s, v_s = _layer_sample(y_s, mod[Bp:], cache_ckv[l], cache_krope[l], lw)
        for lst, val in zip(outs, (ckv_p, kr_p, ckv_s, kr_s, v_s)):
            lst.append(val)
    return (y_p, y_s) + tuple(jnp.stack(lst) for lst in outs)
```

```python
import functools

import numpy as np
import jax
import jax.numpy as jnp
from jax import lax
from jax.experimental import pallas as pl
from jax.experimental.pallas import tpu as pltpu

F32 = jnp.float32
BF16 = jnp.bfloat16
I32 = jnp.int32

CHUNK = 64
CHUNK_SHIFT = 6
EPS = 1e-6
MLA_HEADS = 8
Q_LORA = 256
KV_LORA = 128
NOPE_DIM = 64
ROPE_DIM = 32
ROPE_HALF = ROPE_DIM // 2
V_DIM = 64
QK_DIM = NOPE_DIM + ROPE_DIM
ATT_WIDTH = MLA_HEADS * V_DIM
ROPE_BASE = 10000.0
GM_HEADS = 8
GM_HEAD_DIM = 64
GM_WIDTH = GM_HEADS * GM_HEAD_DIM
GM_CHUNK = 128
N_GROUPS = 4
EXPERTS_PER_GROUP = 8
N_EXPERTS = N_GROUPS * EXPERTS_PER_GROUP
D_EXPERT = 256
PAIRS_PER_GROUP = EXPERTS_PER_GROUP * (EXPERTS_PER_GROUP - 1) // 2
N_CLASSES = N_GROUPS * PAIRS_PER_GROUP

LANES = 128
SUBLANES = 8
HEAD_PAD = LANES
PROJ_COLS = 1536
ROUTER_COLS = LANES
ROUTER_EXPERT_LANE0 = SUBLANES
CLASS_ROWS = LANES
MOE_ROWS = 128
MOD_BATCH_PAD = 16
VMEM_LIMIT = 48 * 1024 * 1024
NEG_BIG = -1e30
LOG2E = 1.4426950408889634
BF16_SLACK = 1.02
SCORE_BOUND = 90.0

assert CHUNK == 1 << CHUNK_SHIFT


def _params(n_axes, vmem=VMEM_LIMIT):
    return pltpu.CompilerParams(dimension_semantics=("arbitrary",) * n_axes, vmem_limit_bytes=vmem)


def _full(shape):
    nd = len(shape)
    return pl.BlockSpec(shape, lambda *_: (0,) * nd)


def _split_bf16(x):
    hi = x.astype(BF16)
    lo = (x - hi.astype(F32)).astype(BF16)
    return hi, lo


def _dot(a, b):
    return jnp.dot(a, b, preferred_element_type=F32)


def _ada_kernel(c_ref, w_ref, b_ref, o_ref):
    a_hi, a_lo = _split_bf16(jax.nn.silu(c_ref[...]))
    w_hi, w_lo = _split_bf16(w_ref[...])
    o_ref[...] = _dot(a_hi, w_hi) + _dot(a_lo, w_hi) + _dot(a_hi, w_lo) + b_ref[...]


def _ada_mod(c, w_ada, b_ada):
    n, d = c.shape
    cols = w_ada.shape[1]
    tn = 1536
    return pl.pallas_call(
        _ada_kernel,
        grid=(cols // tn,),
        in_specs=[_full((n, d)), pl.BlockSpec((d, tn), lambda j: (0, j)), pl.BlockSpec((1, tn), lambda j: (0, j))],
        out_specs=pl.BlockSpec((n, tn), lambda j: (0, j)),
        out_shape=jax.ShapeDtypeStruct((n, cols), F32),
        compiler_params=_params(1),
    )(c, w_ada, b_ada.reshape(1, cols))


def _rms(x, g):
    return x * lax.rsqrt(jnp.mean(x * x, axis=-1, keepdims=True) + EPS) * g


def _head_norm_rope(xh, xh_swapped, tab_c, tab_s):
    ms = jnp.sum(xh * xh, axis=-1, keepdims=True) * (1.0 / QK_DIM)
    return (xh * tab_c + xh_swapped * tab_s) * lax.rsqrt(ms + EPS)


def _keys_values(ckv, kr_slot, gkv, wk, wv, tab_c, tab_s, k_ref, v_ref):
    cb = _rms(ckv, gkv).astype(BF16)
    kall = _dot(cb, wk)
    vall = _dot(cb, wv)
    lane = lax.broadcasted_iota(I32, (1, LANES), 1)
    one_col = jnp.where(lane == V_DIM, 1.0, 0.0)
    kr_swapped = jnp.where(lane < NOPE_DIM + ROPE_HALF, pltpu.roll(kr_slot, HEAD_PAD - ROPE_HALF, 1),
                           pltpu.roll(kr_slot, ROPE_HALF, 1))
    kr_swapped = jnp.where((lane >= NOPE_DIM) & (lane < QK_DIM), kr_swapped, 0.0)
    for h in range(MLA_HEADS):
        kh = kall[:, h * HEAD_PAD:(h + 1) * HEAD_PAD] + kr_slot
        k_ref[0, h] = _head_norm_rope(kh, kr_swapped, tab_c, tab_s).astype(BF16)
        v_ref[0, h] = (vall[:, h * HEAD_PAD:(h + 1) * HEAD_PAD] + one_col).astype(BF16)


def _mix_in_kernel(x_ref, mod_ref, gmix_ref, win_ref, gql_ref, wuq_ref, gkv_ref, wk_ref, wv_ref,
                   cq_ref, sq_ref, ck_ref, sk_ref, glnv_ref, blnv_ref, wsp_ref, bsp_ref,
                   ggm_ref, *rest, chunk_len, emit_kv, emit_vrows):
    outs = list(rest[:-1])
    mixed_scr = rest[-1]
    ckv_ref, kr_ref, q_ref, gm_ref = outs[:4]
    outs = outs[4:]
    if emit_kv:
        k_ref, v_ref = outs[:2]
        outs = outs[2:]
    if emit_vrows:
        vrows_ref = outs[0]

    x = x_ref[0]
    tm = x.shape[0]
    shift, scale = mod_ref[0, 0:1, :], mod_ref[0, 1:2, :]
    h = _rms(x, gmix_ref[...]) * (1.0 + scale) + shift
    proj = _dot(h.astype(BF16), win_ref[...])

    q_lat = proj[:, 0:Q_LORA]
    ckv = proj[:, Q_LORA:Q_LORA + KV_LORA]
    kr_blk = proj[:, Q_LORA + KV_LORA:Q_LORA + KV_LORA + LANES]
    ckv_ref[0] = ckv
    kr_ref[0] = kr_blk[:, 0:ROPE_DIM]

    q = _dot(_rms(q_lat, gql_ref[...]).astype(BF16), wuq_ref[...])
    cq, sq = cq_ref[...], sq_ref[...]
    n_q = MLA_HEADS * HEAD_PAD
    for hd in range(MLA_HEADS):
        qh = q[:, hd * HEAD_PAD:(hd + 1) * HEAD_PAD]
        qh_swapped = q[:, n_q + hd * HEAD_PAD:n_q + (hd + 1) * HEAD_PAD]
        q_ref[0, hd] = _head_norm_rope(qh, qh_swapped, cq, sq).astype(BF16)

    if emit_kv:
        lane = lax.broadcasted_iota(I32, (1, LANES), 1)
        kr_slot = jnp.where(lane >= NOPE_DIM, kr_blk, 0.0)
        _keys_values(ckv, kr_slot, gkv_ref[...], wk_ref[...], wv_ref[...], ck_ref[...], sk_ref[...], k_ref, v_ref)

    g_u = proj[:, 512:512 + GM_WIDTH]
    g_v = proj[:, 1024:1024 + GM_WIDTH]
    u = jax.nn.gelu(g_u)
    gv = jax.nn.gelu(g_v)
    mu = jnp.mean(gv, axis=-1, keepdims=True)
    xc = gv - mu
    var = jnp.mean(xc * xc, axis=-1, keepdims=True)
    v_rows = xc * lax.rsqrt(var + EPS) * glnv_ref[...] + blnv_ref[...]
    if emit_vrows:
        vrows_ref[0] = v_rows
    vb = v_rows.astype(BF16)

    L = chunk_len
    t = lax.broadcasted_iota(I32, (2 * L, L), 0)
    s = lax.broadcasted_iota(I32, (2 * L, L), 1)
    t = jnp.where(t >= L, t - L, t)
    allowed = (s >> CHUNK_SHIFT) <= (t >> CHUNK_SHIFT)
    lane = lax.broadcasted_iota(I32, (1, LANES), 1)
    first_head = lane < GM_HEAD_DIM
    for p in range(GM_HEADS // 2):
        w_pair = jnp.where(allowed, wsp_ref[p], jnp.zeros((), BF16))
        for c in range(tm // L):
            vp = vb[c * L:(c + 1) * L, p * LANES:(p + 1) * LANES]
            r = _dot(w_pair, vp)
            mixed = jnp.where(first_head, r[:L], r[L:])
            mixed_scr[c * L:(c + 1) * L, p * LANES:(p + 1) * LANES] = mixed + bsp_ref[:, p * LANES:(p + 1) * LANES]
    gm = u * mixed_scr[...]
    gm_ref[0] = _rms(gm, ggm_ref[...]).astype(BF16)


def _mix_in(x, mod, lw, q_tabs, k_tabs, *, tm, chunk_len, emit_kv, emit_vrows):
    B, S, D = x.shape
    nt = S // tm
    H = MLA_HEADS
    row = lambda b, i: (b, i, 0)
    tab = pl.BlockSpec((tm, LANES), lambda b, i: (i, 0))
    in_specs = [
        pl.BlockSpec((1, tm, D), row),
        pl.BlockSpec((1, 6, D), lambda b, i: (b, 0, 0)),
        _full((1, D)), _full((D, PROJ_COLS)), _full((1, Q_LORA)), _full((Q_LORA, 2 * H * HEAD_PAD)),
        _full((1, KV_LORA)), _full((KV_LORA, H * HEAD_PAD)), _full((KV_LORA, H * HEAD_PAD)),
        tab, tab, tab, tab,
        _full((1, GM_WIDTH)), _full((1, GM_WIDTH)),
        _full((GM_HEADS // 2, 2 * chunk_len, chunk_len)), _full((chunk_len, GM_WIDTH)), _full((1, GM_WIDTH)),
    ]
    out_shape = [
        jax.ShapeDtypeStruct((B, S, KV_LORA), F32),
        jax.ShapeDtypeStruct((B, S, ROPE_DIM), F32),
        jax.ShapeDtypeStruct((B, H, S, HEAD_PAD), BF16),
        jax.ShapeDtypeStruct((B, S, GM_WIDTH), BF16),
    ]
    head_blk = pl.BlockSpec((1, H, tm, HEAD_PAD), lambda b, i: (b, 0, i, 0))
    out_specs = [
        pl.BlockSpec((1, tm, KV_LORA), row),
        pl.BlockSpec((1, tm, ROPE_DIM), row),
        head_blk,
        pl.BlockSpec((1, tm, GM_WIDTH), row),
    ]
    if emit_kv:
        out_shape += [jax.ShapeDtypeStruct((B, H, S, HEAD_PAD), BF16), jax.ShapeDtypeStruct((B, H, S, HEAD_PAD), BF16)]
        out_specs += [head_blk, head_blk]
    if emit_vrows:
        out_shape += [jax.ShapeDtypeStruct((B, S, GM_WIDTH), F32)]
        out_specs += [pl.BlockSpec((1, tm, GM_WIDTH), row)]
    kern = functools.partial(_mix_in_kernel, chunk_len=chunk_len, emit_kv=emit_kv, emit_vrows=emit_vrows)
    return pl.pallas_call(
        kern,
        grid=(B, nt),
        in_specs=in_specs,
        out_specs=out_specs,
        out_shape=out_shape,
        scratch_shapes=[pltpu.VMEM((tm, GM_WIDTH), F32)],
        compiler_params=_params(2),
    )(x, mod, lw["g_norm_mix"], lw["w_in_p"], lw["g_q_lat"], lw["w_uq_p"], lw["g_kv_lat"], lw["w_k_p"], lw["w_v_p"],
      *q_tabs, *k_tabs, lw["g_ln_v"], lw["b_ln_v"],
      lw["w_sp_pairs_%d" % chunk_len], lw["b_sp_rows_%d" % chunk_len], lw["g_out_gmlp"])


def _kv_latent_kernel(ckv_ref, kr_ref, gkv_ref, wk_ref, wv_ref, c_ref, s_ref, k_ref, v_ref):
    _keys_values(ckv_ref[0], kr_ref[0], gkv_ref[...], wk_ref[...], wv_ref[...], c_ref[...], s_ref[...], k_ref, v_ref)


def _kv_latent(ckv_all, kr_slot_all, lw, k_tabs, *, tr):
    B, K, _ = ckv_all.shape
    H = MLA_HEADS
    row = lambda b, i: (b, i, 0)
    tab = pl.BlockSpec((tr, LANES), lambda b, i: (i, 0))
    return pl.pallas_call(
        _kv_latent_kernel,
        grid=(B, K // tr),
        in_specs=[pl.BlockSpec((1, tr, KV_LORA), row), pl.BlockSpec((1, tr, LANES), row),
                  _full((1, KV_LORA)), _full((KV_LORA, H * HEAD_PAD)), _full((KV_LORA, H * HEAD_PAD)),
                  tab, tab],
        out_specs=[pl.BlockSpec((1, H, tr, HEAD_PAD), lambda b, i: (b, 0, i, 0))] * 2,
        out_shape=[jax.ShapeDtypeStruct((B, H, K, HEAD_PAD), BF16)] * 2,
        compiler_params=_params(2),
    )(ckv_all, kr_slot_all, lw["g_kv_lat"], lw["w_k_p"], lw["w_v_p"], *k_tabs)


def _attn_kernel(bounded_ref, q_ref, k_ref, v_ref, o_ref, *, tq, tk, n_q, q_off, kv_valid):
    i = pl.program_id(2) if n_q > 1 else 0
    q_first = q_off + i * tq
    vis_first = jnp.minimum(((q_first >> CHUNK_SHIFT) + 1) << CHUNK_SHIFT, kv_valid)
    vis_last = jnp.minimum((((q_first + tq - 1) >> CHUNK_SHIFT) + 1) << CHUNK_SHIFT, kv_valid)
    n_unmasked = vis_first // tk
    n_total = (vis_last + tk - 1) // tk

    q_pos = q_first + lax.broadcasted_iota(I32, (tq, 1), 0)
    limit = jnp.minimum(((q_pos >> CHUNK_SHIFT) + 1) << CHUNK_SHIFT, kv_valid)
    lane = lax.broadcasted_iota(I32, (1, LANES), 1)

    def scores(j, start, masked):
        s = lax.dot_general(q_ref[0, j], k_ref[0, j, pl.ds(start, tk), :], (((1,), (1,)), ((), ())),
                            preferred_element_type=F32)
        if masked:
            k_pos = start + lax.broadcasted_iota(I32, (1, tk), 1)
            s = jnp.where(k_pos < limit, s, NEG_BIG)
        return s

    def plain_block(kb, carry, masked):
        start = pl.multiple_of(kb * tk, tk)
        new = []
        for j in range(2):
            p = jnp.exp2(scores(j, start, masked)).astype(BF16)
            new.append(carry[j] + _dot(p, v_ref[0, j, pl.ds(start, tk), :]))
        return tuple(new)

    def online_block(kb, carry, masked):
        start = pl.multiple_of(kb * tk, tk)
        new = []
        for j in range(2):
            m, acc = carry[2 * j:2 * j + 2]
            s = scores(j, start, masked)
            m_new = jnp.maximum(m, jnp.max(s, axis=-1, keepdims=True))
            p = jnp.exp2(s - m_new).astype(BF16)
            acc = jnp.exp2(m - m_new) * acc + _dot(p, v_ref[0, j, pl.ds(start, tk), :])
            new += [m_new, acc]
        return tuple(new)

    def run(block, init):
        carry = lax.fori_loop(0, n_unmasked, lambda kb, c: block(kb, c, False), tuple(init))
        return lax.fori_loop(n_unmasked, n_total, lambda kb, c: block(kb, c, True), carry)

    def finish(acc0, acc1):
        outs = []
        for acc in (acc0, acc1):
            denom = jnp.sum(jnp.where(lane == V_DIM, acc, 0.0), axis=-1, keepdims=True)
            outs.append(acc / denom)
        o_ref[0] = jnp.where(lane < V_DIM, outs[0], pltpu.roll(outs[1], V_DIM, 1)).astype(BF16)

    zeros = jnp.zeros((tq, LANES), F32)

    @pl.when(bounded_ref[0] == 1)
    def _():
        finish(*run(plain_block, [zeros, zeros]))

    @pl.when(bounded_ref[0] != 1)
    def _():
        m0 = jnp.full((tq, 1), NEG_BIG, F32)
        c = run(online_block, [m0, zeros, m0, zeros])
        finish(c[1], c[3])


def _attention(bounded, q, k, v, *, tq, tk, q_off, kv_valid):
    B, H, Sq, _ = q.shape
    Sk = k.shape[2]
    kern = functools.partial(_attn_kernel, tq=tq, tk=tk, n_q=Sq // tq, q_off=q_off, kv_valid=kv_valid)
    grid_spec = pltpu.PrefetchScalarGridSpec(
        num_scalar_prefetch=1,
        grid=(B, H // 2, Sq // tq),
        in_specs=[pl.BlockSpec((1, 2, tq, HEAD_PAD), lambda b, hp, i, f: (b, hp, i, 0)),
                  pl.BlockSpec((1, 2, Sk, HEAD_PAD), lambda b, hp, i, f: (b, hp, 0, 0)),
                  pl.BlockSpec((1, 2, Sk, HEAD_PAD), lambda b, hp, i, f: (b, hp, 0, 0))],
        out_specs=pl.BlockSpec((1, tq, LANES), lambda b, hp, i, f: (b, i, hp)),
    )
    return pl.pallas_call(
        kern,
        grid_spec=grid_spec,
        out_shape=jax.ShapeDtypeStruct((B, Sq, ATT_WIDTH), BF16),
        compiler_params=_params(3),
    )(bounded, q, k, v)


def _scores_bounded(w, l):
    gq = jnp.max(jnp.abs(w["g_qnorm"][l]))
    gk = jnp.max(jnp.abs(w["g_knorm"][l]))
    bound = (QK_DIM ** 0.5) * LOG2E * BF16_SLACK * gq * gk
    return (bound <= SCORE_BOUND).astype(I32).reshape(1)


def _pack_bf16_pairs(h):
    n = h.shape[1] // 2
    hi = pltpu.bitcast(h[:, :n].astype(BF16).astype(F32), jnp.uint32)
    lo = pltpu.bitcast(h[:, n:].astype(BF16).astype(F32), jnp.uint32)
    return pltpu.bitcast(hi | (lo >> 16), F32)


def _unpack_bf16_pairs(words):
    w = pltpu.bitcast(words, jnp.uint32)
    hi = pltpu.bitcast(w & jnp.uint32(0xFFFF0000), F32)
    lo = pltpu.bitcast(w << 16, F32)
    return jnp.concatenate([hi, lo], axis=1).astype(BF16)


def _out_route_kernel(x_ref, att_ref, gm_ref, mod_ref, goa_ref, woa_ref, wog_ref, gffn_ref, wra_ref, wrb_ref, br_ref,
                      *rest, pack_rows):
    carry_scr = rest[-1]
    if pack_rows:
        rec_ref, ri_ref, cnt_ref = rest[:3]
    else:
        x1_ref, h2_ref, ri_ref, rf_ref, cnt_ref = rest[:5]
    first_step = (pl.program_id(0) == 0) & (pl.program_id(1) == 0)

    @pl.when(first_step)
    def _():
        carry_scr[...] = jnp.zeros_like(carry_scr)

    x = x_ref[0]
    tm, D = x.shape
    gate_a = mod_ref[0, 2:3, :]
    shift_m, scale_m = mod_ref[0, 3:4, :], mod_ref[0, 4:5, :]
    att_n = _rms(att_ref[0].astype(F32), goa_ref[...]).astype(BF16)
    mix = _dot(att_n, woa_ref[...]) + _dot(gm_ref[0], wog_ref[...])
    x1 = x + gate_a * mix
    h2 = _rms(x1, gffn_ref[...]) * (1.0 + scale_m) + shift_m
    if pack_rows:
        rec_ref[0, :, 0:D] = x1
        rec_ref[0, :, D:D + D // 2] = _pack_bf16_pairs(h2)
    else:
        x1_ref[0] = x1
        h2_ref[0] = h2.astype(BF16)

    h_hi, h_lo = _split_bf16(h2)
    la = _dot(h_hi, wra_ref[...])
    logits = la[:, :ROUTER_COLS] + la[:, ROUTER_COLS:] + _dot(h_lo, wrb_ref[...]) + br_ref[...]
    lt = logits.T

    g = [lt[r:r + 1] for r in range(N_GROUPS)]
    gmax = jnp.maximum(jnp.maximum(g[0], g[1]), jnp.maximum(g[2], g[3]))
    gsum = sum(jnp.exp(gr - gmax) for gr in g)
    g_prob = 1.0 / gsum
    g_idx = jnp.where(g[0] == gmax, 0.0, jnp.where(g[1] == gmax, 1.0, jnp.where(g[2] == gmax, 2.0, 3.0)))

    e0 = ROUTER_EXPERT_LANE0
    grp = [lt[e0 + EXPERTS_PER_GROUP * r:e0 + EXPERTS_PER_GROUP * (r + 1)] for r in range(N_GROUPS)]
    sel = jnp.where(g_idx == 0.0, grp[0], jnp.where(g_idx == 1.0, grp[1], jnp.where(g_idx == 2.0, grp[2], grp[3])))
    sub = lax.broadcasted_iota(I32, (EXPERTS_PER_GROUP, tm), 0).astype(F32)
    m1 = jnp.max(sel, axis=0, keepdims=True)
    i1 = jnp.min(jnp.where(sel == m1, sub, float(EXPERTS_PER_GROUP)), axis=0, keepdims=True)
    sel2 = jnp.where(sub == i1, -jnp.inf, sel)
    m2 = jnp.max(sel2, axis=0, keepdims=True)
    i2 = jnp.min(jnp.where(sel2 == m2, sub, float(EXPERTS_PER_GROUP)), axis=0, keepdims=True)
    d = jnp.exp(m2 - m1)
    w1 = g_prob / (1.0 + d)
    w2 = g_prob * d / (1.0 + d)
    first_lower = i1 < i2
    lo = jnp.minimum(i1, i2)
    hi = jnp.maximum(i1, i2)
    w_lo = jnp.where(first_lower, w1, w2)
    w_hi = jnp.where(first_lower, w2, w1)
    pair = lo * EXPERTS_PER_GROUP - lo * (lo + 1.0) * 0.5 + hi - lo - 1.0
    cls = g_idx * PAIRS_PER_GROUP + pair

    crow = lax.broadcasted_iota(I32, (CLASS_ROWS, tm), 0).astype(F32)
    onehot = jnp.where(crow == cls, 1.0, 0.0)
    ta = lax.broadcasted_iota(I32, (tm, tm), 0)
    tb = lax.broadcasted_iota(I32, (tm, tm), 1)
    earlier = jnp.where(ta < tb, 1.0, 0.0).astype(BF16)
    before = _dot(onehot.astype(BF16), earlier)
    carry = carry_scr[...]
    rank = jnp.sum(onehot * (before + carry[:, 0:1]), axis=0, keepdims=True)
    carry = carry + jnp.sum(onehot, axis=1, keepdims=True)
    carry_scr[...] = carry
    cnt_ref[...] = carry

    ri_ref[...] = jnp.zeros_like(ri_ref)
    ri_ref[0, 0:1, :] = cls.astype(I32)
    ri_ref[0, 1:2, :] = rank.astype(I32)
    if pack_rows:
        mrow = lax.broadcasted_iota(I32, (LANES, tm), 0)
        batch = pl.program_id(0).astype(F32)
        meta_t = jnp.where(mrow == 0, w_lo, jnp.where(mrow == 1, w_hi, jnp.where(mrow == 2, batch, 0.0)))
        rec_ref[0, :, D + D // 2:] = meta_t.T
    else:
        rf_ref[...] = jnp.zeros_like(rf_ref)
        rf_ref[0, 0:1, :] = w_lo
        rf_ref[0, 1:2, :] = w_hi


def _out_route(x, att, gm, mod, lw, *, tm, pack_rows):
    B, S, D = x.shape
    nt = S // tm
    row = lambda b, i: (b, i, 0)
    tile = lambda b, i: (b * nt + i, 0, 0)
    route_i = (jax.ShapeDtypeStruct((B * nt, SUBLANES, tm), I32), pl.BlockSpec((1, SUBLANES, tm), tile))
    route_f = (jax.ShapeDtypeStruct((B * nt, SUBLANES, tm), F32), pl.BlockSpec((1, SUBLANES, tm), tile))
    counts = (jax.ShapeDtypeStruct((CLASS_ROWS, LANES), F32), _full((CLASS_ROWS, LANES)))
    if pack_rows:
        rec_w = D + D // 2 + LANES
        outs = [(jax.ShapeDtypeStruct((B, S, rec_w), F32), pl.BlockSpec((1, tm, rec_w), row)), route_i, counts]
    else:
        outs = [(jax.ShapeDtypeStruct((B, S, D), F32), pl.BlockSpec((1, tm, D), row)),
                (jax.ShapeDtypeStruct((B, S, D), BF16), pl.BlockSpec((1, tm, D), row)), route_i, route_f, counts]
    out_shape = [o[0] for o in outs]
    out_specs = [o[1] for o in outs]
    return pl.pallas_call(
        functools.partial(_out_route_kernel, pack_rows=pack_rows),
        grid=(B, nt),
        in_specs=[pl.BlockSpec((1, tm, D), row), pl.BlockSpec((1, tm, ATT_WIDTH), row), pl.BlockSpec((1, tm, GM_WIDTH), row),
                  pl.BlockSpec((1, 6, D), lambda b, i: (b, 0, 0)),
                  _full((1, ATT_WIDTH)), _full((ATT_WIDTH, D)), _full((GM_WIDTH, D)), _full((1, D)),
                  _full((D, 2 * ROUTER_COLS)), _full((D, ROUTER_COLS)), _full((1, ROUTER_COLS))],
        out_specs=out_specs,
        out_shape=out_shape,
        scratch_shapes=[pltpu.VMEM((CLASS_ROWS, LANES), F32)],
        compiler_params=_params(2),
    )(x, att, gm, mod, lw["g_out_attn"], lw["w_out_a"], lw["w_out_g"], lw["g_norm_ffn"],
      lw["w_r_a"], lw["w_r_b"], lw["b_r"])


def _swiglu(hb, wgu, wd):
    hid = _dot(hb, wgu)
    act = jax.nn.silu(hid[:, :D_EXPERT]) * hid[:, D_EXPERT:]
    return _dot(act.astype(BF16), wd)


def _moe_pairs_kernel(elo_ref, ehi_ref, nv_ref, dest_ref, rec_hbm, gate_ref,
                      wgu_lo_ref, wd_lo_ref, wgu_hi_ref, wd_hi_ref, y_hbm, xbuf, ybuf, tok_ref, gsem, ssem):
    i = pl.program_id(0)
    n = pl.num_programs(0)
    slot = i % 2
    D = ybuf.shape[-1]

    @pl.when(i == 0)
    def _():
        def invert(g, c):
            for u in range(SUBLANES):
                t = g * SUBLANES + u
                tok_ref[dest_ref[t]] = t
            return c
        lax.fori_loop(0, dest_ref.shape[0] // SUBLANES, invert, 0)

    def gather_start(t, s, g, u):
        pltpu.make_async_copy(rec_hbm.at[pl.ds(t, 1)], xbuf.at[s, g, pl.ds(u, 1)], gsem.at[s]).start()

    def scatter_start(t, s, g, u):
        pltpu.make_async_copy(ybuf.at[s, g, pl.ds(u, 1)], y_hbm.at[pl.ds(t, 1)], ssem.at[s]).start()

    def for_rows(blk, fn):
        nv = nv_ref[blk]
        base = blk * MOE_ROWS
        n_groups = nv // SUBLANES

        def group(g, c):
            for u in range(SUBLANES):
                fn(tok_ref[base + g * SUBLANES + u], g, u)
            return c

        def single(r, c):
            fn(tok_ref[base + r], n_groups, r - n_groups * SUBLANES)
            return c

        lax.fori_loop(0, n_groups, group, 0)
        lax.fori_loop(n_groups * SUBLANES, nv, single, 0)

    def wait_rows(blk, buf, sem):
        nv = nv_ref[blk]
        n_groups = nv // SUBLANES

        @pl.when(n_groups > 0)
        def _():
            rows = buf.at[pl.ds(0, n_groups)]
            pltpu.make_async_copy(rows, rows, sem).wait()

        def single(r, c):
            row = buf.at[0, pl.ds(0, 1)]
            pltpu.make_async_copy(row, row, sem).wait()
            return c

        lax.fori_loop(n_groups * SUBLANES, nv, single, 0)

    def gather_wait(blk, s):
        wait_rows(blk, xbuf.at[s], gsem.at[s])

    def scatter_wait(blk, s):
        wait_rows(blk, ybuf.at[s], ssem.at[s])

    @pl.when(i == 0)
    def _():
        xbuf[...] = jnp.zeros_like(xbuf)
        for_rows(0, lambda t, g, u: gather_start(t, 0, g, u))

    @pl.when(i + 1 < n)
    def _():
        for_rows(i + 1, lambda t, g, u: gather_start(t, 1 - slot, g, u))

    gather_wait(i, slot)

    @pl.when(i >= 2)
    def _():
        scatter_wait(i - 2, slot)

    @pl.when(nv_ref[i] > 0)
    def _():
        x = xbuf[slot, :, :, 0:D].reshape(MOE_ROWS, D)
        hb = _unpack_bf16_pairs(xbuf[slot, :, :, D:D + D // 2].reshape(MOE_ROWS, D // 2))
        meta = xbuf[slot, :, :, D + D // 2:].reshape(MOE_ROWS, LANES)
        w_lo, w_hi, bidx = meta[:, 0:1], meta[:, 1:2], meta[:, 2:3]
        blane = lax.broadcasted_iota(I32, (1, 2 * MOD_BATCH_PAD), 1)
        blane = jnp.where(blane >= MOD_BATCH_PAD, blane - MOD_BATCH_PAD, blane).astype(F32)
        onehot = jnp.where(bidx == blane, 1.0, 0.0).astype(BF16)
        gate_m = _dot(onehot, gate_ref[...])
        moe = w_lo * _swiglu(hb, wgu_lo_ref[0], wd_lo_ref[0]) + w_hi * _swiglu(hb, wgu_hi_ref[0], wd_hi_ref[0])
        ybuf[slot] = (x + gate_m * moe).reshape(MOE_ROWS // SUBLANES, SUBLANES, D)
        for_rows(i, lambda t, g, u: scatter_start(t, slot, g, u))

    @pl.when(i == n - 1)
    def _():
        @pl.when(i >= 1)
        def _():
            scatter_wait(i - 1, 1 - slot)
        scatter_wait(i, slot)


def _moe_pairs(rec, blk_elo, blk_ehi, blk_nv, dest, gate_tab, lw):
    T, rec_w = rec.shape
    assert T % SUBLANES == 0
    D = gate_tab.shape[1]
    nb = blk_nv.shape[0]
    wgu, wd = lw["w_gu_e"], lw["w_d_e"]
    grid_spec = pltpu.PrefetchScalarGridSpec(
        num_scalar_prefetch=4,
        grid=(nb,),
        in_specs=[
            pl.BlockSpec(memory_space=pl.ANY),
            pl.BlockSpec(gate_tab.shape, lambda i, *_: (0, 0)),
            pl.BlockSpec((1, D, 2 * D_EXPERT), lambda i, elo, ehi, nv, tok: (elo[i], 0, 0)),
            pl.BlockSpec((1, D_EXPERT, D), lambda i, elo, ehi, nv, tok: (elo[i], 0, 0)),
            pl.BlockSpec((1, D, 2 * D_EXPERT), lambda i, elo, ehi, nv, tok: (ehi[i], 0, 0)),
            pl.BlockSpec((1, D_EXPERT, D), lambda i, elo, ehi, nv, tok: (ehi[i], 0, 0)),
        ],
        out_specs=pl.BlockSpec(memory_space=pl.ANY),
        scratch_shapes=[pltpu.VMEM((2, MOE_ROWS // SUBLANES, SUBLANES, rec_w), F32),
                        pltpu.VMEM((2, MOE_ROWS // SUBLANES, SUBLANES, D), F32),
                        pltpu.SMEM((nb * MOE_ROWS,), I32),
                        pltpu.SemaphoreType.DMA((2,)), pltpu.SemaphoreType.DMA((2,))],
    )
    return pl.pallas_call(
        _moe_pairs_kernel,
        grid_spec=grid_spec,
        out_shape=jax.ShapeDtypeStruct((T, D), F32),
        compiler_params=_params(1),
    )(blk_elo, blk_ehi, blk_nv, dest, rec, gate_tab, wgu, wd, wgu, wd)


def _small_lookup(table, idx):
    ids = jnp.arange(table.shape[0], dtype=I32)
    return jnp.sum(jnp.where(idx[:, None] == ids[None, :], table[None, :], 0), axis=1)


def _pair_tables():
    lo, hi = [], []
    for g in range(N_GROUPS):
        for a in range(EXPERTS_PER_GROUP):
            for b in range(a + 1, EXPERTS_PER_GROUP):
                lo.append(g * EXPERTS_PER_GROUP + a)
                hi.append(g * EXPERTS_PER_GROUP + b)
    return np.asarray(lo, np.int32), np.asarray(hi, np.int32)


def _moe_prompt(rec, cls, rank, counts, mod, lw):
    B, S, rec_w = rec.shape
    D = mod.shape[-1]
    T = B * S
    nb = T // MOE_ROWS + N_CLASSES
    nblk = (counts + MOE_ROWS - 1) // MOE_ROWS
    blk_end = jnp.cumsum(nblk)
    blk_start = blk_end - nblk
    dest = _small_lookup(blk_start, cls) * MOE_ROWS + rank
    ids = jnp.arange(nb, dtype=I32)
    used = blk_end[-1]
    class_of = lambda blk: jnp.sum((blk_end[None, :] <= blk[:, None]).astype(I32), axis=1)
    blk_cls = jnp.where(ids < used, jnp.minimum(class_of(ids), N_CLASSES - 1), class_of(used[None] - 1))
    first_row = (ids - _small_lookup(blk_start, blk_cls)) * MOE_ROWS
    blk_nv = jnp.where(ids < used, jnp.clip(_small_lookup(counts, blk_cls) - first_row, 0, MOE_ROWS), 0)
    lo_tab, hi_tab = _pair_tables()
    blk_elo = _small_lookup(jnp.asarray(lo_tab), blk_cls)
    blk_ehi = _small_lookup(jnp.asarray(hi_tab), blk_cls)
    gate_hi, gate_lo = _split_bf16(jnp.pad(mod[:, 5, :], ((0, MOD_BATCH_PAD - B), (0, 0))))
    gate_tab = jnp.concatenate([gate_hi, gate_lo], axis=0)
    y = _moe_pairs(rec.reshape(T, rec_w), blk_elo, blk_ehi, blk_nv.astype(I32), dest.astype(I32), gate_tab, lw)
    return y.reshape(B, S, D)


def _moe_dense_kernel(h_ref, x1_ref, gate_ref, w_ref, sel_ref, wgu_ref, wd_ref, y_ref):
    e = pl.program_id(0)

    @pl.when(e == 0)
    def _():
        y_ref[...] = jnp.zeros_like(y_ref)

    ye = _swiglu(h_ref[...], wgu_ref[0], wd_ref[0])
    y_ref[...] += jnp.where(sel_ref[0] > 0.5, w_ref[0] * ye, 0.0)

    @pl.when(e == pl.num_programs(0) - 1)
    def _():
        y_ref[...] = x1_ref[...] + gate_ref[...] * y_ref[...]


def _moe_dense(h2, x1, gate_rows, w_sel, sel, lw):
    T, D = x1.shape
    per_e = lambda e: (e, 0, 0)
    return pl.pallas_call(
        _moe_dense_kernel,
        grid=(N_EXPERTS,),
        in_specs=[_full((T, D)), _full((T, D)), _full((T, D)),
                  pl.BlockSpec((1, T, 1), per_e), pl.BlockSpec((1, T, 1), per_e),
                  pl.BlockSpec((1, D, 2 * D_EXPERT), per_e), pl.BlockSpec((1, D_EXPERT, D), per_e)],
        out_specs=_full((T, D)),
        out_shape=jax.ShapeDtypeStruct((T, D), F32),
        compiler_params=_params(1),
    )(h2, x1, gate_rows, w_sel, sel, lw["w_gu_e"], lw["w_d_e"])


def _rope_tables(pos, gain, scale):
    inv = 1.0 / (ROPE_BASE ** (jnp.arange(ROPE_HALF, dtype=F32) / ROPE_HALF))
    ang = pos.astype(F32)[:, None] * inv[None, :]
    cos, sin = jnp.cos(ang) * scale, jnp.sin(ang) * scale
    n = pos.shape[0]
    z = lambda w: jnp.zeros((n, w), F32)
    pad = HEAD_PAD - QK_DIM
    g_nope, g1, g2 = gain[:NOPE_DIM], gain[NOPE_DIM:NOPE_DIM + ROPE_HALF], gain[NOPE_DIM + ROPE_HALF:]
    tab_c = jnp.concatenate([jnp.broadcast_to(g_nope * scale, (n, NOPE_DIM)), cos * g1, cos * g2, z(pad)], axis=1)
    tab_s = jnp.concatenate([z(NOPE_DIM), -sin * g2, sin * g1, z(pad)], axis=1)
    return tab_c, tab_s


def _prep_layer(w, l, chunk_lens):
    D = w["w_in"].shape[1]
    H = MLA_HEADS
    lw = {}
    row = lambda name: w[name][l].reshape(1, -1)
    for name in ("g_norm_mix", "g_q_lat", "g_kv_lat", "g_ln_v", "b_ln_v", "g_out_attn", "g_out_gmlp", "g_norm_ffn"):
        lw[name] = row(name)
    w_in = w["w_in"][l]
    o1, o2, o3 = Q_LORA, Q_LORA + KV_LORA, Q_LORA + KV_LORA + ROPE_DIM
    o4 = o3 + GM_WIDTH
    z32 = jnp.zeros((D, ROPE_DIM), F32)
    wr = w_in[:, o2:o3]
    lw["w_in_p"] = jnp.concatenate([w_in[:, :o2], wr, z32, wr, z32, w_in[:, o3:o4], w_in[:, o4:]], axis=1).astype(BF16)
    assert lw["w_in_p"].shape[1] == PROJ_COLS
    pad = HEAD_PAD - QK_DIM
    w_uq = w["w_uq"][l].reshape(Q_LORA, H, QK_DIM)
    w_uq_swapped = jnp.concatenate([jnp.zeros((Q_LORA, H, NOPE_DIM), F32), w_uq[:, :, NOPE_DIM + ROPE_HALF:],
                                    w_uq[:, :, NOPE_DIM:NOPE_DIM + ROPE_HALF]], axis=2)
    head_pad = lambda a: jnp.pad(a, ((0, 0), (0, 0), (0, pad))).reshape(Q_LORA, H * HEAD_PAD)
    lw["w_uq_p"] = jnp.concatenate([head_pad(w_uq), head_pad(w_uq_swapped)], axis=1).astype(BF16)
    w_ukv = w["w_ukv"][l].reshape(KV_LORA, H, NOPE_DIM + V_DIM)
    lw["w_k_p"] = jnp.pad(w_ukv[:, :, :NOPE_DIM], ((0, 0), (0, 0), (0, HEAD_PAD - NOPE_DIM))).reshape(KV_LORA, H * HEAD_PAD).astype(BF16)
    lw["w_v_p"] = jnp.pad(w_ukv[:, :, NOPE_DIM:], ((0, 0), (0, 0), (0, HEAD_PAD - V_DIM))).reshape(KV_LORA, H * HEAD_PAD).astype(BF16)
    lw["scores_bounded"] = _scores_bounded(w, l)
    lw["g_qnorm"] = w["g_qnorm"][l]
    lw["g_knorm"] = w["g_knorm"][l]
    for L in chunk_lens:
        wsp = w["w_spatial"][l][:, :L, :L]
        lw["w_sp_pairs_%d" % L] = wsp.reshape(GM_HEADS // 2, 2 * L, L).astype(BF16)
        lw["b_sp_rows_%d" % L] = jnp.repeat(jnp.transpose(w["b_spatial"][l][:, :L]), GM_HEAD_DIM, axis=1)
    w_out = w["w_out"][l].astype(BF16)
    lw["w_out_a"], lw["w_out_g"] = w_out[:ATT_WIDTH], w_out[ATT_WIDTH:]
    wr_full = jnp.zeros((D, ROUTER_COLS), F32)
    wr_full = wr_full.at[:, :N_GROUPS].set(w["w_router_group"][l])
    wr_full = wr_full.at[:, ROUTER_EXPERT_LANE0:ROUTER_EXPERT_LANE0 + N_EXPERTS].set(w["w_router_expert"][l])
    r_hi, r_lo = _split_bf16(wr_full)
    lw["w_r_a"] = jnp.concatenate([r_hi, r_lo], axis=1)
    lw["w_r_b"] = r_hi
    br = jnp.zeros((1, ROUTER_COLS), F32)
    br = br.at[0, :N_GROUPS].set(w["b_router_group"][l])
    lw["b_r"] = br.at[0, ROUTER_EXPERT_LANE0:ROUTER_EXPERT_LANE0 + N_EXPERTS].set(w["b_router_expert"][l])
    lw["w_gu_e"] = jnp.concatenate([w["w_gate_e"][l], w["w_up_e"][l]], axis=-1).astype(BF16)
    lw["w_d_e"] = w["w_down_e"][l].astype(BF16)
    return lw


def _tiles(seq):
    tm = min(seq, 512)
    return tm, min(seq, 512)


def _layer_prompt(x, mod, lw):
    B, S, D = x.shape
    tm, tq = _tiles(S)
    pos = jnp.arange(S)
    q_tabs = _rope_tables(pos, lw["g_qnorm"], QK_DIM ** -0.5 * LOG2E)
    k_tabs = _rope_tables(pos, lw["g_knorm"], 1.0)
    ckv, krope, q, gm, k, v = _mix_in(x, mod, lw, q_tabs, k_tabs, tm=tm, chunk_len=GM_CHUNK, emit_kv=True, emit_vrows=False)
    att = _attention(lw["scores_bounded"], q, k, v, tq=tq, tk=tq, q_off=0, kv_valid=S)
    rec, ri, cnt = _out_route(x, att, gm, mod, lw, tm=tm, pack_rows=True)
    cls = ri[:, 0, :].reshape(B * S)
    rank = ri[:, 1, :].reshape(B * S)
    counts = cnt[:N_CLASSES, 0].astype(I32)
    y = _moe_prompt(rec, cls, rank, counts, mod, lw)
    return y, ckv, krope


def _layer_sample(x, mod, past_ckv, past_krope, lw):
    B, S, D = x.shape
    past = past_ckv.shape[1]
    q_tabs = _rope_tables(past + jnp.arange(S), lw["g_qnorm"], QK_DIM ** -0.5 * LOG2E)
    ckv, krope, q, gm, v_rows = _mix_in(x, mod, lw, q_tabs, q_tabs, tm=S, chunk_len=S, emit_kv=False, emit_vrows=True)
    kv_valid = past + S
    kv_pad = -(-kv_valid // LANES) * LANES
    extra = kv_pad - kv_valid
    ckv_all = jnp.concatenate([past_ckv, ckv, jnp.zeros((B, extra, KV_LORA), F32)], axis=1)
    kr_all = jnp.concatenate([past_krope, krope, jnp.zeros((B, extra, ROPE_DIM), F32)], axis=1)
    kr_slot = jnp.pad(kr_all, ((0, 0), (0, 0), (NOPE_DIM, HEAD_PAD - QK_DIM)))
    k_tabs = _rope_tables(jnp.arange(kv_pad), lw["g_knorm"], 1.0)
    k, v = _kv_latent(ckv_all, kr_slot, lw, k_tabs, tr=kv_pad)
    att = _attention(lw["scores_bounded"], q, k, v, tq=S, tk=kv_pad, q_off=past, kv_valid=kv_valid)
    x1, h2, ri, rf, _ = _out_route(x, att, gm, mod, lw, tm=S, pack_rows=False)
    T = B * S
    cls, w_lo, w_hi = ri[:, 0, :].reshape(T), rf[:, 0, :].reshape(T), rf[:, 1, :].reshape(T)
    lo_tab, hi_tab = _pair_tables()
    e_lo, e_hi = jnp.asarray(lo_tab)[cls], jnp.asarray(hi_tab)[cls]
    eids = jnp.arange(N_EXPERTS, dtype=I32)[:, None]
    is_lo, is_hi = eids == e_lo[None, :], eids == e_hi[None, :]
    w_sel = (jnp.where(is_lo, w_lo[None, :], 0.0) + jnp.where(is_hi, w_hi[None, :], 0.0))[:, :, None]
    sel = (is_lo | is_hi).astype(F32)[:, :, None]
    gate_rows = jnp.repeat(mod[:, 5, :], S, axis=0)
    y = _moe_dense(h2.reshape(T, D), x1.reshape(T, D), gate_rows, w_sel, sel, lw)
    return y.reshape(B, S, D), ckv, krope, v_rows


def kernel(x_prompt, x_sample, cache_ckv, cache_krope, c_prompt, c_sample, w_ada, b_ada, g_norm_mix, w_in, g_q_lat, w_uq, g_kv_lat, w_ukv, g_qnorm, g_knorm, g_ln_v, b_ln_v, w_spatial, b_spatial, g_out_attn, g_out_gmlp, w_out, g_norm_ffn, w_router_group, b_router_group, w_router_expert, b_router_expert, w_gate_e, w_up_e, w_down_e):
    w = dict(w_in=w_in, g_norm_mix=g_norm_mix, g_q_lat=g_q_lat, w_uq=w_uq, g_kv_lat=g_kv_lat, w_ukv=w_ukv,
             g_qnorm=g_qnorm, g_knorm=g_knorm, g_ln_v=g_ln_v, b_ln_v=b_ln_v, w_spatial=w_spatial, b_spatial=b_spatial,
             g_out_attn=g_out_attn, g_out_gmlp=g_out_gmlp, w_out=w_out, g_norm_ffn=g_norm_ffn,
             w_router_group=w_router_group, b_router_group=b_router_group, w_router_expert=w_router_expert,
             b_router_expert=b_router_expert, w_gate_e=w_gate_e, w_up_e=w_up_e, w_down_e=w_down_e)
    depth = w_ada.shape[0]
    Bp, Sp, D = x_prompt.shape
    Bs, Ss, _ = x_sample.shape
    assert Sp % GM_CHUNK == 0 and Ss <= GM_CHUNK and Ss % CHUNK == 0 and Bp <= MOD_BATCH_PAD
    c_all = jnp.concatenate([c_prompt, c_sample], axis=0)
    y_p, y_s = x_prompt, x_sample
    outs = [[] for _ in range(5)]
    for l in range(depth):
        lw = _prep_layer(w, l, (GM_CHUNK, Ss))
        mod = _ada_mod(c_all, w_ada[l], b_ada[l]).reshape(Bp + Bs, 6, D)
        y_p, ckv_p, kr_p = _layer_prompt(y_p, mod[:Bp], lw)
        y_s, ckv_s, kr_s, v_s = _layer_sample(y_s, mod[Bp:], cache_ckv[l], cache_krope[l], lw)
        for lst, val in zip(outs, (ckv_p, kr_p, ckv_s, kr_s, v_s)):
            lst.append(val)
    return (y_p, y_s) + tuple(jnp.stack(lst) for lst in outs)
```

```python
import functools

import numpy as np
import jax
import jax.numpy as jnp
from jax import lax
from jax.experimental import pallas as pl
from jax.experimental.pallas import tpu as pltpu

F32 = jnp.float32
BF16 = jnp.bfloat16
I32 = jnp.int32

CHUNK = 64
CHUNK_SHIFT = 6
EPS = 1e-6
MLA_HEADS = 8
Q_LORA = 256
KV_LORA = 128
NOPE_DIM = 64
ROPE_DIM = 32
ROPE_HALF = ROPE_DIM // 2
V_DIM = 64
QK_DIM = NOPE_DIM + ROPE_DIM
ATT_WIDTH = MLA_HEADS * V_DIM
ROPE_BASE = 10000.0
GM_HEADS = 8
GM_HEAD_DIM = 64
GM_WIDTH = GM_HEADS * GM_HEAD_DIM
GM_CHUNK = 128
N_GROUPS = 4
EXPERTS_PER_GROUP = 8
N_EXPERTS = N_GROUPS * EXPERTS_PER_GROUP
D_EXPERT = 256
PAIRS_PER_GROUP = EXPERTS_PER_GROUP * (EXPERTS_PER_GROUP - 1) // 2
N_CLASSES = N_GROUPS * PAIRS_PER_GROUP

LANES = 128
SUBLANES = 8
HEAD_PAD = LANES
PROJ_COLS = 1536
ROUTER_COLS = LANES
ROUTER_EXPERT_LANE0 = SUBLANES
CLASS_ROWS = LANES
MOE_ROWS = 128
MOD_BATCH_PAD = 16
VMEM_LIMIT = 48 * 1024 * 1024
NEG_BIG = -1e30
LOG2E = 1.4426950408889634
BF16_SLACK = 1.02
SCORE_BOUND = 90.0

assert CHUNK == 1 << CHUNK_SHIFT


def _params(n_axes, vmem=VMEM_LIMIT):
    return pltpu.CompilerParams(dimension_semantics=("arbitrary",) * n_axes, vmem_limit_bytes=vmem)


def _full(shape):
    nd = len(shape)
    return pl.BlockSpec(shape, lambda *_: (0,) * nd)


def _split_bf16(x):
    hi = x.astype(BF16)
    lo = (x - hi.astype(F32)).astype(BF16)
    return hi, lo


def _dot(a, b):
    return jnp.dot(a, b, preferred_element_type=F32)


def _ada_kernel(c_ref, w_ref, b_ref, o_ref):
    a_hi, a_lo = _split_bf16(jax.nn.silu(c_ref[...]))
    w_hi, w_lo = _split_bf16(w_ref[...])
    o_ref[...] = _dot(a_hi, w_hi) + _dot(a_lo, w_hi) + _dot(a_hi, w_lo) + b_ref[...]


def _ada_mod(c, w_ada, b_ada):
    n, d = c.shape
    cols = w_ada.shape[1]
    tn = 1536
    return pl.pallas_call(
        _ada_kernel,
        grid=(cols // tn,),
        in_specs=[_full((n, d)), pl.BlockSpec((d, tn), lambda j: (0, j)), pl.BlockSpec((1, tn), lambda j: (0, j))],
        out_specs=pl.BlockSpec((n, tn), lambda j: (0, j)),
        out_shape=jax.ShapeDtypeStruct((n, cols), F32),
        compiler_params=_params(1),
    )(c, w_ada, b_ada.reshape(1, cols))


def _rms(x, g):
    return x * lax.rsqrt(jnp.mean(x * x, axis=-1, keepdims=True) + EPS) * g


def _head_norm_rope(xh, xh_swapped, tab_c, tab_s):
    ms = jnp.sum(xh * xh, axis=-1, keepdims=True) * (1.0 / QK_DIM)
    return (xh * tab_c + xh_swapped * tab_s) * lax.rsqrt(ms + EPS)


def _keys_values(ckv, kr_slot, gkv, wk, wv, tab_c, tab_s, k_ref, v_ref):
    cb = _rms(ckv, gkv).astype(BF16)
    kall = _dot(cb, wk)
    vall = _dot(cb, wv)
    lane = lax.broadcasted_iota(I32, (1, LANES), 1)
    one_col = jnp.where(lane == V_DIM, 1.0, 0.0)
    kr_swapped = jnp.where(lane < NOPE_DIM + ROPE_HALF, pltpu.roll(kr_slot, HEAD_PAD - ROPE_HALF, 1),
                           pltpu.roll(kr_slot, ROPE_HALF, 1))
    kr_swapped = jnp.where((lane >= NOPE_DIM) & (lane < QK_DIM), kr_swapped, 0.0)
    for h in range(MLA_HEADS):
        kh = kall[:, h * HEAD_PAD:(h + 1) * HEAD_PAD] + kr_slot
        k_ref[0, h] = _head_norm_rope(kh, kr_swapped, tab_c, tab_s).astype(BF16)
        v_ref[0, h] = (vall[:, h * HEAD_PAD:(h + 1) * HEAD_PAD] + one_col).astype(BF16)


def _mix_in_kernel(x_ref, mod_ref, gmix_ref, win_ref, gql_ref, wuq_ref, gkv_ref, wk_ref, wv_ref,
                   cq_ref, sq_ref, ck_ref, sk_ref, glnv_ref, blnv_ref, wsp_ref, bsp_ref,
                   ggm_ref, *rest, chunk_len, emit_kv, emit_vrows):
    outs = list(rest[:-1])
    mixed_scr = rest[-1]
    ckv_ref, kr_ref, q_ref, gm_ref = outs[:4]
    outs = outs[4:]
    if emit_kv:
        k_ref, v_ref = outs[:2]
        outs = outs[2:]
    if emit_vrows:
        vrows_ref = outs[0]

    x = x_ref[0]
    tm = x.shape[0]
    shift, scale = mod_ref[0, 0:1, :], mod_ref[0, 1:2, :]
    h = _rms(x, gmix_ref[...]) * (1.0 + scale) + shift
    proj = _dot(h.astype(BF16), win_ref[...])

    q_lat = proj[:, 0:Q_LORA]
    ckv = proj[:, Q_LORA:Q_LORA + KV_LORA]
    kr_blk = proj[:, Q_LORA + KV_LORA:Q_LORA + KV_LORA + LANES]
    ckv_ref[0] = ckv
    kr_ref[0] = kr_blk[:, 0:ROPE_DIM]

    q = _dot(_rms(q_lat, gql_ref[...]).astype(BF16), wuq_ref[...])
    cq, sq = cq_ref[...], sq_ref[...]
    n_q = MLA_HEADS * HEAD_PAD
    for hd in range(MLA_HEADS):
        qh = q[:, hd * HEAD_PAD:(hd + 1) * HEAD_PAD]
        qh_swapped = q[:, n_q + hd * HEAD_PAD:n_q + (hd + 1) * HEAD_PAD]
        q_ref[0, hd] = _head_norm_rope(qh, qh_swapped, cq, sq).astype(BF16)

    if emit_kv:
        lane = lax.broadcasted_iota(I32, (1, LANES), 1)
        kr_slot = jnp.where(lane >= NOPE_DIM, kr_blk, 0.0)
        _keys_values(ckv, kr_slot, gkv_ref[...], wk_ref[...], wv_ref[...], ck_ref[...], sk_ref[...], k_ref, v_ref)

    g_u = proj[:, 512:512 + GM_WIDTH]
    g_v = proj[:, 1024:1024 + GM_WIDTH]
    u = jax.nn.gelu(g_u)
    gv = jax.nn.gelu(g_v)
    mu = jnp.mean(gv, axis=-1, keepdims=True)
    xc = gv - mu
    var = jnp.mean(xc * xc, axis=-1, keepdims=True)
    v_rows = xc * lax.rsqrt(var + EPS) * glnv_ref[...] + blnv_ref[...]
    if emit_vrows:
        vrows_ref[0] = v_rows
    vb = v_rows.astype(BF16)

    L = chunk_len
    t = lax.broadcasted_iota(I32, (2 * L, L), 0)
    s = lax.broadcasted_iota(I32, (2 * L, L), 1)
    t = jnp.where(t >= L, t - L, t)
    allowed = (s >> CHUNK_SHIFT) <= (t >> CHUNK_SHIFT)
    lane = lax.broadcasted_iota(I32, (1, LANES), 1)
    first_head = lane < GM_HEAD_DIM
    for p in range(GM_HEADS // 2):
        w_pair = jnp.where(allowed, wsp_ref[p], jnp.zeros((), BF16))
        for c in range(tm // L):
            vp = vb[c * L:(c + 1) * L, p * LANES:(p + 1) * LANES]
            r = _dot(w_pair, vp)
            mixed = jnp.where(first_head, r[:L], r[L:])
            mixed_scr[c * L:(c + 1) * L, p * LANES:(p + 1) * LANES] = mixed + bsp_ref[:, p * LANES:(p + 1) * LANES]
    gm = u * mixed_scr[...]
    gm_ref[0] = _rms(gm, ggm_ref[...]).astype(BF16)


def _mix_in(x, mod, lw, q_tabs, k_tabs, *, tm, chunk_len, emit_kv, emit_vrows):
    B, S, D = x.shape
    nt = S // tm
    H = MLA_HEADS
    row = lambda b, i: (b, i, 0)
    tab = pl.BlockSpec((tm, LANES), lambda b, i: (i, 0))
    in_specs = [
        pl.BlockSpec((1, tm, D), row),
        pl.BlockSpec((1, 6, D), lambda b, i: (b, 0, 0)),
        _full((1, D)), _full((D, PROJ_COLS)), _full((1, Q_LORA)), _full((Q_LORA, 2 * H * HEAD_PAD)),
        _full((1, KV_LORA)), _full((KV_LORA, H * HEAD_PAD)), _full((KV_LORA, H * HEAD_PAD)),
        tab, tab, tab, tab,
        _full((1, GM_WIDTH)), _full((1, GM_WIDTH)),
        _full((GM_HEADS // 2, 2 * chunk_len, chunk_len)), _full((chunk_len, GM_WIDTH)), _full((1, GM_WIDTH)),
    ]
    out_shape = [
        jax.ShapeDtypeStruct((B, S, KV_LORA), F32),
        jax.ShapeDtypeStruct((B, S, ROPE_DIM), F32),
        jax.ShapeDtypeStruct((B, H, S, HEAD_PAD), BF16),
        jax.ShapeDtypeStruct((B, S, GM_WIDTH), BF16),
    ]
    head_blk = pl.BlockSpec((1, H, tm, HEAD_PAD), lambda b, i: (b, 0, i, 0))
    out_specs = [
        pl.BlockSpec((1, tm, KV_LORA), row),
        pl.BlockSpec((1, tm, ROPE_DIM), row),
        head_blk,
        pl.BlockSpec((1, tm, GM_WIDTH), row),
    ]
    if emit_kv:
        out_shape += [jax.ShapeDtypeStruct((B, H, S, HEAD_PAD), BF16), jax.ShapeDtypeStruct((B, H, S, HEAD_PAD), BF16)]
        out_specs += [head_blk, head_blk]
    if emit_vrows:
        out_shape += [jax.ShapeDtypeStruct((B, S, GM_WIDTH), F32)]
        out_specs += [pl.BlockSpec((1, tm, GM_WIDTH), row)]
    kern = functools.partial(_mix_in_kernel, chunk_len=chunk_len, emit_kv=emit_kv, emit_vrows=emit_vrows)
    return pl.pallas_call(
        kern,
        grid=(B, nt),
        in_specs=in_specs,
        out_specs=out_specs,
        out_shape=out_shape,
        scratch_shapes=[pltpu.VMEM((tm, GM_WIDTH), F32)],
        compiler_params=_params(2),
    )(x, mod, lw["g_norm_mix"], lw["w_in_p"], lw["g_q_lat"], lw["w_uq_p"], lw["g_kv_lat"], lw["w_k_p"], lw["w_v_p"],
      *q_tabs, *k_tabs, lw["g_ln_v"], lw["b_ln_v"],
      lw["w_sp_pairs_%d" % chunk_len], lw["b_sp_rows_%d" % chunk_len], lw["g_out_gmlp"])


def _kv_latent_kernel(ckv_ref, kr_ref, gkv_ref, wk_ref, wv_ref, c_ref, s_ref, k_ref, v_ref):
    _keys_values(ckv_ref[0], kr_ref[0], gkv_ref[...], wk_ref[...], wv_ref[...], c_ref[...], s_ref[...], k_ref, v_ref)


def _kv_latent(ckv_all, kr_slot_all, lw, k_tabs, *, tr):
    B, K, _ = ckv_all.shape
    H = MLA_HEADS
    row = lambda b, i: (b, i, 0)
    tab = pl.BlockSpec((tr, LANES), lambda b, i: (i, 0))
    return pl.pallas_call(
        _kv_latent_kernel,
        grid=(B, K // tr),
        in_specs=[pl.BlockSpec((1, tr, KV_LORA), row), pl.BlockSpec((1, tr, LANES), row),
                  _full((1, KV_LORA)), _full((KV_LORA, H * HEAD_PAD)), _full((KV_LORA, H * HEAD_PAD)),
                  tab, tab],
        out_specs=[pl.BlockSpec((1, H, tr, HEAD_PAD), lambda b, i: (b, 0, i, 0))] * 2,
        out_shape=[jax.ShapeDtypeStruct((B, H, K, HEAD_PAD), BF16)] * 2,
        compiler_params=_params(2),
    )(ckv_all, kr_slot_all, lw["g_kv_lat"], lw["w_k_p"], lw["w_v_p"], *k_tabs)


def _attn_kernel(bounded_ref, q_ref, k_ref, v_ref, o_ref, *, tq, tk, n_q, q_off, kv_valid):
    i = pl.program_id(2) if n_q > 1 else 0
    q_first = q_off + i * tq
    vis_first = jnp.minimum(((q_first >> CHUNK_SHIFT) + 1) << CHUNK_SHIFT, kv_valid)
    vis_last = jnp.minimum((((q_first + tq - 1) >> CHUNK_SHIFT) + 1) << CHUNK_SHIFT, kv_valid)
    n_unmasked = vis_first // tk
    n_total = (vis_last + tk - 1) // tk

    q_pos = q_first + lax.broadcasted_iota(I32, (tq, 1), 0)
    limit = jnp.minimum(((q_pos >> CHUNK_SHIFT) + 1) << CHUNK_SHIFT, kv_valid)
    lane = lax.broadcasted_iota(I32, (1, LANES), 1)

    def scores(j, start, masked):
        s = lax.dot_general(q_ref[0, j], k_ref[0, j, pl.ds(start, tk), :], (((1,), (1,)), ((), ())),
                            preferred_element_type=F32)
        if masked:
            k_pos = start + lax.broadcasted_iota(I32, (1, tk), 1)
            s = jnp.where(k_pos < limit, s, NEG_BIG)
        return s

    def plain_block(kb, carry, masked):
        start = pl.multiple_of(kb * tk, tk)
        new = []
        for j in range(2):
            p = jnp.exp2(scores(j, start, masked)).astype(BF16)
            new.append(carry[j] + _dot(p, v_ref[0, j, pl.ds(start, tk), :]))
        return tuple(new)

    def online_block(kb, carry, masked):
        start = pl.multiple_of(kb * tk, tk)
        new = []
        for j in range(2):
            m, acc = carry[2 * j:2 * j + 2]
            s = scores(j, start, masked)
            m_new = jnp.maximum(m, jnp.max(s, axis=-1, keepdims=True))
            p = jnp.exp2(s - m_new).astype(BF16)
            acc = jnp.exp2(m - m_new) * acc + _dot(p, v_ref[0, j, pl.ds(start, tk), :])
            new += [m_new, acc]
        return tuple(new)

    def run(block, init):
        n_pairs = n_unmasked // 2
        carry = lax.fori_loop(0, n_pairs, lambda p, c: block(2 * p + 1, block(2 * p, c, False), False), tuple(init))
        carry = lax.fori_loop(2 * n_pairs, n_unmasked, lambda kb, c: block(kb, c, False), carry)
        return lax.fori_loop(n_unmasked, n_total, lambda kb, c: block(kb, c, True), carry)

    def finish(acc0, acc1):
        outs = []
        for acc in (acc0, acc1):
            denom = jnp.sum(jnp.where(lane == V_DIM, acc, 0.0), axis=-1, keepdims=True)
            outs.append(acc / denom)
        o_ref[0] = jnp.where(lane < V_DIM, outs[0], pltpu.roll(outs[1], V_DIM, 1)).astype(BF16)

    zeros = jnp.zeros((tq, LANES), F32)

    @pl.when(bounded_ref[0] == 1)
    def _():
        finish(*run(plain_block, [zeros, zeros]))

    @pl.when(bounded_ref[0] != 1)
    def _():
        m0 = jnp.full((tq, 1), NEG_BIG, F32)
        c = run(online_block, [m0, zeros, m0, zeros])
        finish(c[1], c[3])


def _attention(bounded, q, k, v, *, tq, tk, q_off, kv_valid):
    B, H, Sq, _ = q.shape
    Sk = k.shape[2]
    kern = functools.partial(_attn_kernel, tq=tq, tk=tk, n_q=Sq // tq, q_off=q_off, kv_valid=kv_valid)
    grid_spec = pltpu.PrefetchScalarGridSpec(
        num_scalar_prefetch=1,
        grid=(B, H // 2, Sq // tq),
        in_specs=[pl.BlockSpec((1, 2, tq, HEAD_PAD), lambda b, hp, i, f: (b, hp, i, 0)),
                  pl.BlockSpec((1, 2, Sk, HEAD_PAD), lambda b, hp, i, f: (b, hp, 0, 0)),
                  pl.BlockSpec((1, 2, Sk, HEAD_PAD), lambda b, hp, i, f: (b, hp, 0, 0))],
        out_specs=pl.BlockSpec((1, tq, LANES), lambda b, hp, i, f: (b, i, hp)),
    )
    return pl.pallas_call(
        kern,
        grid_spec=grid_spec,
        out_shape=jax.ShapeDtypeStruct((B, Sq, ATT_WIDTH), BF16),
        compiler_params=_params(3),
    )(bounded, q, k, v)


def _scores_bounded(w, l):
    gq = jnp.max(jnp.abs(w["g_qnorm"][l]))
    gk = jnp.max(jnp.abs(w["g_knorm"][l]))
    bound = (QK_DIM ** 0.5) * LOG2E * BF16_SLACK * gq * gk
    return (bound <= SCORE_BOUND).astype(I32).reshape(1)


def _pack_bf16_pairs(h):
    n = h.shape[1] // 2
    hi = pltpu.bitcast(h[:, :n].astype(BF16).astype(F32), jnp.uint32)
    lo = pltpu.bitcast(h[:, n:].astype(BF16).astype(F32), jnp.uint32)
    return pltpu.bitcast(hi | (lo >> 16), F32)


def _unpack_bf16_pairs(words):
    w = pltpu.bitcast(words, jnp.uint32)
    hi = pltpu.bitcast(w & jnp.uint32(0xFFFF0000), F32)
    lo = pltpu.bitcast(w << 16, F32)
    return jnp.concatenate([hi, lo], axis=1).astype(BF16)


def _out_route_kernel(x_ref, att_ref, gm_ref, mod_ref, goa_ref, woa_ref, wog_ref, gffn_ref, wra_ref, wrb_ref, br_ref,
                      *rest, pack_rows):
    carry_scr = rest[-1]
    if pack_rows:
        rec_ref, ri_ref, cnt_ref = rest[:3]
    else:
        x1_ref, h2_ref, ri_ref, rf_ref, cnt_ref = rest[:5]
    first_step = (pl.program_id(0) == 0) & (pl.program_id(1) == 0)

    @pl.when(first_step)
    def _():
        carry_scr[...] = jnp.zeros_like(carry_scr)

    x = x_ref[0]
    tm, D = x.shape
    gate_a = mod_ref[0, 2:3, :]
    shift_m, scale_m = mod_ref[0, 3:4, :], mod_ref[0, 4:5, :]
    att_n = _rms(att_ref[0].astype(F32), goa_ref[...]).astype(BF16)
    mix = _dot(att_n, woa_ref[...]) + _dot(gm_ref[0], wog_ref[...])
    x1 = x + gate_a * mix
    h2 = _rms(x1, gffn_ref[...]) * (1.0 + scale_m) + shift_m
    if pack_rows:
        rec_ref[0, :, 0:D] = x1
        rec_ref[0, :, D:D + D // 2] = _pack_bf16_pairs(h2)
    else:
        x1_ref[0] = x1
        h2_ref[0] = h2.astype(BF16)

    h_hi, h_lo = _split_bf16(h2)
    la = _dot(h_hi, wra_ref[...])
    logits = la[:, :ROUTER_COLS] + la[:, ROUTER_COLS:] + _dot(h_lo, wrb_ref[...]) + br_ref[...]
    lt = logits.T

    g = [lt[r:r + 1] for r in range(N_GROUPS)]
    gmax = jnp.maximum(jnp.maximum(g[0], g[1]), jnp.maximum(g[2], g[3]))
    gsum = sum(jnp.exp(gr - gmax) for gr in g)
    g_prob = 1.0 / gsum
    g_idx = jnp.where(g[0] == gmax, 0.0, jnp.where(g[1] == gmax, 1.0, jnp.where(g[2] == gmax, 2.0, 3.0)))

    e0 = ROUTER_EXPERT_LANE0
    grp = [lt[e0 + EXPERTS_PER_GROUP * r:e0 + EXPERTS_PER_GROUP * (r + 1)] for r in range(N_GROUPS)]
    sel = jnp.where(g_idx == 0.0, grp[0], jnp.where(g_idx == 1.0, grp[1], jnp.where(g_idx == 2.0, grp[2], grp[3])))
    sub = lax.broadcasted_iota(I32, (EXPERTS_PER_GROUP, tm), 0).astype(F32)
    m1 = jnp.max(sel, axis=0, keepdims=True)
    i1 = jnp.min(jnp.where(sel == m1, sub, float(EXPERTS_PER_GROUP)), axis=0, keepdims=True)
    sel2 = jnp.where(sub == i1, -jnp.inf, sel)
    m2 = jnp.max(sel2, axis=0, keepdims=True)
    i2 = jnp.min(jnp.where(sel2 == m2, sub, float(EXPERTS_PER_GROUP)), axis=0, keepdims=True)
    d = jnp.exp(m2 - m1)
    w1 = g_prob / (1.0 + d)
    w2 = g_prob * d / (1.0 + d)
    first_lower = i1 < i2
    lo = jnp.minimum(i1, i2)
    hi = jnp.maximum(i1, i2)
    w_lo = jnp.where(first_lower, w1, w2)
    w_hi = jnp.where(first_lower, w2, w1)
    pair = lo * EXPERTS_PER_GROUP - lo * (lo + 1.0) * 0.5 + hi - lo - 1.0
    cls = g_idx * PAIRS_PER_GROUP + pair

    crow = lax.broadcasted_iota(I32, (CLASS_ROWS, tm), 0).astype(F32)
    onehot = jnp.where(crow == cls, 1.0, 0.0)
    ta = lax.broadcasted_iota(I32, (tm, tm), 0)
    tb = lax.broadcasted_iota(I32, (tm, tm), 1)
    earlier = jnp.where(ta < tb, 1.0, 0.0).astype(BF16)
    before = _dot(onehot.astype(BF16), earlier)
    carry = carry_scr[...]
    rank = jnp.sum(onehot * (before + carry[:, 0:1]), axis=0, keepdims=True)
    carry = carry + jnp.sum(onehot, axis=1, keepdims=True)
    carry_scr[...] = carry
    cnt_ref[...] = carry

    ri_ref[...] = jnp.zeros_like(ri_ref)
    ri_ref[0, 0:1, :] = cls.astype(I32)
    ri_ref[0, 1:2, :] = rank.astype(I32)
    if pack_rows:
        mrow = lax.broadcasted_iota(I32, (LANES, tm), 0)
        batch = pl.program_id(0).astype(F32)
        meta_t = jnp.where(mrow == 0, w_lo, jnp.where(mrow == 1, w_hi, jnp.where(mrow == 2, batch, 0.0)))
        rec_ref[0, :, D + D // 2:] = meta_t.T
    else:
        rf_ref[...] = jnp.zeros_like(rf_ref)
        rf_ref[0, 0:1, :] = w_lo
        rf_ref[0, 1:2, :] = w_hi


def _out_route(x, att, gm, mod, lw, *, tm, pack_rows):
    B, S, D = x.shape
    nt = S // tm
    row = lambda b, i: (b, i, 0)
    tile = lambda b, i: (b * nt + i, 0, 0)
    route_i = (jax.ShapeDtypeStruct((B * nt, SUBLANES, tm), I32), pl.BlockSpec((1, SUBLANES, tm), tile))
    route_f = (jax.ShapeDtypeStruct((B * nt, SUBLANES, tm), F32), pl.BlockSpec((1, SUBLANES, tm), tile))
    counts = (jax.ShapeDtypeStruct((CLASS_ROWS, LANES), F32), _full((CLASS_ROWS, LANES)))
    if pack_rows:
        rec_w = D + D // 2 + LANES
        outs = [(jax.ShapeDtypeStruct((B, S, rec_w), F32), pl.BlockSpec((1, tm, rec_w), row)), route_i, counts]
    else:
        outs = [(jax.ShapeDtypeStruct((B, S, D), F32), pl.BlockSpec((1, tm, D), row)),
                (jax.ShapeDtypeStruct((B, S, D), BF16), pl.BlockSpec((1, tm, D), row)), route_i, route_f, counts]
    out_shape = [o[0] for o in outs]
    out_specs = [o[1] for o in outs]
    return pl.pallas_call(
        functools.partial(_out_route_kernel, pack_rows=pack_rows),
        grid=(B, nt),
        in_specs=[pl.BlockSpec((1, tm, D), row), pl.BlockSpec((1, tm, ATT_WIDTH), row), pl.BlockSpec((1, tm, GM_WIDTH), row),
                  pl.BlockSpec((1, 6, D), lambda b, i: (b, 0, 0)),
                  _full((1, ATT_WIDTH)), _full((ATT_WIDTH, D)), _full((GM_WIDTH, D)), _full((1, D)),
                  _full((D, 2 * ROUTER_COLS)), _full((D, ROUTER_COLS)), _full((1, ROUTER_COLS))],
        out_specs=out_specs,
        out_shape=out_shape,
        scratch_shapes=[pltpu.VMEM((CLASS_ROWS, LANES), F32)],
        compiler_params=_params(2),
    )(x, att, gm, mod, lw["g_out_attn"], lw["w_out_a"], lw["w_out_g"], lw["g_norm_ffn"],
      lw["w_r_a"], lw["w_r_b"], lw["b_r"])


def _swiglu(hb, wgu, wd):
    hid = _dot(hb, wgu)
    act = jax.nn.silu(hid[:, :D_EXPERT]) * hid[:, D_EXPERT:]
    return _dot(act.astype(BF16), wd)


def _moe_pairs_kernel(elo_ref, ehi_ref, nv_ref, dest_ref, rec_hbm, zeros_hbm, gate_ref,
                      wgu_lo_ref, wd_lo_ref, wgu_hi_ref, wd_hi_ref, y_hbm,
                      xbuf0, xbuf1, ybuf0, ybuf1, res_scr, tok_ref, gsem, ssem, zsem):
    i = pl.program_id(0)
    xbuf, ybuf = (xbuf0, xbuf1), (ybuf0, ybuf1)
    D = ybuf0.shape[-1]
    n_groups_all = MOE_ROWS // SUBLANES

    def gather_start(t, s, g, u):
        pltpu.make_async_copy(rec_hbm.at[pl.ds(t, 1)], xbuf[s].at[g, pl.ds(u, 1)], gsem.at[s]).start()

    def scatter_start(t, s, g, u):
        pltpu.make_async_copy(ybuf[s].at[g, pl.ds(u, 1)], y_hbm.at[pl.ds(t, 1)], ssem.at[s]).start()

    def all_rows(blk, fn):
        base = blk * MOE_ROWS
        for g in range(n_groups_all):
            for u in range(SUBLANES):
                fn(tok_ref[base + g * SUBLANES + u], g, u)

    def valid_rows(blk, fn):
        nv = nv_ref[blk]
        base = blk * MOE_ROWS
        n_groups = nv // SUBLANES

        def group(g, c):
            for u in range(SUBLANES):
                fn(tok_ref[base + g * SUBLANES + u], g, u)
            return c

        def single(r, c):
            fn(tok_ref[base + r], n_groups, r - n_groups * SUBLANES)
            return c

        lax.fori_loop(0, n_groups, group, 0)
        lax.fori_loop(n_groups * SUBLANES, nv, single, 0)

    def gather_wait(s):
        pltpu.make_async_copy(xbuf[s], xbuf[s], gsem.at[s]).wait()

    def scatter_wait(blk, s):
        nv = nv_ref[blk]
        n_groups = nv // SUBLANES
        buf = ybuf[s]

        @pl.when(n_groups > 0)
        def _():
            rows = buf.at[pl.ds(0, n_groups)]
            pltpu.make_async_copy(rows, rows, ssem.at[s]).wait()

        def single(r, c):
            row = buf.at[0, pl.ds(0, 1)]
            pltpu.make_async_copy(row, row, ssem.at[s]).wait()
            return c

        lax.fori_loop(n_groups * SUBLANES, nv, single, 0)

    def experts(s, start_copies):
        start_copies()
        hb = _unpack_bf16_pairs(xbuf[s][:, :, D:D + D // 2].reshape(MOE_ROWS, D // 2))
        meta = xbuf[s][:, :, D + D // 2:].reshape(MOE_ROWS, LANES)
        x = xbuf[s][:, :, 0:D].reshape(MOE_ROWS, D)
        w_lo, w_hi, bidx = meta[:, 0:1], meta[:, 1:2], meta[:, 2:3]
        blane = lax.broadcasted_iota(I32, (1, 2 * MOD_BATCH_PAD), 1)
        blane = jnp.where(blane >= MOD_BATCH_PAD, blane - MOD_BATCH_PAD, blane).astype(F32)
        onehot = jnp.where(bidx == blane, 1.0, 0.0).astype(BF16)
        gate_m = _dot(onehot, gate_ref[...])
        moe = w_lo * _swiglu(hb, wgu_lo_ref[0], wd_lo_ref[0]) + w_hi * _swiglu(hb, wgu_hi_ref[0], wd_hi_ref[0])
        res_scr[...] = x + gate_m * moe

    @pl.when(i == 0)
    def _():
        zero = pltpu.make_async_copy(zeros_hbm, tok_ref, zsem)
        zero.start()
        zero.wait()

        def invert(g, c):
            for u in range(SUBLANES):
                t = g * SUBLANES + u
                tok_ref[dest_ref[t]] = t
            return c
        lax.fori_loop(0, dest_ref.shape[0] // SUBLANES, invert, 0)
        all_rows(0, lambda t, g, u: gather_start(t, 0, g, u))

    prev = jnp.maximum(i - 1, 0)
    has_rows = nv_ref[i] > 0
    prev_full = (i >= 1) & (nv_ref[prev] == MOE_ROWS)
    prev_partial = (i >= 1) & (nv_ref[prev] > 0) & (nv_ref[prev] < MOE_ROWS)

    def step(slot):
        other = 1 - slot

        @pl.when(has_rows)
        def _():
            gather_wait(slot)

            def start_gather():
                all_rows(i + 1, lambda t, g, u: gather_start(t, other, g, u))

            def start_scatter_and_gather():
                all_rows(i - 1, lambda t, g, u: scatter_start(t, other, g, u))
                start_gather()

            @pl.when(prev_full)
            def _():
                experts(slot, start_scatter_and_gather)

            @pl.when(jnp.logical_not(prev_full))
            def _():
                @pl.when(prev_partial)
                def _():
                    valid_rows(i - 1, lambda t, g, u: scatter_start(t, other, g, u))
                experts(slot, start_gather)

            @pl.when(i >= 2)
            def _():
                scatter_wait(i - 2, slot)
            ybuf[slot][...] = res_scr[...].reshape(n_groups_all, SUBLANES, D)

        @pl.when(jnp.logical_not(has_rows) & (i >= 1) & (nv_ref[prev] > 0))
        def _():
            gather_wait(slot)
            valid_rows(i - 1, lambda t, g, u: scatter_start(t, other, g, u))

            @pl.when(i >= 2)
            def _():
                scatter_wait(i - 2, slot)
            scatter_wait(i - 1, other)

    for parity in (0, 1):
        pl.when(i % 2 == parity)(functools.partial(step, parity))


def _moe_pairs(rec, blk_elo, blk_ehi, blk_nv, dest, gate_tab, lw):
    T, rec_w = rec.shape
    assert T % SUBLANES == 0
    D = gate_tab.shape[1]
    nb = blk_nv.shape[0]
    wgu, wd = lw["w_gu_e"], lw["w_d_e"]
    grid_spec = pltpu.PrefetchScalarGridSpec(
        num_scalar_prefetch=4,
        grid=(nb,),
        in_specs=[
            pl.BlockSpec(memory_space=pl.ANY),
            pl.BlockSpec(memory_space=pl.ANY),
            pl.BlockSpec(gate_tab.shape, lambda i, *_: (0, 0)),
            pl.BlockSpec((1, D, 2 * D_EXPERT), lambda i, elo, ehi, nv, tok: (elo[i], 0, 0)),
            pl.BlockSpec((1, D_EXPERT, D), lambda i, elo, ehi, nv, tok: (elo[i], 0, 0)),
            pl.BlockSpec((1, D, 2 * D_EXPERT), lambda i, elo, ehi, nv, tok: (ehi[i], 0, 0)),
            pl.BlockSpec((1, D_EXPERT, D), lambda i, elo, ehi, nv, tok: (ehi[i], 0, 0)),
        ],
        out_specs=pl.BlockSpec(memory_space=pl.ANY),
        scratch_shapes=[pltpu.VMEM((MOE_ROWS // SUBLANES, SUBLANES, rec_w), F32),
                        pltpu.VMEM((MOE_ROWS // SUBLANES, SUBLANES, rec_w), F32),
                        pltpu.VMEM((MOE_ROWS // SUBLANES, SUBLANES, D), F32),
                        pltpu.VMEM((MOE_ROWS // SUBLANES, SUBLANES, D), F32),
                        pltpu.VMEM((MOE_ROWS, D), F32),
                        pltpu.SMEM((nb * MOE_ROWS,), I32),
                        pltpu.SemaphoreType.DMA((2,)), pltpu.SemaphoreType.DMA((2,)), pltpu.SemaphoreType.DMA(())],
    )
    return pl.pallas_call(
        _moe_pairs_kernel,
        grid_spec=grid_spec,
        out_shape=jax.ShapeDtypeStruct((T, D), F32),
        compiler_params=_params(1),
    )(blk_elo, blk_ehi, blk_nv, dest, rec, jnp.zeros((nb * MOE_ROWS,), I32), gate_tab, wgu, wd, wgu, wd)


def _small_lookup(table, idx):
    ids = jnp.arange(table.shape[0], dtype=I32)
    return jnp.sum(jnp.where(idx[:, None] == ids[None, :], table[None, :], 0), axis=1)


def _pair_tables():
    lo, hi = [], []
    for g in range(N_GROUPS):
        for a in range(EXPERTS_PER_GROUP):
            for b in range(a + 1, EXPERTS_PER_GROUP):
                lo.append(g * EXPERTS_PER_GROUP + a)
                hi.append(g * EXPERTS_PER_GROUP + b)
    return np.asarray(lo, np.int32), np.asarray(hi, np.int32)


def _moe_prompt(rec, cls, rank, counts, mod, lw):
    B, S, rec_w = rec.shape
    D = mod.shape[-1]
    T = B * S
    nb = T // MOE_ROWS + N_CLASSES + 1
    nblk = (counts + MOE_ROWS - 1) // MOE_ROWS
    blk_end = jnp.cumsum(nblk)
    blk_start = blk_end - nblk
    dest = _small_lookup(blk_start, cls) * MOE_ROWS + rank
    ids = jnp.arange(nb, dtype=I32)
    used = blk_end[-1]
    class_of = lambda blk: jnp.sum((blk_end[None, :] <= blk[:, None]).astype(I32), axis=1)
    blk_cls = jnp.where(ids < used, jnp.minimum(class_of(ids), N_CLASSES - 1), class_of(used[None] - 1))
    first_row = (ids - _small_lookup(blk_start, blk_cls)) * MOE_ROWS
    blk_nv = jnp.where(ids < used, jnp.clip(_small_lookup(counts, blk_cls) - first_row, 0, MOE_ROWS), 0)
    lo_tab, hi_tab = _pair_tables()
    blk_elo = _small_lookup(jnp.asarray(lo_tab), blk_cls)
    blk_ehi = _small_lookup(jnp.asarray(hi_tab), blk_cls)
    gate_hi, gate_lo = _split_bf16(jnp.pad(mod[:, 5, :], ((0, MOD_BATCH_PAD - B), (0, 0))))
    gate_tab = jnp.concatenate([gate_hi, gate_lo], axis=0)
    y = _moe_pairs(rec.reshape(T, rec_w), blk_elo, blk_ehi, blk_nv.astype(I32), dest.astype(I32), gate_tab, lw)
    return y.reshape(B, S, D)


def _moe_dense_kernel(h_ref, x1_ref, gate_ref, w_ref, sel_ref, wgu_ref, wd_ref, y_ref):
    e = pl.program_id(0)

    @pl.when(e == 0)
    def _():
        y_ref[...] = jnp.zeros_like(y_ref)

    ye = _swiglu(h_ref[...], wgu_ref[0], wd_ref[0])
    y_ref[...] += jnp.where(sel_ref[0] > 0.5, w_ref[0] * ye, 0.0)

    @pl.when(e == pl.num_programs(0) - 1)
    def _():
        y_ref[...] = x1_ref[...] + gate_ref[...] * y_ref[...]


def _moe_dense(h2, x1, gate_rows, w_sel, sel, lw):
    T, D = x1.shape
    per_e = lambda e: (e, 0, 0)
    return pl.pallas_call(
        _moe_dense_kernel,
        grid=(N_EXPERTS,),
        in_specs=[_full((T, D)), _full((T, D)), _full((T, D)),
                  pl.BlockSpec((1, T, 1), per_e), pl.BlockSpec((1, T, 1), per_e),
                  pl.BlockSpec((1, D, 2 * D_EXPERT), per_e), pl.BlockSpec((1, D_EXPERT, D), per_e)],
        out_specs=_full((T, D)),
        out_shape=jax.ShapeDtypeStruct((T, D), F32),
        compiler_params=_params(1),
    )(h2, x1, gate_rows, w_sel, sel, lw["w_gu_e"], lw["w_d_e"])


def _rope_tables(pos, gain, scale):
    inv = 1.0 / (ROPE_BASE ** (jnp.arange(ROPE_HALF, dtype=F32) / ROPE_HALF))
    ang = pos.astype(F32)[:, None] * inv[None, :]
    cos, sin = jnp.cos(ang) * scale, jnp.sin(ang) * scale
    n = pos.shape[0]
    z = lambda w: jnp.zeros((n, w), F32)
    pad = HEAD_PAD - QK_DIM
    g_nope, g1, g2 = gain[:NOPE_DIM], gain[NOPE_DIM:NOPE_DIM + ROPE_HALF], gain[NOPE_DIM + ROPE_HALF:]
    tab_c = jnp.concatenate([jnp.broadcast_to(g_nope * scale, (n, NOPE_DIM)), cos * g1, cos * g2, z(pad)], axis=1)
    tab_s = jnp.concatenate([z(NOPE_DIM), -sin * g2, sin * g1, z(pad)], axis=1)
    return tab_c, tab_s


def _prep_layer(w, l, chunk_lens):
    D = w["w_in"].shape[1]
    H = MLA_HEADS
    lw = {}
    row = lambda name: w[name][l].reshape(1, -1)
    for name in ("g_norm_mix", "g_q_lat", "g_kv_lat", "g_ln_v", "b_ln_v", "g_out_attn", "g_out_gmlp", "g_norm_ffn"):
        lw[name] = row(name)
    w_in = w["w_in"][l]
    o1, o2, o3 = Q_LORA, Q_LORA + KV_LORA, Q_LORA + KV_LORA + ROPE_DIM
    o4 = o3 + GM_WIDTH
    z32 = jnp.zeros((D, ROPE_DIM), F32)
    wr = w_in[:, o2:o3]
    lw["w_in_p"] = jnp.concatenate([w_in[:, :o2], wr, z32, wr, z32, w_in[:, o3:o4], w_in[:, o4:]], axis=1).astype(BF16)
    assert lw["w_in_p"].shape[1] == PROJ_COLS
    pad = HEAD_PAD - QK_DIM
    w_uq = w["w_uq"][l].reshape(Q_LORA, H, QK_DIM)
    w_uq_swapped = jnp.concatenate([jnp.zeros((Q_LORA, H, NOPE_DIM), F32), w_uq[:, :, NOPE_DIM + ROPE_HALF:],
                                    w_uq[:, :, NOPE_DIM:NOPE_DIM + ROPE_HALF]], axis=2)
    head_pad = lambda a: jnp.pad(a, ((0, 0), (0, 0), (0, pad))).reshape(Q_LORA, H * HEAD_PAD)
    lw["w_uq_p"] = jnp.concatenate([head_pad(w_uq), head_pad(w_uq_swapped)], axis=1).astype(BF16)
    w_ukv = w["w_ukv"][l].reshape(KV_LORA, H, NOPE_DIM + V_DIM)
    lw["w_k_p"] = jnp.pad(w_ukv[:, :, :NOPE_DIM], ((0, 0), (0, 0), (0, HEAD_PAD - NOPE_DIM))).reshape(KV_LORA, H * HEAD_PAD).astype(BF16)
    lw["w_v_p"] = jnp.pad(w_ukv[:, :, NOPE_DIM:], ((0, 0), (0, 0), (0, HEAD_PAD - V_DIM))).reshape(KV_LORA, H * HEAD_PAD).astype(BF16)
    lw["scores_bounded"] = _scores_bounded(w, l)
    lw["g_qnorm"] = w["g_qnorm"][l]
    lw["g_knorm"] = w["g_knorm"][l]
    for L in chunk_lens:
        wsp = w["w_spatial"][l][:, :L, :L]
        lw["w_sp_pairs_%d" % L] = wsp.reshape(GM_HEADS // 2, 2 * L, L).astype(BF16)
        lw["b_sp_rows_%d" % L] = jnp.repeat(jnp.transpose(w["b_spatial"][l][:, :L]), GM_HEAD_DIM, axis=1)
    w_out = w["w_out"][l].astype(BF16)
    lw["w_out_a"], lw["w_out_g"] = w_out[:ATT_WIDTH], w_out[ATT_WIDTH:]
    wr_full = jnp.zeros((D, ROUTER_COLS), F32)
    wr_full = wr_full.at[:, :N_GROUPS].set(w["w_router_group"][l])
    wr_full = wr_full.at[:, ROUTER_EXPERT_LANE0:ROUTER_EXPERT_LANE0 + N_EXPERTS].set(w["w_router_expert"][l])
    r_hi, r_lo = _split_bf16(wr_full)
    lw["w_r_a"] = jnp.concatenate([r_hi, r_lo], axis=1)
    lw["w_r_b"] = r_hi
    br = jnp.zeros((1, ROUTER_COLS), F32)
    br = br.at[0, :N_GROUPS].set(w["b_router_group"][l])
    lw["b_r"] = br.at[0, ROUTER_EXPERT_LANE0:ROUTER_EXPERT_LANE0 + N_EXPERTS].set(w["b_router_expert"][l])
    lw["w_gu_e"] = jnp.concatenate([w["w_gate_e"][l], w["w_up_e"][l]], axis=-1).astype(BF16)
    lw["w_d_e"] = w["w_down_e"][l].astype(BF16)
    return lw


def _tiles(seq):
    tm = min(seq, 512)
    return tm, min(seq, 512)


def _layer_prompt(x, mod, lw):
    B, S, D = x.shape
    tm, tq = _tiles(S)
    pos = jnp.arange(S)
    q_tabs = _rope_tables(pos, lw["g_qnorm"], QK_DIM ** -0.5 * LOG2E)
    k_tabs = _rope_tables(pos, lw["g_knorm"], 1.0)
    ckv, krope, q, gm, k, v = _mix_in(x, mod, lw, q_tabs, k_tabs, tm=tm, chunk_len=GM_CHUNK, emit_kv=True, emit_vrows=False)
    att = _attention(lw["scores_bounded"], q, k, v, tq=tq, tk=tq, q_off=0, kv_valid=S)
    rec, ri, cnt = _out_route(x, att, gm, mod, lw, tm=tm, pack_rows=True)
    cls = ri[:, 0, :].reshape(B * S)
    rank = ri[:, 1, :].reshape(B * S)
    counts = cnt[:N_CLASSES, 0].astype(I32)
    y = _moe_prompt(rec, cls, rank, counts, mod, lw)
    return y, ckv, krope


def _layer_sample(x, mod, past_ckv, past_krope, lw):
    B, S, D = x.shape
    past = past_ckv.shape[1]
    q_tabs = _rope_tables(past + jnp.arange(S), lw["g_qnorm"], QK_DIM ** -0.5 * LOG2E)
    ckv, krope, q, gm, v_rows = _mix_in(x, mod, lw, q_tabs, q_tabs, tm=S, chunk_len=S, emit_kv=False, emit_vrows=True)
    kv_valid = past + S
    kv_pad = -(-kv_valid // LANES) * LANES
    extra = kv_pad - kv_valid
    ckv_all = jnp.concatenate([past_ckv, ckv, jnp.zeros((B, extra, KV_LORA), F32)], axis=1)
    kr_all = jnp.concatenate([past_krope, krope, jnp.zeros((B, extra, ROPE_DIM), F32)], axis=1)
    kr_slot = jnp.pad(kr_all, ((0, 0), (0, 0), (NOPE_DIM, HEAD_PAD - QK_DIM)))
    k_tabs = _rope_tables(jnp.arange(kv_pad), lw["g_knorm"], 1.0)
    k, v = _kv_latent(ckv_all, kr_slot, lw, k_tabs, tr=kv_pad)
    att = _attention(lw["scores_bounded"], q, k, v, tq=S, tk=kv_pad, q_off=past, kv_valid=kv_valid)
    x1, h2, ri, rf, _ = _out_route(x, att, gm, mod, lw, tm=S, pack_rows=False)
    T = B * S
    cls, w_lo, w_hi = ri[:, 0, :].reshape(T), rf[:, 0, :].reshape(T), rf[:, 1, :].reshape(T)
    lo_tab, hi_tab = _pair_tables()
    e_lo, e_hi = jnp.asarray(lo_tab)[cls], jnp.asarray(hi_tab)[cls]
    eids = jnp.arange(N_EXPERTS, dtype=I32)[:, None]
    is_lo, is_hi = eids == e_lo[None, :], eids == e_hi[None, :]
    w_sel = (jnp.where(is_lo, w_lo[None, :], 0.0) + jnp.where(is_hi, w_hi[None, :], 0.0))[:, :, None]
    sel = (is_lo | is_hi).astype(F32)[:, :, None]
    gate_rows = jnp.repeat(mod[:, 5, :], S, axis=0)
    y = _moe_dense(h2.reshape(T, D), x1.reshape(T, D), gate_rows, w_sel, sel, lw)
    return y.reshape(B, S, D), ckv, krope, v_rows


def kernel(x_prompt, x_sample, cache_ckv, cache_krope, c_prompt, c_sample, w_ada, b_ada, g_norm_mix, w_in, g_q_lat, w_uq, g_kv_lat, w_ukv, g_qnorm, g_knorm, g_ln_v, b_ln_v, w_spatial, b_spatial, g_out_attn, g_out_gmlp, w_out, g_norm_ffn, w_router_group, b_router_group, w_router_expert, b_router_expert, w_gate_e, w_up_e, w_down_e):
    w = dict(w_in=w_in, g_norm_mix=g_norm_mix, g_q_lat=g_q_lat, w_uq=w_uq, g_kv_lat=g_kv_lat, w_ukv=w_ukv,
             g_qnorm=g_qnorm, g_knorm=g_knorm, g_ln_v=g_ln_v, b_ln_v=b_ln_v, w_spatial=w_spatial, b_spatial=b_spatial,
             g_out_attn=g_out_attn, g_out_gmlp=g_out_gmlp, w_out=w_out, g_norm_ffn=g_norm_ffn,
             w_router_group=w_router_group, b_router_group=b_router_group, w_router_expert=w_router_expert,
             b_router_expert=b_router_expert, w_gate_e=w_gate_e, w_up_e=w_up_e, w_down_e=w_down_e)
    depth = w_ada.shape[0]
    Bp, Sp, D = x_prompt.shape
    Bs, Ss, _ = x_sample.shape
    assert Sp % GM_CHUNK == 0 and Ss <= GM_CHUNK and Ss % CHUNK == 0 and Bp <= MOD_BATCH_PAD
    c_all = jnp.concatenate([c_prompt, c_sample], axis=0)
    y_p, y_s = x_prompt, x_sample
    outs = [[] for _ in range(5)]
    for l in range(depth):
        lw = _prep_layer(w, l, (GM_CHUNK, Ss))
        mod = _ada_mod(c_all, w_ada[l], b_ada[l]).reshape(Bp + Bs, 6, D)
        y_p, ckv_p, kr_p = _layer_prompt(y_p, mod[:Bp], lw)
        y_s, ckv_s, kr_s, v_s = _layer_sample(y_s, mod[Bp:], cache_ckv[l], cache_krope[l], lw)
        for lst, val in zip(outs, (ckv_p, kr_p, ckv_s, kr_s, v_s)):
            lst.append(val)
    return (y_p, y_s) + tuple(jnp.stack(lst) for lst in outs)
```

```python
import functools

import numpy as np
import jax
import jax.numpy as jnp
from jax import lax
from jax.experimental import pallas as pl
from jax.experimental.pallas import tpu as pltpu

F32 = jnp.float32
BF16 = jnp.bfloat16
I32 = jnp.int32

CHUNK = 64
CHUNK_SHIFT = 6
EPS = 1e-6
MLA_HEADS = 8
Q_LORA = 256
KV_LORA = 128
NOPE_DIM = 64
ROPE_DIM = 32
ROPE_HALF = ROPE_DIM // 2
V_DIM = 64
QK_DIM = NOPE_DIM + ROPE_DIM
ATT_WIDTH = MLA_HEADS * V_DIM
ROPE_BASE = 10000.0
GM_HEADS = 8
GM_HEAD_DIM = 64
GM_WIDTH = GM_HEADS * GM_HEAD_DIM
GM_CHUNK = 128
N_GROUPS = 4
EXPERTS_PER_GROUP = 8
N_EXPERTS = N_GROUPS * EXPERTS_PER_GROUP
D_EXPERT = 256
PAIRS_PER_GROUP = EXPERTS_PER_GROUP * (EXPERTS_PER_GROUP - 1) // 2
N_CLASSES = N_GROUPS * PAIRS_PER_GROUP

LANES = 128
SUBLANES = 8
HEAD_PAD = LANES
PROJ_COLS = 1536
ROUTER_COLS = LANES
ROUTER_EXPERT_LANE0 = SUBLANES
CLASS_ROWS = LANES
MOE_ROWS = 128
MOD_BATCH_PAD = 16
VMEM_LIMIT = 48 * 1024 * 1024
NEG_BIG = -1e30
LOG2E = 1.4426950408889634
BF16_SLACK = 1.02
SCORE_BOUND = 90.0

assert CHUNK == 1 << CHUNK_SHIFT


def _params(n_axes, vmem=VMEM_LIMIT):
    return pltpu.CompilerParams(dimension_semantics=("arbitrary",) * n_axes, vmem_limit_bytes=vmem)


def _full(shape):
    nd = len(shape)
    return pl.BlockSpec(shape, lambda *_: (0,) * nd)


def _split_bf16(x):
    hi = x.astype(BF16)
    lo = (x - hi.astype(F32)).astype(BF16)
    return hi, lo


def _dot(a, b):
    return jnp.dot(a, b, preferred_element_type=F32)


def _ada_kernel(c_ref, w_ref, b_ref, o_ref):
    a_hi, a_lo = _split_bf16(jax.nn.silu(c_ref[...]))
    w_hi, w_lo = _split_bf16(w_ref[...])
    o_ref[...] = _dot(a_hi, w_hi) + _dot(a_lo, w_hi) + _dot(a_hi, w_lo) + b_ref[...]


def _ada_mod(c, w_ada, b_ada):
    n, d = c.shape
    cols = w_ada.shape[1]
    tn = 1536
    return pl.pallas_call(
        _ada_kernel,
        grid=(cols // tn,),
        in_specs=[_full((n, d)), pl.BlockSpec((d, tn), lambda j: (0, j)), pl.BlockSpec((1, tn), lambda j: (0, j))],
        out_specs=pl.BlockSpec((n, tn), lambda j: (0, j)),
        out_shape=jax.ShapeDtypeStruct((n, cols), F32),
        compiler_params=_params(1),
    )(c, w_ada, b_ada.reshape(1, cols))


def _rms(x, g):
    return x * lax.rsqrt(jnp.mean(x * x, axis=-1, keepdims=True) + EPS) * g


def _head_norm_rope(xh, xh_swapped, tab_c, tab_s):
    ms = jnp.sum(xh * xh, axis=-1, keepdims=True) * (1.0 / QK_DIM)
    return (xh * tab_c + xh_swapped * tab_s) * lax.rsqrt(ms + EPS)


def _keys_values(ckv, kr_slot, gkv, wk, wv, tab_c, tab_s, k_ref, v_ref):
    cb = _rms(ckv, gkv).astype(BF16)
    kall = _dot(cb, wk)
    vall = _dot(cb, wv)
    lane = lax.broadcasted_iota(I32, (1, LANES), 1)
    one_col = jnp.where(lane == V_DIM, 1.0, 0.0)
    kr_swapped = jnp.where(lane < NOPE_DIM + ROPE_HALF, pltpu.roll(kr_slot, HEAD_PAD - ROPE_HALF, 1),
                           pltpu.roll(kr_slot, ROPE_HALF, 1))
    kr_swapped = jnp.where((lane >= NOPE_DIM) & (lane < QK_DIM), kr_swapped, 0.0)
    for h in range(MLA_HEADS):
        kh = kall[:, h * HEAD_PAD:(h + 1) * HEAD_PAD] + kr_slot
        k_ref[0, h] = _head_norm_rope(kh, kr_swapped, tab_c, tab_s).astype(BF16)
        v_ref[0, h] = (vall[:, h * HEAD_PAD:(h + 1) * HEAD_PAD] + one_col).astype(BF16)


def _mix_in_kernel(x_ref, mod_ref, gmix_ref, win_ref, gql_ref, wuq_ref, gkv_ref, wk_ref, wv_ref,
                   cq_ref, sq_ref, ck_ref, sk_ref, glnv_ref, blnv_ref, wsp_ref, bsp_ref,
                   ggm_ref, *rest, chunk_len, emit_kv, emit_vrows):
    outs = list(rest[:-1])
    mixed_scr = rest[-1]
    ckv_ref, kr_ref, q_ref, gm_ref = outs[:4]
    outs = outs[4:]
    if emit_kv:
        k_ref, v_ref = outs[:2]
        outs = outs[2:]
    if emit_vrows:
        vrows_ref = outs[0]

    x = x_ref[0]
    tm = x.shape[0]
    shift, scale = mod_ref[0, 0:1, :], mod_ref[0, 1:2, :]
    h = _rms(x, gmix_ref[...]) * (1.0 + scale) + shift
    proj = _dot(h.astype(BF16), win_ref[...])

    q_lat = proj[:, 0:Q_LORA]
    ckv = proj[:, Q_LORA:Q_LORA + KV_LORA]
    kr_blk = proj[:, Q_LORA + KV_LORA:Q_LORA + KV_LORA + LANES]
    ckv_ref[0] = ckv
    kr_ref[0] = kr_blk[:, 0:ROPE_DIM]

    q = _dot(_rms(q_lat, gql_ref[...]).astype(BF16), wuq_ref[...])
    cq, sq = cq_ref[...], sq_ref[...]
    n_q = MLA_HEADS * HEAD_PAD
    for hd in range(MLA_HEADS):
        qh = q[:, hd * HEAD_PAD:(hd + 1) * HEAD_PAD]
        qh_swapped = q[:, n_q + hd * HEAD_PAD:n_q + (hd + 1) * HEAD_PAD]
        q_ref[0, hd] = _head_norm_rope(qh, qh_swapped, cq, sq).astype(BF16)

    if emit_kv:
        lane = lax.broadcasted_iota(I32, (1, LANES), 1)
        kr_slot = jnp.where(lane >= NOPE_DIM, kr_blk, 0.0)
        _keys_values(ckv, kr_slot, gkv_ref[...], wk_ref[...], wv_ref[...], ck_ref[...], sk_ref[...], k_ref, v_ref)

    g_u = proj[:, 512:512 + GM_WIDTH]
    g_v = proj[:, 1024:1024 + GM_WIDTH]
    u = jax.nn.gelu(g_u)
    gv = jax.nn.gelu(g_v)
    mu = jnp.mean(gv, axis=-1, keepdims=True)
    xc = gv - mu
    var = jnp.mean(xc * xc, axis=-1, keepdims=True)
    v_rows = xc * lax.rsqrt(var + EPS) * glnv_ref[...] + blnv_ref[...]
    if emit_vrows:
        vrows_ref[0] = v_rows
    vb = v_rows.astype(BF16)

    L = chunk_len
    t = lax.broadcasted_iota(I32, (2 * L, L), 0)
    s = lax.broadcasted_iota(I32, (2 * L, L), 1)
    t = jnp.where(t >= L, t - L, t)
    allowed = (s >> CHUNK_SHIFT) <= (t >> CHUNK_SHIFT)
    lane = lax.broadcasted_iota(I32, (1, LANES), 1)
    first_head = lane < GM_HEAD_DIM
    for p in range(GM_HEADS // 2):
        w_pair = jnp.where(allowed, wsp_ref[p], jnp.zeros((), BF16))
        for c in range(tm // L):
            vp = vb[c * L:(c + 1) * L, p * LANES:(p + 1) * LANES]
            r = _dot(w_pair, vp)
            mixed = jnp.where(first_head, r[:L], r[L:])
            mixed_scr[c * L:(c + 1) * L, p * LANES:(p + 1) * LANES] = mixed + bsp_ref[:, p * LANES:(p + 1) * LANES]
    gm = u * mixed_scr[...]
    gm_ref[0] = _rms(gm, ggm_ref[...]).astype(BF16)


def _mix_in(x, mod, lw, q_tabs, k_tabs, *, tm, chunk_len, emit_kv, emit_vrows):
    B, S, D = x.shape
    nt = S // tm
    H = MLA_HEADS
    row = lambda b, i: (b, i, 0)
    tab = pl.BlockSpec((tm, LANES), lambda b, i: (i, 0))
    in_specs = [
        pl.BlockSpec((1, tm, D), row),
        pl.BlockSpec((1, 6, D), lambda b, i: (b, 0, 0)),
        _full((1, D)), _full((D, PROJ_COLS)), _full((1, Q_LORA)), _full((Q_LORA, 2 * H * HEAD_PAD)),
        _full((1, KV_LORA)), _full((KV_LORA, H * HEAD_PAD)), _full((KV_LORA, H * HEAD_PAD)),
        tab, tab, tab, tab,
        _full((1, GM_WIDTH)), _full((1, GM_WIDTH)),
        _full((GM_HEADS // 2, 2 * chunk_len, chunk_len)), _full((chunk_len, GM_WIDTH)), _full((1, GM_WIDTH)),
    ]
    out_shape = [
        jax.ShapeDtypeStruct((B, S, KV_LORA), F32),
        jax.ShapeDtypeStruct((B, S, ROPE_DIM), F32),
        jax.ShapeDtypeStruct((B, H, S, HEAD_PAD), BF16),
        jax.ShapeDtypeStruct((B, S, GM_WIDTH), BF16),
    ]
    head_blk = pl.BlockSpec((1, H, tm, HEAD_PAD), lambda b, i: (b, 0, i, 0))
    out_specs = [
        pl.BlockSpec((1, tm, KV_LORA), row),
        pl.BlockSpec((1, tm, ROPE_DIM), row),
        head_blk,
        pl.BlockSpec((1, tm, GM_WIDTH), row),
    ]
    if emit_kv:
        out_shape += [jax.ShapeDtypeStruct((B, H, S, HEAD_PAD), BF16), jax.ShapeDtypeStruct((B, H, S, HEAD_PAD), BF16)]
        out_specs += [head_blk, head_blk]
    if emit_vrows:
        out_shape += [jax.ShapeDtypeStruct((B, S, GM_WIDTH), F32)]
        out_specs += [pl.BlockSpec((1, tm, GM_WIDTH), row)]
    kern = functools.partial(_mix_in_kernel, chunk_len=chunk_len, emit_kv=emit_kv, emit_vrows=emit_vrows)
    return pl.pallas_call(
        kern,
        grid=(B, nt),
        in_specs=in_specs,
        out_specs=out_specs,
        out_shape=out_shape,
        scratch_shapes=[pltpu.VMEM((tm, GM_WIDTH), F32)],
        compiler_params=_params(2),
    )(x, mod, lw["g_norm_mix"], lw["w_in_p"], lw["g_q_lat"], lw["w_uq_p"], lw["g_kv_lat"], lw["w_k_p"], lw["w_v_p"],
      *q_tabs, *k_tabs, lw["g_ln_v"], lw["b_ln_v"],
      lw["w_sp_pairs_%d" % chunk_len], lw["b_sp_rows_%d" % chunk_len], lw["g_out_gmlp"])


def _kv_latent_kernel(ckv_ref, kr_ref, gkv_ref, wk_ref, wv_ref, c_ref, s_ref, k_ref, v_ref):
    _keys_values(ckv_ref[0], kr_ref[0], gkv_ref[...], wk_ref[...], wv_ref[...], c_ref[...], s_ref[...], k_ref, v_ref)


def _kv_latent(ckv_all, kr_slot_all, lw, k_tabs, *, tr):
    B, K, _ = ckv_all.shape
    H = MLA_HEADS
    row = lambda b, i: (b, i, 0)
    tab = pl.BlockSpec((tr, LANES), lambda b, i: (i, 0))
    return pl.pallas_call(
        _kv_latent_kernel,
        grid=(B, K // tr),
        in_specs=[pl.BlockSpec((1, tr, KV_LORA), row), pl.BlockSpec((1, tr, LANES), row),
                  _full((1, KV_LORA)), _full((KV_LORA, H * HEAD_PAD)), _full((KV_LORA, H * HEAD_PAD)),
                  tab, tab],
        out_specs=[pl.BlockSpec((1, H, tr, HEAD_PAD), lambda b, i: (b, 0, i, 0))] * 2,
        out_shape=[jax.ShapeDtypeStruct((B, H, K, HEAD_PAD), BF16)] * 2,
        compiler_params=_params(2),
    )(ckv_all, kr_slot_all, lw["g_kv_lat"], lw["w_k_p"], lw["w_v_p"], *k_tabs)


def _attn_kernel(bounded_ref, q_ref, k_ref, v_ref, o_ref, *, tq, tk, n_q, q_off, kv_valid):
    i = pl.program_id(2) if n_q > 1 else 0
    q_first = q_off + i * tq
    vis_first = jnp.minimum(((q_first >> CHUNK_SHIFT) + 1) << CHUNK_SHIFT, kv_valid)
    vis_last = jnp.minimum((((q_first + tq - 1) >> CHUNK_SHIFT) + 1) << CHUNK_SHIFT, kv_valid)
    n_unmasked = vis_first // tk
    n_total = (vis_last + tk - 1) // tk

    q_pos = q_first + lax.broadcasted_iota(I32, (tq, 1), 0)
    limit = jnp.minimum(((q_pos >> CHUNK_SHIFT) + 1) << CHUNK_SHIFT, kv_valid)
    lane = lax.broadcasted_iota(I32, (1, LANES), 1)

    def scores(j, start, masked):
        s = lax.dot_general(q_ref[0, j], k_ref[0, j, pl.ds(start, tk), :], (((1,), (1,)), ((), ())),
                            preferred_element_type=F32)
        if masked:
            k_pos = start + lax.broadcasted_iota(I32, (1, tk), 1)
            s = jnp.where(k_pos < limit, s, NEG_BIG)
        return s

    def plain_block(kb, carry, masked):
        start = pl.multiple_of(kb * tk, tk)
        new = []
        for j in range(2):
            p = jnp.exp2(scores(j, start, masked)).astype(BF16)
            new.append(carry[j] + _dot(p, v_ref[0, j, pl.ds(start, tk), :]))
        return tuple(new)

    def online_block(kb, carry, masked):
        start = pl.multiple_of(kb * tk, tk)
        new = []
        for j in range(2):
            m, acc = carry[2 * j:2 * j + 2]
            s = scores(j, start, masked)
            m_new = jnp.maximum(m, jnp.max(s, axis=-1, keepdims=True))
            p = jnp.exp2(s - m_new).astype(BF16)
            acc = jnp.exp2(m - m_new) * acc + _dot(p, v_ref[0, j, pl.ds(start, tk), :])
            new += [m_new, acc]
        return tuple(new)

    def run(block, init):
        n_pairs = n_unmasked // 2
        carry = lax.fori_loop(0, n_pairs, lambda p, c: block(2 * p + 1, block(2 * p, c, False), False), tuple(init))
        carry = lax.fori_loop(2 * n_pairs, n_unmasked, lambda kb, c: block(kb, c, False), carry)
        return lax.fori_loop(n_unmasked, n_total, lambda kb, c: block(kb, c, True), carry)

    def finish(acc0, acc1):
        outs = []
        for acc in (acc0, acc1):
            denom = jnp.sum(jnp.where(lane == V_DIM, acc, 0.0), axis=-1, keepdims=True)
            outs.append(acc / denom)
        o_ref[0] = jnp.where(lane < V_DIM, outs[0], pltpu.roll(outs[1], V_DIM, 1)).astype(BF16)

    zeros = jnp.zeros((tq, LANES), F32)

    @pl.when(bounded_ref[0] == 1)
    def _():
        finish(*run(plain_block, [zeros, zeros]))

    @pl.when(bounded_ref[0] != 1)
    def _():
        m0 = jnp.full((tq, 1), NEG_BIG, F32)
        c = run(online_block, [m0, zeros, m0, zeros])
        finish(c[1], c[3])


def _attention(bounded, q, k, v, *, tq, tk, q_off, kv_valid):
    B, H, Sq, _ = q.shape
    Sk = k.shape[2]
    kern = functools.partial(_attn_kernel, tq=tq, tk=tk, n_q=Sq // tq, q_off=q_off, kv_valid=kv_valid)
    grid_spec = pltpu.PrefetchScalarGridSpec(
        num_scalar_prefetch=1,
        grid=(B, H // 2, Sq // tq),
        in_specs=[pl.BlockSpec((1, 2, tq, HEAD_PAD), lambda b, hp, i, f: (b, hp, i, 0)),
                  pl.BlockSpec((1, 2, Sk, HEAD_PAD), lambda b, hp, i, f: (b, hp, 0, 0)),
                  pl.BlockSpec((1, 2, Sk, HEAD_PAD), lambda b, hp, i, f: (b, hp, 0, 0))],
        out_specs=pl.BlockSpec((1, tq, LANES), lambda b, hp, i, f: (b, i, hp)),
    )
    return pl.pallas_call(
        kern,
        grid_spec=grid_spec,
        out_shape=jax.ShapeDtypeStruct((B, Sq, ATT_WIDTH), BF16),
        compiler_params=_params(3),
    )(bounded, q, k, v)


def _scores_bounded(w, l):
    gq = jnp.max(jnp.abs(w["g_qnorm"][l]))
    gk = jnp.max(jnp.abs(w["g_knorm"][l]))
    bound = (QK_DIM ** 0.5) * LOG2E * BF16_SLACK * gq * gk
    return (bound <= SCORE_BOUND).astype(I32).reshape(1)


def _pack_bf16_pairs(h):
    n = h.shape[1] // 2
    hi = pltpu.bitcast(h[:, :n].astype(BF16).astype(F32), jnp.uint32)
    lo = pltpu.bitcast(h[:, n:].astype(BF16).astype(F32), jnp.uint32)
    return pltpu.bitcast(hi | (lo >> 16), F32)


def _unpack_bf16_pairs(words):
    w = pltpu.bitcast(words, jnp.uint32)
    hi = pltpu.bitcast(w & jnp.uint32(0xFFFF0000), F32)
    lo = pltpu.bitcast(w << 16, F32)
    return jnp.concatenate([hi, lo], axis=1).astype(BF16)


def _out_route_kernel(x_ref, att_ref, gm_ref, mod_ref, goa_ref, woa_ref, wog_ref, gffn_ref, wra_ref, wrb_ref, br_ref,
                      *rest, pack_rows):
    carry_scr = rest[-1]
    if pack_rows:
        rec_ref, ri_ref, cnt_ref = rest[:3]
    else:
        x1_ref, h2_ref, ri_ref, rf_ref, cnt_ref = rest[:5]
    first_step = (pl.program_id(0) == 0) & (pl.program_id(1) == 0)

    @pl.when(first_step)
    def _():
        carry_scr[...] = jnp.zeros_like(carry_scr)

    x = x_ref[0]
    tm, D = x.shape
    gate_a = mod_ref[0, 2:3, :]
    shift_m, scale_m = mod_ref[0, 3:4, :], mod_ref[0, 4:5, :]
    att_n = _rms(att_ref[0].astype(F32), goa_ref[...]).astype(BF16)
    mix = _dot(att_n, woa_ref[...]) + _dot(gm_ref[0], wog_ref[...])
    x1 = x + gate_a * mix
    h2 = _rms(x1, gffn_ref[...]) * (1.0 + scale_m) + shift_m
    if pack_rows:
        rec_ref[0, :, 0:D] = x1
        rec_ref[0, :, D:D + D // 2] = _pack_bf16_pairs(h2)
    else:
        x1_ref[0] = x1
        h2_ref[0] = h2.astype(BF16)

    h_hi, h_lo = _split_bf16(h2)
    la = _dot(h_hi, wra_ref[...])
    logits = la[:, :ROUTER_COLS] + la[:, ROUTER_COLS:] + _dot(h_lo, wrb_ref[...]) + br_ref[...]
    lt = logits.T

    g = [lt[r:r + 1] for r in range(N_GROUPS)]
    gmax = jnp.maximum(jnp.maximum(g[0], g[1]), jnp.maximum(g[2], g[3]))
    gsum = sum(jnp.exp(gr - gmax) for gr in g)
    g_prob = 1.0 / gsum
    g_idx = jnp.where(g[0] == gmax, 0.0, jnp.where(g[1] == gmax, 1.0, jnp.where(g[2] == gmax, 2.0, 3.0)))

    e0 = ROUTER_EXPERT_LANE0
    grp = [lt[e0 + EXPERTS_PER_GROUP * r:e0 + EXPERTS_PER_GROUP * (r + 1)] for r in range(N_GROUPS)]
    sel = jnp.where(g_idx == 0.0, grp[0], jnp.where(g_idx == 1.0, grp[1], jnp.where(g_idx == 2.0, grp[2], grp[3])))
    sub = lax.broadcasted_iota(I32, (EXPERTS_PER_GROUP, tm), 0).astype(F32)
    m1 = jnp.max(sel, axis=0, keepdims=True)
    i1 = jnp.min(jnp.where(sel == m1, sub, float(EXPERTS_PER_GROUP)), axis=0, keepdims=True)
    sel2 = jnp.where(sub == i1, -jnp.inf, sel)
    m2 = jnp.max(sel2, axis=0, keepdims=True)
    i2 = jnp.min(jnp.where(sel2 == m2, sub, float(EXPERTS_PER_GROUP)), axis=0, keepdims=True)
    d = jnp.exp(m2 - m1)
    w1 = g_prob / (1.0 + d)
    w2 = g_prob * d / (1.0 + d)
    first_lower = i1 < i2
    lo = jnp.minimum(i1, i2)
    hi = jnp.maximum(i1, i2)
    w_lo = jnp.where(first_lower, w1, w2)
    w_hi = jnp.where(first_lower, w2, w1)
    pair = lo * EXPERTS_PER_GROUP - lo * (lo + 1.0) * 0.5 + hi - lo - 1.0
    cls = g_idx * PAIRS_PER_GROUP + pair

    crow = lax.broadcasted_iota(I32, (CLASS_ROWS, tm), 0).astype(F32)
    onehot = jnp.where(crow == cls, 1.0, 0.0)
    ta = lax.broadcasted_iota(I32, (tm, tm), 0)
    tb = lax.broadcasted_iota(I32, (tm, tm), 1)
    earlier = jnp.where(ta < tb, 1.0, 0.0).astype(BF16)
    before = _dot(onehot.astype(BF16), earlier)
    carry = carry_scr[...]
    rank = jnp.sum(onehot * (before + carry[:, 0:1]), axis=0, keepdims=True)
    carry = carry + jnp.sum(onehot, axis=1, keepdims=True)
    carry_scr[...] = carry
    cnt_ref[...] = carry

    ri_ref[...] = jnp.zeros_like(ri_ref)
    ri_ref[0, 0:1, :] = cls.astype(I32)
    ri_ref[0, 1:2, :] = rank.astype(I32)
    if pack_rows:
        mrow = lax.broadcasted_iota(I32, (LANES, tm), 0)
        batch = pl.program_id(0).astype(F32)
        meta_t = jnp.where(mrow == 0, w_lo, jnp.where(mrow == 1, w_hi, jnp.where(mrow == 2, batch, 0.0)))
        rec_ref[0, :, D + D // 2:] = meta_t.T
    else:
        rf_ref[...] = jnp.zeros_like(rf_ref)
        rf_ref[0, 0:1, :] = w_lo
        rf_ref[0, 1:2, :] = w_hi


def _out_route(x, att, gm, mod, lw, *, tm, pack_rows):
    B, S, D = x.shape
    nt = S // tm
    row = lambda b, i: (b, i, 0)
    tile = lambda b, i: (b * nt + i, 0, 0)
    route_i = (jax.ShapeDtypeStruct((B * nt, SUBLANES, tm), I32), pl.BlockSpec((1, SUBLANES, tm), tile))
    route_f = (jax.ShapeDtypeStruct((B * nt, SUBLANES, tm), F32), pl.BlockSpec((1, SUBLANES, tm), tile))
    counts = (jax.ShapeDtypeStruct((CLASS_ROWS, LANES), F32), _full((CLASS_ROWS, LANES)))
    if pack_rows:
        rec_w = D + D // 2 + LANES
        outs = [(jax.ShapeDtypeStruct((B, S, rec_w), F32), pl.BlockSpec((1, tm, rec_w), row)), route_i, counts]
    else:
        outs = [(jax.ShapeDtypeStruct((B, S, D), F32), pl.BlockSpec((1, tm, D), row)),
                (jax.ShapeDtypeStruct((B, S, D), BF16), pl.BlockSpec((1, tm, D), row)), route_i, route_f, counts]
    out_shape = [o[0] for o in outs]
    out_specs = [o[1] for o in outs]
    return pl.pallas_call(
        functools.partial(_out_route_kernel, pack_rows=pack_rows),
        grid=(B, nt),
        in_specs=[pl.BlockSpec((1, tm, D), row), pl.BlockSpec((1, tm, ATT_WIDTH), row), pl.BlockSpec((1, tm, GM_WIDTH), row),
                  pl.BlockSpec((1, 6, D), lambda b, i: (b, 0, 0)),
                  _full((1, ATT_WIDTH)), _full((ATT_WIDTH, D)), _full((GM_WIDTH, D)), _full((1, D)),
                  _full((D, 2 * ROUTER_COLS)), _full((D, ROUTER_COLS)), _full((1, ROUTER_COLS))],
        out_specs=out_specs,
        out_shape=out_shape,
        scratch_shapes=[pltpu.VMEM((CLASS_ROWS, LANES), F32)],
        compiler_params=_params(2),
    )(x, att, gm, mod, lw["g_out_attn"], lw["w_out_a"], lw["w_out_g"], lw["g_norm_ffn"],
      lw["w_r_a"], lw["w_r_b"], lw["b_r"])


def _swiglu(hb, wgu, wd):
    hid = _dot(hb, wgu)
    act = jax.nn.silu(hid[:, :D_EXPERT]) * hid[:, D_EXPERT:]
    return _dot(act.astype(BF16), wd)


def _moe_pairs_kernel(elo_ref, ehi_ref, nv_ref, dest_ref, rec_hbm, zeros_hbm, gate_ref,
                      wgu_lo_ref, wd_lo_ref, wgu_hi_ref, wd_hi_ref, y_hbm,
                      xbuf0, xbuf1, ybuf0, ybuf1, tok_ref, gsem, ssem, zsem):
    i = pl.program_id(0)
    xbuf, ybuf = (xbuf0, xbuf1), (ybuf0, ybuf1)
    D = ybuf0.shape[-1]
    n_groups_all = MOE_ROWS // SUBLANES

    def gather_start(t, s, g, u):
        pltpu.make_async_copy(rec_hbm.at[pl.ds(t, 1)], xbuf[s].at[g, pl.ds(u, 1)], gsem.at[s]).start()

    def scatter_start(t, s, g, u):
        pltpu.make_async_copy(ybuf[s].at[g, pl.ds(u, 1)], y_hbm.at[pl.ds(t, 1)], ssem.at[s]).start()

    def all_rows(blk, fn):
        base = blk * MOE_ROWS
        for g in range(n_groups_all):
            for u in range(SUBLANES):
                fn(tok_ref[base + g * SUBLANES + u], g, u)

    def valid_rows(blk, fn):
        nv = nv_ref[blk]
        base = blk * MOE_ROWS
        n_groups = nv // SUBLANES

        def group(g, c):
            for u in range(SUBLANES):
                fn(tok_ref[base + g * SUBLANES + u], g, u)
            return c

        def single(r, c):
            fn(tok_ref[base + r], n_groups, r - n_groups * SUBLANES)
            return c

        lax.fori_loop(0, n_groups, group, 0)
        lax.fori_loop(n_groups * SUBLANES, nv, single, 0)

    def gather_wait(s):
        pltpu.make_async_copy(xbuf[s], xbuf[s], gsem.at[s]).wait()

    def scatter_wait(blk, s):
        nv = nv_ref[blk]
        n_groups = nv // SUBLANES
        buf = ybuf[s]

        @pl.when(n_groups > 0)
        def _():
            rows = buf.at[pl.ds(0, n_groups)]
            pltpu.make_async_copy(rows, rows, ssem.at[s]).wait()

        def single(r, c):
            row = buf.at[0, pl.ds(0, 1)]
            pltpu.make_async_copy(row, row, ssem.at[s]).wait()
            return c

        lax.fori_loop(n_groups * SUBLANES, nv, single, 0)

    def experts(s):
        hb = _unpack_bf16_pairs(xbuf[s][:, :, D:D + D // 2].reshape(MOE_ROWS, D // 2))
        meta = xbuf[s][:, :, D + D // 2:].reshape(MOE_ROWS, LANES)
        x = xbuf[s][:, :, 0:D].reshape(MOE_ROWS, D)
        w_lo, w_hi, bidx = meta[:, 0:1], meta[:, 1:2], meta[:, 2:3]
        blane = lax.broadcasted_iota(I32, (1, 2 * MOD_BATCH_PAD), 1)
        blane = jnp.where(blane >= MOD_BATCH_PAD, blane - MOD_BATCH_PAD, blane).astype(F32)
        onehot = jnp.where(bidx == blane, 1.0, 0.0).astype(BF16)
        gate_m = _dot(onehot, gate_ref[...])
        moe = w_lo * _swiglu(hb, wgu_lo_ref[0], wd_lo_ref[0]) + w_hi * _swiglu(hb, wgu_hi_ref[0], wd_hi_ref[0])
        ybuf[s][...] = (x + gate_m * moe).reshape(n_groups_all, SUBLANES, D)

    @pl.when(i == 0)
    def _():
        zero = pltpu.make_async_copy(zeros_hbm, tok_ref, zsem)
        zero.start()
        zero.wait()

        def invert(g, c):
            for u in range(SUBLANES):
                t = g * SUBLANES + u
                tok_ref[dest_ref[t]] = t
            return c
        lax.fori_loop(0, dest_ref.shape[0] // SUBLANES, invert, 0)
        all_rows(0, lambda t, g, u: gather_start(t, 0, g, u))

    prev = jnp.maximum(i - 1, 0)
    has_rows = nv_ref[i] > 0
    prev_full = (i >= 1) & (nv_ref[prev] == MOE_ROWS)
    prev_partial = (i >= 1) & (nv_ref[prev] > 0) & (nv_ref[prev] < MOE_ROWS)

    def step(slot):
        other = 1 - slot

        @pl.when(has_rows)
        def _():
            all_rows(i + 1, lambda t, g, u: gather_start(t, other, g, u))
            gather_wait(slot)

            @pl.when(prev_full)
            def _():
                all_rows(i - 1, lambda t, g, u: scatter_start(t, other, g, u))

            @pl.when(prev_partial)
            def _():
                valid_rows(i - 1, lambda t, g, u: scatter_start(t, other, g, u))

            @pl.when(i >= 2)
            def _():
                scatter_wait(i - 2, slot)
            experts(slot)

        @pl.when(jnp.logical_not(has_rows) & (i >= 1) & (nv_ref[prev] > 0))
        def _():
            gather_wait(slot)
            valid_rows(i - 1, lambda t, g, u: scatter_start(t, other, g, u))

            @pl.when(i >= 2)
            def _():
                scatter_wait(i - 2, slot)
            scatter_wait(i - 1, other)

    for parity in (0, 1):
        pl.when(i % 2 == parity)(functools.partial(step, parity))


def _moe_pairs(rec, blk_elo, blk_ehi, blk_nv, dest, gate_tab, lw):
    T, rec_w = rec.shape
    assert T % SUBLANES == 0
    D = gate_tab.shape[1]
    nb = blk_nv.shape[0]
    wgu, wd = lw["w_gu_e"], lw["w_d_e"]
    grid_spec = pltpu.PrefetchScalarGridSpec(
        num_scalar_prefetch=4,
        grid=(nb,),
        in_specs=[
            pl.BlockSpec(memory_space=pl.ANY),
            pl.BlockSpec(memory_space=pl.ANY),
            pl.BlockSpec(gate_tab.shape, lambda i, *_: (0, 0)),
            pl.BlockSpec((1, D, 2 * D_EXPERT), lambda i, elo, ehi, nv, tok: (elo[i], 0, 0)),
            pl.BlockSpec((1, D_EXPERT, D), lambda i, elo, ehi, nv, tok: (elo[i], 0, 0)),
            pl.BlockSpec((1, D, 2 * D_EXPERT), lambda i, elo, ehi, nv, tok: (ehi[i], 0, 0)),
            pl.BlockSpec((1, D_EXPERT, D), lambda i, elo, ehi, nv, tok: (ehi[i], 0, 0)),
        ],
        out_specs=pl.BlockSpec(memory_space=pl.ANY),
        scratch_shapes=[pltpu.VMEM((MOE_ROWS // SUBLANES, SUBLANES, rec_w), F32),
                        pltpu.VMEM((MOE_ROWS // SUBLANES, SUBLANES, rec_w), F32),
                        pltpu.VMEM((MOE_ROWS // SUBLANES, SUBLANES, D), F32),
                        pltpu.VMEM((MOE_ROWS // SUBLANES, SUBLANES, D), F32),
                        pltpu.SMEM((nb * MOE_ROWS,), I32),
                        pltpu.SemaphoreType.DMA((2,)), pltpu.SemaphoreType.DMA((2,)), pltpu.SemaphoreType.DMA(())],
    )
    return pl.pallas_call(
        _moe_pairs_kernel,
        grid_spec=grid_spec,
        out_shape=jax.ShapeDtypeStruct((T, D), F32),
        compiler_params=_params(1),
    )(blk_elo, blk_ehi, blk_nv, dest, rec, jnp.zeros((nb * MOE_ROWS,), I32), gate_tab, wgu, wd, wgu, wd)


def _small_lookup(table, idx):
    ids = jnp.arange(table.shape[0], dtype=I32)
    return jnp.sum(jnp.where(idx[:, None] == ids[None, :], table[None, :], 0), axis=1)


def _pair_tables():
    lo, hi = [], []
    for g in range(N_GROUPS):
        for a in range(EXPERTS_PER_GROUP):
            for b in range(a + 1, EXPERTS_PER_GROUP):
                lo.append(g * EXPERTS_PER_GROUP + a)
                hi.append(g * EXPERTS_PER_GROUP + b)
    return np.asarray(lo, np.int32), np.asarray(hi, np.int32)


def _moe_prompt(rec, cls, rank, counts, mod, lw):
    B, S, rec_w = rec.shape
    D = mod.shape[-1]
    T = B * S
    nb = T // MOE_ROWS + N_CLASSES + 1
    nblk = (counts + MOE_ROWS - 1) // MOE_ROWS
    blk_end = jnp.cumsum(nblk)
    blk_start = blk_end - nblk
    dest = _small_lookup(blk_start, cls) * MOE_ROWS + rank
    ids = jnp.arange(nb, dtype=I32)
    used = blk_end[-1]
    class_of = lambda blk: jnp.sum((blk_end[None, :] <= blk[:, None]).astype(I32), axis=1)
    blk_cls = jnp.where(ids < used, jnp.minimum(class_of(ids), N_CLASSES - 1), class_of(used[None] - 1))
    first_row = (ids - _small_lookup(blk_start, blk_cls)) * MOE_ROWS
    blk_nv = jnp.where(ids < used, jnp.clip(_small_lookup(counts, blk_cls) - first_row, 0, MOE_ROWS), 0)
    lo_tab, hi_tab = _pair_tables()
    blk_elo = _small_lookup(jnp.asarray(lo_tab), blk_cls)
    blk_ehi = _small_lookup(jnp.asarray(hi_tab), blk_cls)
    gate_hi, gate_lo = _split_bf16(jnp.pad(mod[:, 5, :], ((0, MOD_BATCH_PAD - B), (0, 0))))
    gate_tab = jnp.concatenate([gate_hi, gate_lo], axis=0)
    y = _moe_pairs(rec.reshape(T, rec_w), blk_elo, blk_ehi, blk_nv.astype(I32), dest.astype(I32), gate_tab, lw)
    return y.reshape(B, S, D)


def _moe_dense_kernel(h_ref, x1_ref, gate_ref, w_ref, sel_ref, wgu_ref, wd_ref, y_ref):
    e = pl.program_id(0)

    @pl.when(e == 0)
    def _():
        y_ref[...] = jnp.zeros_like(y_ref)

    ye = _swiglu(h_ref[...], wgu_ref[0], wd_ref[0])
    y_ref[...] += jnp.where(sel_ref[0] > 0.5, w_ref[0] * ye, 0.0)

    @pl.when(e == pl.num_programs(0) - 1)
    def _():
        y_ref[...] = x1_ref[...] + gate_ref[...] * y_ref[...]


def _moe_dense(h2, x1, gate_rows, w_sel, sel, lw):
    T, D = x1.shape
    per_e = lambda e: (e, 0, 0)
    return pl.pallas_call(
        _moe_dense_kernel,
        grid=(N_EXPERTS,),
        in_specs=[_full((T, D)), _full((T, D)), _full((T, D)),
                  pl.BlockSpec((1, T, 1), per_e), pl.BlockSpec((1, T, 1), per_e),
                  pl.BlockSpec((1, D, 2 * D_EXPERT), per_e), pl.BlockSpec((1, D_EXPERT, D), per_e)],
        out_specs=_full((T, D)),
        out_shape=jax.ShapeDtypeStruct((T, D), F32),
        compiler_params=_params(1),
    )(h2, x1, gate_rows, w_sel, sel, lw["w_gu_e"], lw["w_d_e"])


def _rope_tables(pos, gain, scale):
    inv = 1.0 / (ROPE_BASE ** (jnp.arange(ROPE_HALF, dtype=F32) / ROPE_HALF))
    ang = pos.astype(F32)[:, None] * inv[None, :]
    cos, sin = jnp.cos(ang) * scale, jnp.sin(ang) * scale
    n = pos.shape[0]
    z = lambda w: jnp.zeros((n, w), F32)
    pad = HEAD_PAD - QK_DIM
    g_nope, g1, g2 = gain[:NOPE_DIM], gain[NOPE_DIM:NOPE_DIM + ROPE_HALF], gain[NOPE_DIM + ROPE_HALF:]
    tab_c = jnp.concatenate([jnp.broadcast_to(g_nope * scale, (n, NOPE_DIM)), cos * g1, cos * g2, z(pad)], axis=1)
    tab_s = jnp.concatenate([z(NOPE_DIM), -sin * g2, sin * g1, z(pad)], axis=1)
    return tab_c, tab_s


def _prep_layer(w, l, chunk_lens):
    D = w["w_in"].shape[1]
    H = MLA_HEADS
    lw = {}
    row = lambda name: w[name][l].reshape(1, -1)
    for name in ("g_norm_mix", "g_q_lat", "g_kv_lat", "g_ln_v", "b_ln_v", "g_out_attn", "g_out_gmlp", "g_norm_ffn"):
        lw[name] = row(name)
    w_in = w["w_in"][l]
    o1, o2, o3 = Q_LORA, Q_LORA + KV_LORA, Q_LORA + KV_LORA + ROPE_DIM
    o4 = o3 + GM_WIDTH
    z32 = jnp.zeros((D, ROPE_DIM), F32)
    wr = w_in[:, o2:o3]
    lw["w_in_p"] = jnp.concatenate([w_in[:, :o2], wr, z32, wr, z32, w_in[:, o3:o4], w_in[:, o4:]], axis=1).astype(BF16)
    assert lw["w_in_p"].shape[1] == PROJ_COLS
    pad = HEAD_PAD - QK_DIM
    w_uq = w["w_uq"][l].reshape(Q_LORA, H, QK_DIM)
    w_uq_swapped = jnp.concatenate([jnp.zeros((Q_LORA, H, NOPE_DIM), F32), w_uq[:, :, NOPE_DIM + ROPE_HALF:],
                                    w_uq[:, :, NOPE_DIM:NOPE_DIM + ROPE_HALF]], axis=2)
    head_pad = lambda a: jnp.pad(a, ((0, 0), (0, 0), (0, pad))).reshape(Q_LORA, H * HEAD_PAD)
    lw["w_uq_p"] = jnp.concatenate([head_pad(w_uq), head_pad(w_uq_swapped)], axis=1).astype(BF16)
    w_ukv = w["w_ukv"][l].reshape(KV_LORA, H, NOPE_DIM + V_DIM)
    lw["w_k_p"] = jnp.pad(w_ukv[:, :, :NOPE_DIM], ((0, 0), (0, 0), (0, HEAD_PAD - NOPE_DIM))).reshape(KV_LORA, H * HEAD_PAD).astype(BF16)
    lw["w_v_p"] = jnp.pad(w_ukv[:, :, NOPE_DIM:], ((0, 0), (0, 0), (0, HEAD_PAD - V_DIM))).reshape(KV_LORA, H * HEAD_PAD).astype(BF16)
    lw["scores_bounded"] = _scores_bounded(w, l)
    lw["g_qnorm"] = w["g_qnorm"][l]
    lw["g_knorm"] = w["g_knorm"][l]
    for L in chunk_lens:
        wsp = w["w_spatial"][l][:, :L, :L]
        lw["w_sp_pairs_%d" % L] = wsp.reshape(GM_HEADS // 2, 2 * L, L).astype(BF16)
        lw["b_sp_rows_%d" % L] = jnp.repeat(jnp.transpose(w["b_spatial"][l][:, :L]), GM_HEAD_DIM, axis=1)
    w_out = w["w_out"][l].astype(BF16)
    lw["w_out_a"], lw["w_out_g"] = w_out[:ATT_WIDTH], w_out[ATT_WIDTH:]
    wr_full = jnp.zeros((D, ROUTER_COLS), F32)
    wr_full = wr_full.at[:, :N_GROUPS].set(w["w_router_group"][l])
    wr_full = wr_full.at[:, ROUTER_EXPERT_LANE0:ROUTER_EXPERT_LANE0 + N_EXPERTS].set(w["w_router_expert"][l])
    r_hi, r_lo = _split_bf16(wr_full)
    lw["w_r_a"] = jnp.concatenate([r_hi, r_lo], axis=1)
    lw["w_r_b"] = r_hi
    br = jnp.zeros((1, ROUTER_COLS), F32)
    br = br.at[0, :N_GROUPS].set(w["b_router_group"][l])
    lw["b_r"] = br.at[0, ROUTER_EXPERT_LANE0:ROUTER_EXPERT_LANE0 + N_EXPERTS].set(w["b_router_expert"][l])
    lw["w_gu_e"] = jnp.concatenate([w["w_gate_e"][l], w["w_up_e"][l]], axis=-1).astype(BF16)
    lw["w_d_e"] = w["w_down_e"][l].astype(BF16)
    return lw


def _tiles(seq):
    tm = min(seq, 512)
    return tm, min(seq, 512)


def _layer_prompt(x, mod, lw):
    B, S, D = x.shape
    tm, tq = _tiles(S)
    pos = jnp.arange(S)
    q_tabs = _rope_tables(pos, lw["g_qnorm"], QK_DIM ** -0.5 * LOG2E)
    k_tabs = _rope_tables(pos, lw["g_knorm"], 1.0)
    ckv, krope, q, gm, k, v = _mix_in(x, mod, lw, q_tabs, k_tabs, tm=tm, chunk_len=GM_CHUNK, emit_kv=True, emit_vrows=False)
    att = _attention(lw["scores_bounded"], q, k, v, tq=tq, tk=tq, q_off=0, kv_valid=S)
    rec, ri, cnt = _out_route(x, att, gm, mod, lw, tm=tm, pack_rows=True)
    cls = ri[:, 0, :].reshape(B * S)
    rank = ri[:, 1, :].reshape(B * S)
    counts = cnt[:N_CLASSES, 0].astype(I32)
    y = _moe_prompt(rec, cls, rank, counts, mod, lw)
    return y, ckv, krope


def _layer_sample(x, mod, past_ckv, past_krope, lw):
    B, S, D = x.shape
    past = past_ckv.shape[1]
    q_tabs = _rope_tables(past + jnp.arange(S), lw["g_qnorm"], QK_DIM ** -0.5 * LOG2E)
    ckv, krope, q, gm, v_rows = _mix_in(x, mod, lw, q_tabs, q_tabs, tm=S, chunk_len=S, emit_kv=False, emit_vrows=True)
    kv_valid = past + S
    kv_pad = -(-kv_valid // LANES) * LANES
    extra = kv_pad - kv_valid
    ckv_all = jnp.concatenate([past_ckv, ckv, jnp.zeros((B, extra, KV_LORA), F32)], axis=1)
    kr_all = jnp.concatenate([past_krope, krope, jnp.zeros((B, extra, ROPE_DIM), F32)], axis=1)
    kr_slot = jnp.pad(kr_all, ((0, 0), (0, 0), (NOPE_DIM, HEAD_PAD - QK_DIM)))
    k_tabs = _rope_tables(jnp.arange(kv_pad), lw["g_knorm"], 1.0)
    k, v = _kv_latent(ckv_all, kr_slot, lw, k_tabs, tr=kv_pad)
    att = _attention(lw["scores_bounded"], q, k, v, tq=S, tk=kv_pad, q_off=past, kv_valid=kv_valid)
    x1, h2, ri, rf, _ = _out_route(x, att, gm, mod, lw, tm=S, pack_rows=False)
    T = B * S
    cls, w_lo, w_hi = ri[:, 0, :].reshape(T), rf[:, 0, :].reshape(T), rf[:, 1, :].reshape(T)
    lo_tab, hi_tab = _pair_tables()
    e_lo, e_hi = jnp.asarray(lo_tab)[cls], jnp.asarray(hi_tab)[cls]
    eids = jnp.arange(N_EXPERTS, dtype=I32)[:, None]
    is_lo, is_hi = eids == e_lo[None, :], eids == e_hi[None, :]
    w_sel = (jnp.where(is_lo, w_lo[None, :], 0.0) + jnp.where(is_hi, w_hi[None, :], 0.0))[:, :, None]
    sel = (is_lo | is_hi).astype(F32)[:, :, None]
    gate_rows = jnp.repeat(mod[:, 5, :], S, axis=0)
    y = _moe_dense(h2.reshape(T, D), x1.reshape(T, D), gate_rows, w_sel, sel, lw)
    return y.reshape(B, S, D), ckv, krope, v_rows


def kernel(x_prompt, x_sample, cache_ckv, cache_krope, c_prompt, c_sample, w_ada, b_ada, g_norm_mix, w_in, g_q_lat, w_uq, g_kv_lat, w_ukv, g_qnorm, g_knorm, g_ln_v, b_ln_v, w_spatial, b_spatial, g_out_attn, g_out_gmlp, w_out, g_norm_ffn, w_router_group, b_router_group, w_router_expert, b_router_expert, w_gate_e, w_up_e, w_down_e):
    w = dict(w_in=w_in, g_norm_mix=g_norm_mix, g_q_lat=g_q_lat, w_uq=w_uq, g_kv_lat=g_kv_lat, w_ukv=w_ukv,
             g_qnorm=g_qnorm, g_knorm=g_knorm, g_ln_v=g_ln_v, b_ln_v=b_ln_v, w_spatial=w_spatial, b_spatial=b_spatial,
             g_out_attn=g_out_attn, g_out_gmlp=g_out_gmlp, w_out=w_out, g_norm_ffn=g_norm_ffn,
             w_router_group=w_router_group, b_router_group=b_router_group, w_router_expert=w_router_expert,
             b_router_expert=b_router_expert, w_gate_e=w_gate_e, w_up_e=w_up_e, w_down_e=w_down_e)
    depth = w_ada.shape[0]
    Bp, Sp, D = x_prompt.shape
    Bs, Ss, _ = x_sample.shape
    assert Sp % GM_CHUNK == 0 and Ss <= GM_CHUNK and Ss % CHUNK == 0 and Bp <= MOD_BATCH_PAD
    c_all = jnp.concatenate([c_prompt, c_sample], axis=0)
    y_p, y_s = x_prompt, x_sample
    outs = [[] for _ in range(5)]
    for l in range(depth):
        lw = _prep_layer(w, l, (GM_CHUNK, Ss))
        mod = _ada_mod(c_all, w_ada[l], b_ada[l]).reshape(Bp + Bs, 6, D)
        y_p, ckv_p, kr_p = _layer_prompt(y_p, mod[:Bp], lw)
        y_s, ckv_s, kr_s, v_s = _layer_sample(y_s, mod[Bp:], cache_ckv[l], cache_krope[l], lw)
        for lst, val in zip(outs, (ckv_p, kr_p, ckv_s, kr_s, v_s)):
            lst.append(val)
    return (y_p, y_s) + tuple(jnp.stack(lst) for lst in outs)
```

```python
import functools

import numpy as np
import jax
import jax.numpy as jnp
from jax import lax
from jax.experimental import pallas as pl
from jax.experimental.pallas import tpu as pltpu

F32 = jnp.float32
BF16 = jnp.bfloat16
I32 = jnp.int32

CHUNK = 64
CHUNK_SHIFT = 6
EPS = 1e-6
MLA_HEADS = 8
Q_LORA = 256
KV_LORA = 128
NOPE_DIM = 64
ROPE_DIM = 32
ROPE_HALF = ROPE_DIM // 2
V_DIM = 64
QK_DIM = NOPE_DIM + ROPE_DIM
ATT_WIDTH = MLA_HEADS * V_DIM
ROPE_BASE = 10000.0
GM_HEADS = 8
GM_HEAD_DIM = 64
GM_WIDTH = GM_HEADS * GM_HEAD_DIM
GM_CHUNK = 128
N_GROUPS = 4
EXPERTS_PER_GROUP = 8
N_EXPERTS = N_GROUPS * EXPERTS_PER_GROUP
D_EXPERT = 256
PAIRS_PER_GROUP = EXPERTS_PER_GROUP * (EXPERTS_PER_GROUP - 1) // 2
N_CLASSES = N_GROUPS * PAIRS_PER_GROUP

LANES = 128
SUBLANES = 8
HEAD_PAD = LANES
PROJ_COLS = 1536
ROUTER_COLS = LANES
ROUTER_EXPERT_LANE0 = SUBLANES
CLASS_ROWS = LANES
MOE_ROWS = 128
MOD_BATCH_PAD = 16
VMEM_LIMIT = 48 * 1024 * 1024
NEG_BIG = -1e30
LOG2E = 1.4426950408889634
BF16_SLACK = 1.02
SCORE_BOUND = 90.0

assert CHUNK == 1 << CHUNK_SHIFT


def _params(n_axes, vmem=VMEM_LIMIT):
    return pltpu.CompilerParams(dimension_semantics=("arbitrary",) * n_axes, vmem_limit_bytes=vmem)


def _full(shape):
    nd = len(shape)
    return pl.BlockSpec(shape, lambda *_: (0,) * nd)


def _split_bf16(x):
    hi = x.astype(BF16)
    lo = (x - hi.astype(F32)).astype(BF16)
    return hi, lo


def _dot(a, b):
    return jnp.dot(a, b, preferred_element_type=F32)


def _ada_kernel(c_ref, w_ref, b_ref, o_ref):
    a_hi, a_lo = _split_bf16(jax.nn.silu(c_ref[...]))
    w_hi, w_lo = _split_bf16(w_ref[...])
    o_ref[...] = _dot(a_hi, w_hi) + _dot(a_lo, w_hi) + _dot(a_hi, w_lo) + b_ref[...]


def _ada_mod(c, w_ada, b_ada):
    n, d = c.shape
    cols = w_ada.shape[1]
    tn = 1536
    return pl.pallas_call(
        _ada_kernel,
        grid=(cols // tn,),
        in_specs=[_full((n, d)), pl.BlockSpec((d, tn), lambda j: (0, j)), pl.BlockSpec((1, tn), lambda j: (0, j))],
        out_specs=pl.BlockSpec((n, tn), lambda j: (0, j)),
        out_shape=jax.ShapeDtypeStruct((n, cols), F32),
        compiler_params=_params(1),
    )(c, w_ada, b_ada.reshape(1, cols))


def _rms(x, g):
    return x * lax.rsqrt(jnp.mean(x * x, axis=-1, keepdims=True) + EPS) * g


def _head_norm_rope(xh, xh_swapped, tab_c, tab_s):
    ms = jnp.sum(xh * xh, axis=-1, keepdims=True) * (1.0 / QK_DIM)
    return (xh * tab_c + xh_swapped * tab_s) * lax.rsqrt(ms + EPS)


def _keys_values(ckv, kr_slot, gkv, wk, wv, tab_c, tab_s, k_ref, v_ref):
    cb = _rms(ckv, gkv).astype(BF16)
    kall = _dot(cb, wk)
    vall = _dot(cb, wv)
    lane = lax.broadcasted_iota(I32, (1, LANES), 1)
    ones_hi = jnp.where(lane >= V_DIM, 1.0, 0.0)
    ones_lo = 1.0 - ones_hi
    kr_swapped = jnp.where(lane < NOPE_DIM + ROPE_HALF, pltpu.roll(kr_slot, HEAD_PAD - ROPE_HALF, 1),
                           pltpu.roll(kr_slot, ROPE_HALF, 1))
    kr_swapped = jnp.where((lane >= NOPE_DIM) & (lane < QK_DIM), kr_swapped, 0.0)
    for h in range(MLA_HEADS):
        kh = kall[:, h * HEAD_PAD:(h + 1) * HEAD_PAD] + kr_slot
        k_ref[0, h] = _head_norm_rope(kh, kr_swapped, tab_c, tab_s).astype(BF16)
        ones = ones_hi if h % 2 == 0 else ones_lo
        v_ref[0, h] = (vall[:, h * HEAD_PAD:(h + 1) * HEAD_PAD] + ones).astype(BF16)


def _mix_in_kernel(x_ref, mod_ref, gmix_ref, win_ref, gql_ref, wuq_ref, gkv_ref, wk_ref, wv_ref,
                   cq_ref, sq_ref, ck_ref, sk_ref, glnv_ref, blnv_ref, wsp_ref, bsp_ref,
                   ggm_ref, *rest, chunk_len, emit_kv, emit_vrows):
    outs = list(rest[:-1])
    mixed_scr = rest[-1]
    ckv_ref, kr_ref, q_ref, gm_ref = outs[:4]
    outs = outs[4:]
    if emit_kv:
        k_ref, v_ref = outs[:2]
        outs = outs[2:]
    if emit_vrows:
        vrows_ref = outs[0]

    x = x_ref[0]
    tm = x.shape[0]
    shift, scale = mod_ref[0, 0:1, :], mod_ref[0, 1:2, :]
    h = _rms(x, gmix_ref[...]) * (1.0 + scale) + shift
    proj = _dot(h.astype(BF16), win_ref[...])

    q_lat = proj[:, 0:Q_LORA]
    ckv = proj[:, Q_LORA:Q_LORA + KV_LORA]
    kr_blk = proj[:, Q_LORA + KV_LORA:Q_LORA + KV_LORA + LANES]
    ckv_ref[0] = ckv
    kr_ref[0] = kr_blk[:, 0:ROPE_DIM]

    q = _dot(_rms(q_lat, gql_ref[...]).astype(BF16), wuq_ref[...])
    cq, sq = cq_ref[...], sq_ref[...]
    n_q = MLA_HEADS * HEAD_PAD
    for hd in range(MLA_HEADS):
        qh = q[:, hd * HEAD_PAD:(hd + 1) * HEAD_PAD]
        qh_swapped = q[:, n_q + hd * HEAD_PAD:n_q + (hd + 1) * HEAD_PAD]
        q_ref[0, hd] = _head_norm_rope(qh, qh_swapped, cq, sq).astype(BF16)

    if emit_kv:
        lane = lax.broadcasted_iota(I32, (1, LANES), 1)
        kr_slot = jnp.where(lane >= NOPE_DIM, kr_blk, 0.0)
        _keys_values(ckv, kr_slot, gkv_ref[...], wk_ref[...], wv_ref[...], ck_ref[...], sk_ref[...], k_ref, v_ref)

    g_u = proj[:, 512:512 + GM_WIDTH]
    g_v = proj[:, 1024:1024 + GM_WIDTH]
    u = jax.nn.gelu(g_u)
    gv = jax.nn.gelu(g_v)
    mu = jnp.mean(gv, axis=-1, keepdims=True)
    xc = gv - mu
    var = jnp.mean(xc * xc, axis=-1, keepdims=True)
    v_rows = xc * lax.rsqrt(var + EPS) * glnv_ref[...] + blnv_ref[...]
    if emit_vrows:
        vrows_ref[0] = v_rows
    vb = v_rows.astype(BF16)

    L = chunk_len
    t = lax.broadcasted_iota(I32, (2 * L, L), 0)
    s = lax.broadcasted_iota(I32, (2 * L, L), 1)
    t = jnp.where(t >= L, t - L, t)
    allowed = (s >> CHUNK_SHIFT) <= (t >> CHUNK_SHIFT)
    lane = lax.broadcasted_iota(I32, (1, LANES), 1)
    first_head = lane < GM_HEAD_DIM
    for p in range(GM_HEADS // 2):
        w_pair = jnp.where(allowed, wsp_ref[p], jnp.zeros((), BF16))
        for c in range(tm // L):
            vp = vb[c * L:(c + 1) * L, p * LANES:(p + 1) * LANES]
            r = _dot(w_pair, vp)
            mixed = jnp.where(first_head, r[:L], r[L:])
            mixed_scr[c * L:(c + 1) * L, p * LANES:(p + 1) * LANES] = mixed + bsp_ref[:, p * LANES:(p + 1) * LANES]
    gm = u * mixed_scr[...]
    gm_ref[0] = _rms(gm, ggm_ref[...]).astype(BF16)


def _mix_in(x, mod, lw, q_tabs, k_tabs, *, tm, chunk_len, emit_kv, emit_vrows):
    B, S, D = x.shape
    nt = S // tm
    H = MLA_HEADS
    row = lambda b, i: (b, i, 0)
    tab = pl.BlockSpec((tm, LANES), lambda b, i: (i, 0))
    in_specs = [
        pl.BlockSpec((1, tm, D), row),
        pl.BlockSpec((1, 6, D), lambda b, i: (b, 0, 0)),
        _full((1, D)), _full((D, PROJ_COLS)), _full((1, Q_LORA)), _full((Q_LORA, 2 * H * HEAD_PAD)),
        _full((1, KV_LORA)), _full((KV_LORA, H * HEAD_PAD)), _full((KV_LORA, H * HEAD_PAD)),
        tab, tab, tab, tab,
        _full((1, GM_WIDTH)), _full((1, GM_WIDTH)),
        _full((GM_HEADS // 2, 2 * chunk_len, chunk_len)), _full((chunk_len, GM_WIDTH)), _full((1, GM_WIDTH)),
    ]
    out_shape = [
        jax.ShapeDtypeStruct((B, S, KV_LORA), F32),
        jax.ShapeDtypeStruct((B, S, ROPE_DIM), F32),
        jax.ShapeDtypeStruct((B, H, S, HEAD_PAD), BF16),
        jax.ShapeDtypeStruct((B, S, GM_WIDTH), BF16),
    ]
    head_blk = pl.BlockSpec((1, H, tm, HEAD_PAD), lambda b, i: (b, 0, i, 0))
    out_specs = [
        pl.BlockSpec((1, tm, KV_LORA), row),
        pl.BlockSpec((1, tm, ROPE_DIM), row),
        head_blk,
        pl.BlockSpec((1, tm, GM_WIDTH), row),
    ]
    if emit_kv:
        out_shape += [jax.ShapeDtypeStruct((B, H, S, HEAD_PAD), BF16), jax.ShapeDtypeStruct((B, H, S, HEAD_PAD), BF16)]
        out_specs += [head_blk, head_blk]
    if emit_vrows:
        out_shape += [jax.ShapeDtypeStruct((B, S, GM_WIDTH), F32)]
        out_specs += [pl.BlockSpec((1, tm, GM_WIDTH), row)]
    kern = functools.partial(_mix_in_kernel, chunk_len=chunk_len, emit_kv=emit_kv, emit_vrows=emit_vrows)
    return pl.pallas_call(
        kern,
        grid=(B, nt),
        in_specs=in_specs,
        out_specs=out_specs,
        out_shape=out_shape,
        scratch_shapes=[pltpu.VMEM((tm, GM_WIDTH), F32)],
        compiler_params=_params(2),
    )(x, mod, lw["g_norm_mix"], lw["w_in_p"], lw["g_q_lat"], lw["w_uq_p"], lw["g_kv_lat"], lw["w_k_p"], lw["w_v_p"],
      *q_tabs, *k_tabs, lw["g_ln_v"], lw["b_ln_v"],
      lw["w_sp_pairs_%d" % chunk_len], lw["b_sp_rows_%d" % chunk_len], lw["g_out_gmlp"])


def _kv_latent_kernel(ckv_ref, kr_ref, gkv_ref, wk_ref, wv_ref, c_ref, s_ref, k_ref, v_ref):
    _keys_values(ckv_ref[0], kr_ref[0], gkv_ref[...], wk_ref[...], wv_ref[...], c_ref[...], s_ref[...], k_ref, v_ref)


def _kv_latent(ckv_all, kr_slot_all, lw, k_tabs, *, tr):
    B, K, _ = ckv_all.shape
    H = MLA_HEADS
    row = lambda b, i: (b, i, 0)
    tab = pl.BlockSpec((tr, LANES), lambda b, i: (i, 0))
    return pl.pallas_call(
        _kv_latent_kernel,
        grid=(B, K // tr),
        in_specs=[pl.BlockSpec((1, tr, KV_LORA), row), pl.BlockSpec((1, tr, LANES), row),
                  _full((1, KV_LORA)), _full((KV_LORA, H * HEAD_PAD)), _full((KV_LORA, H * HEAD_PAD)),
                  tab, tab],
        out_specs=[pl.BlockSpec((1, H, tr, HEAD_PAD), lambda b, i: (b, 0, i, 0))] * 2,
        out_shape=[jax.ShapeDtypeStruct((B, H, K, HEAD_PAD), BF16)] * 2,
        compiler_params=_params(2),
    )(ckv_all, kr_slot_all, lw["g_kv_lat"], lw["w_k_p"], lw["w_v_p"], *k_tabs)


def _attn_kernel(bounded_ref, q_ref, k_ref, v_ref, o_ref, acc_ref, *, tq, tk, n_q, q_off, kv_valid, split_diag):
    i = pl.program_id(2) if n_q > 1 else 0
    q_first = q_off + i * tq
    vis_first = jnp.minimum(((q_first >> CHUNK_SHIFT) + 1) << CHUNK_SHIFT, kv_valid)
    vis_last = jnp.minimum((((q_first + tq - 1) >> CHUNK_SHIFT) + 1) << CHUNK_SHIFT, kv_valid)
    n_unmasked = vis_first // tk
    n_total = (vis_last + tk - 1) // tk

    q_pos = q_first + lax.broadcasted_iota(I32, (tq, 1), 0)
    limit = jnp.minimum(((q_pos >> CHUNK_SHIFT) + 1) << CHUNK_SHIFT, kv_valid)
    lane = lax.broadcasted_iota(I32, (1, LANES), 1)

    def scores(j, start, masked, r0=0, nr=tq, nk=tk):
        s = lax.dot_general(q_ref[0, j, r0:r0 + nr, :], k_ref[0, j, pl.ds(start, nk), :], (((1,), (1,)), ((), ())),
                            preferred_element_type=F32)
        if masked:
            k_pos = start + lax.broadcasted_iota(I32, (1, nk), 1)
            s = jnp.where(k_pos < limit[r0:r0 + nr], s, NEG_BIG)
        return s

    def plain(start, masked, r0=0, nr=tq, nk=tk):
        for j in range(2):
            p = jnp.exp2(scores(j, start, masked, r0, nr, nk)).astype(BF16)
            acc_ref[j, r0:r0 + nr, :] += _dot(p, v_ref[0, j, pl.ds(start, nk), :])

    def online_block(kb, carry, masked):
        start = pl.multiple_of(kb * tk, tk)
        new = []
        for j in range(2):
            m, acc = carry[2 * j:2 * j + 2]
            s = scores(j, start, masked)
            m_new = jnp.maximum(m, jnp.max(s, axis=-1, keepdims=True))
            p = jnp.exp2(s - m_new).astype(BF16)
            acc = jnp.exp2(m - m_new) * acc + _dot(p, v_ref[0, j, pl.ds(start, tk), :])
            new += [m_new, acc]
        return tuple(new)

    def finish(acc0, acc1):
        first = lane < V_DIM
        num = jnp.where(first, acc0, acc1)
        den = pltpu.roll(jnp.where(first, acc1, acc0), V_DIM, 1)
        o_ref[0] = (num / den).astype(BF16)

    @pl.when(bounded_ref[0] == 1)
    def _():
        acc_ref[...] = jnp.zeros_like(acc_ref)
        at = lambda kb: pl.multiple_of(kb * tk, tk)

        def pair(p, c):
            plain(at(2 * p), False)
            plain(at(2 * p + 1), False)
            return c

        def single(kb, c, masked):
            plain(at(kb), masked)
            return c

        n_pairs = n_unmasked // 2
        lax.fori_loop(0, n_pairs, pair, 0)
        lax.fori_loop(2 * n_pairs, n_unmasked, lambda kb, c: single(kb, c, False), 0)
        if split_diag:
            half = tq // 2
            plain(at(n_unmasked), True, 0, half, half)
            plain(at(n_unmasked), True, half, half, tk)
        else:
            lax.fori_loop(n_unmasked, n_total, lambda kb, c: single(kb, c, True), 0)
        finish(acc_ref[0], acc_ref[1])

    @pl.when(bounded_ref[0] != 1)
    def _():
        zeros = jnp.zeros((tq, LANES), F32)
        m0 = jnp.full((tq, 1), NEG_BIG, F32)
        c = lax.fori_loop(0, n_unmasked, lambda kb, c: online_block(kb, c, False), (m0, zeros, m0, zeros))
        c = lax.fori_loop(n_unmasked, n_total, lambda kb, c: online_block(kb, c, True), c)
        finish(c[1], c[3])


def _attention(bounded, q, k, v, *, tq, tk, q_off, kv_valid):
    B, H, Sq, _ = q.shape
    Sk = k.shape[2]
    split_diag = tq == tk and q_off == 0 and kv_valid >= Sq and tq % (2 * CHUNK) == 0
    kern = functools.partial(_attn_kernel, tq=tq, tk=tk, n_q=Sq // tq, q_off=q_off, kv_valid=kv_valid,
                             split_diag=split_diag)
    grid_spec = pltpu.PrefetchScalarGridSpec(
        num_scalar_prefetch=1,
        grid=(B, H // 2, Sq // tq),
        in_specs=[pl.BlockSpec((1, 2, tq, HEAD_PAD), lambda b, hp, i, f: (b, hp, i, 0)),
                  pl.BlockSpec((1, 2, Sk, HEAD_PAD), lambda b, hp, i, f: (b, hp, 0, 0)),
                  pl.BlockSpec((1, 2, Sk, HEAD_PAD), lambda b, hp, i, f: (b, hp, 0, 0))],
        out_specs=pl.BlockSpec((1, tq, LANES), lambda b, hp, i, f: (b, i, hp)),
        scratch_shapes=[pltpu.VMEM((2, tq, LANES), F32)],
    )
    return pl.pallas_call(
        kern,
        grid_spec=grid_spec,
        out_shape=jax.ShapeDtypeStruct((B, Sq, ATT_WIDTH), BF16),
        compiler_params=_params(3),
    )(bounded, q, k, v)


def _scores_bounded(w, l):
    gq = jnp.max(jnp.abs(w["g_qnorm"][l]))
    gk = jnp.max(jnp.abs(w["g_knorm"][l]))
    bound = (QK_DIM ** 0.5) * LOG2E * BF16_SLACK * gq * gk
    return (bound <= SCORE_BOUND).astype(I32).reshape(1)


def _pack_bf16_pairs(h):
    n = h.shape[1] // 2
    hi = pltpu.bitcast(h[:, :n].astype(BF16).astype(F32), jnp.uint32)
    lo = pltpu.bitcast(h[:, n:].astype(BF16).astype(F32), jnp.uint32)
    return pltpu.bitcast(hi | (lo >> 16), F32)


def _unpack_bf16_pairs(words):
    w = pltpu.bitcast(words, jnp.uint32)
    hi = pltpu.bitcast(w & jnp.uint32(0xFFFF0000), F32)
    lo = pltpu.bitcast(w << 16, F32)
    return jnp.concatenate([hi, lo], axis=1).astype(BF16)


def _out_route_kernel(x_ref, att_ref, gm_ref, mod_ref, goa_ref, woa_ref, wog_ref, gffn_ref, wra_ref, wrb_ref, br_ref,
                      *rest, pack_rows):
    carry_scr = rest[-1]
    if pack_rows:
        rec_ref, ri_ref, cnt_ref = rest[:3]
    else:
        x1_ref, h2_ref, ri_ref, rf_ref, cnt_ref = rest[:5]
    first_step = (pl.program_id(0) == 0) & (pl.program_id(1) == 0)

    @pl.when(first_step)
    def _():
        carry_scr[...] = jnp.zeros_like(carry_scr)

    x = x_ref[0]
    tm, D = x.shape
    gate_a = mod_ref[0, 2:3, :]
    shift_m, scale_m = mod_ref[0, 3:4, :], mod_ref[0, 4:5, :]
    att_n = _rms(att_ref[0].astype(F32), goa_ref[...]).astype(BF16)
    mix = _dot(att_n, woa_ref[...]) + _dot(gm_ref[0], wog_ref[...])
    x1 = x + gate_a * mix
    h2 = _rms(x1, gffn_ref[...]) * (1.0 + scale_m) + shift_m
    if pack_rows:
        rec_ref[0, :, 0:D] = x1
        rec_ref[0, :, D:D + D // 2] = _pack_bf16_pairs(h2)
    else:
        x1_ref[0] = x1
        h2_ref[0] = h2.astype(BF16)

    h_hi, h_lo = _split_bf16(h2)
    la = _dot(h_hi, wra_ref[...])
    logits = la[:, :ROUTER_COLS] + la[:, ROUTER_COLS:] + _dot(h_lo, wrb_ref[...]) + br_ref[...]
    lt = logits.T

    g = [lt[r:r + 1] for r in range(N_GROUPS)]
    gmax = jnp.maximum(jnp.maximum(g[0], g[1]), jnp.maximum(g[2], g[3]))
    gsum = sum(jnp.exp(gr - gmax) for gr in g)
    g_prob = 1.0 / gsum
    g_idx = jnp.where(g[0] == gmax, 0.0, jnp.where(g[1] == gmax, 1.0, jnp.where(g[2] == gmax, 2.0, 3.0)))

    e0 = ROUTER_EXPERT_LANE0
    grp = [lt[e0 + EXPERTS_PER_GROUP * r:e0 + EXPERTS_PER_GROUP * (r + 1)] for r in range(N_GROUPS)]
    sel = jnp.where(g_idx == 0.0, grp[0], jnp.where(g_idx == 1.0, grp[1], jnp.where(g_idx == 2.0, grp[2], grp[3])))
    sub = lax.broadcasted_iota(I32, (EXPERTS_PER_GROUP, tm), 0).astype(F32)
    m1 = jnp.max(sel, axis=0, keepdims=True)
    i1 = jnp.min(jnp.where(sel == m1, sub, float(EXPERTS_PER_GROUP)), axis=0, keepdims=True)
    sel2 = jnp.where(sub == i1, -jnp.inf, sel)
    m2 = jnp.max(sel2, axis=0, keepdims=True)
    i2 = jnp.min(jnp.where(sel2 == m2, sub, float(EXPERTS_PER_GROUP)), axis=0, keepdims=True)
    d = jnp.exp(m2 - m1)
    w1 = g_prob / (1.0 + d)
    w2 = g_prob * d / (1.0 + d)
    first_lower = i1 < i2
    lo = jnp.minimum(i1, i2)
    hi = jnp.maximum(i1, i2)
    w_lo = jnp.where(first_lower, w1, w2)
    w_hi = jnp.where(first_lower, w2, w1)
    pair = lo * EXPERTS_PER_GROUP - lo * (lo + 1.0) * 0.5 + hi - lo - 1.0
    cls = g_idx * PAIRS_PER_GROUP + pair

    crow = lax.broadcasted_iota(I32, (CLASS_ROWS, tm), 0).astype(F32)
    onehot = jnp.where(crow == cls, 1.0, 0.0)
    ta = lax.broadcasted_iota(I32, (tm, tm), 0)
    tb = lax.broadcasted_iota(I32, (tm, tm), 1)
    earlier = jnp.where(ta < tb, 1.0, 0.0).astype(BF16)
    before = _dot(onehot.astype(BF16), earlier)
    carry = carry_scr[...]
    rank = jnp.sum(onehot * (before + carry[:, 0:1]), axis=0, keepdims=True)
    carry = carry + jnp.sum(onehot, axis=1, keepdims=True)
    carry_scr[...] = carry
    cnt_ref[...] = carry

    ri_ref[...] = jnp.zeros_like(ri_ref)
    ri_ref[0, 0:1, :] = cls.astype(I32)
    ri_ref[0, 1:2, :] = rank.astype(I32)
    if pack_rows:
        mrow = lax.broadcasted_iota(I32, (LANES, tm), 0)
        batch = pl.program_id(0).astype(F32)
        meta_t = jnp.where(mrow == 0, w_lo, jnp.where(mrow == 1, w_hi, jnp.where(mrow == 2, batch, 0.0)))
        rec_ref[0, :, D + D // 2:] = meta_t.T
    else:
        rf_ref[...] = jnp.zeros_like(rf_ref)
        rf_ref[0, 0:1, :] = w_lo
        rf_ref[0, 1:2, :] = w_hi


def _out_route(x, att, gm, mod, lw, *, tm, pack_rows):
    B, S, D = x.shape
    nt = S // tm
    row = lambda b, i: (b, i, 0)
    tile = lambda b, i: (b * nt + i, 0, 0)
    route_i = (jax.ShapeDtypeStruct((B * nt, SUBLANES, tm), I32), pl.BlockSpec((1, SUBLANES, tm), tile))
    route_f = (jax.ShapeDtypeStruct((B * nt, SUBLANES, tm), F32), pl.BlockSpec((1, SUBLANES, tm), tile))
    counts = (jax.ShapeDtypeStruct((CLASS_ROWS, LANES), F32), _full((CLASS_ROWS, LANES)))
    if pack_rows:
        rec_w = D + D // 2 + LANES
        outs = [(jax.ShapeDtypeStruct((B, S, rec_w), F32), pl.BlockSpec((1, tm, rec_w), row)), route_i, counts]
    else:
        outs = [(jax.ShapeDtypeStruct((B, S, D), F32), pl.BlockSpec((1, tm, D), row)),
                (jax.ShapeDtypeStruct((B, S, D), BF16), pl.BlockSpec((1, tm, D), row)), route_i, route_f, counts]
    out_shape = [o[0] for o in outs]
    out_specs = [o[1] for o in outs]
    return pl.pallas_call(
        functools.partial(_out_route_kernel, pack_rows=pack_rows),
        grid=(B, nt),
        in_specs=[pl.BlockSpec((1, tm, D), row), pl.BlockSpec((1, tm, ATT_WIDTH), row), pl.BlockSpec((1, tm, GM_WIDTH), row),
                  pl.BlockSpec((1, 6, D), lambda b, i: (b, 0, 0)),
                  _full((1, ATT_WIDTH)), _full((ATT_WIDTH, D)), _full((GM_WIDTH, D)), _full((1, D)),
                  _full((D, 2 * ROUTER_COLS)), _full((D, ROUTER_COLS)), _full((1, ROUTER_COLS))],
        out_specs=out_specs,
        out_shape=out_shape,
        scratch_shapes=[pltpu.VMEM((CLASS_ROWS, LANES), F32)],
        compiler_params=_params(2),
    )(x, att, gm, mod, lw["g_out_attn"], lw["w_out_a"], lw["w_out_g"], lw["g_norm_ffn"],
      lw["w_r_a"], lw["w_r_b"], lw["b_r"])


def _swiglu(hb, wgu, wd):
    hid = _dot(hb, wgu)
    act = jax.nn.silu(hid[:, :D_EXPERT]) * hid[:, D_EXPERT:]
    return _dot(act.astype(BF16), wd)


def _moe_pairs_kernel(elo_ref, ehi_ref, nv_ref, dest_ref, rec_hbm, zeros_hbm, gate_ref,
                      wgu_lo_ref, wd_lo_ref, wgu_hi_ref, wd_hi_ref, y_hbm,
                      xbuf0, xbuf1, ybuf0, ybuf1, tok_ref, gsem, ssem, zsem):
    i = pl.program_id(0)
    xbuf, ybuf = (xbuf0, xbuf1), (ybuf0, ybuf1)
    D = ybuf0.shape[-1]
    n_groups_all = MOE_ROWS // SUBLANES

    def gather_start(t, s, g, u):
        pltpu.make_async_copy(rec_hbm.at[pl.ds(t, 1)], xbuf[s].at[g, pl.ds(u, 1)], gsem.at[s]).start()

    def scatter_start(t, s, g, u):
        pltpu.make_async_copy(ybuf[s].at[g, pl.ds(u, 1)], y_hbm.at[pl.ds(t, 1)], ssem.at[s]).start()

    def all_rows(blk, fn):
        base = blk * MOE_ROWS
        for g in range(n_groups_all):
            for u in range(SUBLANES):
                fn(tok_ref[base + g * SUBLANES + u], g, u)

    def valid_rows(blk, fn):
        nv = nv_ref[blk]
        base = blk * MOE_ROWS
        n_groups = nv // SUBLANES

        def group(g, c):
            for u in range(SUBLANES):
                fn(tok_ref[base + g * SUBLANES + u], g, u)
            return c

        def single(r, c):
            fn(tok_ref[base + r], n_groups, r - n_groups * SUBLANES)
            return c

        lax.fori_loop(0, n_groups, group, 0)
        lax.fori_loop(n_groups * SUBLANES, nv, single, 0)

    def gather_wait(s):
        pltpu.make_async_copy(xbuf[s], xbuf[s], gsem.at[s]).wait()

    def scatter_wait(blk, s):
        nv = nv_ref[blk]
        n_groups = nv // SUBLANES
        buf = ybuf[s]

        @pl.when(n_groups > 0)
        def _():
            rows = buf.at[pl.ds(0, n_groups)]
            pltpu.make_async_copy(rows, rows, ssem.at[s]).wait()

        def single(r, c):
            row = buf.at[0, pl.ds(0, 1)]
            pltpu.make_async_copy(row, row, ssem.at[s]).wait()
            return c

        lax.fori_loop(n_groups * SUBLANES, nv, single, 0)

    def experts(s):
        hb = _unpack_bf16_pairs(xbuf[s][:, :, D:D + D // 2].reshape(MOE_ROWS, D // 2))
        meta = xbuf[s][:, :, D + D // 2:].reshape(MOE_ROWS, LANES)
        x = xbuf[s][:, :, 0:D].reshape(MOE_ROWS, D)
        w_lo, w_hi, bidx = meta[:, 0:1], meta[:, 1:2], meta[:, 2:3]
        blane = lax.broadcasted_iota(I32, (1, 2 * MOD_BATCH_PAD), 1)
        blane = jnp.where(blane >= MOD_BATCH_PAD, blane - MOD_BATCH_PAD, blane).astype(F32)
        onehot = jnp.where(bidx == blane, 1.0, 0.0).astype(BF16)
        gate_m = _dot(onehot, gate_ref[...])
        moe = w_lo * _swiglu(hb, wgu_lo_ref[0], wd_lo_ref[0]) + w_hi * _swiglu(hb, wgu_hi_ref[0], wd_hi_ref[0])
        ybuf[s][...] = (x + gate_m * moe).reshape(n_groups_all, SUBLANES, D)

    @pl.when(i == 0)
    def _():
        zero = pltpu.make_async_copy(zeros_hbm, tok_ref, zsem)
        zero.start()
        zero.wait()

        def invert(g, c):
            for u in range(SUBLANES):
                t = g * SUBLANES + u
                tok_ref[dest_ref[t]] = t
            return c
        lax.fori_loop(0, dest_ref.shape[0] // SUBLANES, invert, 0)
        all_rows(0, lambda t, g, u: gather_start(t, 0, g, u))

    prev = jnp.maximum(i - 1, 0)
    has_rows = nv_ref[i] > 0
    prev_full = (i >= 1) & (nv_ref[prev] == MOE_ROWS)
    prev_partial = (i >= 1) & (nv_ref[prev] > 0) & (nv_ref[prev] < MOE_ROWS)

    def step(slot):
        other = 1 - slot

        @pl.when(has_rows)
        def _():
            all_rows(i + 1, lambda t, g, u: gather_start(t, other, g, u))
            gather_wait(slot)

            @pl.when(prev_full)
            def _():
                all_rows(i - 1, lambda t, g, u: scatter_start(t, other, g, u))

            @pl.when(prev_partial)
            def _():
                valid_rows(i - 1, lambda t, g, u: scatter_start(t, other, g, u))

            @pl.when(i >= 2)
            def _():
                scatter_wait(i - 2, slot)
            experts(slot)

        @pl.when(jnp.logical_not(has_rows) & (i >= 1) & (nv_ref[prev] > 0))
        def _():
            gather_wait(slot)
            valid_rows(i - 1, lambda t, g, u: scatter_start(t, other, g, u))

            @pl.when(i >= 2)
            def _():
                scatter_wait(i - 2, slot)
            scatter_wait(i - 1, other)

    for parity in (0, 1):
        pl.when(i % 2 == parity)(functools.partial(step, parity))


def _moe_pairs(rec, blk_elo, blk_ehi, blk_nv, dest, gate_tab, lw):
    T, rec_w = rec.shape
    assert T % SUBLANES == 0
    D = gate_tab.shape[1]
    nb = blk_nv.shape[0]
    wgu, wd = lw["w_gu_e"], lw["w_d_e"]
    grid_spec = pltpu.PrefetchScalarGridSpec(
        num_scalar_prefetch=4,
        grid=(nb,),
        in_specs=[
            pl.BlockSpec(memory_space=pl.ANY),
            pl.BlockSpec(memory_space=pl.ANY),
            pl.BlockSpec(gate_tab.shape, lambda i, *_: (0, 0)),
            pl.BlockSpec((1, D, 2 * D_EXPERT), lambda i, elo, ehi, nv, tok: (elo[i], 0, 0)),
            pl.BlockSpec((1, D_EXPERT, D), lambda i, elo, ehi, nv, tok: (elo[i], 0, 0)),
            pl.BlockSpec((1, D, 2 * D_EXPERT), lambda i, elo, ehi, nv, tok: (ehi[i], 0, 0)),
            pl.BlockSpec((1, D_EXPERT, D), lambda i, elo, ehi, nv, tok: (ehi[i], 0, 0)),
        ],
        out_specs=pl.BlockSpec(memory_space=pl.ANY),
        scratch_shapes=[pltpu.VMEM((MOE_ROWS // SUBLANES, SUBLANES, rec_w), F32),
                        pltpu.VMEM((MOE_ROWS // SUBLANES, SUBLANES, rec_w), F32),
                        pltpu.VMEM((MOE_ROWS // SUBLANES, SUBLANES, D), F32),
                        pltpu.VMEM((MOE_ROWS // SUBLANES, SUBLANES, D), F32),
                        pltpu.SMEM((nb * MOE_ROWS,), I32),
                        pltpu.SemaphoreType.DMA((2,)), pltpu.SemaphoreType.DMA((2,)), pltpu.SemaphoreType.DMA(())],
    )
    return pl.pallas_call(
        _moe_pairs_kernel,
        grid_spec=grid_spec,
        out_shape=jax.ShapeDtypeStruct((T, D), F32),
        compiler_params=_params(1),
    )(blk_elo, blk_ehi, blk_nv, dest, rec, jnp.zeros((nb * MOE_ROWS,), I32), gate_tab, wgu, wd, wgu, wd)


def _small_lookup(table, idx):
    ids = jnp.arange(table.shape[0], dtype=I32)
    return jnp.sum(jnp.where(idx[:, None] == ids[None, :], table[None, :], 0), axis=1)


def _pair_tables():
    lo, hi = [], []
    for g in range(N_GROUPS):
        for a in range(EXPERTS_PER_GROUP):
            for b in range(a + 1, EXPERTS_PER_GROUP):
                lo.append(g * EXPERTS_PER_GROUP + a)
                hi.append(g * EXPERTS_PER_GROUP + b)
    return np.asarray(lo, np.int32), np.asarray(hi, np.int32)


def _moe_prompt(rec, cls, rank, counts, mod, lw):
    B, S, rec_w = rec.shape
    D = mod.shape[-1]
    T = B * S
    nb = T // MOE_ROWS + N_CLASSES + 1
    nblk = (counts + MOE_ROWS - 1) // MOE_ROWS
    blk_end = jnp.cumsum(nblk)
    blk_start = blk_end - nblk
    dest = _small_lookup(blk_start, cls) * MOE_ROWS + rank
    ids = jnp.arange(nb, dtype=I32)
    used = blk_end[-1]
    class_of = lambda blk: jnp.sum((blk_end[None, :] <= blk[:, None]).astype(I32), axis=1)
    blk_cls = jnp.where(ids < used, jnp.minimum(class_of(ids), N_CLASSES - 1), class_of(used[None] - 1))
    first_row = (ids - _small_lookup(blk_start, blk_cls)) * MOE_ROWS
    blk_nv = jnp.where(ids < used, jnp.clip(_small_lookup(counts, blk_cls) - first_row, 0, MOE_ROWS), 0)
    lo_tab, hi_tab = _pair_tables()
    blk_elo = _small_lookup(jnp.asarray(lo_tab), blk_cls)
    blk_ehi = _small_lookup(jnp.asarray(hi_tab), blk_cls)
    gate_hi, gate_lo = _split_bf16(jnp.pad(mod[:, 5, :], ((0, MOD_BATCH_PAD - B), (0, 0))))
    gate_tab = jnp.concatenate([gate_hi, gate_lo], axis=0)
    y = _moe_pairs(rec.reshape(T, rec_w), blk_elo, blk_ehi, blk_nv.astype(I32), dest.astype(I32), gate_tab, lw)
    return y.reshape(B, S, D)


def _moe_dense_kernel(h_ref, x1_ref, gate_ref, w_ref, sel_ref, wgu_ref, wd_ref, y_ref):
    e = pl.program_id(0)

    @pl.when(e == 0)
    def _():
        y_ref[...] = jnp.zeros_like(y_ref)

    ye = _swiglu(h_ref[...], wgu_ref[0], wd_ref[0])
    y_ref[...] += jnp.where(sel_ref[0] > 0.5, w_ref[0] * ye, 0.0)

    @pl.when(e == pl.num_programs(0) - 1)
    def _():
        y_ref[...] = x1_ref[...] + gate_ref[...] * y_ref[...]


def _moe_dense(h2, x1, gate_rows, w_sel, sel, lw):
    T, D = x1.shape
    per_e = lambda e: (e, 0, 0)
    return pl.pallas_call(
        _moe_dense_kernel,
        grid=(N_EXPERTS,),
        in_specs=[_full((T, D)), _full((T, D)), _full((T, D)),
                  pl.BlockSpec((1, T, 1), per_e), pl.BlockSpec((1, T, 1), per_e),
                  pl.BlockSpec((1, D, 2 * D_EXPERT), per_e), pl.BlockSpec((1, D_EXPERT, D), per_e)],
        out_specs=_full((T, D)),
        out_shape=jax.ShapeDtypeStruct((T, D), F32),
        compiler_params=_params(1),
    )(h2, x1, gate_rows, w_sel, sel, lw["w_gu_e"], lw["w_d_e"])


def _rope_tables(pos, gain, scale):
    inv = 1.0 / (ROPE_BASE ** (jnp.arange(ROPE_HALF, dtype=F32) / ROPE_HALF))
    ang = pos.astype(F32)[:, None] * inv[None, :]
    cos, sin = jnp.cos(ang) * scale, jnp.sin(ang) * scale
    n = pos.shape[0]
    z = lambda w: jnp.zeros((n, w), F32)
    pad = HEAD_PAD - QK_DIM
    g_nope, g1, g2 = gain[:NOPE_DIM], gain[NOPE_DIM:NOPE_DIM + ROPE_HALF], gain[NOPE_DIM + ROPE_HALF:]
    tab_c = jnp.concatenate([jnp.broadcast_to(g_nope * scale, (n, NOPE_DIM)), cos * g1, cos * g2, z(pad)], axis=1)
    tab_s = jnp.concatenate([z(NOPE_DIM), -sin * g2, sin * g1, z(pad)], axis=1)
    return tab_c, tab_s


def _prep_layer(w, l, chunk_lens):
    D = w["w_in"].shape[1]
    H = MLA_HEADS
    lw = {}
    row = lambda name: w[name][l].reshape(1, -1)
    for name in ("g_norm_mix", "g_q_lat", "g_kv_lat", "g_ln_v", "b_ln_v", "g_out_attn", "g_out_gmlp", "g_norm_ffn"):
        lw[name] = row(name)
    w_in = w["w_in"][l]
    o1, o2, o3 = Q_LORA, Q_LORA + KV_LORA, Q_LORA + KV_LORA + ROPE_DIM
    o4 = o3 + GM_WIDTH
    z32 = jnp.zeros((D, ROPE_DIM), F32)
    wr = w_in[:, o2:o3]
    lw["w_in_p"] = jnp.concatenate([w_in[:, :o2], wr, z32, wr, z32, w_in[:, o3:o4], w_in[:, o4:]], axis=1).astype(BF16)
    assert lw["w_in_p"].shape[1] == PROJ_COLS
    pad = HEAD_PAD - QK_DIM
    w_uq = w["w_uq"][l].reshape(Q_LORA, H, QK_DIM)
    w_uq_swapped = jnp.concatenate([jnp.zeros((Q_LORA, H, NOPE_DIM), F32), w_uq[:, :, NOPE_DIM + ROPE_HALF:],
                                    w_uq[:, :, NOPE_DIM:NOPE_DIM + ROPE_HALF]], axis=2)
    head_pad = lambda a: jnp.pad(a, ((0, 0), (0, 0), (0, pad))).reshape(Q_LORA, H * HEAD_PAD)
    lw["w_uq_p"] = jnp.concatenate([head_pad(w_uq), head_pad(w_uq_swapped)], axis=1).astype(BF16)
    w_ukv = w["w_ukv"][l].reshape(KV_LORA, H, NOPE_DIM + V_DIM)
    lw["w_k_p"] = jnp.pad(w_ukv[:, :, :NOPE_DIM], ((0, 0), (0, 0), (0, HEAD_PAD - NOPE_DIM))).reshape(KV_LORA, H * HEAD_PAD).astype(BF16)
    w_v = w_ukv[:, :, NOPE_DIM:]
    zeros_v = jnp.zeros_like(w_v)
    odd_head = (jnp.arange(H) % 2 == 1)[None, :, None]
    w_v_lo, w_v_hi = jnp.where(odd_head, zeros_v, w_v), jnp.where(odd_head, w_v, zeros_v)
    lw["w_v_p"] = jnp.concatenate([w_v_lo, w_v_hi], axis=2).reshape(KV_LORA, H * HEAD_PAD).astype(BF16)
    lw["scores_bounded"] = _scores_bounded(w, l)
    lw["g_qnorm"] = w["g_qnorm"][l]
    lw["g_knorm"] = w["g_knorm"][l]
    for L in chunk_lens:
        wsp = w["w_spatial"][l][:, :L, :L]
        lw["w_sp_pairs_%d" % L] = wsp.reshape(GM_HEADS // 2, 2 * L, L).astype(BF16)
        lw["b_sp_rows_%d" % L] = jnp.repeat(jnp.transpose(w["b_spatial"][l][:, :L]), GM_HEAD_DIM, axis=1)
    w_out = w["w_out"][l].astype(BF16)
    lw["w_out_a"], lw["w_out_g"] = w_out[:ATT_WIDTH], w_out[ATT_WIDTH:]
    wr_full = jnp.zeros((D, ROUTER_COLS), F32)
    wr_full = wr_full.at[:, :N_GROUPS].set(w["w_router_group"][l])
    wr_full = wr_full.at[:, ROUTER_EXPERT_LANE0:ROUTER_EXPERT_LANE0 + N_EXPERTS].set(w["w_router_expert"][l])
    r_hi, r_lo = _split_bf16(wr_full)
    lw["w_r_a"] = jnp.concatenate([r_hi, r_lo], axis=1)
    lw["w_r_b"] = r_hi
    br = jnp.zeros((1, ROUTER_COLS), F32)
    br = br.at[0, :N_GROUPS].set(w["b_router_group"][l])
    lw["b_r"] = br.at[0, ROUTER_EXPERT_LANE0:ROUTER_EXPERT_LANE0 + N_EXPERTS].set(w["b_router_expert"][l])
    lw["w_gu_e"] = jnp.concatenate([w["w_gate_e"][l], w["w_up_e"][l]], axis=-1).astype(BF16)
    lw["w_d_e"] = w["w_down_e"][l].astype(BF16)
    return lw


def _tiles(seq):
    tm = min(seq, 512)
    return tm, min(seq, 512)


def _layer_prompt(x, mod, lw):
    B, S, D = x.shape
    tm, tq = _tiles(S)
    pos = jnp.arange(S)
    q_tabs = _rope_tables(pos, lw["g_qnorm"], QK_DIM ** -0.5 * LOG2E)
    k_tabs = _rope_tables(pos, lw["g_knorm"], 1.0)
    ckv, krope, q, gm, k, v = _mix_in(x, mod, lw, q_tabs, k_tabs, tm=tm, chunk_len=GM_CHUNK, emit_kv=True, emit_vrows=False)
    att = _attention(lw["scores_bounded"], q, k, v, tq=tq, tk=tq, q_off=0, kv_valid=S)
    rec, ri, cnt = _out_route(x, att, gm, mod, lw, tm=tm, pack_rows=True)
    cls = ri[:, 0, :].reshape(B * S)
    rank = ri[:, 1, :].reshape(B * S)
    counts = cnt[:N_CLASSES, 0].astype(I32)
    y = _moe_prompt(rec, cls, rank, counts, mod, lw)
    return y, ckv, krope


def _layer_sample(x, mod, past_ckv, past_krope, lw):
    B, S, D = x.shape
    past = past_ckv.shape[1]
    q_tabs = _rope_tables(past + jnp.arange(S), lw["g_qnorm"], QK_DIM ** -0.5 * LOG2E)
    ckv, krope, q, gm, v_rows = _mix_in(x, mod, lw, q_tabs, q_tabs, tm=S, chunk_len=S, emit_kv=False, emit_vrows=True)
    kv_valid = past + S
    kv_pad = -(-kv_valid // LANES) * LANES
    extra = kv_pad - kv_valid
    ckv_all = jnp.concatenate([past_ckv, ckv, jnp.zeros((B, extra, KV_LORA), F32)], axis=1)
    kr_all = jnp.concatenate([past_krope, krope, jnp.zeros((B, extra, ROPE_DIM), F32)], axis=1)
    kr_slot = jnp.pad(kr_all, ((0, 0), (0, 0), (NOPE_DIM, HEAD_PAD - QK_DIM)))
    k_tabs = _rope_tables(jnp.arange(kv_pad), lw["g_knorm"], 1.0)
    k, v = _kv_latent(ckv_all, kr_slot, lw, k_tabs, tr=kv_pad)
    att = _attention(lw["scores_bounded"], q, k, v, tq=S, tk=kv_pad, q_off=past, kv_valid=kv_valid)
    x1, h2, ri, rf, _ = _out_route(x, att, gm, mod, lw, tm=S, pack_rows=False)
    T = B * S
    cls, w_lo, w_hi = ri[:, 0, :].reshape(T), rf[:, 0, :].reshape(T), rf[:, 1, :].reshape(T)
    lo_tab, hi_tab = _pair_tables()
    e_lo, e_hi = jnp.asarray(lo_tab)[cls], jnp.asarray(hi_tab)[cls]
    eids = jnp.arange(N_EXPERTS, dtype=I32)[:, None]
    is_lo, is_hi = eids == e_lo[None, :], eids == e_hi[None, :]
    w_sel = (jnp.where(is_lo, w_lo[None, :], 0.0) + jnp.where(is_hi, w_hi[None, :], 0.0))[:, :, None]
    sel = (is_lo | is_hi).astype(F32)[:, :, None]
    gate_rows = jnp.repeat(mod[:, 5, :], S, axis=0)
    y = _moe_dense(h2.reshape(T, D), x1.reshape(T, D), gate_rows, w_sel, sel, lw)
    return y.reshape(B, S, D), ckv, krope, v_rows


def kernel(x_prompt, x_sample, cache_ckv, cache_krope, c_prompt, c_sample, w_ada, b_ada, g_norm_mix, w_in, g_q_lat, w_uq, g_kv_lat, w_ukv, g_qnorm, g_knorm, g_ln_v, b_ln_v, w_spatial, b_spatial, g_out_attn, g_out_gmlp, w_out, g_norm_ffn, w_router_group, b_router_group, w_router_expert, b_router_expert, w_gate_e, w_up_e, w_down_e):
    w = dict(w_in=w_in, g_norm_mix=g_norm_mix, g_q_lat=g_q_lat, w_uq=w_uq, g_kv_lat=g_kv_lat, w_ukv=w_ukv,
             g_qnorm=g_qnorm, g_knorm=g_knorm, g_ln_v=g_ln_v, b_ln_v=b_ln_v, w_spatial=w_spatial, b_spatial=b_spatial,
             g_out_attn=g_out_attn, g_out_gmlp=g_out_gmlp, w_out=w_out, g_norm_ffn=g_norm_ffn,
             w_router_group=w_router_group, b_router_group=b_router_group, w_router_expert=w_router_expert,
             b_router_expert=b_router_expert, w_gate_e=w_gate_e, w_up_e=w_up_e, w_down_e=w_down_e)
    depth = w_ada.shape[0]
    Bp, Sp, D = x_prompt.shape
    Bs, Ss, _ = x_sample.shape
    assert Sp % GM_CHUNK == 0 and Ss <= GM_CHUNK and Ss % CHUNK == 0 and Bp <= MOD_BATCH_PAD
    c_all = jnp.concatenate([c_prompt, c_sample], axis=0)
    y_p, y_s = x_prompt, x_sample
    outs = [[] for _ in range(5)]
    for l in range(depth):
        lw = _prep_layer(w, l, (GM_CHUNK, Ss))
        mod = _ada_mod(c_all, w_ada[l], b_ada[l]).reshape(Bp + Bs, 6, D)
        y_p, ckv_p, kr_p = _layer_prompt(y_p, mod[:Bp], lw)
        y_s, ckv_s, kr_s, v_s = _layer_sample(y_s, mod[Bp:], cache_ckv[l], cache_krope[l], lw)
        for lst, val in zip(outs, (ckv_p, kr_p, ckv_s, kr_s, v_s)):
            lst.append(val)
    return (y_p, y_s) + tuple(jnp.stack(lst) for lst in outs)
```

```python
import functools

import numpy as np
import jax
import jax.numpy as jnp
from jax import lax
from jax.experimental import pallas as pl
from jax.experimental.pallas import tpu as pltpu

F32 = jnp.float32
BF16 = jnp.bfloat16
I32 = jnp.int32

CHUNK = 64
CHUNK_SHIFT = 6
EPS = 1e-6
MLA_HEADS = 8
Q_LORA = 256
KV_LORA = 128
NOPE_DIM = 64
ROPE_DIM = 32
ROPE_HALF = ROPE_DIM // 2
V_DIM = 64
QK_DIM = NOPE_DIM + ROPE_DIM
ATT_WIDTH = MLA_HEADS * V_DIM
ROPE_BASE = 10000.0
GM_HEADS = 8
GM_HEAD_DIM = 64
GM_WIDTH = GM_HEADS * GM_HEAD_DIM
GM_CHUNK = 128
N_GROUPS = 4
EXPERTS_PER_GROUP = 8
N_EXPERTS = N_GROUPS * EXPERTS_PER_GROUP
D_EXPERT = 256
PAIRS_PER_GROUP = EXPERTS_PER_GROUP * (EXPERTS_PER_GROUP - 1) // 2
N_CLASSES = N_GROUPS * PAIRS_PER_GROUP

LANES = 128
SUBLANES = 8
HEAD_PAD = LANES
PROJ_COLS = 1536
ROUTER_COLS = LANES
ROUTER_EXPERT_LANE0 = SUBLANES
CLASS_ROWS = LANES
MOE_ROWS = 128
MOD_BATCH_PAD = 16
VMEM_LIMIT = 48 * 1024 * 1024
NEG_BIG = -1e30
LOG2E = 1.4426950408889634
BF16_SLACK = 1.02
SCORE_BOUND = 90.0

assert CHUNK == 1 << CHUNK_SHIFT


def _params(n_axes, vmem=VMEM_LIMIT):
    return pltpu.CompilerParams(dimension_semantics=("arbitrary",) * n_axes, vmem_limit_bytes=vmem)


def _full(shape):
    nd = len(shape)
    return pl.BlockSpec(shape, lambda *_: (0,) * nd)


def _split_bf16(x):
    hi = x.astype(BF16)
    lo = (x - hi.astype(F32)).astype(BF16)
    return hi, lo


def _dot(a, b):
    return jnp.dot(a, b, preferred_element_type=F32)


def _ada_kernel(c_ref, w_ref, b_ref, o_ref):
    a_hi, a_lo = _split_bf16(jax.nn.silu(c_ref[...]))
    w_hi, w_lo = _split_bf16(w_ref[...])
    o_ref[...] = _dot(a_hi, w_hi) + _dot(a_lo, w_hi) + _dot(a_hi, w_lo) + b_ref[...]


def _ada_mod(c, w_ada, b_ada):
    n, d = c.shape
    cols = w_ada.shape[1]
    tn = 1536
    return pl.pallas_call(
        _ada_kernel,
        grid=(cols // tn,),
        in_specs=[_full((n, d)), pl.BlockSpec((d, tn), lambda j: (0, j)), pl.BlockSpec((1, tn), lambda j: (0, j))],
        out_specs=pl.BlockSpec((n, tn), lambda j: (0, j)),
        out_shape=jax.ShapeDtypeStruct((n, cols), F32),
        compiler_params=_params(1),
    )(c, w_ada, b_ada.reshape(1, cols))


def _rms(x, g):
    return x * lax.rsqrt(jnp.mean(x * x, axis=-1, keepdims=True) + EPS) * g


def _head_norm_rope(xh, xh_swapped, tab_c, tab_s):
    ms = jnp.sum(xh * xh, axis=-1, keepdims=True) * (1.0 / QK_DIM)
    return (xh * tab_c + xh_swapped * tab_s) * lax.rsqrt(ms + EPS)


def _keys_values(ckv, kr_slot, gkv, wk, wv, tab_c, tab_s, k_ref, v_ref):
    cb = _rms(ckv, gkv).astype(BF16)
    kall = _dot(cb, wk)
    vall = _dot(cb, wv)
    lane = lax.broadcasted_iota(I32, (1, LANES), 1)
    ones_hi = jnp.where(lane >= V_DIM, 1.0, 0.0)
    ones_lo = 1.0 - ones_hi
    kr_swapped = jnp.where(lane < NOPE_DIM + ROPE_HALF, pltpu.roll(kr_slot, HEAD_PAD - ROPE_HALF, 1),
                           pltpu.roll(kr_slot, ROPE_HALF, 1))
    kr_swapped = jnp.where((lane >= NOPE_DIM) & (lane < QK_DIM), kr_swapped, 0.0)
    for h in range(MLA_HEADS):
        kh = kall[:, h * HEAD_PAD:(h + 1) * HEAD_PAD] + kr_slot
        k_ref[0, h] = _head_norm_rope(kh, kr_swapped, tab_c, tab_s).astype(BF16)
        ones = ones_hi if h % 2 == 0 else ones_lo
        v_ref[0, h] = (vall[:, h * HEAD_PAD:(h + 1) * HEAD_PAD] + ones).astype(BF16)


def _mix_in_kernel(x_ref, mod_ref, gmix_ref, win_ref, gql_ref, wuq_ref, gkv_ref, wk_ref, wv_ref,
                   cq_ref, sq_ref, ck_ref, sk_ref, glnv_ref, blnv_ref, wsp_ref, bsp_ref,
                   ggm_ref, *rest, chunk_len, emit_kv, emit_vrows):
    outs = list(rest[:-1])
    mixed_scr = rest[-1]
    ckv_ref, kr_ref, q_ref, gm_ref = outs[:4]
    outs = outs[4:]
    if emit_kv:
        k_ref, v_ref = outs[:2]
        outs = outs[2:]
    if emit_vrows:
        vrows_ref = outs[0]

    x = x_ref[0]
    tm = x.shape[0]
    shift, scale = mod_ref[0, 0:1, :], mod_ref[0, 1:2, :]
    h = _rms(x, gmix_ref[...]) * (1.0 + scale) + shift
    proj = _dot(h.astype(BF16), win_ref[...])

    q_lat = proj[:, 0:Q_LORA]
    ckv = proj[:, Q_LORA:Q_LORA + KV_LORA]
    kr_blk = proj[:, Q_LORA + KV_LORA:Q_LORA + KV_LORA + LANES]
    ckv_ref[0] = ckv
    kr_ref[0] = kr_blk[:, 0:ROPE_DIM]

    q = _dot(_rms(q_lat, gql_ref[...]).astype(BF16), wuq_ref[...])
    cq, sq = cq_ref[...], sq_ref[...]
    n_q = MLA_HEADS * HEAD_PAD
    for hd in range(MLA_HEADS):
        qh = q[:, hd * HEAD_PAD:(hd + 1) * HEAD_PAD]
        qh_swapped = q[:, n_q + hd * HEAD_PAD:n_q + (hd + 1) * HEAD_PAD]
        q_ref[0, hd] = _head_norm_rope(qh, qh_swapped, cq, sq).astype(BF16)

    if emit_kv:
        lane = lax.broadcasted_iota(I32, (1, LANES), 1)
        kr_slot = jnp.where(lane >= NOPE_DIM, kr_blk, 0.0)
        _keys_values(ckv, kr_slot, gkv_ref[...], wk_ref[...], wv_ref[...], ck_ref[...], sk_ref[...], k_ref, v_ref)

    g_u = proj[:, 512:512 + GM_WIDTH]
    g_v = proj[:, 1024:1024 + GM_WIDTH]
    u = jax.nn.gelu(g_u)
    gv = jax.nn.gelu(g_v)
    mu = jnp.mean(gv, axis=-1, keepdims=True)
    xc = gv - mu
    var = jnp.mean(xc * xc, axis=-1, keepdims=True)
    v_rows = xc * lax.rsqrt(var + EPS) * glnv_ref[...] + blnv_ref[...]
    if emit_vrows:
        vrows_ref[0] = v_rows
    vb = v_rows.astype(BF16)

    L = chunk_len
    t = lax.broadcasted_iota(I32, (2 * L, L), 0)
    s = lax.broadcasted_iota(I32, (2 * L, L), 1)
    t = jnp.where(t >= L, t - L, t)
    allowed = (s >> CHUNK_SHIFT) <= (t >> CHUNK_SHIFT)
    lane = lax.broadcasted_iota(I32, (1, LANES), 1)
    first_head = lane < GM_HEAD_DIM
    for p in range(GM_HEADS // 2):
        w_pair = jnp.where(allowed, wsp_ref[p], jnp.zeros((), BF16))
        for c in range(tm // L):
            vp = vb[c * L:(c + 1) * L, p * LANES:(p + 1) * LANES]
            r = _dot(w_pair, vp)
            mixed = jnp.where(first_head, r[:L], r[L:])
            mixed_scr[c * L:(c + 1) * L, p * LANES:(p + 1) * LANES] = mixed + bsp_ref[:, p * LANES:(p + 1) * LANES]
    gm = u * mixed_scr[...]
    gm_ref[0] = _rms(gm, ggm_ref[...]).astype(BF16)


def _mix_in(x, mod, lw, q_tabs, k_tabs, *, tm, chunk_len, emit_kv, emit_vrows):
    B, S, D = x.shape
    nt = S // tm
    H = MLA_HEADS
    row = lambda b, i: (b, i, 0)
    tab = pl.BlockSpec((tm, LANES), lambda b, i: (i, 0))
    in_specs = [
        pl.BlockSpec((1, tm, D), row),
        pl.BlockSpec((1, 6, D), lambda b, i: (b, 0, 0)),
        _full((1, D)), _full((D, PROJ_COLS)), _full((1, Q_LORA)), _full((Q_LORA, 2 * H * HEAD_PAD)),
        _full((1, KV_LORA)), _full((KV_LORA, H * HEAD_PAD)), _full((KV_LORA, H * HEAD_PAD)),
        tab, tab, tab, tab,
        _full((1, GM_WIDTH)), _full((1, GM_WIDTH)),
        _full((GM_HEADS // 2, 2 * chunk_len, chunk_len)), _full((chunk_len, GM_WIDTH)), _full((1, GM_WIDTH)),
    ]
    out_shape = [
        jax.ShapeDtypeStruct((B, S, KV_LORA), F32),
        jax.ShapeDtypeStruct((B, S, ROPE_DIM), F32),
        jax.ShapeDtypeStruct((B, H, S, HEAD_PAD), BF16),
        jax.ShapeDtypeStruct((B, S, GM_WIDTH), BF16),
    ]
    head_blk = pl.BlockSpec((1, H, tm, HEAD_PAD), lambda b, i: (b, 0, i, 0))
    out_specs = [
        pl.BlockSpec((1, tm, KV_LORA), row),
        pl.BlockSpec((1, tm, ROPE_DIM), row),
        head_blk,
        pl.BlockSpec((1, tm, GM_WIDTH), row),
    ]
    if emit_kv:
        out_shape += [jax.ShapeDtypeStruct((B, H, S, HEAD_PAD), BF16), jax.ShapeDtypeStruct((B, H, S, HEAD_PAD), BF16)]
        out_specs += [head_blk, head_blk]
    if emit_vrows:
        out_shape += [jax.ShapeDtypeStruct((B, S, GM_WIDTH), F32)]
        out_specs += [pl.BlockSpec((1, tm, GM_WIDTH), row)]
    kern = functools.partial(_mix_in_kernel, chunk_len=chunk_len, emit_kv=emit_kv, emit_vrows=emit_vrows)
    return pl.pallas_call(
        kern,
        grid=(B, nt),
        in_specs=in_specs,
        out_specs=out_specs,
        out_shape=out_shape,
        scratch_shapes=[pltpu.VMEM((tm, GM_WIDTH), F32)],
        compiler_params=_params(2),
    )(x, mod, lw["g_norm_mix"], lw["w_in_p"], lw["g_q_lat"], lw["w_uq_p"], lw["g_kv_lat"], lw["w_k_p"], lw["w_v_p"],
      *q_tabs, *k_tabs, lw["g_ln_v"], lw["b_ln_v"],
      lw["w_sp_pairs_%d" % chunk_len], lw["b_sp_rows_%d" % chunk_len], lw["g_out_gmlp"])


def _kv_latent_kernel(ckv_ref, kr_ref, gkv_ref, wk_ref, wv_ref, c_ref, s_ref, k_ref, v_ref):
    _keys_values(ckv_ref[0], kr_ref[0], gkv_ref[...], wk_ref[...], wv_ref[...], c_ref[...], s_ref[...], k_ref, v_ref)


def _kv_latent(ckv_all, kr_slot_all, lw, k_tabs, *, tr):
    B, K, _ = ckv_all.shape
    H = MLA_HEADS
    row = lambda b, i: (b, i, 0)
    tab = pl.BlockSpec((tr, LANES), lambda b, i: (i, 0))
    return pl.pallas_call(
        _kv_latent_kernel,
        grid=(B, K // tr),
        in_specs=[pl.BlockSpec((1, tr, KV_LORA), row), pl.BlockSpec((1, tr, LANES), row),
                  _full((1, KV_LORA)), _full((KV_LORA, H * HEAD_PAD)), _full((KV_LORA, H * HEAD_PAD)),
                  tab, tab],
        out_specs=[pl.BlockSpec((1, H, tr, HEAD_PAD), lambda b, i: (b, 0, i, 0))] * 2,
        out_shape=[jax.ShapeDtypeStruct((B, H, K, HEAD_PAD), BF16)] * 2,
        compiler_params=_params(2),
    )(ckv_all, kr_slot_all, lw["g_kv_lat"], lw["w_k_p"], lw["w_v_p"], *k_tabs)


def _attn_kernel(bounded_ref, q_ref, k_ref, v_ref, o_ref, acc_ref, *, tq, tk, n_q, q_off, kv_valid, split_diag):
    i = pl.program_id(2) if n_q > 1 else 0
    q_first = q_off + i * tq
    vis_first = jnp.minimum(((q_first >> CHUNK_SHIFT) + 1) << CHUNK_SHIFT, kv_valid)
    vis_last = jnp.minimum((((q_first + tq - 1) >> CHUNK_SHIFT) + 1) << CHUNK_SHIFT, kv_valid)
    n_unmasked = vis_first // tk
    n_total = (vis_last + tk - 1) // tk

    q_pos = q_first + lax.broadcasted_iota(I32, (tq, 1), 0)
    limit = jnp.minimum(((q_pos >> CHUNK_SHIFT) + 1) << CHUNK_SHIFT, kv_valid)
    lane = lax.broadcasted_iota(I32, (1, LANES), 1)

    def scores(j, start, masked, r0=0, nr=tq, nk=tk):
        s = lax.dot_general(q_ref[0, j, r0:r0 + nr, :], k_ref[0, j, pl.ds(start, nk), :], (((1,), (1,)), ((), ())),
                            preferred_element_type=F32)
        if masked:
            k_pos = start + lax.broadcasted_iota(I32, (1, nk), 1)
            s = jnp.where(k_pos < limit[r0:r0 + nr], s, NEG_BIG)
        return s

    def plain(start, masked, r0=0, nr=tq, nk=tk):
        for j in range(2):
            p = jnp.exp2(scores(j, start, masked, r0, nr, nk)).astype(BF16)
            acc_ref[j, r0:r0 + nr, :] += _dot(p, v_ref[0, j, pl.ds(start, nk), :])

    def online_block(kb, carry, masked):
        start = pl.multiple_of(kb * tk, tk)
        new = []
        for j in range(2):
            m, acc = carry[2 * j:2 * j + 2]
            s = scores(j, start, masked)
            m_new = jnp.maximum(m, jnp.max(s, axis=-1, keepdims=True))
            p = jnp.exp2(s - m_new).astype(BF16)
            acc = jnp.exp2(m - m_new) * acc + _dot(p, v_ref[0, j, pl.ds(start, tk), :])
            new += [m_new, acc]
        return tuple(new)

    def finish(acc0, acc1):
        first = lane < V_DIM
        num = jnp.where(first, acc0, acc1)
        den = pltpu.roll(jnp.where(first, acc1, acc0), V_DIM, 1)
        o_ref[0] = (num / den).astype(BF16)

    @pl.when(bounded_ref[0] == 1)
    def _():
        acc_ref[...] = jnp.zeros_like(acc_ref)
        at = lambda kb: pl.multiple_of(kb * tk, tk)

        def pair(p, c):
            plain(at(2 * p), False)
            plain(at(2 * p + 1), False)
            return c

        def single(kb, c, masked):
            plain(at(kb), masked)
            return c

        n_pairs = n_unmasked // 2
        lax.fori_loop(0, n_pairs, pair, 0)
        lax.fori_loop(2 * n_pairs, n_unmasked, lambda kb, c: single(kb, c, False), 0)
        if split_diag:
            half = tq // 2
            plain(at(n_unmasked), True, 0, half, half)
            plain(at(n_unmasked), True, half, half, tk)
        else:
            lax.fori_loop(n_unmasked, n_total, lambda kb, c: single(kb, c, True), 0)
        finish(acc_ref[0], acc_ref[1])

    @pl.when(bounded_ref[0] != 1)
    def _():
        zeros = jnp.zeros((tq, LANES), F32)
        m0 = jnp.full((tq, 1), NEG_BIG, F32)
        c = lax.fori_loop(0, n_unmasked, lambda kb, c: online_block(kb, c, False), (m0, zeros, m0, zeros))
        c = lax.fori_loop(n_unmasked, n_total, lambda kb, c: online_block(kb, c, True), c)
        finish(c[1], c[3])


def _attention(bounded, q, k, v, *, tq, tk, q_off, kv_valid):
    B, H, Sq, _ = q.shape
    Sk = k.shape[2]
    split_diag = tq == tk and q_off == 0 and kv_valid >= Sq and tq % (2 * CHUNK) == 0
    kern = functools.partial(_attn_kernel, tq=tq, tk=tk, n_q=Sq // tq, q_off=q_off, kv_valid=kv_valid,
                             split_diag=split_diag)
    grid_spec = pltpu.PrefetchScalarGridSpec(
        num_scalar_prefetch=1,
        grid=(B, H // 2, Sq // tq),
        in_specs=[pl.BlockSpec((1, 2, tq, HEAD_PAD), lambda b, hp, i, f: (b, hp, i, 0)),
                  pl.BlockSpec((1, 2, Sk, HEAD_PAD), lambda b, hp, i, f: (b, hp, 0, 0)),
                  pl.BlockSpec((1, 2, Sk, HEAD_PAD), lambda b, hp, i, f: (b, hp, 0, 0))],
        out_specs=pl.BlockSpec((1, tq, LANES), lambda b, hp, i, f: (b, i, hp)),
        scratch_shapes=[pltpu.VMEM((2, tq, LANES), F32)],
    )
    return pl.pallas_call(
        kern,
        grid_spec=grid_spec,
        out_shape=jax.ShapeDtypeStruct((B, Sq, ATT_WIDTH), BF16),
        compiler_params=_params(3),
    )(bounded, q, k, v)


def _scores_bounded(w, l):
    gq = jnp.max(jnp.abs(w["g_qnorm"][l]))
    gk = jnp.max(jnp.abs(w["g_knorm"][l]))
    bound = (QK_DIM ** 0.5) * LOG2E * BF16_SLACK * gq * gk
    return (bound <= SCORE_BOUND).astype(I32).reshape(1)


def _record_rows(d_model):
    used = d_model // LANES + d_model // 2 // LANES + 1
    return -(-used // SUBLANES) * SUBLANES


def _pack_bf16_pairs(h):
    n = h.shape[1] // 2
    hi = pltpu.bitcast(h[:, :n].astype(BF16).astype(F32), jnp.uint32)
    lo = pltpu.bitcast(h[:, n:].astype(BF16).astype(F32), jnp.uint32)
    return pltpu.bitcast(hi | (lo >> 16), F32)


def _unpack_bf16_pairs(words):
    w = pltpu.bitcast(words, jnp.uint32)
    hi = pltpu.bitcast(w & jnp.uint32(0xFFFF0000), F32)
    lo = pltpu.bitcast(w << 16, F32)
    return jnp.concatenate([hi, lo], axis=1).astype(BF16)


def _out_route_kernel(x_ref, att_ref, gm_ref, mod_ref, goa_ref, woa_ref, wog_ref, gffn_ref, wra_ref, wrb_ref, br_ref,
                      *rest, pack_rows):
    carry_scr = rest[-1]
    if pack_rows:
        rec_ref, ri_ref, cnt_ref = rest[:3]
    else:
        x1_ref, h2_ref, ri_ref, rf_ref, cnt_ref = rest[:5]
    first_step = (pl.program_id(0) == 0) & (pl.program_id(1) == 0)

    @pl.when(first_step)
    def _():
        carry_scr[...] = jnp.zeros_like(carry_scr)

    x = x_ref[0]
    tm, D = x.shape
    gate_a = mod_ref[0, 2:3, :]
    shift_m, scale_m = mod_ref[0, 3:4, :], mod_ref[0, 4:5, :]
    att_n = _rms(att_ref[0].astype(F32), goa_ref[...]).astype(BF16)
    mix = _dot(att_n, woa_ref[...]) + _dot(gm_ref[0], wog_ref[...])
    x1 = x + gate_a * mix
    h2 = _rms(x1, gffn_ref[...]) * (1.0 + scale_m) + shift_m
    if pack_rows:
        rec_rows = _record_rows(D)

        def put(j, block):
            rec_ref[0, pl.ds(j, tm, stride=rec_rows), :] = block

        packed = _pack_bf16_pairs(h2)
        for j in range(D // LANES):
            put(j, x1[:, j * LANES:(j + 1) * LANES])
        for j in range(D // 2 // LANES):
            put(D // LANES + j, packed[:, j * LANES:(j + 1) * LANES])
    else:
        x1_ref[0] = x1
        h2_ref[0] = h2.astype(BF16)

    h_hi, h_lo = _split_bf16(h2)
    la = _dot(h_hi, wra_ref[...])
    logits = la[:, :ROUTER_COLS] + la[:, ROUTER_COLS:] + _dot(h_lo, wrb_ref[...]) + br_ref[...]
    lt = logits.T

    g = [lt[r:r + 1] for r in range(N_GROUPS)]
    gmax = jnp.maximum(jnp.maximum(g[0], g[1]), jnp.maximum(g[2], g[3]))
    gsum = sum(jnp.exp(gr - gmax) for gr in g)
    g_prob = 1.0 / gsum
    g_idx = jnp.where(g[0] == gmax, 0.0, jnp.where(g[1] == gmax, 1.0, jnp.where(g[2] == gmax, 2.0, 3.0)))

    e0 = ROUTER_EXPERT_LANE0
    grp = [lt[e0 + EXPERTS_PER_GROUP * r:e0 + EXPERTS_PER_GROUP * (r + 1)] for r in range(N_GROUPS)]
    sel = jnp.where(g_idx == 0.0, grp[0], jnp.where(g_idx == 1.0, grp[1], jnp.where(g_idx == 2.0, grp[2], grp[3])))
    sub = lax.broadcasted_iota(I32, (EXPERTS_PER_GROUP, tm), 0).astype(F32)
    m1 = jnp.max(sel, axis=0, keepdims=True)
    i1 = jnp.min(jnp.where(sel == m1, sub, float(EXPERTS_PER_GROUP)), axis=0, keepdims=True)
    sel2 = jnp.where(sub == i1, -jnp.inf, sel)
    m2 = jnp.max(sel2, axis=0, keepdims=True)
    i2 = jnp.min(jnp.where(sel2 == m2, sub, float(EXPERTS_PER_GROUP)), axis=0, keepdims=True)
    d = jnp.exp(m2 - m1)
    w1 = g_prob / (1.0 + d)
    w2 = g_prob * d / (1.0 + d)
    first_lower = i1 < i2
    lo = jnp.minimum(i1, i2)
    hi = jnp.maximum(i1, i2)
    w_lo = jnp.where(first_lower, w1, w2)
    w_hi = jnp.where(first_lower, w2, w1)
    pair = lo * EXPERTS_PER_GROUP - lo * (lo + 1.0) * 0.5 + hi - lo - 1.0
    cls = g_idx * PAIRS_PER_GROUP + pair

    crow = lax.broadcasted_iota(I32, (CLASS_ROWS, tm), 0).astype(F32)
    onehot = jnp.where(crow == cls, 1.0, 0.0)
    ta = lax.broadcasted_iota(I32, (tm, tm), 0)
    tb = lax.broadcasted_iota(I32, (tm, tm), 1)
    earlier = jnp.where(ta < tb, 1.0, 0.0).astype(BF16)
    before = _dot(onehot.astype(BF16), earlier)
    carry = carry_scr[...]
    rank = jnp.sum(onehot * (before + carry[:, 0:1]), axis=0, keepdims=True)
    carry = carry + jnp.sum(onehot, axis=1, keepdims=True)
    carry_scr[...] = carry
    cnt_ref[...] = carry

    ri_ref[...] = jnp.zeros_like(ri_ref)
    ri_ref[0, 0:1, :] = cls.astype(I32)
    ri_ref[0, 1:2, :] = rank.astype(I32)
    if pack_rows:
        mrow = lax.broadcasted_iota(I32, (LANES, tm), 0)
        batch = pl.program_id(0).astype(F32)
        meta_t = jnp.where(mrow == 0, w_lo, jnp.where(mrow == 1, w_hi, jnp.where(mrow == 2, batch, 0.0)))
        n_data = D // LANES + D // 2 // LANES
        put(n_data, meta_t.T)
        for j in range(n_data + 1, rec_rows):
            put(j, jnp.zeros((tm, LANES), F32))
    else:
        rf_ref[...] = jnp.zeros_like(rf_ref)
        rf_ref[0, 0:1, :] = w_lo
        rf_ref[0, 1:2, :] = w_hi


def _out_route(x, att, gm, mod, lw, *, tm, pack_rows):
    B, S, D = x.shape
    nt = S // tm
    row = lambda b, i: (b, i, 0)
    tile = lambda b, i: (b * nt + i, 0, 0)
    route_i = (jax.ShapeDtypeStruct((B * nt, SUBLANES, tm), I32), pl.BlockSpec((1, SUBLANES, tm), tile))
    route_f = (jax.ShapeDtypeStruct((B * nt, SUBLANES, tm), F32), pl.BlockSpec((1, SUBLANES, tm), tile))
    counts = (jax.ShapeDtypeStruct((CLASS_ROWS, LANES), F32), _full((CLASS_ROWS, LANES)))
    if pack_rows:
        rec_rows = _record_rows(D)
        outs = [(jax.ShapeDtypeStruct((B, S * rec_rows, LANES), F32), pl.BlockSpec((1, tm * rec_rows, LANES), row)),
                route_i, counts]
    else:
        outs = [(jax.ShapeDtypeStruct((B, S, D), F32), pl.BlockSpec((1, tm, D), row)),
                (jax.ShapeDtypeStruct((B, S, D), BF16), pl.BlockSpec((1, tm, D), row)), route_i, route_f, counts]
    out_shape = [o[0] for o in outs]
    out_specs = [o[1] for o in outs]
    return pl.pallas_call(
        functools.partial(_out_route_kernel, pack_rows=pack_rows),
        grid=(B, nt),
        in_specs=[pl.BlockSpec((1, tm, D), row), pl.BlockSpec((1, tm, ATT_WIDTH), row), pl.BlockSpec((1, tm, GM_WIDTH), row),
                  pl.BlockSpec((1, 6, D), lambda b, i: (b, 0, 0)),
                  _full((1, ATT_WIDTH)), _full((ATT_WIDTH, D)), _full((GM_WIDTH, D)), _full((1, D)),
                  _full((D, 2 * ROUTER_COLS)), _full((D, ROUTER_COLS)), _full((1, ROUTER_COLS))],
        out_specs=out_specs,
        out_shape=out_shape,
        scratch_shapes=[pltpu.VMEM((CLASS_ROWS, LANES), F32)],
        compiler_params=_params(2),
    )(x, att, gm, mod, lw["g_out_attn"], lw["w_out_a"], lw["w_out_g"], lw["g_norm_ffn"],
      lw["w_r_a"], lw["w_r_b"], lw["b_r"])


def _swiglu(hb, wgu, wd):
    hid = _dot(hb, wgu)
    act = jax.nn.silu(hid[:, :D_EXPERT]) * hid[:, D_EXPERT:]
    return _dot(act.astype(BF16), wd)


def _moe_pairs_kernel(elo_ref, ehi_ref, nv_ref, dest_ref, rec_hbm, zeros_hbm, gate_ref,
                      wgu_lo_ref, wd_lo_ref, wgu_hi_ref, wd_hi_ref, y_hbm,
                      xbuf0, xbuf1, ybuf0, ybuf1, tok_ref, gsem, ssem, zsem):
    i = pl.program_id(0)
    xbuf, ybuf = (xbuf0, xbuf1), (ybuf0, ybuf1)
    D = ybuf0.shape[-1]
    n_groups_all = MOE_ROWS // SUBLANES

    rec_rows = xbuf0.shape[0] // MOE_ROWS

    def gather_start(t, s, g, u):
        src = rec_hbm.at[pl.ds(pl.multiple_of(t * rec_rows, rec_rows), rec_rows)]
        pltpu.make_async_copy(src, xbuf[s].at[pl.ds((g * SUBLANES + u) * rec_rows, rec_rows)], gsem.at[s]).start()

    def scatter_start(t, s, g, u):
        pltpu.make_async_copy(ybuf[s].at[g, pl.ds(u, 1)], y_hbm.at[pl.ds(t, 1)], ssem.at[s]).start()

    def all_rows(blk, fn):
        base = blk * MOE_ROWS
        for g in range(n_groups_all):
            for u in range(SUBLANES):
                fn(tok_ref[base + g * SUBLANES + u], g, u)

    def valid_rows(blk, fn):
        nv = nv_ref[blk]
        base = blk * MOE_ROWS
        n_groups = nv // SUBLANES

        def group(g, c):
            for u in range(SUBLANES):
                fn(tok_ref[base + g * SUBLANES + u], g, u)
            return c

        def single(r, c):
            fn(tok_ref[base + r], n_groups, r - n_groups * SUBLANES)
            return c

        lax.fori_loop(0, n_groups, group, 0)
        lax.fori_loop(n_groups * SUBLANES, nv, single, 0)

    def gather_wait(s):
        pltpu.make_async_copy(xbuf[s], xbuf[s], gsem.at[s]).wait()

    def scatter_wait(blk, s):
        nv = nv_ref[blk]
        n_groups = nv // SUBLANES
        buf = ybuf[s]

        @pl.when(n_groups > 0)
        def _():
            rows = buf.at[pl.ds(0, n_groups)]
            pltpu.make_async_copy(rows, rows, ssem.at[s]).wait()

        def single(r, c):
            row = buf.at[0, pl.ds(0, 1)]
            pltpu.make_async_copy(row, row, ssem.at[s]).wait()
            return c

        lax.fori_loop(n_groups * SUBLANES, nv, single, 0)

    def experts(s):
        piece = lambda j: xbuf[s][pl.ds(j, MOE_ROWS, stride=rec_rows), :]
        n_x, n_h = D // LANES, D // 2 // LANES
        hb = _unpack_bf16_pairs(jnp.concatenate([piece(n_x + j) for j in range(n_h)], axis=1))
        meta = piece(n_x + n_h)
        x = jnp.concatenate([piece(j) for j in range(n_x)], axis=1)
        w_lo, w_hi, bidx = meta[:, 0:1], meta[:, 1:2], meta[:, 2:3]
        blane = lax.broadcasted_iota(I32, (1, 2 * MOD_BATCH_PAD), 1)
        blane = jnp.where(blane >= MOD_BATCH_PAD, blane - MOD_BATCH_PAD, blane).astype(F32)
        onehot = jnp.where(bidx == blane, 1.0, 0.0).astype(BF16)
        gate_m = _dot(onehot, gate_ref[...])
        moe = w_lo * _swiglu(hb, wgu_lo_ref[0], wd_lo_ref[0]) + w_hi * _swiglu(hb, wgu_hi_ref[0], wd_hi_ref[0])
        ybuf[s][...] = (x + gate_m * moe).reshape(n_groups_all, SUBLANES, D)

    @pl.when(i == 0)
    def _():
        zero = pltpu.make_async_copy(zeros_hbm, tok_ref, zsem)
        zero.start()
        zero.wait()

        def invert(g, c):
            for u in range(SUBLANES):
                t = g * SUBLANES + u
                tok_ref[dest_ref[t]] = t
            return c
        lax.fori_loop(0, dest_ref.shape[0] // SUBLANES, invert, 0)
        all_rows(0, lambda t, g, u: gather_start(t, 0, g, u))

    prev = jnp.maximum(i - 1, 0)
    has_rows = nv_ref[i] > 0
    prev_full = (i >= 1) & (nv_ref[prev] == MOE_ROWS)
    prev_partial = (i >= 1) & (nv_ref[prev] > 0) & (nv_ref[prev] < MOE_ROWS)

    def step(slot):
        other = 1 - slot

        @pl.when(has_rows)
        def _():
            all_rows(i + 1, lambda t, g, u: gather_start(t, other, g, u))
            gather_wait(slot)

            @pl.when(prev_full)
            def _():
                all_rows(i - 1, lambda t, g, u: scatter_start(t, other, g, u))

            @pl.when(prev_partial)
            def _():
                valid_rows(i - 1, lambda t, g, u: scatter_start(t, other, g, u))

            @pl.when(i >= 2)
            def _():
                scatter_wait(i - 2, slot)
            experts(slot)

        @pl.when(jnp.logical_not(has_rows) & (i >= 1) & (nv_ref[prev] > 0))
        def _():
            gather_wait(slot)
            valid_rows(i - 1, lambda t, g, u: scatter_start(t, other, g, u))

            @pl.when(i >= 2)
            def _():
                scatter_wait(i - 2, slot)
            scatter_wait(i - 1, other)

    for parity in (0, 1):
        pl.when(i % 2 == parity)(functools.partial(step, parity))


def _moe_pairs(rec, blk_elo, blk_ehi, blk_nv, dest, gate_tab, lw):
    D = gate_tab.shape[1]
    rec_rows = _record_rows(D)
    T = rec.shape[0] // rec_rows
    assert T % SUBLANES == 0
    nb = blk_nv.shape[0]
    wgu, wd = lw["w_gu_e"], lw["w_d_e"]
    grid_spec = pltpu.PrefetchScalarGridSpec(
        num_scalar_prefetch=4,
        grid=(nb,),
        in_specs=[
            pl.BlockSpec(memory_space=pl.ANY),
            pl.BlockSpec(memory_space=pl.ANY),
            pl.BlockSpec(gate_tab.shape, lambda i, *_: (0, 0)),
            pl.BlockSpec((1, D, 2 * D_EXPERT), lambda i, elo, ehi, nv, tok: (elo[i], 0, 0)),
            pl.BlockSpec((1, D_EXPERT, D), lambda i, elo, ehi, nv, tok: (elo[i], 0, 0)),
            pl.BlockSpec((1, D, 2 * D_EXPERT), lambda i, elo, ehi, nv, tok: (ehi[i], 0, 0)),
            pl.BlockSpec((1, D_EXPERT, D), lambda i, elo, ehi, nv, tok: (ehi[i], 0, 0)),
        ],
        out_specs=pl.BlockSpec(memory_space=pl.ANY),
        scratch_shapes=[pltpu.VMEM((MOE_ROWS * rec_rows, LANES), F32),
                        pltpu.VMEM((MOE_ROWS * rec_rows, LANES), F32),
                        pltpu.VMEM((MOE_ROWS // SUBLANES, SUBLANES, D), F32),
                        pltpu.VMEM((MOE_ROWS // SUBLANES, SUBLANES, D), F32),
                        pltpu.SMEM((nb * MOE_ROWS,), I32),
                        pltpu.SemaphoreType.DMA((2,)), pltpu.SemaphoreType.DMA((2,)), pltpu.SemaphoreType.DMA(())],
    )
    return pl.pallas_call(
        _moe_pairs_kernel,
        grid_spec=grid_spec,
        out_shape=jax.ShapeDtypeStruct((T, D), F32),
        compiler_params=_params(1),
    )(blk_elo, blk_ehi, blk_nv, dest, rec, jnp.zeros((nb * MOE_ROWS,), I32), gate_tab, wgu, wd, wgu, wd)


def _small_lookup(table, idx):
    ids = jnp.arange(table.shape[0], dtype=I32)
    return jnp.sum(jnp.where(idx[:, None] == ids[None, :], table[None, :], 0), axis=1)


def _pair_tables():
    lo, hi = [], []
    for g in range(N_GROUPS):
        for a in range(EXPERTS_PER_GROUP):
            for b in range(a + 1, EXPERTS_PER_GROUP):
                lo.append(g * EXPERTS_PER_GROUP + a)
                hi.append(g * EXPERTS_PER_GROUP + b)
    return np.asarray(lo, np.int32), np.asarray(hi, np.int32)


def _moe_prompt(rec, cls, rank, counts, mod, lw):
    B, D = mod.shape[0], mod.shape[-1]
    S = rec.shape[1] // _record_rows(D)
    T = B * S
    nb = T // MOE_ROWS + N_CLASSES + 1
    nblk = (counts + MOE_ROWS - 1) // MOE_ROWS
    blk_end = jnp.cumsum(nblk)
    blk_start = blk_end - nblk
    dest = _small_lookup(blk_start, cls) * MOE_ROWS + rank
    ids = jnp.arange(nb, dtype=I32)
    used = blk_end[-1]
    class_of = lambda blk: jnp.sum((blk_end[None, :] <= blk[:, None]).astype(I32), axis=1)
    blk_cls = jnp.where(ids < used, jnp.minimum(class_of(ids), N_CLASSES - 1), class_of(used[None] - 1))
    first_row = (ids - _small_lookup(blk_start, blk_cls)) * MOE_ROWS
    blk_nv = jnp.where(ids < used, jnp.clip(_small_lookup(counts, blk_cls) - first_row, 0, MOE_ROWS), 0)
    lo_tab, hi_tab = _pair_tables()
    blk_elo = _small_lookup(jnp.asarray(lo_tab), blk_cls)
    blk_ehi = _small_lookup(jnp.asarray(hi_tab), blk_cls)
    gate_hi, gate_lo = _split_bf16(jnp.pad(mod[:, 5, :], ((0, MOD_BATCH_PAD - B), (0, 0))))
    gate_tab = jnp.concatenate([gate_hi, gate_lo], axis=0)
    y = _moe_pairs(rec.reshape(T * _record_rows(D), LANES), blk_elo, blk_ehi, blk_nv.astype(I32), dest.astype(I32), gate_tab, lw)
    return y.reshape(B, S, D)


def _moe_dense_kernel(h_ref, x1_ref, gate_ref, w_ref, sel_ref, wgu_ref, wd_ref, y_ref):
    e = pl.program_id(0)

    @pl.when(e == 0)
    def _():
        y_ref[...] = jnp.zeros_like(y_ref)

    ye = _swiglu(h_ref[...], wgu_ref[0], wd_ref[0])
    y_ref[...] += jnp.where(sel_ref[0] > 0.5, w_ref[0] * ye, 0.0)

    @pl.when(e == pl.num_programs(0) - 1)
    def _():
        y_ref[...] = x1_ref[...] + gate_ref[...] * y_ref[...]


def _moe_dense(h2, x1, gate_rows, w_sel, sel, lw):
    T, D = x1.shape
    per_e = lambda e: (e, 0, 0)
    return pl.pallas_call(
        _moe_dense_kernel,
        grid=(N_EXPERTS,),
        in_specs=[_full((T, D)), _full((T, D)), _full((T, D)),
                  pl.BlockSpec((1, T, 1), per_e), pl.BlockSpec((1, T, 1), per_e),
                  pl.BlockSpec((1, D, 2 * D_EXPERT), per_e), pl.BlockSpec((1, D_EXPERT, D), per_e)],
        out_specs=_full((T, D)),
        out_shape=jax.ShapeDtypeStruct((T, D), F32),
        compiler_params=_params(1),
    )(h2, x1, gate_rows, w_sel, sel, lw["w_gu_e"], lw["w_d_e"])


def _rope_tables(pos, gain, scale):
    inv = 1.0 / (ROPE_BASE ** (jnp.arange(ROPE_HALF, dtype=F32) / ROPE_HALF))
    ang = pos.astype(F32)[:, None] * inv[None, :]
    cos, sin = jnp.cos(ang) * scale, jnp.sin(ang) * scale
    n = pos.shape[0]
    z = lambda w: jnp.zeros((n, w), F32)
    pad = HEAD_PAD - QK_DIM
    g_nope, g1, g2 = gain[:NOPE_DIM], gain[NOPE_DIM:NOPE_DIM + ROPE_HALF], gain[NOPE_DIM + ROPE_HALF:]
    tab_c = jnp.concatenate([jnp.broadcast_to(g_nope * scale, (n, NOPE_DIM)), cos * g1, cos * g2, z(pad)], axis=1)
    tab_s = jnp.concatenate([z(NOPE_DIM), -sin * g2, sin * g1, z(pad)], axis=1)
    return tab_c, tab_s


def _prep_layer(w, l, chunk_lens):
    D = w["w_in"].shape[1]
    H = MLA_HEADS
    lw = {}
    row = lambda name: w[name][l].reshape(1, -1)
    for name in ("g_norm_mix", "g_q_lat", "g_kv_lat", "g_ln_v", "b_ln_v", "g_out_attn", "g_out_gmlp", "g_norm_ffn"):
        lw[name] = row(name)
    w_in = w["w_in"][l]
    o1, o2, o3 = Q_LORA, Q_LORA + KV_LORA, Q_LORA + KV_LORA + ROPE_DIM
    o4 = o3 + GM_WIDTH
    z32 = jnp.zeros((D, ROPE_DIM), F32)
    wr = w_in[:, o2:o3]
    lw["w_in_p"] = jnp.concatenate([w_in[:, :o2], wr, z32, wr, z32, w_in[:, o3:o4], w_in[:, o4:]], axis=1).astype(BF16)
    assert lw["w_in_p"].shape[1] == PROJ_COLS
    pad = HEAD_PAD - QK_DIM
    w_uq = w["w_uq"][l].reshape(Q_LORA, H, QK_DIM)
    w_uq_swapped = jnp.concatenate([jnp.zeros((Q_LORA, H, NOPE_DIM), F32), w_uq[:, :, NOPE_DIM + ROPE_HALF:],
                                    w_uq[:, :, NOPE_DIM:NOPE_DIM + ROPE_HALF]], axis=2)
    head_pad = lambda a: jnp.pad(a, ((0, 0), (0, 0), (0, pad))).reshape(Q_LORA, H * HEAD_PAD)
    lw["w_uq_p"] = jnp.concatenate([head_pad(w_uq), head_pad(w_uq_swapped)], axis=1).astype(BF16)
    w_ukv = w["w_ukv"][l].reshape(KV_LORA, H, NOPE_DIM + V_DIM)
    lw["w_k_p"] = jnp.pad(w_ukv[:, :, :NOPE_DIM], ((0, 0), (0, 0), (0, HEAD_PAD - NOPE_DIM))).reshape(KV_LORA, H * HEAD_PAD).astype(BF16)
    w_v = w_ukv[:, :, NOPE_DIM:]
    zeros_v = jnp.zeros_like(w_v)
    odd_head = (jnp.arange(H) % 2 == 1)[None, :, None]
    w_v_lo, w_v_hi = jnp.where(odd_head, zeros_v, w_v), jnp.where(odd_head, w_v, zeros_v)
    lw["w_v_p"] = jnp.concatenate([w_v_lo, w_v_hi], axis=2).reshape(KV_LORA, H * HEAD_PAD).astype(BF16)
    lw["scores_bounded"] = _scores_bounded(w, l)
    lw["g_qnorm"] = w["g_qnorm"][l]
    lw["g_knorm"] = w["g_knorm"][l]
    for L in chunk_lens:
        wsp = w["w_spatial"][l][:, :L, :L]
        lw["w_sp_pairs_%d" % L] = wsp.reshape(GM_HEADS // 2, 2 * L, L).astype(BF16)
        lw["b_sp_rows_%d" % L] = jnp.repeat(jnp.transpose(w["b_spatial"][l][:, :L]), GM_HEAD_DIM, axis=1)
    w_out = w["w_out"][l].astype(BF16)
    lw["w_out_a"], lw["w_out_g"] = w_out[:ATT_WIDTH], w_out[ATT_WIDTH:]
    wr_full = jnp.zeros((D, ROUTER_COLS), F32)
    wr_full = wr_full.at[:, :N_GROUPS].set(w["w_router_group"][l])
    wr_full = wr_full.at[:, ROUTER_EXPERT_LANE0:ROUTER_EXPERT_LANE0 + N_EXPERTS].set(w["w_router_expert"][l])
    r_hi, r_lo = _split_bf16(wr_full)
    lw["w_r_a"] = jnp.concatenate([r_hi, r_lo], axis=1)
    lw["w_r_b"] = r_hi
    br = jnp.zeros((1, ROUTER_COLS), F32)
    br = br.at[0, :N_GROUPS].set(w["b_router_group"][l])
    lw["b_r"] = br.at[0, ROUTER_EXPERT_LANE0:ROUTER_EXPERT_LANE0 + N_EXPERTS].set(w["b_router_expert"][l])
    lw["w_gu_e"] = jnp.concatenate([w["w_gate_e"][l], w["w_up_e"][l]], axis=-1).astype(BF16)
    lw["w_d_e"] = w["w_down_e"][l].astype(BF16)
    return lw


def _tiles(seq):
    tm = min(seq, 512)
    return tm, min(seq, 512)


def _layer_prompt(x, mod, lw):
    B, S, D = x.shape
    tm, tq = _tiles(S)
    pos = jnp.arange(S)
    q_tabs = _rope_tables(pos, lw["g_qnorm"], QK_DIM ** -0.5 * LOG2E)
    k_tabs = _rope_tables(pos, lw["g_knorm"], 1.0)
    ckv, krope, q, gm, k, v = _mix_in(x, mod, lw, q_tabs, k_tabs, tm=tm, chunk_len=GM_CHUNK, emit_kv=True, emit_vrows=False)
    att = _attention(lw["scores_bounded"], q, k, v, tq=tq, tk=tq, q_off=0, kv_valid=S)
    rec, ri, cnt = _out_route(x, att, gm, mod, lw, tm=tm, pack_rows=True)
    cls = ri[:, 0, :].reshape(B * S)
    rank = ri[:, 1, :].reshape(B * S)
    counts = cnt[:N_CLASSES, 0].astype(I32)
    y = _moe_prompt(rec, cls, rank, counts, mod, lw)
    return y, ckv, krope


def _layer_sample(x, mod, past_ckv, past_krope, lw):
    B, S, D = x.shape
    past = past_ckv.shape[1]
    q_tabs = _rope_tables(past + jnp.arange(S), lw["g_qnorm"], QK_DIM ** -0.5 * LOG2E)
    ckv, krope, q, gm, v_rows = _mix_in(x, mod, lw, q_tabs, q_tabs, tm=S, chunk_len=S, emit_kv=False, emit_vrows=True)
    kv_valid = past + S
    kv_pad = -(-kv_valid // LANES) * LANES
    extra = kv_pad - kv_valid
    ckv_all = jnp.concatenate([past_ckv, ckv, jnp.zeros((B, extra, KV_LORA), F32)], axis=1)
    kr_all = jnp.concatenate([past_krope, krope, jnp.zeros((B, extra, ROPE_DIM), F32)], axis=1)
    kr_slot = jnp.pad(kr_all, ((0, 0), (0, 0), (NOPE_DIM, HEAD_PAD - QK_DIM)))
    k_tabs = _rope_tables(jnp.arange(kv_pad), lw["g_knorm"], 1.0)
    k, v = _kv_latent(ckv_all, kr_slot, lw, k_tabs, tr=kv_pad)
    att = _attention(lw["scores_bounded"], q, k, v, tq=S, tk=kv_pad, q_off=past, kv_valid=kv_valid)
    x1, h2, ri, rf, _ = _out_route(x, att, gm, mod, lw, tm=S, pack_rows=False)
    T = B * S
    cls, w_lo, w_hi = ri[:, 0, :].reshape(T), rf[:, 0, :].reshape(T), rf[:, 1, :].reshape(T)
    lo_tab, hi_tab = _pair_tables()
    e_lo, e_hi = jnp.asarray(lo_tab)[cls], jnp.asarray(hi_tab)[cls]
    eids = jnp.arange(N_EXPERTS, dtype=I32)[:, None]
    is_lo, is_hi = eids == e_lo[None, :], eids == e_hi[None, :]
    w_sel = (jnp.where(is_lo, w_lo[None, :], 0.0) + jnp.where(is_hi, w_hi[None, :], 0.0))[:, :, None]
    sel = (is_lo | is_hi).astype(F32)[:, :, None]
    gate_rows = jnp.repeat(mod[:, 5, :], S, axis=0)
    y = _moe_dense(h2.reshape(T, D), x1.reshape(T, D), gate_rows, w_sel, sel, lw)
    return y.reshape(B, S, D), ckv, krope, v_rows


def kernel(x_prompt, x_sample, cache_ckv, cache_krope, c_prompt, c_sample, w_ada, b_ada, g_norm_mix, w_in, g_q_lat, w_uq, g_kv_lat, w_ukv, g_qnorm, g_knorm, g_ln_v, b_ln_v, w_spatial, b_spatial, g_out_attn, g_out_gmlp, w_out, g_norm_ffn, w_router_group, b_router_group, w_router_expert, b_router_expert, w_gate_e, w_up_e, w_down_e):
    w = dict(w_in=w_in, g_norm_mix=g_norm_mix, g_q_lat=g_q_lat, w_uq=w_uq, g_kv_lat=g_kv_lat, w_ukv=w_ukv,
             g_qnorm=g_qnorm, g_knorm=g_knorm, g_ln_v=g_ln_v, b_ln_v=b_ln_v, w_spatial=w_spatial, b_spatial=b_spatial,
             g_out_attn=g_out_attn, g_out_gmlp=g_out_gmlp, w_out=w_out, g_norm_ffn=g_norm_ffn,
             w_router_group=w_router_group, b_router_group=b_router_group, w_router_expert=w_router_expert,
             b_router_expert=b_router_expert, w_gate_e=w_gate_e, w_up_e=w_up_e, w_down_e=w_down_e)
    depth = w_ada.shape[0]
    Bp, Sp, D = x_prompt.shape
    Bs, Ss, _ = x_sample.shape
    assert Sp % GM_CHUNK == 0 and Ss <= GM_CHUNK and Ss % CHUNK == 0 and Bp <= MOD_BATCH_PAD
    c_all = jnp.concatenate([c_prompt, c_sample], axis=0)
    y_p, y_s = x_prompt, x_sample
    outs = [[] for _ in range(5)]
    for l in range(depth):
        lw = _prep_layer(w, l, (GM_CHUNK, Ss))
        mod = _ada_mod(c_all, w_ada[l], b_ada[l]).reshape(Bp + Bs, 6, D)
        y_p, ckv_p, kr_p = _layer_prompt(y_p, mod[:Bp], lw)
        y_s, ckv_s, kr_s, v_s = _layer_sample(y_s, mod[Bp:], cache_ckv[l], cache_krope[l], lw)
        for lst, val in zip(outs, (ckv_p, kr_p, ckv_s, kr_s, v_s)):
            lst.append(val)
    return (y_p, y_s) + tuple(jnp.stack(lst) for lst in outs)
```

```python
import functools

import numpy as np
import jax
import jax.numpy as jnp
from jax import lax
from jax.experimental import pallas as pl
from jax.experimental.pallas import tpu as pltpu

F32 = jnp.float32
BF16 = jnp.bfloat16
I32 = jnp.int32

CHUNK = 64
CHUNK_SHIFT = 6
EPS = 1e-6
MLA_HEADS = 8
Q_LORA = 256
KV_LORA = 128
NOPE_DIM = 64
ROPE_DIM = 32
ROPE_HALF = ROPE_DIM // 2
V_DIM = 64
QK_DIM = NOPE_DIM + ROPE_DIM
ATT_WIDTH = MLA_HEADS * V_DIM
ROPE_BASE = 10000.0
GM_HEADS = 8
GM_HEAD_DIM = 64
GM_WIDTH = GM_HEADS * GM_HEAD_DIM
GM_CHUNK = 128
N_GROUPS = 4
EXPERTS_PER_GROUP = 8
N_EXPERTS = N_GROUPS * EXPERTS_PER_GROUP
D_EXPERT = 256
PAIRS_PER_GROUP = EXPERTS_PER_GROUP * (EXPERTS_PER_GROUP - 1) // 2
N_CLASSES = N_GROUPS * PAIRS_PER_GROUP

LANES = 128
SUBLANES = 8
HEAD_PAD = LANES
PROJ_COLS = 1536
ROUTER_COLS = LANES
ROUTER_EXPERT_LANE0 = SUBLANES
CLASS_ROWS = LANES
MOE_ROWS = 128
MOD_BATCH_PAD = 16
DIAG_ROWS = 256
VMEM_LIMIT = 48 * 1024 * 1024
NEG_BIG = -1e30
LOG2E = 1.4426950408889634
BF16_SLACK = 1.02
SCORE_BOUND = 90.0

assert CHUNK == 1 << CHUNK_SHIFT


def _params(n_axes, vmem=VMEM_LIMIT):
    return pltpu.CompilerParams(dimension_semantics=("arbitrary",) * n_axes, vmem_limit_bytes=vmem)


def _full(shape):
    nd = len(shape)
    return pl.BlockSpec(shape, lambda *_: (0,) * nd)


def _split_bf16(x):
    hi = x.astype(BF16)
    lo = (x - hi.astype(F32)).astype(BF16)
    return hi, lo


def _dot(a, b):
    return jnp.dot(a, b, preferred_element_type=F32)


def _ada_kernel(c_ref, w_ref, b_ref, o_ref):
    a_hi, a_lo = _split_bf16(jax.nn.silu(c_ref[...]))
    w_hi, w_lo = _split_bf16(w_ref[...])
    o_ref[...] = _dot(a_hi, w_hi) + _dot(a_lo, w_hi) + _dot(a_hi, w_lo) + b_ref[...]


def _ada_mod(c, w_ada, b_ada):
    n, d = c.shape
    cols = w_ada.shape[1]
    tn = 1536
    return pl.pallas_call(
        _ada_kernel,
        grid=(cols // tn,),
        in_specs=[_full((n, d)), pl.BlockSpec((d, tn), lambda j: (0, j)), pl.BlockSpec((1, tn), lambda j: (0, j))],
        out_specs=pl.BlockSpec((n, tn), lambda j: (0, j)),
        out_shape=jax.ShapeDtypeStruct((n, cols), F32),
        compiler_params=_params(1),
    )(c, w_ada, b_ada.reshape(1, cols))


def _rms(x, g):
    return x * lax.rsqrt(jnp.mean(x * x, axis=-1, keepdims=True) + EPS) * g


def _head_norm_rope(xh, xh_swapped, tab_c, tab_s):
    ms = jnp.sum(xh * xh, axis=-1, keepdims=True) * (1.0 / QK_DIM)
    return (xh * tab_c + xh_swapped * tab_s) * lax.rsqrt(ms + EPS)


def _keys_values(ckv, kr_slot, gkv, wk, wv, tab_c, tab_s, k_ref, v_ref):
    cb = _rms(ckv, gkv).astype(BF16)
    kall = _dot(cb, wk)
    vall = _dot(cb, wv)
    lane = lax.broadcasted_iota(I32, (1, LANES), 1)
    ones_hi = jnp.where(lane >= V_DIM, 1.0, 0.0)
    ones_lo = 1.0 - ones_hi
    kr_swapped = jnp.where(lane < NOPE_DIM + ROPE_HALF, pltpu.roll(kr_slot, HEAD_PAD - ROPE_HALF, 1),
                           pltpu.roll(kr_slot, ROPE_HALF, 1))
    kr_swapped = jnp.where((lane >= NOPE_DIM) & (lane < QK_DIM), kr_swapped, 0.0)
    for h in range(MLA_HEADS):
        kh = kall[:, h * HEAD_PAD:(h + 1) * HEAD_PAD] + kr_slot
        k_ref[0, h] = _head_norm_rope(kh, kr_swapped, tab_c, tab_s).astype(BF16)
        ones = ones_hi if h % 2 == 0 else ones_lo
        v_ref[0, h] = (vall[:, h * HEAD_PAD:(h + 1) * HEAD_PAD] + ones).astype(BF16)


def _mix_in_kernel(x_ref, mod_ref, gmix_ref, win_ref, gql_ref, wuq_ref, gkv_ref, wk_ref, wv_ref,
                   cq_ref, sq_ref, ck_ref, sk_ref, glnv_ref, blnv_ref, wsp_ref, bsp_ref,
                   ggm_ref, *rest, chunk_len, emit_kv, emit_vrows):
    outs = list(rest[:-1])
    mixed_scr = rest[-1]
    ckv_ref, kr_ref, q_ref, gm_ref = outs[:4]
    outs = outs[4:]
    if emit_kv:
        k_ref, v_ref = outs[:2]
        outs = outs[2:]
    if emit_vrows:
        vrows_ref = outs[0]

    x = x_ref[0]
    tm = x.shape[0]
    shift, scale = mod_ref[0, 0:1, :], mod_ref[0, 1:2, :]
    h = _rms(x, gmix_ref[...]) * (1.0 + scale) + shift
    proj = _dot(h.astype(BF16), win_ref[...])

    q_lat = proj[:, 0:Q_LORA]
    ckv = proj[:, Q_LORA:Q_LORA + KV_LORA]
    kr_blk = proj[:, Q_LORA + KV_LORA:Q_LORA + KV_LORA + LANES]
    ckv_ref[0] = ckv
    kr_ref[0] = kr_blk[:, 0:ROPE_DIM]

    q = _dot(_rms(q_lat, gql_ref[...]).astype(BF16), wuq_ref[...])
    cq, sq = cq_ref[...], sq_ref[...]
    n_q = MLA_HEADS * HEAD_PAD
    for hd in range(MLA_HEADS):
        qh = q[:, hd * HEAD_PAD:(hd + 1) * HEAD_PAD]
        qh_swapped = q[:, n_q + hd * HEAD_PAD:n_q + (hd + 1) * HEAD_PAD]
        q_ref[0, hd] = _head_norm_rope(qh, qh_swapped, cq, sq).astype(BF16)

    if emit_kv:
        lane = lax.broadcasted_iota(I32, (1, LANES), 1)
        kr_slot = jnp.where(lane >= NOPE_DIM, kr_blk, 0.0)
        _keys_values(ckv, kr_slot, gkv_ref[...], wk_ref[...], wv_ref[...], ck_ref[...], sk_ref[...], k_ref, v_ref)

    g_u = proj[:, 512:512 + GM_WIDTH]
    g_v = proj[:, 1024:1024 + GM_WIDTH]
    u = jax.nn.gelu(g_u)
    gv = jax.nn.gelu(g_v)
    mu = jnp.mean(gv, axis=-1, keepdims=True)
    xc = gv - mu
    var = jnp.mean(xc * xc, axis=-1, keepdims=True)
    v_rows = xc * lax.rsqrt(var + EPS) * glnv_ref[...] + blnv_ref[...]
    if emit_vrows:
        vrows_ref[0] = v_rows
    vb = v_rows.astype(BF16)

    L = chunk_len
    t = lax.broadcasted_iota(I32, (2 * L, L), 0)
    s = lax.broadcasted_iota(I32, (2 * L, L), 1)
    t = jnp.where(t >= L, t - L, t)
    allowed = (s >> CHUNK_SHIFT) <= (t >> CHUNK_SHIFT)
    lane = lax.broadcasted_iota(I32, (1, LANES), 1)
    first_head = lane < GM_HEAD_DIM
    for p in range(GM_HEADS // 2):
        w_pair = jnp.where(allowed, wsp_ref[p], jnp.zeros((), BF16))
        for c in range(tm // L):
            vp = vb[c * L:(c + 1) * L, p * LANES:(p + 1) * LANES]
            r = _dot(w_pair, vp)
            mixed = jnp.where(first_head, r[:L], r[L:])
            mixed_scr[c * L:(c + 1) * L, p * LANES:(p + 1) * LANES] = mixed + bsp_ref[:, p * LANES:(p + 1) * LANES]
    gm = u * mixed_scr[...]
    gm_ref[0] = _rms(gm, ggm_ref[...]).astype(BF16)


def _mix_in(x, mod, lw, q_tabs, k_tabs, *, tm, chunk_len, emit_kv, emit_vrows):
    B, S, D = x.shape
    nt = S // tm
    H = MLA_HEADS
    row = lambda b, i: (b, i, 0)
    tab = pl.BlockSpec((tm, LANES), lambda b, i: (i, 0))
    in_specs = [
        pl.BlockSpec((1, tm, D), row),
        pl.BlockSpec((1, 6, D), lambda b, i: (b, 0, 0)),
        _full((1, D)), _full((D, PROJ_COLS)), _full((1, Q_LORA)), _full((Q_LORA, 2 * H * HEAD_PAD)),
        _full((1, KV_LORA)), _full((KV_LORA, H * HEAD_PAD)), _full((KV_LORA, H * HEAD_PAD)),
        tab, tab, tab, tab,
        _full((1, GM_WIDTH)), _full((1, GM_WIDTH)),
        _full((GM_HEADS // 2, 2 * chunk_len, chunk_len)), _full((chunk_len, GM_WIDTH)), _full((1, GM_WIDTH)),
    ]
    out_shape = [
        jax.ShapeDtypeStruct((B, S, KV_LORA), F32),
        jax.ShapeDtypeStruct((B, S, ROPE_DIM), F32),
        jax.ShapeDtypeStruct((B, H, S, HEAD_PAD), BF16),
        jax.ShapeDtypeStruct((B, S, GM_WIDTH), BF16),
    ]
    head_blk = pl.BlockSpec((1, H, tm, HEAD_PAD), lambda b, i: (b, 0, i, 0))
    out_specs = [
        pl.BlockSpec((1, tm, KV_LORA), row),
        pl.BlockSpec((1, tm, ROPE_DIM), row),
        head_blk,
        pl.BlockSpec((1, tm, GM_WIDTH), row),
    ]
    if emit_kv:
        out_shape += [jax.ShapeDtypeStruct((B, H, S, HEAD_PAD), BF16), jax.ShapeDtypeStruct((B, H, S, HEAD_PAD), BF16)]
        out_specs += [head_blk, head_blk]
    if emit_vrows:
        out_shape += [jax.ShapeDtypeStruct((B, S, GM_WIDTH), F32)]
        out_specs += [pl.BlockSpec((1, tm, GM_WIDTH), row)]
    kern = functools.partial(_mix_in_kernel, chunk_len=chunk_len, emit_kv=emit_kv, emit_vrows=emit_vrows)
    return pl.pallas_call(
        kern,
        grid=(B, nt),
        in_specs=in_specs,
        out_specs=out_specs,
        out_shape=out_shape,
        scratch_shapes=[pltpu.VMEM((tm, GM_WIDTH), F32)],
        compiler_params=_params(2),
    )(x, mod, lw["g_norm_mix"], lw["w_in_p"], lw["g_q_lat"], lw["w_uq_p"], lw["g_kv_lat"], lw["w_k_p"], lw["w_v_p"],
      *q_tabs, *k_tabs, lw["g_ln_v"], lw["b_ln_v"],
      lw["w_sp_pairs_%d" % chunk_len], lw["b_sp_rows_%d" % chunk_len], lw["g_out_gmlp"])


def _kv_latent_kernel(ckv_ref, kr_ref, gkv_ref, wk_ref, wv_ref, c_ref, s_ref, k_ref, v_ref):
    _keys_values(ckv_ref[0], kr_ref[0], gkv_ref[...], wk_ref[...], wv_ref[...], c_ref[...], s_ref[...], k_ref, v_ref)


def _kv_latent(ckv_all, kr_slot_all, lw, k_tabs, *, tr):
    B, K, _ = ckv_all.shape
    H = MLA_HEADS
    row = lambda b, i: (b, i, 0)
    tab = pl.BlockSpec((tr, LANES), lambda b, i: (i, 0))
    return pl.pallas_call(
        _kv_latent_kernel,
        grid=(B, K // tr),
        in_specs=[pl.BlockSpec((1, tr, KV_LORA), row), pl.BlockSpec((1, tr, LANES), row),
                  _full((1, KV_LORA)), _full((KV_LORA, H * HEAD_PAD)), _full((KV_LORA, H * HEAD_PAD)),
                  tab, tab],
        out_specs=[pl.BlockSpec((1, H, tr, HEAD_PAD), lambda b, i: (b, 0, i, 0))] * 2,
        out_shape=[jax.ShapeDtypeStruct((B, H, K, HEAD_PAD), BF16)] * 2,
        compiler_params=_params(2),
    )(ckv_all, kr_slot_all, lw["g_kv_lat"], lw["w_k_p"], lw["w_v_p"], *k_tabs)


def _attn_kernel(bounded_ref, q_ref, k_ref, v_ref, o_ref, acc_ref, *, tq, tk, n_q, q_off, kv_valid, split_diag):
    i = pl.program_id(2) if n_q > 1 else 0
    q_first = q_off + i * tq
    vis_first = jnp.minimum(((q_first >> CHUNK_SHIFT) + 1) << CHUNK_SHIFT, kv_valid)
    vis_last = jnp.minimum((((q_first + tq - 1) >> CHUNK_SHIFT) + 1) << CHUNK_SHIFT, kv_valid)
    n_unmasked = vis_first // tk
    n_total = (vis_last + tk - 1) // tk

    q_pos = q_first + lax.broadcasted_iota(I32, (tq, 1), 0)
    limit = jnp.minimum(((q_pos >> CHUNK_SHIFT) + 1) << CHUNK_SHIFT, kv_valid)
    lane = lax.broadcasted_iota(I32, (1, LANES), 1)

    def scores(j, start, masked, r0=0, nr=tq, nk=tk):
        s = lax.dot_general(q_ref[0, j, r0:r0 + nr, :], k_ref[0, j, pl.ds(start, nk), :], (((1,), (1,)), ((), ())),
                            preferred_element_type=F32)
        if masked:
            k_pos = start + lax.broadcasted_iota(I32, (1, nk), 1)
            s = jnp.where(k_pos < limit[r0:r0 + nr], s, NEG_BIG)
        return s

    def plain(start, masked, r0=0, nr=tq, nk=tk, assign=False):
        for j in range(2):
            p = jnp.exp2(scores(j, start, masked, r0, nr, nk)).astype(BF16)
            pv = _dot(p, v_ref[0, j, pl.ds(start, nk), :])
            if assign:
                acc_ref[j, r0:r0 + nr, :] = pv
            else:
                acc_ref[j, r0:r0 + nr, :] += pv

    def online_block(kb, carry, masked):
        start = pl.multiple_of(kb * tk, tk)
        new = []
        for j in range(2):
            m, acc = carry[2 * j:2 * j + 2]
            s = scores(j, start, masked)
            m_new = jnp.maximum(m, jnp.max(s, axis=-1, keepdims=True))
            p = jnp.exp2(s - m_new).astype(BF16)
            acc = jnp.exp2(m - m_new) * acc + _dot(p, v_ref[0, j, pl.ds(start, tk), :])
            new += [m_new, acc]
        return tuple(new)

    def finish(acc0, acc1):
        first = lane < V_DIM
        num = jnp.where(first, acc0, acc1)
        den = pltpu.roll(jnp.where(first, acc1, acc0), V_DIM, 1)
        o_ref[0] = (num / den).astype(BF16)

    @pl.when(bounded_ref[0] == 1)
    def _():
        at = lambda kb: pl.multiple_of(kb * tk, tk)

        def pair(p, c):
            plain(at(2 * p), False)
            plain(at(2 * p + 1), False)
            return c

        def single(kb, c, masked):
            plain(at(kb), masked)
            return c

        if split_diag:
            own = 0 if n_q == 1 else pl.multiple_of(i * tq, tq)
            for r in range(tq // DIAG_ROWS):
                plain(own, True, r * DIAG_ROWS, DIAG_ROWS, (r + 1) * DIAG_ROWS, assign=True)
        else:
            acc_ref[...] = jnp.zeros_like(acc_ref)
        n_pairs = n_unmasked // 2
        lax.fori_loop(0, n_pairs, pair, 0)
        lax.fori_loop(2 * n_pairs, n_unmasked, lambda kb, c: single(kb, c, False), 0)
        if not split_diag:
            lax.fori_loop(n_unmasked, n_total, lambda kb, c: single(kb, c, True), 0)
        finish(acc_ref[0], acc_ref[1])

    @pl.when(bounded_ref[0] != 1)
    def _():
        zeros = jnp.zeros((tq, LANES), F32)
        m0 = jnp.full((tq, 1), NEG_BIG, F32)
        c = lax.fori_loop(0, n_unmasked, lambda kb, c: online_block(kb, c, False), (m0, zeros, m0, zeros))
        c = lax.fori_loop(n_unmasked, n_total, lambda kb, c: online_block(kb, c, True), c)
        finish(c[1], c[3])


def _attention(bounded, q, k, v, *, tq, tk, q_off, kv_valid):
    B, H, Sq, _ = q.shape
    Sk = k.shape[2]
    split_diag = q_off == 0 and kv_valid >= Sq and tq % DIAG_ROWS == 0 and tq % tk == 0
    kern = functools.partial(_attn_kernel, tq=tq, tk=tk, n_q=Sq // tq, q_off=q_off, kv_valid=kv_valid,
                             split_diag=split_diag)
    grid_spec = pltpu.PrefetchScalarGridSpec(
        num_scalar_prefetch=1,
        grid=(B, H // 2, Sq // tq),
        in_specs=[pl.BlockSpec((1, 2, tq, HEAD_PAD), lambda b, hp, i, f: (b, hp, i, 0)),
                  pl.BlockSpec((1, 2, Sk, HEAD_PAD), lambda b, hp, i, f: (b, hp, 0, 0)),
                  pl.BlockSpec((1, 2, Sk, HEAD_PAD), lambda b, hp, i, f: (b, hp, 0, 0))],
        out_specs=pl.BlockSpec((1, tq, LANES), lambda b, hp, i, f: (b, i, hp)),
        scratch_shapes=[pltpu.VMEM((2, tq, LANES), F32)],
    )
    return pl.pallas_call(
        kern,
        grid_spec=grid_spec,
        out_shape=jax.ShapeDtypeStruct((B, Sq, ATT_WIDTH), BF16),
        compiler_params=_params(3),
    )(bounded, q, k, v)


def _scores_bounded(w, l):
    gq = jnp.max(jnp.abs(w["g_qnorm"][l]))
    gk = jnp.max(jnp.abs(w["g_knorm"][l]))
    bound = (QK_DIM ** 0.5) * LOG2E * BF16_SLACK * gq * gk
    return (bound <= SCORE_BOUND).astype(I32).reshape(1)


def _pack_bf16_pairs(h):
    n = h.shape[1] // 2
    hi = pltpu.bitcast(h[:, :n].astype(BF16).astype(F32), jnp.uint32)
    lo = pltpu.bitcast(h[:, n:].astype(BF16).astype(F32), jnp.uint32)
    return pltpu.bitcast(hi | (lo >> 16), F32)


def _unpack_bf16_pairs(words):
    w = pltpu.bitcast(words, jnp.uint32)
    hi = pltpu.bitcast(w & jnp.uint32(0xFFFF0000), F32)
    lo = pltpu.bitcast(w << 16, F32)
    return jnp.concatenate([hi, lo], axis=1).astype(BF16)


def _out_route_kernel(x_ref, att_ref, gm_ref, mod_ref, goa_ref, woa_ref, wog_ref, gffn_ref, wra_ref, wrb_ref, br_ref,
                      *rest, pack_rows):
    carry_scr = rest[-1]
    if pack_rows:
        rec_ref, ri_ref, cnt_ref = rest[:3]
    else:
        x1_ref, h2_ref, ri_ref, rf_ref, cnt_ref = rest[:5]
    first_step = (pl.program_id(0) == 0) & (pl.program_id(1) == 0)

    @pl.when(first_step)
    def _():
        carry_scr[...] = jnp.zeros_like(carry_scr)

    x = x_ref[0]
    tm, D = x.shape
    gate_a = mod_ref[0, 2:3, :]
    shift_m, scale_m = mod_ref[0, 3:4, :], mod_ref[0, 4:5, :]
    att_n = _rms(att_ref[0].astype(F32), goa_ref[...]).astype(BF16)
    mix = _dot(att_n, woa_ref[...]) + _dot(gm_ref[0], wog_ref[...])
    x1 = x + gate_a * mix
    h2 = _rms(x1, gffn_ref[...]) * (1.0 + scale_m) + shift_m
    if pack_rows:
        rec_ref[0, :, 0:D] = x1
        rec_ref[0, :, D:D + D // 2] = _pack_bf16_pairs(h2)
    else:
        x1_ref[0] = x1
        h2_ref[0] = h2.astype(BF16)

    h_hi, h_lo = _split_bf16(h2)
    la = _dot(h_hi, wra_ref[...])
    logits = la[:, :ROUTER_COLS] + la[:, ROUTER_COLS:] + _dot(h_lo, wrb_ref[...]) + br_ref[...]
    lt = logits.T

    g = [lt[r:r + 1] for r in range(N_GROUPS)]
    gmax = jnp.maximum(jnp.maximum(g[0], g[1]), jnp.maximum(g[2], g[3]))
    gsum = sum(jnp.exp(gr - gmax) for gr in g)
    g_prob = 1.0 / gsum
    g_idx = jnp.where(g[0] == gmax, 0.0, jnp.where(g[1] == gmax, 1.0, jnp.where(g[2] == gmax, 2.0, 3.0)))

    e0 = ROUTER_EXPERT_LANE0
    grp = [lt[e0 + EXPERTS_PER_GROUP * r:e0 + EXPERTS_PER_GROUP * (r + 1)] for r in range(N_GROUPS)]
    sel = jnp.where(g_idx == 0.0, grp[0], jnp.where(g_idx == 1.0, grp[1], jnp.where(g_idx == 2.0, grp[2], grp[3])))
    sub = lax.broadcasted_iota(I32, (EXPERTS_PER_GROUP, tm), 0).astype(F32)
    m1 = jnp.max(sel, axis=0, keepdims=True)
    i1 = jnp.min(jnp.where(sel == m1, sub, float(EXPERTS_PER_GROUP)), axis=0, keepdims=True)
    sel2 = jnp.where(sub == i1, -jnp.inf, sel)
    m2 = jnp.max(sel2, axis=0, keepdims=True)
    i2 = jnp.min(jnp.where(sel2 == m2, sub, float(EXPERTS_PER_GROUP)), axis=0, keepdims=True)
    d = jnp.exp(m2 - m1)
    w1 = g_prob / (1.0 + d)
    w2 = g_prob * d / (1.0 + d)
    first_lower = i1 < i2
    lo = jnp.minimum(i1, i2)
    hi = jnp.maximum(i1, i2)
    w_lo = jnp.where(first_lower, w1, w2)
    w_hi = jnp.where(first_lower, w2, w1)
    pair = lo * EXPERTS_PER_GROUP - lo * (lo + 1.0) * 0.5 + hi - lo - 1.0
    cls = g_idx * PAIRS_PER_GROUP + pair

    crow = lax.broadcasted_iota(I32, (CLASS_ROWS, tm), 0).astype(F32)
    onehot = jnp.where(crow == cls, 1.0, 0.0)
    ta = lax.broadcasted_iota(I32, (tm, tm), 0)
    tb = lax.broadcasted_iota(I32, (tm, tm), 1)
    earlier = jnp.where(ta < tb, 1.0, 0.0).astype(BF16)
    before = _dot(onehot.astype(BF16), earlier)
    carry = carry_scr[...]
    rank = jnp.sum(onehot * (before + carry[:, 0:1]), axis=0, keepdims=True)
    carry = carry + jnp.sum(onehot, axis=1, keepdims=True)
    carry_scr[...] = carry
    cnt_ref[...] = carry

    ri_ref[...] = jnp.zeros_like(ri_ref)
    ri_ref[0, 0:1, :] = cls.astype(I32)
    ri_ref[0, 1:2, :] = rank.astype(I32)
    if pack_rows:
        mrow = lax.broadcasted_iota(I32, (LANES, tm), 0)
        batch = pl.program_id(0).astype(F32)
        meta_t = jnp.where(mrow == 0, w_lo, jnp.where(mrow == 1, w_hi, jnp.where(mrow == 2, batch, 0.0)))
        rec_ref[0, :, D + D // 2:] = meta_t.T
    else:
        rf_ref[...] = jnp.zeros_like(rf_ref)
        rf_ref[0, 0:1, :] = w_lo
        rf_ref[0, 1:2, :] = w_hi


def _out_route(x, att, gm, mod, lw, *, tm, pack_rows):
    B, S, D = x.shape
    nt = S // tm
    row = lambda b, i: (b, i, 0)
    tile = lambda b, i: (b * nt + i, 0, 0)
    route_i = (jax.ShapeDtypeStruct((B * nt, SUBLANES, tm), I32), pl.BlockSpec((1, SUBLANES, tm), tile))
    route_f = (jax.ShapeDtypeStruct((B * nt, SUBLANES, tm), F32), pl.BlockSpec((1, SUBLANES, tm), tile))
    counts = (jax.ShapeDtypeStruct((CLASS_ROWS, LANES), F32), _full((CLASS_ROWS, LANES)))
    if pack_rows:
        rec_w = D + D // 2 + LANES
        outs = [(jax.ShapeDtypeStruct((B, S, rec_w), F32), pl.BlockSpec((1, tm, rec_w), row)), route_i, counts]
    else:
        outs = [(jax.ShapeDtypeStruct((B, S, D), F32), pl.BlockSpec((1, tm, D), row)),
                (jax.ShapeDtypeStruct((B, S, D), BF16), pl.BlockSpec((1, tm, D), row)), route_i, route_f, counts]
    out_shape = [o[0] for o in outs]
    out_specs = [o[1] for o in outs]
    return pl.pallas_call(
        functools.partial(_out_route_kernel, pack_rows=pack_rows),
        grid=(B, nt),
        in_specs=[pl.BlockSpec((1, tm, D), row), pl.BlockSpec((1, tm, ATT_WIDTH), row), pl.BlockSpec((1, tm, GM_WIDTH), row),
                  pl.BlockSpec((1, 6, D), lambda b, i: (b, 0, 0)),
                  _full((1, ATT_WIDTH)), _full((ATT_WIDTH, D)), _full((GM_WIDTH, D)), _full((1, D)),
                  _full((D, 2 * ROUTER_COLS)), _full((D, ROUTER_COLS)), _full((1, ROUTER_COLS))],
        out_specs=out_specs,
        out_shape=out_shape,
        scratch_shapes=[pltpu.VMEM((CLASS_ROWS, LANES), F32)],
        compiler_params=_params(2),
    )(x, att, gm, mod, lw["g_out_attn"], lw["w_out_a"], lw["w_out_g"], lw["g_norm_ffn"],
      lw["w_r_a"], lw["w_r_b"], lw["b_r"])


def _swiglu(hb, wgu, wd):
    hid = _dot(hb, wgu)
    act = jax.nn.silu(hid[:, :D_EXPERT]) * hid[:, D_EXPERT:]
    return _dot(act.astype(BF16), wd)


def _moe_pairs_kernel(elo_ref, ehi_ref, nv_ref, dest_ref, rec_hbm, zeros_hbm, gate_ref,
                      wgu_lo_ref, wd_lo_ref, wgu_hi_ref, wd_hi_ref, y_hbm,
                      xbuf0, xbuf1, ybuf0, ybuf1, tok_ref, gsem, ssem, zsem):
    i = pl.program_id(0)
    xbuf, ybuf = (xbuf0, xbuf1), (ybuf0, ybuf1)
    D = ybuf0.shape[-1]
    n_groups_all = MOE_ROWS // SUBLANES

    def gather_start(t, s, g, u):
        pltpu.make_async_copy(rec_hbm.at[pl.ds(t, 1)], xbuf[s].at[g, pl.ds(u, 1)], gsem.at[s]).start()

    def scatter_start(t, s, g, u):
        pltpu.make_async_copy(ybuf[s].at[g, pl.ds(u, 1)], y_hbm.at[pl.ds(t, 1)], ssem.at[s]).start()

    def all_rows(blk, fn):
        base = blk * MOE_ROWS
        for g in range(n_groups_all):
            for u in range(SUBLANES):
                fn(tok_ref[base + g * SUBLANES + u], g, u)

    def valid_rows(blk, fn):
        nv = nv_ref[blk]
        base = blk * MOE_ROWS
        n_groups = nv // SUBLANES

        def group(g, c):
            for u in range(SUBLANES):
                fn(tok_ref[base + g * SUBLANES + u], g, u)
            return c

        def single(r, c):
            fn(tok_ref[base + r], n_groups, r - n_groups * SUBLANES)
            return c

        lax.fori_loop(0, n_groups, group, 0)
        lax.fori_loop(n_groups * SUBLANES, nv, single, 0)

    def gather_wait(s):
        pltpu.make_async_copy(xbuf[s], xbuf[s], gsem.at[s]).wait()

    def scatter_wait(blk, s):
        nv = nv_ref[blk]
        n_groups = nv // SUBLANES
        buf = ybuf[s]

        @pl.when(n_groups > 0)
        def _():
            rows = buf.at[pl.ds(0, n_groups)]
            pltpu.make_async_copy(rows, rows, ssem.at[s]).wait()

        def single(r, c):
            row = buf.at[0, pl.ds(0, 1)]
            pltpu.make_async_copy(row, row, ssem.at[s]).wait()
            return c

        lax.fori_loop(n_groups * SUBLANES, nv, single, 0)

    def experts(s):
        hb = _unpack_bf16_pairs(xbuf[s][:, :, D:D + D // 2].reshape(MOE_ROWS, D // 2))
        meta = xbuf[s][:, :, D + D // 2:].reshape(MOE_ROWS, LANES)
        x = xbuf[s][:, :, 0:D].reshape(MOE_ROWS, D)
        w_lo, w_hi, bidx = meta[:, 0:1], meta[:, 1:2], meta[:, 2:3]
        blane = lax.broadcasted_iota(I32, (1, 2 * MOD_BATCH_PAD), 1)
        blane = jnp.where(blane >= MOD_BATCH_PAD, blane - MOD_BATCH_PAD, blane).astype(F32)
        onehot = jnp.where(bidx == blane, 1.0, 0.0).astype(BF16)
        gate_m = _dot(onehot, gate_ref[...])
        moe = w_lo * _swiglu(hb, wgu_lo_ref[0], wd_lo_ref[0]) + w_hi * _swiglu(hb, wgu_hi_ref[0], wd_hi_ref[0])
        ybuf[s][...] = (x + gate_m * moe).reshape(n_groups_all, SUBLANES, D)

    @pl.when(i == 0)
    def _():
        zero = pltpu.make_async_copy(zeros_hbm, tok_ref, zsem)
        zero.start()
        zero.wait()

        def invert(g, c):
            for u in range(SUBLANES):
                t = g * SUBLANES + u
                tok_ref[dest_ref[t]] = t
            return c
        lax.fori_loop(0, dest_ref.shape[0] // SUBLANES, invert, 0)
        all_rows(0, lambda t, g, u: gather_start(t, 0, g, u))

    prev = jnp.maximum(i - 1, 0)
    has_rows = nv_ref[i] > 0
    prev_full = (i >= 1) & (nv_ref[prev] == MOE_ROWS)
    prev_partial = (i >= 1) & (nv_ref[prev] > 0) & (nv_ref[prev] < MOE_ROWS)

    def step(slot):
        other = 1 - slot

        @pl.when(has_rows)
        def _():
            all_rows(i + 1, lambda t, g, u: gather_start(t, other, g, u))
            gather_wait(slot)

            @pl.when(prev_full)
            def _():
                all_rows(i - 1, lambda t, g, u: scatter_start(t, other, g, u))

            @pl.when(prev_partial)
            def _():
                valid_rows(i - 1, lambda t, g, u: scatter_start(t, other, g, u))

            @pl.when(i >= 2)
            def _():
                scatter_wait(i - 2, slot)
            experts(slot)

        @pl.when(jnp.logical_not(has_rows) & (i >= 1) & (nv_ref[prev] > 0))
        def _():
            gather_wait(slot)
            valid_rows(i - 1, lambda t, g, u: scatter_start(t, other, g, u))

            @pl.when(i >= 2)
            def _():
                scatter_wait(i - 2, slot)
            scatter_wait(i - 1, other)

    for parity in (0, 1):
        pl.when(i % 2 == parity)(functools.partial(step, parity))


def _moe_pairs(rec, blk_elo, blk_ehi, blk_nv, dest, gate_tab, lw):
    T, rec_w = rec.shape
    assert T % SUBLANES == 0
    D = gate_tab.shape[1]
    nb = blk_nv.shape[0]
    wgu, wd = lw["w_gu_e"], lw["w_d_e"]
    grid_spec = pltpu.PrefetchScalarGridSpec(
        num_scalar_prefetch=4,
        grid=(nb,),
        in_specs=[
            pl.BlockSpec(memory_space=pl.ANY),
            pl.BlockSpec(memory_space=pl.ANY),
            pl.BlockSpec(gate_tab.shape, lambda i, *_: (0, 0)),
            pl.BlockSpec((1, D, 2 * D_EXPERT), lambda i, elo, ehi, nv, tok: (elo[i], 0, 0)),
            pl.BlockSpec((1, D_EXPERT, D), lambda i, elo, ehi, nv, tok: (elo[i], 0, 0)),
            pl.BlockSpec((1, D, 2 * D_EXPERT), lambda i, elo, ehi, nv, tok: (ehi[i], 0, 0)),
            pl.BlockSpec((1, D_EXPERT, D), lambda i, elo, ehi, nv, tok: (ehi[i], 0, 0)),
        ],
        out_specs=pl.BlockSpec(memory_space=pl.ANY),
        scratch_shapes=[pltpu.VMEM((MOE_ROWS // SUBLANES, SUBLANES, rec_w), F32),
                        pltpu.VMEM((MOE_ROWS // SUBLANES, SUBLANES, rec_w), F32),
                        pltpu.VMEM((MOE_ROWS // SUBLANES, SUBLANES, D), F32),
                        pltpu.VMEM((MOE_ROWS // SUBLANES, SUBLANES, D), F32),
                        pltpu.SMEM((nb * MOE_ROWS,), I32),
                        pltpu.SemaphoreType.DMA((2,)), pltpu.SemaphoreType.DMA((2,)), pltpu.SemaphoreType.DMA(())],
    )
    return pl.pallas_call(
        _moe_pairs_kernel,
        grid_spec=grid_spec,
        out_shape=jax.ShapeDtypeStruct((T, D), F32),
        compiler_params=_params(1),
    )(blk_elo, blk_ehi, blk_nv, dest, rec, jnp.zeros((nb * MOE_ROWS,), I32), gate_tab, wgu, wd, wgu, wd)


def _small_lookup(table, idx):
    ids = jnp.arange(table.shape[0], dtype=I32)
    return jnp.sum(jnp.where(idx[:, None] == ids[None, :], table[None, :], 0), axis=1)


def _pair_tables():
    lo, hi = [], []
    for g in range(N_GROUPS):
        for a in range(EXPERTS_PER_GROUP):
            for b in range(a + 1, EXPERTS_PER_GROUP):
                lo.append(g * EXPERTS_PER_GROUP + a)
                hi.append(g * EXPERTS_PER_GROUP + b)
    return np.asarray(lo, np.int32), np.asarray(hi, np.int32)


def _moe_prompt(rec, cls, rank, counts, mod, lw):
    B, S, rec_w = rec.shape
    D = mod.shape[-1]
    T = B * S
    nb = T // MOE_ROWS + N_CLASSES + 1
    nblk = (counts + MOE_ROWS - 1) // MOE_ROWS
    blk_end = jnp.cumsum(nblk)
    blk_start = blk_end - nblk
    dest = _small_lookup(blk_start, cls) * MOE_ROWS + rank
    ids = jnp.arange(nb, dtype=I32)
    used = blk_end[-1]
    class_of = lambda blk: jnp.sum((blk_end[None, :] <= blk[:, None]).astype(I32), axis=1)
    blk_cls = jnp.where(ids < used, jnp.minimum(class_of(ids), N_CLASSES - 1), class_of(used[None] - 1))
    first_row = (ids - _small_lookup(blk_start, blk_cls)) * MOE_ROWS
    blk_nv = jnp.where(ids < used, jnp.clip(_small_lookup(counts, blk_cls) - first_row, 0, MOE_ROWS), 0)
    lo_tab, hi_tab = _pair_tables()
    blk_elo = _small_lookup(jnp.asarray(lo_tab), blk_cls)
    blk_ehi = _small_lookup(jnp.asarray(hi_tab), blk_cls)
    gate_hi, gate_lo = _split_bf16(jnp.pad(mod[:, 5, :], ((0, MOD_BATCH_PAD - B), (0, 0))))
    gate_tab = jnp.concatenate([gate_hi, gate_lo], axis=0)
    y = _moe_pairs(rec.reshape(T, rec_w), blk_elo, blk_ehi, blk_nv.astype(I32), dest.astype(I32), gate_tab, lw)
    return y.reshape(B, S, D)


def _moe_dense_kernel(h_ref, x1_ref, gate_ref, w_ref, sel_ref, wgu_ref, wd_ref, y_ref):
    e = pl.program_id(0)

    @pl.when(e == 0)
    def _():
        y_ref[...] = jnp.zeros_like(y_ref)

    ye = _swiglu(h_ref[...], wgu_ref[0], wd_ref[0])
    y_ref[...] += jnp.where(sel_ref[0] > 0.5, w_ref[0] * ye, 0.0)

    @pl.when(e == pl.num_programs(0) - 1)
    def _():
        y_ref[...] = x1_ref[...] + gate_ref[...] * y_ref[...]


def _moe_dense(h2, x1, gate_rows, w_sel, sel, lw):
    T, D = x1.shape
    per_e = lambda e: (e, 0, 0)
    return pl.pallas_call(
        _moe_dense_kernel,
        grid=(N_EXPERTS,),
        in_specs=[_full((T, D)), _full((T, D)), _full((T, D)),
                  pl.BlockSpec((1, T, 1), per_e), pl.BlockSpec((1, T, 1), per_e),
                  pl.BlockSpec((1, D, 2 * D_EXPERT), per_e), pl.BlockSpec((1, D_EXPERT, D), per_e)],
        out_specs=_full((T, D)),
        out_shape=jax.ShapeDtypeStruct((T, D), F32),
        compiler_params=_params(1),
    )(h2, x1, gate_rows, w_sel, sel, lw["w_gu_e"], lw["w_d_e"])


def _rope_tables(pos, gain, scale):
    inv = 1.0 / (ROPE_BASE ** (jnp.arange(ROPE_HALF, dtype=F32) / ROPE_HALF))
    ang = pos.astype(F32)[:, None] * inv[None, :]
    cos, sin = jnp.cos(ang) * scale, jnp.sin(ang) * scale
    n = pos.shape[0]
    z = lambda w: jnp.zeros((n, w), F32)
    pad = HEAD_PAD - QK_DIM
    g_nope, g1, g2 = gain[:NOPE_DIM], gain[NOPE_DIM:NOPE_DIM + ROPE_HALF], gain[NOPE_DIM + ROPE_HALF:]
    tab_c = jnp.concatenate([jnp.broadcast_to(g_nope * scale, (n, NOPE_DIM)), cos * g1, cos * g2, z(pad)], axis=1)
    tab_s = jnp.concatenate([z(NOPE_DIM), -sin * g2, sin * g1, z(pad)], axis=1)
    return tab_c, tab_s


def _prep_layer(w, l, chunk_lens):
    D = w["w_in"].shape[1]
    H = MLA_HEADS
    lw = {}
    row = lambda name: w[name][l].reshape(1, -1)
    for name in ("g_norm_mix", "g_q_lat", "g_kv_lat", "g_ln_v", "b_ln_v", "g_out_attn", "g_out_gmlp", "g_norm_ffn"):
        lw[name] = row(name)
    w_in = w["w_in"][l]
    o1, o2, o3 = Q_LORA, Q_LORA + KV_LORA, Q_LORA + KV_LORA + ROPE_DIM
    o4 = o3 + GM_WIDTH
    z32 = jnp.zeros((D, ROPE_DIM), F32)
    wr = w_in[:, o2:o3]
    lw["w_in_p"] = jnp.concatenate([w_in[:, :o2], wr, z32, wr, z32, w_in[:, o3:o4], w_in[:, o4:]], axis=1).astype(BF16)
    assert lw["w_in_p"].shape[1] == PROJ_COLS
    pad = HEAD_PAD - QK_DIM
    w_uq = w["w_uq"][l].reshape(Q_LORA, H, QK_DIM)
    w_uq_swapped = jnp.concatenate([jnp.zeros((Q_LORA, H, NOPE_DIM), F32), w_uq[:, :, NOPE_DIM + ROPE_HALF:],
                                    w_uq[:, :, NOPE_DIM:NOPE_DIM + ROPE_HALF]], axis=2)
    head_pad = lambda a: jnp.pad(a, ((0, 0), (0, 0), (0, pad))).reshape(Q_LORA, H * HEAD_PAD)
    lw["w_uq_p"] = jnp.concatenate([head_pad(w_uq), head_pad(w_uq_swapped)], axis=1).astype(BF16)
    w_ukv = w["w_ukv"][l].reshape(KV_LORA, H, NOPE_DIM + V_DIM)
    lw["w_k_p"] = jnp.pad(w_ukv[:, :, :NOPE_DIM], ((0, 0), (0, 0), (0, HEAD_PAD - NOPE_DIM))).reshape(KV_LORA, H * HEAD_PAD).astype(BF16)
    w_v = w_ukv[:, :, NOPE_DIM:]
    zeros_v = jnp.zeros_like(w_v)
    odd_head = (jnp.arange(H) % 2 == 1)[None, :, None]
    w_v_lo, w_v_hi = jnp.where(odd_head, zeros_v, w_v), jnp.where(odd_head, w_v, zeros_v)
    lw["w_v_p"] = jnp.concatenate([w_v_lo, w_v_hi], axis=2).reshape(KV_LORA, H * HEAD_PAD).astype(BF16)
    lw["scores_bounded"] = _scores_bounded(w, l)
    lw["g_qnorm"] = w["g_qnorm"][l]
    lw["g_knorm"] = w["g_knorm"][l]
    for L in chunk_lens:
        wsp = w["w_spatial"][l][:, :L, :L]
        lw["w_sp_pairs_%d" % L] = wsp.reshape(GM_HEADS // 2, 2 * L, L).astype(BF16)
        lw["b_sp_rows_%d" % L] = jnp.repeat(jnp.transpose(w["b_spatial"][l][:, :L]), GM_HEAD_DIM, axis=1)
    w_out = w["w_out"][l].astype(BF16)
    lw["w_out_a"], lw["w_out_g"] = w_out[:ATT_WIDTH], w_out[ATT_WIDTH:]
    wr_full = jnp.zeros((D, ROUTER_COLS), F32)
    wr_full = wr_full.at[:, :N_GROUPS].set(w["w_router_group"][l])
    wr_full = wr_full.at[:, ROUTER_EXPERT_LANE0:ROUTER_EXPERT_LANE0 + N_EXPERTS].set(w["w_router_expert"][l])
    r_hi, r_lo = _split_bf16(wr_full)
    lw["w_r_a"] = jnp.concatenate([r_hi, r_lo], axis=1)
    lw["w_r_b"] = r_hi
    br = jnp.zeros((1, ROUTER_COLS), F32)
    br = br.at[0, :N_GROUPS].set(w["b_router_group"][l])
    lw["b_r"] = br.at[0, ROUTER_EXPERT_LANE0:ROUTER_EXPERT_LANE0 + N_EXPERTS].set(w["b_router_expert"][l])
    lw["w_gu_e"] = jnp.concatenate([w["w_gate_e"][l], w["w_up_e"][l]], axis=-1).astype(BF16)
    lw["w_d_e"] = w["w_down_e"][l].astype(BF16)
    return lw


def _tiles(seq):
    return min(seq, 512), min(seq, 1024), min(seq, 512)


def _layer_prompt(x, mod, lw):
    B, S, D = x.shape
    tm, tq, tk = _tiles(S)
    pos = jnp.arange(S)
    q_tabs = _rope_tables(pos, lw["g_qnorm"], QK_DIM ** -0.5 * LOG2E)
    k_tabs = _rope_tables(pos, lw["g_knorm"], 1.0)
    ckv, krope, q, gm, k, v = _mix_in(x, mod, lw, q_tabs, k_tabs, tm=tm, chunk_len=GM_CHUNK, emit_kv=True, emit_vrows=False)
    att = _attention(lw["scores_bounded"], q, k, v, tq=tq, tk=tk, q_off=0, kv_valid=S)
    rec, ri, cnt = _out_route(x, att, gm, mod, lw, tm=tm, pack_rows=True)
    cls = ri[:, 0, :].reshape(B * S)
    rank = ri[:, 1, :].reshape(B * S)
    counts = cnt[:N_CLASSES, 0].astype(I32)
    y = _moe_prompt(rec, cls, rank, counts, mod, lw)
    return y, ckv, krope


def _layer_sample(x, mod, past_ckv, past_krope, lw):
    B, S, D = x.shape
    past = past_ckv.shape[1]
    q_tabs = _rope_tables(past + jnp.arange(S), lw["g_qnorm"], QK_DIM ** -0.5 * LOG2E)
    ckv, krope, q, gm, v_rows = _mix_in(x, mod, lw, q_tabs, q_tabs, tm=S, chunk_len=S, emit_kv=False, emit_vrows=True)
    kv_valid = past + S
    kv_pad = -(-kv_valid // LANES) * LANES
    extra = kv_pad - kv_valid
    ckv_all = jnp.concatenate([past_ckv, ckv, jnp.zeros((B, extra, KV_LORA), F32)], axis=1)
    kr_all = jnp.concatenate([past_krope, krope, jnp.zeros((B, extra, ROPE_DIM), F32)], axis=1)
    kr_slot = jnp.pad(kr_all, ((0, 0), (0, 0), (NOPE_DIM, HEAD_PAD - QK_DIM)))
    k_tabs = _rope_tables(jnp.arange(kv_pad), lw["g_knorm"], 1.0)
    k, v = _kv_latent(ckv_all, kr_slot, lw, k_tabs, tr=kv_pad)
    att = _attention(lw["scores_bounded"], q, k, v, tq=S, tk=kv_pad, q_off=past, kv_valid=kv_valid)
    x1, h2, ri, rf, _ = _out_route(x, att, gm, mod, lw, tm=S, pack_rows=False)
    T = B * S
    cls, w_lo, w_hi = ri[:, 0, :].reshape(T), rf[:, 0, :].reshape(T), rf[:, 1, :].reshape(T)
    lo_tab, hi_tab = _pair_tables()
    e_lo, e_hi = jnp.asarray(lo_tab)[cls], jnp.asarray(hi_tab)[cls]
    eids = jnp.arange(N_EXPERTS, dtype=I32)[:, None]
    is_lo, is_hi = eids == e_lo[None, :], eids == e_hi[None, :]
    w_sel = (jnp.where(is_lo, w_lo[None, :], 0.0) + jnp.where(is_hi, w_hi[None, :], 0.0))[:, :, None]
    sel = (is_lo | is_hi).astype(F32)[:, :, None]
    gate_rows = jnp.repeat(mod[:, 5, :], S, axis=0)
    y = _moe_dense(h2.reshape(T, D), x1.reshape(T, D), gate_rows, w_sel, sel, lw)
    return y.reshape(B, S, D), ckv, krope, v_rows


def kernel(x_prompt, x_sample, cache_ckv, cache_krope, c_prompt, c_sample, w_ada, b_ada, g_norm_mix, w_in, g_q_lat, w_uq, g_kv_lat, w_ukv, g_qnorm, g_knorm, g_ln_v, b_ln_v, w_spatial, b_spatial, g_out_attn, g_out_gmlp, w_out, g_norm_ffn, w_router_group, b_router_group, w_router_expert, b_router_expert, w_gate_e, w_up_e, w_down_e):
    w = dict(w_in=w_in, g_norm_mix=g_norm_mix, g_q_lat=g_q_lat, w_uq=w_uq, g_kv_lat=g_kv_lat, w_ukv=w_ukv,
             g_qnorm=g_qnorm, g_knorm=g_knorm, g_ln_v=g_ln_v, b_ln_v=b_ln_v, w_spatial=w_spatial, b_spatial=b_spatial,
             g_out_attn=g_out_attn, g_out_gmlp=g_out_gmlp, w_out=w_out, g_norm_ffn=g_norm_ffn,
             w_router_group=w_router_group, b_router_group=b_router_group, w_router_expert=w_router_expert,
             b_router_expert=b_router_expert, w_gate_e=w_gate_e, w_up_e=w_up_e, w_down_e=w_down_e)
    depth = w_ada.shape[0]
    Bp, Sp, D = x_prompt.shape
    Bs, Ss, _ = x_sample.shape
    assert Sp % GM_CHUNK == 0 and Ss <= GM_CHUNK and Ss % CHUNK == 0 and Bp <= MOD_BATCH_PAD
    c_all = jnp.concatenate([c_prompt, c_sample], axis=0)
    y_p, y_s = x_prompt, x_sample
    outs = [[] for _ in range(5)]
    for l in range(depth):
        lw = _prep_layer(w, l, (GM_CHUNK, Ss))
        mod = _ada_mod(c_all, w_ada[l], b_ada[l]).reshape(Bp + Bs, 6, D)
        y_p, ckv_p, kr_p = _layer_prompt(y_p, mod[:Bp], lw)
        y_s, ckv_s, kr_s, v_s = _layer_sample(y_s, mod[Bp:], cache_ckv[l], cache_krope[l], lw)
        for lst, val in zip(outs, (ckv_p, kr_p, ckv_s, kr_s, v_s)):
            lst.append(val)
    return (y_p, y_s) + tuple(jnp.stack(lst) for lst in outs)
```

```python
import functools

import numpy as np
import jax
import jax.numpy as jnp
from jax import lax
from jax.experimental import pallas as pl
from jax.experimental.pallas import tpu as pltpu

F32 = jnp.float32
BF16 = jnp.bfloat16
I32 = jnp.int32

CHUNK = 64
CHUNK_SHIFT = 6
EPS = 1e-6
MLA_HEADS = 8
Q_LORA = 256
KV_LORA = 128
NOPE_DIM = 64
ROPE_DIM = 32
ROPE_HALF = ROPE_DIM // 2
V_DIM = 64
QK_DIM = NOPE_DIM + ROPE_DIM
ATT_WIDTH = MLA_HEADS * V_DIM
ROPE_BASE = 10000.0
GM_HEADS = 8
GM_HEAD_DIM = 64
GM_WIDTH = GM_HEADS * GM_HEAD_DIM
GM_CHUNK = 128
N_GROUPS = 4
EXPERTS_PER_GROUP = 8
N_EXPERTS = N_GROUPS * EXPERTS_PER_GROUP
D_EXPERT = 256
PAIRS_PER_GROUP = EXPERTS_PER_GROUP * (EXPERTS_PER_GROUP - 1) // 2
N_CLASSES = N_GROUPS * PAIRS_PER_GROUP

LANES = 128
SUBLANES = 8
HEAD_PAD = LANES
PROJ_COLS = 1536
ROUTER_COLS = LANES
ROUTER_EXPERT_LANE0 = SUBLANES
CLASS_ROWS = LANES
MOE_ROWS = 128
MOD_BATCH_PAD = 16
DIAG_ROWS = 256
COPY_BURST_GROUPS = (0, 4, 8, 12, 16)
VMEM_LIMIT = 48 * 1024 * 1024
NEG_BIG = -1e30
LOG2E = 1.4426950408889634
BF16_SLACK = 1.02
SCORE_BOUND = 90.0

assert CHUNK == 1 << CHUNK_SHIFT


def _params(n_axes, vmem=VMEM_LIMIT):
    return pltpu.CompilerParams(dimension_semantics=("arbitrary",) * n_axes, vmem_limit_bytes=vmem)


def _full(shape):
    nd = len(shape)
    return pl.BlockSpec(shape, lambda *_: (0,) * nd)


def _split_bf16(x):
    hi = x.astype(BF16)
    lo = (x - hi.astype(F32)).astype(BF16)
    return hi, lo


def _dot(a, b):
    return jnp.dot(a, b, preferred_element_type=F32)


def _ada_kernel(c_ref, w_ref, b_ref, o_ref):
    a_hi, a_lo = _split_bf16(jax.nn.silu(c_ref[...]))
    w_hi, w_lo = _split_bf16(w_ref[...])
    o_ref[...] = _dot(a_hi, w_hi) + _dot(a_lo, w_hi) + _dot(a_hi, w_lo) + b_ref[...]


def _ada_mod(c, w_ada, b_ada):
    n, d = c.shape
    cols = w_ada.shape[1]
    tn = 1536
    return pl.pallas_call(
        _ada_kernel,
        grid=(cols // tn,),
        in_specs=[_full((n, d)), pl.BlockSpec((d, tn), lambda j: (0, j)), pl.BlockSpec((1, tn), lambda j: (0, j))],
        out_specs=pl.BlockSpec((n, tn), lambda j: (0, j)),
        out_shape=jax.ShapeDtypeStruct((n, cols), F32),
        compiler_params=_params(1),
    )(c, w_ada, b_ada.reshape(1, cols))


def _rms(x, g):
    return x * lax.rsqrt(jnp.mean(x * x, axis=-1, keepdims=True) + EPS) * g


def _head_norm_rope(xh, xh_swapped, tab_c, tab_s):
    ms = jnp.sum(xh * xh, axis=-1, keepdims=True) * (1.0 / QK_DIM)
    return (xh * tab_c + xh_swapped * tab_s) * lax.rsqrt(ms + EPS)


def _keys_values(ckv, kr_slot, gkv, wk, wv, tab_c, tab_s, k_ref, v_ref):
    cb = _rms(ckv, gkv).astype(BF16)
    kall = _dot(cb, wk)
    vall = _dot(cb, wv)
    lane = lax.broadcasted_iota(I32, (1, LANES), 1)
    ones_hi = jnp.where(lane >= V_DIM, 1.0, 0.0)
    ones_lo = 1.0 - ones_hi
    kr_swapped = jnp.where(lane < NOPE_DIM + ROPE_HALF, pltpu.roll(kr_slot, HEAD_PAD - ROPE_HALF, 1),
                           pltpu.roll(kr_slot, ROPE_HALF, 1))
    kr_swapped = jnp.where((lane >= NOPE_DIM) & (lane < QK_DIM), kr_swapped, 0.0)
    for h in range(MLA_HEADS):
        kh = kall[:, h * HEAD_PAD:(h + 1) * HEAD_PAD] + kr_slot
        k_ref[0, h] = _head_norm_rope(kh, kr_swapped, tab_c, tab_s).astype(BF16)
        ones = ones_hi if h % 2 == 0 else ones_lo
        v_ref[0, h] = (vall[:, h * HEAD_PAD:(h + 1) * HEAD_PAD] + ones).astype(BF16)


def _mix_in_kernel(x_ref, mod_ref, gmix_ref, win_ref, gql_ref, wuq_ref, gkv_ref, wk_ref, wv_ref,
                   cq_ref, sq_ref, ck_ref, sk_ref, glnv_ref, blnv_ref, wsp_ref, bsp_ref,
                   ggm_ref, *rest, chunk_len, emit_kv, emit_vrows):
    outs = list(rest[:-1])
    mixed_scr = rest[-1]
    ckv_ref, kr_ref, q_ref, gm_ref = outs[:4]
    outs = outs[4:]
    if emit_kv:
        k_ref, v_ref = outs[:2]
        outs = outs[2:]
    if emit_vrows:
        vrows_ref = outs[0]

    x = x_ref[0]
    tm = x.shape[0]
    shift, scale = mod_ref[0, 0:1, :], mod_ref[0, 1:2, :]
    h = _rms(x, gmix_ref[...]) * (1.0 + scale) + shift
    proj = _dot(h.astype(BF16), win_ref[...])

    q_lat = proj[:, 0:Q_LORA]
    ckv = proj[:, Q_LORA:Q_LORA + KV_LORA]
    kr_blk = proj[:, Q_LORA + KV_LORA:Q_LORA + KV_LORA + LANES]
    ckv_ref[0] = ckv
    kr_ref[0] = kr_blk[:, 0:ROPE_DIM]

    q = _dot(_rms(q_lat, gql_ref[...]).astype(BF16), wuq_ref[...])
    cq, sq = cq_ref[...], sq_ref[...]
    n_q = MLA_HEADS * HEAD_PAD
    for hd in range(MLA_HEADS):
        qh = q[:, hd * HEAD_PAD:(hd + 1) * HEAD_PAD]
        qh_swapped = q[:, n_q + hd * HEAD_PAD:n_q + (hd + 1) * HEAD_PAD]
        q_ref[0, hd] = _head_norm_rope(qh, qh_swapped, cq, sq).astype(BF16)

    if emit_kv:
        lane = lax.broadcasted_iota(I32, (1, LANES), 1)
        kr_slot = jnp.where(lane >= NOPE_DIM, kr_blk, 0.0)
        _keys_values(ckv, kr_slot, gkv_ref[...], wk_ref[...], wv_ref[...], ck_ref[...], sk_ref[...], k_ref, v_ref)

    g_u = proj[:, 512:512 + GM_WIDTH]
    g_v = proj[:, 1024:1024 + GM_WIDTH]
    u = jax.nn.gelu(g_u)
    gv = jax.nn.gelu(g_v)
    mu = jnp.mean(gv, axis=-1, keepdims=True)
    xc = gv - mu
    var = jnp.mean(xc * xc, axis=-1, keepdims=True)
    v_rows = xc * lax.rsqrt(var + EPS) * glnv_ref[...] + blnv_ref[...]
    if emit_vrows:
        vrows_ref[0] = v_rows
    vb = v_rows.astype(BF16)

    L = chunk_len
    t = lax.broadcasted_iota(I32, (2 * L, L), 0)
    s = lax.broadcasted_iota(I32, (2 * L, L), 1)
    t = jnp.where(t >= L, t - L, t)
    allowed = (s >> CHUNK_SHIFT) <= (t >> CHUNK_SHIFT)
    lane = lax.broadcasted_iota(I32, (1, LANES), 1)
    first_head = lane < GM_HEAD_DIM
    for p in range(GM_HEADS // 2):
        w_pair = jnp.where(allowed, wsp_ref[p], jnp.zeros((), BF16))
        for c in range(tm // L):
            vp = vb[c * L:(c + 1) * L, p * LANES:(p + 1) * LANES]
            r = _dot(w_pair, vp)
            mixed = jnp.where(first_head, r[:L], r[L:])
            mixed_scr[c * L:(c + 1) * L, p * LANES:(p + 1) * LANES] = mixed + bsp_ref[:, p * LANES:(p + 1) * LANES]
    gm = u * mixed_scr[...]
    gm_ref[0] = _rms(gm, ggm_ref[...]).astype(BF16)


def _mix_in(x, mod, lw, q_tabs, k_tabs, *, tm, chunk_len, emit_kv, emit_vrows):
    B, S, D = x.shape
    nt = S // tm
    H = MLA_HEADS
    row = lambda b, i: (b, i, 0)
    tab = pl.BlockSpec((tm, LANES), lambda b, i: (i, 0))
    in_specs = [
        pl.BlockSpec((1, tm, D), row),
        pl.BlockSpec((1, 6, D), lambda b, i: (b, 0, 0)),
        _full((1, D)), _full((D, PROJ_COLS)), _full((1, Q_LORA)), _full((Q_LORA, 2 * H * HEAD_PAD)),
        _full((1, KV_LORA)), _full((KV_LORA, H * HEAD_PAD)), _full((KV_LORA, H * HEAD_PAD)),
        tab, tab, tab, tab,
        _full((1, GM_WIDTH)), _full((1, GM_WIDTH)),
        _full((GM_HEADS // 2, 2 * chunk_len, chunk_len)), _full((chunk_len, GM_WIDTH)), _full((1, GM_WIDTH)),
    ]
    out_shape = [
        jax.ShapeDtypeStruct((B, S, KV_LORA), F32),
        jax.ShapeDtypeStruct((B, S, ROPE_DIM), F32),
        jax.ShapeDtypeStruct((B, H, S, HEAD_PAD), BF16),
        jax.ShapeDtypeStruct((B, S, GM_WIDTH), BF16),
    ]
    head_blk = pl.BlockSpec((1, H, tm, HEAD_PAD), lambda b, i: (b, 0, i, 0))
    out_specs = [
        pl.BlockSpec((1, tm, KV_LORA), row),
        pl.BlockSpec((1, tm, ROPE_DIM), row),
        head_blk,
        pl.BlockSpec((1, tm, GM_WIDTH), row),
    ]
    if emit_kv:
        out_shape += [jax.ShapeDtypeStruct((B, H, S, HEAD_PAD), BF16), jax.ShapeDtypeStruct((B, H, S, HEAD_PAD), BF16)]
        out_specs += [head_blk, head_blk]
    if emit_vrows:
        out_shape += [jax.ShapeDtypeStruct((B, S, GM_WIDTH), F32)]
        out_specs += [pl.BlockSpec((1, tm, GM_WIDTH), row)]
    kern = functools.partial(_mix_in_kernel, chunk_len=chunk_len, emit_kv=emit_kv, emit_vrows=emit_vrows)
    return pl.pallas_call(
        kern,
        grid=(B, nt),
        in_specs=in_specs,
        out_specs=out_specs,
        out_shape=out_shape,
        scratch_shapes=[pltpu.VMEM((tm, GM_WIDTH), F32)],
        compiler_params=_params(2),
    )(x, mod, lw["g_norm_mix"], lw["w_in_p"], lw["g_q_lat"], lw["w_uq_p"], lw["g_kv_lat"], lw["w_k_p"], lw["w_v_p"],
      *q_tabs, *k_tabs, lw["g_ln_v"], lw["b_ln_v"],
      lw["w_sp_pairs_%d" % chunk_len], lw["b_sp_rows_%d" % chunk_len], lw["g_out_gmlp"])


def _kv_latent_kernel(ckv_ref, kr_ref, gkv_ref, wk_ref, wv_ref, c_ref, s_ref, k_ref, v_ref):
    _keys_values(ckv_ref[0], kr_ref[0], gkv_ref[...], wk_ref[...], wv_ref[...], c_ref[...], s_ref[...], k_ref, v_ref)


def _kv_latent(ckv_all, kr_slot_all, lw, k_tabs, *, tr):
    B, K, _ = ckv_all.shape
    H = MLA_HEADS
    row = lambda b, i: (b, i, 0)
    tab = pl.BlockSpec((tr, LANES), lambda b, i: (i, 0))
    return pl.pallas_call(
        _kv_latent_kernel,
        grid=(B, K // tr),
        in_specs=[pl.BlockSpec((1, tr, KV_LORA), row), pl.BlockSpec((1, tr, LANES), row),
                  _full((1, KV_LORA)), _full((KV_LORA, H * HEAD_PAD)), _full((KV_LORA, H * HEAD_PAD)),
                  tab, tab],
        out_specs=[pl.BlockSpec((1, H, tr, HEAD_PAD), lambda b, i: (b, 0, i, 0))] * 2,
        out_shape=[jax.ShapeDtypeStruct((B, H, K, HEAD_PAD), BF16)] * 2,
        compiler_params=_params(2),
    )(ckv_all, kr_slot_all, lw["g_kv_lat"], lw["w_k_p"], lw["w_v_p"], *k_tabs)


def _attn_kernel(bounded_ref, q_ref, k_ref, v_ref, o_ref, acc_ref, *, tq, tk, n_q, q_off, kv_valid, split_diag):
    i = pl.program_id(2) if n_q > 1 else 0
    q_first = q_off + i * tq
    vis_first = jnp.minimum(((q_first >> CHUNK_SHIFT) + 1) << CHUNK_SHIFT, kv_valid)
    vis_last = jnp.minimum((((q_first + tq - 1) >> CHUNK_SHIFT) + 1) << CHUNK_SHIFT, kv_valid)
    n_unmasked = vis_first // tk
    n_total = (vis_last + tk - 1) // tk

    q_pos = q_first + lax.broadcasted_iota(I32, (tq, 1), 0)
    limit = jnp.minimum(((q_pos >> CHUNK_SHIFT) + 1) << CHUNK_SHIFT, kv_valid)
    lane = lax.broadcasted_iota(I32, (1, LANES), 1)

    def scores(j, start, masked, r0=0, nr=tq, nk=tk):
        s = lax.dot_general(q_ref[0, j, r0:r0 + nr, :], k_ref[0, j, pl.ds(start, nk), :], (((1,), (1,)), ((), ())),
                            preferred_element_type=F32)
        if masked:
            k_pos = start + lax.broadcasted_iota(I32, (1, nk), 1)
            s = jnp.where(k_pos < limit[r0:r0 + nr], s, NEG_BIG)
        return s

    def plain(start, masked, r0=0, nr=tq, nk=tk, assign=False):
        for j in range(2):
            p = jnp.exp2(scores(j, start, masked, r0, nr, nk)).astype(BF16)
            pv = _dot(p, v_ref[0, j, pl.ds(start, nk), :])
            if assign:
                acc_ref[j, r0:r0 + nr, :] = pv
            else:
                acc_ref[j, r0:r0 + nr, :] += pv

    def online_block(kb, carry, masked):
        start = pl.multiple_of(kb * tk, tk)
        new = []
        for j in range(2):
            m, acc = carry[2 * j:2 * j + 2]
            s = scores(j, start, masked)
            m_new = jnp.maximum(m, jnp.max(s, axis=-1, keepdims=True))
            p = jnp.exp2(s - m_new).astype(BF16)
            acc = jnp.exp2(m - m_new) * acc + _dot(p, v_ref[0, j, pl.ds(start, tk), :])
            new += [m_new, acc]
        return tuple(new)

    def finish(acc0, acc1):
        first = lane < V_DIM
        num = jnp.where(first, acc0, acc1)
        den = pltpu.roll(jnp.where(first, acc1, acc0), V_DIM, 1)
        o_ref[0] = (num / den).astype(BF16)

    @pl.when(bounded_ref[0] == 1)
    def _():
        at = lambda kb: pl.multiple_of(kb * tk, tk)

        def pair(p, c):
            plain(at(2 * p), False)
            plain(at(2 * p + 1), False)
            return c

        def single(kb, c, masked):
            plain(at(kb), masked)
            return c

        if split_diag:
            own = 0 if n_q == 1 else pl.multiple_of(i * tq, tq)
            for r in range(tq // DIAG_ROWS):
                plain(own, True, r * DIAG_ROWS, DIAG_ROWS, (r + 1) * DIAG_ROWS, assign=True)
        else:
            acc_ref[...] = jnp.zeros_like(acc_ref)
        n_pairs = n_unmasked // 2
        lax.fori_loop(0, n_pairs, pair, 0)
        lax.fori_loop(2 * n_pairs, n_unmasked, lambda kb, c: single(kb, c, False), 0)
        if not split_diag:
            lax.fori_loop(n_unmasked, n_total, lambda kb, c: single(kb, c, True), 0)
        finish(acc_ref[0], acc_ref[1])

    @pl.when(bounded_ref[0] != 1)
    def _():
        zeros = jnp.zeros((tq, LANES), F32)
        m0 = jnp.full((tq, 1), NEG_BIG, F32)
        c = lax.fori_loop(0, n_unmasked, lambda kb, c: online_block(kb, c, False), (m0, zeros, m0, zeros))
        c = lax.fori_loop(n_unmasked, n_total, lambda kb, c: online_block(kb, c, True), c)
        finish(c[1], c[3])


def _attention(bounded, q, k, v, *, tq, tk, q_off, kv_valid):
    B, H, Sq, _ = q.shape
    Sk = k.shape[2]
    split_diag = q_off == 0 and kv_valid >= Sq and tq % DIAG_ROWS == 0 and tq % tk == 0
    kern = functools.partial(_attn_kernel, tq=tq, tk=tk, n_q=Sq // tq, q_off=q_off, kv_valid=kv_valid,
                             split_diag=split_diag)
    grid_spec = pltpu.PrefetchScalarGridSpec(
        num_scalar_prefetch=1,
        grid=(B, H // 2, Sq // tq),
        in_specs=[pl.BlockSpec((1, 2, tq, HEAD_PAD), lambda b, hp, i, f: (b, hp, i, 0)),
                  pl.BlockSpec((1, 2, Sk, HEAD_PAD), lambda b, hp, i, f: (b, hp, 0, 0)),
                  pl.BlockSpec((1, 2, Sk, HEAD_PAD), lambda b, hp, i, f: (b, hp, 0, 0))],
        out_specs=pl.BlockSpec((1, tq, LANES), lambda b, hp, i, f: (b, i, hp)),
        scratch_shapes=[pltpu.VMEM((2, tq, LANES), F32)],
    )
    return pl.pallas_call(
        kern,
        grid_spec=grid_spec,
        out_shape=jax.ShapeDtypeStruct((B, Sq, ATT_WIDTH), BF16),
        compiler_params=_params(3),
    )(bounded, q, k, v)


def _scores_bounded(w, l):
    gq = jnp.max(jnp.abs(w["g_qnorm"][l]))
    gk = jnp.max(jnp.abs(w["g_knorm"][l]))
    bound = (QK_DIM ** 0.5) * LOG2E * BF16_SLACK * gq * gk
    return (bound <= SCORE_BOUND).astype(I32).reshape(1)


def _pack_bf16_pairs(h):
    n = h.shape[1] // 2
    hi = pltpu.bitcast(h[:, :n].astype(BF16).astype(F32), jnp.uint32)
    lo = pltpu.bitcast(h[:, n:].astype(BF16).astype(F32), jnp.uint32)
    return pltpu.bitcast(hi | (lo >> 16), F32)


def _unpack_bf16_pairs(words):
    w = pltpu.bitcast(words, jnp.uint32)
    hi = pltpu.bitcast(w & jnp.uint32(0xFFFF0000), F32)
    lo = pltpu.bitcast(w << 16, F32)
    return jnp.concatenate([hi, lo], axis=1).astype(BF16)


def _out_route_kernel(x_ref, att_ref, gm_ref, mod_ref, goa_ref, woa_ref, wog_ref, gffn_ref, wra_ref, wrb_ref, br_ref,
                      *rest, pack_rows):
    carry_scr = rest[-1]
    if pack_rows:
        rec_ref, ri_ref, cnt_ref = rest[:3]
    else:
        x1_ref, h2_ref, ri_ref, rf_ref, cnt_ref = rest[:5]
    first_step = (pl.program_id(0) == 0) & (pl.program_id(1) == 0)

    @pl.when(first_step)
    def _():
        carry_scr[...] = jnp.zeros_like(carry_scr)

    x = x_ref[0]
    tm, D = x.shape
    gate_a = mod_ref[0, 2:3, :]
    shift_m, scale_m = mod_ref[0, 3:4, :], mod_ref[0, 4:5, :]
    att_n = _rms(att_ref[0].astype(F32), goa_ref[...]).astype(BF16)
    mix = _dot(att_n, woa_ref[...]) + _dot(gm_ref[0], wog_ref[...])
    x1 = x + gate_a * mix
    h2 = _rms(x1, gffn_ref[...]) * (1.0 + scale_m) + shift_m
    if pack_rows:
        rec_ref[0, :, 0:D] = x1
        rec_ref[0, :, D:D + D // 2] = _pack_bf16_pairs(h2)
    else:
        x1_ref[0] = x1
        h2_ref[0] = h2.astype(BF16)

    h_hi, h_lo = _split_bf16(h2)
    la = _dot(h_hi, wra_ref[...])
    logits = la[:, :ROUTER_COLS] + la[:, ROUTER_COLS:] + _dot(h_lo, wrb_ref[...]) + br_ref[...]
    lt = logits.T

    g = [lt[r:r + 1] for r in range(N_GROUPS)]
    gmax = jnp.maximum(jnp.maximum(g[0], g[1]), jnp.maximum(g[2], g[3]))
    gsum = sum(jnp.exp(gr - gmax) for gr in g)
    g_prob = 1.0 / gsum
    g_idx = jnp.where(g[0] == gmax, 0.0, jnp.where(g[1] == gmax, 1.0, jnp.where(g[2] == gmax, 2.0, 3.0)))

    e0 = ROUTER_EXPERT_LANE0
    grp = [lt[e0 + EXPERTS_PER_GROUP * r:e0 + EXPERTS_PER_GROUP * (r + 1)] for r in range(N_GROUPS)]
    sel = jnp.where(g_idx == 0.0, grp[0], jnp.where(g_idx == 1.0, grp[1], jnp.where(g_idx == 2.0, grp[2], grp[3])))
    sub = lax.broadcasted_iota(I32, (EXPERTS_PER_GROUP, tm), 0).astype(F32)
    m1 = jnp.max(sel, axis=0, keepdims=True)
    i1 = jnp.min(jnp.where(sel == m1, sub, float(EXPERTS_PER_GROUP)), axis=0, keepdims=True)
    sel2 = jnp.where(sub == i1, -jnp.inf, sel)
    m2 = jnp.max(sel2, axis=0, keepdims=True)
    i2 = jnp.min(jnp.where(sel2 == m2, sub, float(EXPERTS_PER_GROUP)), axis=0, keepdims=True)
    d = jnp.exp(m2 - m1)
    w1 = g_prob / (1.0 + d)
    w2 = g_prob * d / (1.0 + d)
    first_lower = i1 < i2
    lo = jnp.minimum(i1, i2)
    hi = jnp.maximum(i1, i2)
    w_lo = jnp.where(first_lower, w1, w2)
    w_hi = jnp.where(first_lower, w2, w1)
    pair = lo * EXPERTS_PER_GROUP - lo * (lo + 1.0) * 0.5 + hi - lo - 1.0
    cls = g_idx * PAIRS_PER_GROUP + pair

    crow = lax.broadcasted_iota(I32, (CLASS_ROWS, tm), 0).astype(F32)
    onehot = jnp.where(crow == cls, 1.0, 0.0)
    ta = lax.broadcasted_iota(I32, (tm, tm), 0)
    tb = lax.broadcasted_iota(I32, (tm, tm), 1)
    earlier = jnp.where(ta < tb, 1.0, 0.0).astype(BF16)
    before = _dot(onehot.astype(BF16), earlier)
    carry = carry_scr[...]
    rank = jnp.sum(onehot * (before + carry[:, 0:1]), axis=0, keepdims=True)
    carry = carry + jnp.sum(onehot, axis=1, keepdims=True)
    carry_scr[...] = carry
    cnt_ref[...] = carry

    ri_ref[...] = jnp.zeros_like(ri_ref)
    ri_ref[0, 0:1, :] = cls.astype(I32)
    ri_ref[0, 1:2, :] = rank.astype(I32)
    if pack_rows:
        mrow = lax.broadcasted_iota(I32, (LANES, tm), 0)
        batch = pl.program_id(0).astype(F32)
        meta_t = jnp.where(mrow == 0, w_lo, jnp.where(mrow == 1, w_hi, jnp.where(mrow == 2, batch, 0.0)))
        rec_ref[0, :, D + D // 2:] = meta_t.T
    else:
        rf_ref[...] = jnp.zeros_like(rf_ref)
        rf_ref[0, 0:1, :] = w_lo
        rf_ref[0, 1:2, :] = w_hi


def _out_route(x, att, gm, mod, lw, *, tm, pack_rows):
    B, S, D = x.shape
    nt = S // tm
    row = lambda b, i: (b, i, 0)
    tile = lambda b, i: (b * nt + i, 0, 0)
    route_i = (jax.ShapeDtypeStruct((B * nt, SUBLANES, tm), I32), pl.BlockSpec((1, SUBLANES, tm), tile))
    route_f = (jax.ShapeDtypeStruct((B * nt, SUBLANES, tm), F32), pl.BlockSpec((1, SUBLANES, tm), tile))
    counts = (jax.ShapeDtypeStruct((CLASS_ROWS, LANES), F32), _full((CLASS_ROWS, LANES)))
    if pack_rows:
        rec_w = D + D // 2 + LANES
        outs = [(jax.ShapeDtypeStruct((B, S, rec_w), F32), pl.BlockSpec((1, tm, rec_w), row)), route_i, counts]
    else:
        outs = [(jax.ShapeDtypeStruct((B, S, D), F32), pl.BlockSpec((1, tm, D), row)),
                (jax.ShapeDtypeStruct((B, S, D), BF16), pl.BlockSpec((1, tm, D), row)), route_i, route_f, counts]
    out_shape = [o[0] for o in outs]
    out_specs = [o[1] for o in outs]
    return pl.pallas_call(
        functools.partial(_out_route_kernel, pack_rows=pack_rows),
        grid=(B, nt),
        in_specs=[pl.BlockSpec((1, tm, D), row), pl.BlockSpec((1, tm, ATT_WIDTH), row), pl.BlockSpec((1, tm, GM_WIDTH), row),
                  pl.BlockSpec((1, 6, D), lambda b, i: (b, 0, 0)),
                  _full((1, ATT_WIDTH)), _full((ATT_WIDTH, D)), _full((GM_WIDTH, D)), _full((1, D)),
                  _full((D, 2 * ROUTER_COLS)), _full((D, ROUTER_COLS)), _full((1, ROUTER_COLS))],
        out_specs=out_specs,
        out_shape=out_shape,
        scratch_shapes=[pltpu.VMEM((CLASS_ROWS, LANES), F32)],
        compiler_params=_params(2),
    )(x, att, gm, mod, lw["g_out_attn"], lw["w_out_a"], lw["w_out_g"], lw["g_norm_ffn"],
      lw["w_r_a"], lw["w_r_b"], lw["b_r"])


def _swiglu(hb, wgu, wd):
    hid = _dot(hb, wgu)
    act = jax.nn.silu(hid[:, :D_EXPERT]) * hid[:, D_EXPERT:]
    return _dot(act.astype(BF16), wd)


def _moe_pairs_kernel(elo_ref, ehi_ref, nv_ref, dest_ref, rec_hbm, zeros_hbm, gate_ref,
                      wgu_lo_ref, wd_lo_ref, wgu_hi_ref, wd_hi_ref, y_hbm,
                      xbuf0, xbuf1, ybuf0, ybuf1, tok_ref, gsem, ssem, zsem):
    i = pl.program_id(0)
    xbuf, ybuf = (xbuf0, xbuf1), (ybuf0, ybuf1)
    D = ybuf0.shape[-1]
    n_groups_all = MOE_ROWS // SUBLANES

    def gather_start(t, s, g, u):
        pltpu.make_async_copy(rec_hbm.at[pl.ds(t, 1)], xbuf[s].at[g, pl.ds(u, 1)], gsem.at[s]).start()

    def scatter_start(t, s, g, u):
        pltpu.make_async_copy(ybuf[s].at[g, pl.ds(u, 1)], y_hbm.at[pl.ds(t, 1)], ssem.at[s]).start()

    def some_rows(blk, groups, fn):
        base = blk * MOE_ROWS
        for g in groups:
            for u in range(SUBLANES):
                fn(tok_ref[base + g * SUBLANES + u], g, u)

    def all_rows(blk, fn):
        some_rows(blk, range(n_groups_all), fn)

    def valid_rows(blk, fn):
        nv = nv_ref[blk]
        base = blk * MOE_ROWS
        n_groups = nv // SUBLANES

        def group(g, c):
            for u in range(SUBLANES):
                fn(tok_ref[base + g * SUBLANES + u], g, u)
            return c

        def single(r, c):
            fn(tok_ref[base + r], n_groups, r - n_groups * SUBLANES)
            return c

        lax.fori_loop(0, n_groups, group, 0)
        lax.fori_loop(n_groups * SUBLANES, nv, single, 0)

    def gather_wait(s):
        pltpu.make_async_copy(xbuf[s], xbuf[s], gsem.at[s]).wait()

    def scatter_wait(blk, s):
        nv = nv_ref[blk]
        n_groups = nv // SUBLANES
        buf = ybuf[s]

        @pl.when(n_groups > 0)
        def _():
            rows = buf.at[pl.ds(0, n_groups)]
            pltpu.make_async_copy(rows, rows, ssem.at[s]).wait()

        def single(r, c):
            row = buf.at[0, pl.ds(0, 1)]
            pltpu.make_async_copy(row, row, ssem.at[s]).wait()
            return c

        lax.fori_loop(n_groups * SUBLANES, nv, single, 0)

    def experts(s, start_copies):
        start_copies(0)
        hb = _unpack_bf16_pairs(xbuf[s][:, :, D:D + D // 2].reshape(MOE_ROWS, D // 2))
        meta = xbuf[s][:, :, D + D // 2:].reshape(MOE_ROWS, LANES)
        w_lo, w_hi, bidx = meta[:, 0:1], meta[:, 1:2], meta[:, 2:3]
        blane = lax.broadcasted_iota(I32, (1, 2 * MOD_BATCH_PAD), 1)
        blane = jnp.where(blane >= MOD_BATCH_PAD, blane - MOD_BATCH_PAD, blane).astype(F32)
        onehot = jnp.where(bidx == blane, 1.0, 0.0).astype(BF16)
        gate_m = _dot(onehot, gate_ref[...])
        start_copies(1)
        hid = _dot(hb, wgu_lo_ref[0])
        act_lo = (jax.nn.silu(hid[:, :D_EXPERT]) * hid[:, D_EXPERT:]).astype(BF16)
        start_copies(2)
        moe = w_lo * _dot(act_lo, wd_lo_ref[0])
        start_copies(3)
        hid = _dot(hb, wgu_hi_ref[0])
        act_hi = (jax.nn.silu(hid[:, :D_EXPERT]) * hid[:, D_EXPERT:]).astype(BF16)
        moe = moe + w_hi * _dot(act_hi, wd_hi_ref[0])
        x = xbuf[s][:, :, 0:D].reshape(MOE_ROWS, D)
        ybuf[s][...] = (x + gate_m * moe).reshape(n_groups_all, SUBLANES, D)

    @pl.when(i == 0)
    def _():
        zero = pltpu.make_async_copy(zeros_hbm, tok_ref, zsem)
        zero.start()
        zero.wait()

        def invert(g, c):
            for u in range(SUBLANES):
                t = g * SUBLANES + u
                tok_ref[dest_ref[t]] = t
            return c
        lax.fori_loop(0, dest_ref.shape[0] // SUBLANES, invert, 0)
        all_rows(0, lambda t, g, u: gather_start(t, 0, g, u))

    prev = jnp.maximum(i - 1, 0)
    has_rows = nv_ref[i] > 0
    prev_full = (i >= 1) & (nv_ref[prev] == MOE_ROWS)
    prev_partial = (i >= 1) & (nv_ref[prev] > 0) & (nv_ref[prev] < MOE_ROWS)

    def step(slot):
        other = 1 - slot

        @pl.when(has_rows)
        def _():
            gather_wait(slot)

            @pl.when(prev_partial)
            def _():
                valid_rows(i - 1, lambda t, g, u: scatter_start(t, other, g, u))

            @pl.when(i >= 2)
            def _():
                scatter_wait(i - 2, slot)

            def start_copies(k):
                groups = range(COPY_BURST_GROUPS[k], COPY_BURST_GROUPS[k + 1])

                @pl.when(prev_full)
                def _():
                    some_rows(i - 1, groups, lambda t, g, u: scatter_start(t, other, g, u))

                @pl.when(has_rows)
                def _():
                    some_rows(i + 1, groups, lambda t, g, u: gather_start(t, other, g, u))

            experts(slot, start_copies)

        @pl.when(jnp.logical_not(has_rows) & (i >= 1) & (nv_ref[prev] > 0))
        def _():
            gather_wait(slot)
            valid_rows(i - 1, lambda t, g, u: scatter_start(t, other, g, u))

            @pl.when(i >= 2)
            def _():
                scatter_wait(i - 2, slot)
            scatter_wait(i - 1, other)

    for parity in (0, 1):
        pl.when(i % 2 == parity)(functools.partial(step, parity))


def _moe_pairs(rec, blk_elo, blk_ehi, blk_nv, dest, gate_tab, lw):
    T, rec_w = rec.shape
    assert T % SUBLANES == 0
    D = gate_tab.shape[1]
    nb = blk_nv.shape[0]
    wgu, wd = lw["w_gu_e"], lw["w_d_e"]
    grid_spec = pltpu.PrefetchScalarGridSpec(
        num_scalar_prefetch=4,
        grid=(nb,),
        in_specs=[
            pl.BlockSpec(memory_space=pl.ANY),
            pl.BlockSpec(memory_space=pl.ANY),
            pl.BlockSpec(gate_tab.shape, lambda i, *_: (0, 0)),
            pl.BlockSpec((1, D, 2 * D_EXPERT), lambda i, elo, ehi, nv, tok: (elo[i], 0, 0)),
            pl.BlockSpec((1, D_EXPERT, D), lambda i, elo, ehi, nv, tok: (elo[i], 0, 0)),
            pl.BlockSpec((1, D, 2 * D_EXPERT), lambda i, elo, ehi, nv, tok: (ehi[i], 0, 0)),
            pl.BlockSpec((1, D_EXPERT, D), lambda i, elo, ehi, nv, tok: (ehi[i], 0, 0)),
        ],
        out_specs=pl.BlockSpec(memory_space=pl.ANY),
        scratch_shapes=[pltpu.VMEM((MOE_ROWS // SUBLANES, SUBLANES, rec_w), F32),
                        pltpu.VMEM((MOE_ROWS // SUBLANES, SUBLANES, rec_w), F32),
                        pltpu.VMEM((MOE_ROWS // SUBLANES, SUBLANES, D), F32),
                        pltpu.VMEM((MOE_ROWS // SUBLANES, SUBLANES, D), F32),
                        pltpu.SMEM((nb * MOE_ROWS,), I32),
                        pltpu.SemaphoreType.DMA((2,)), pltpu.SemaphoreType.DMA((2,)), pltpu.SemaphoreType.DMA(())],
    )
    return pl.pallas_call(
        _moe_pairs_kernel,
        grid_spec=grid_spec,
        out_shape=jax.ShapeDtypeStruct((T, D), F32),
        compiler_params=_params(1),
    )(blk_elo, blk_ehi, blk_nv, dest, rec, jnp.zeros((nb * MOE_ROWS,), I32), gate_tab, wgu, wd, wgu, wd)


def _small_lookup(table, idx):
    ids = jnp.arange(table.shape[0], dtype=I32)
    return jnp.sum(jnp.where(idx[:, None] == ids[None, :], table[None, :], 0), axis=1)


def _pair_tables():
    lo, hi = [], []
    for g in range(N_GROUPS):
        for a in range(EXPERTS_PER_GROUP):
            for b in range(a + 1, EXPERTS_PER_GROUP):
                lo.append(g * EXPERTS_PER_GROUP + a)
                hi.append(g * EXPERTS_PER_GROUP + b)
    return np.asarray(lo, np.int32), np.asarray(hi, np.int32)


def _moe_prompt(rec, cls, rank, counts, mod, lw):
    B, S, rec_w = rec.shape
    D = mod.shape[-1]
    T = B * S
    nb = T // MOE_ROWS + N_CLASSES + 1
    nblk = (counts + MOE_ROWS - 1) // MOE_ROWS
    blk_end = jnp.cumsum(nblk)
    blk_start = blk_end - nblk
    dest = _small_lookup(blk_start, cls) * MOE_ROWS + rank
    ids = jnp.arange(nb, dtype=I32)
    used = blk_end[-1]
    class_of = lambda blk: jnp.sum((blk_end[None, :] <= blk[:, None]).astype(I32), axis=1)
    blk_cls = jnp.where(ids < used, jnp.minimum(class_of(ids), N_CLASSES - 1), class_of(used[None] - 1))
    first_row = (ids - _small_lookup(blk_start, blk_cls)) * MOE_ROWS
    blk_nv = jnp.where(ids < used, jnp.clip(_small_lookup(counts, blk_cls) - first_row, 0, MOE_ROWS), 0)
    lo_tab, hi_tab = _pair_tables()
    blk_elo = _small_lookup(jnp.asarray(lo_tab), blk_cls)
    blk_ehi = _small_lookup(jnp.asarray(hi_tab), blk_cls)
    gate_hi, gate_lo = _split_bf16(jnp.pad(mod[:, 5, :], ((0, MOD_BATCH_PAD - B), (0, 0))))
    gate_tab = jnp.concatenate([gate_hi, gate_lo], axis=0)
    y = _moe_pairs(rec.reshape(T, rec_w), blk_elo, blk_ehi, blk_nv.astype(I32), dest.astype(I32), gate_tab, lw)
    return y.reshape(B, S, D)


def _moe_dense_kernel(h_ref, x1_ref, gate_ref, w_ref, sel_ref, wgu_ref, wd_ref, y_ref):
    e = pl.program_id(0)

    @pl.when(e == 0)
    def _():
        y_ref[...] = jnp.zeros_like(y_ref)

    ye = _swiglu(h_ref[...], wgu_ref[0], wd_ref[0])
    y_ref[...] += jnp.where(sel_ref[0] > 0.5, w_ref[0] * ye, 0.0)

    @pl.when(e == pl.num_programs(0) - 1)
    def _():
        y_ref[...] = x1_ref[...] + gate_ref[...] * y_ref[...]


def _moe_dense(h2, x1, gate_rows, w_sel, sel, lw):
    T, D = x1.shape
    per_e = lambda e: (e, 0, 0)
    return pl.pallas_call(
        _moe_dense_kernel,
        grid=(N_EXPERTS,),
        in_specs=[_full((T, D)), _full((T, D)), _full((T, D)),
                  pl.BlockSpec((1, T, 1), per_e), pl.BlockSpec((1, T, 1), per_e),
                  pl.BlockSpec((1, D, 2 * D_EXPERT), per_e), pl.BlockSpec((1, D_EXPERT, D), per_e)],
        out_specs=_full((T, D)),
        out_shape=jax.ShapeDtypeStruct((T, D), F32),
        compiler_params=_params(1),
    )(h2, x1, gate_rows, w_sel, sel, lw["w_gu_e"], lw["w_d_e"])


def _rope_tables(pos, gain, scale):
    inv = 1.0 / (ROPE_BASE ** (jnp.arange(ROPE_HALF, dtype=F32) / ROPE_HALF))
    ang = pos.astype(F32)[:, None] * inv[None, :]
    cos, sin = jnp.cos(ang) * scale, jnp.sin(ang) * scale
    n = pos.shape[0]
    z = lambda w: jnp.zeros((n, w), F32)
    pad = HEAD_PAD - QK_DIM
    g_nope, g1, g2 = gain[:NOPE_DIM], gain[NOPE_DIM:NOPE_DIM + ROPE_HALF], gain[NOPE_DIM + ROPE_HALF:]
    tab_c = jnp.concatenate([jnp.broadcast_to(g_nope * scale, (n, NOPE_DIM)), cos * g1, cos * g2, z(pad)], axis=1)
    tab_s = jnp.concatenate([z(NOPE_DIM), -sin * g2, sin * g1, z(pad)], axis=1)
    return tab_c, tab_s


def _prep_layer(w, l, chunk_lens):
    D = w["w_in"].shape[1]
    H = MLA_HEADS
    lw = {}
    row = lambda name: w[name][l].reshape(1, -1)
    for name in ("g_norm_mix", "g_q_lat", "g_kv_lat", "g_ln_v", "b_ln_v", "g_out_attn", "g_out_gmlp", "g_norm_ffn"):
        lw[name] = row(name)
    w_in = w["w_in"][l]
    o1, o2, o3 = Q_LORA, Q_LORA + KV_LORA, Q_LORA + KV_LORA + ROPE_DIM
    o4 = o3 + GM_WIDTH
    z32 = jnp.zeros((D, ROPE_DIM), F32)
    wr = w_in[:, o2:o3]
    lw["w_in_p"] = jnp.concatenate([w_in[:, :o2], wr, z32, wr, z32, w_in[:, o3:o4], w_in[:, o4:]], axis=1).astype(BF16)
    assert lw["w_in_p"].shape[1] == PROJ_COLS
    pad = HEAD_PAD - QK_DIM
    w_uq = w["w_uq"][l].reshape(Q_LORA, H, QK_DIM)
    w_uq_swapped = jnp.concatenate([jnp.zeros((Q_LORA, H, NOPE_DIM), F32), w_uq[:, :, NOPE_DIM + ROPE_HALF:],
                                    w_uq[:, :, NOPE_DIM:NOPE_DIM + ROPE_HALF]], axis=2)
    head_pad = lambda a: jnp.pad(a, ((0, 0), (0, 0), (0, pad))).reshape(Q_LORA, H * HEAD_PAD)
    lw["w_uq_p"] = jnp.concatenate([head_pad(w_uq), head_pad(w_uq_swapped)], axis=1).astype(BF16)
    w_ukv = w["w_ukv"][l].reshape(KV_LORA, H, NOPE_DIM + V_DIM)
    lw["w_k_p"] = jnp.pad(w_ukv[:, :, :NOPE_DIM], ((0, 0), (0, 0), (0, HEAD_PAD - NOPE_DIM))).reshape(KV_LORA, H * HEAD_PAD).astype(BF16)
    w_v = w_ukv[:, :, NOPE_DIM:]
    zeros_v = jnp.zeros_like(w_v)
    odd_head = (jnp.arange(H) % 2 == 1)[None, :, None]
    w_v_lo, w_v_hi = jnp.where(odd_head, zeros_v, w_v), jnp.where(odd_head, w_v, zeros_v)
    lw["w_v_p"] = jnp.concatenate([w_v_lo, w_v_hi], axis=2).reshape(KV_LORA, H * HEAD_PAD).astype(BF16)
    lw["scores_bounded"] = _scores_bounded(w, l)
    lw["g_qnorm"] = w["g_qnorm"][l]
    lw["g_knorm"] = w["g_knorm"][l]
    for L in chunk_lens:
        wsp = w["w_spatial"][l][:, :L, :L]
        lw["w_sp_pairs_%d" % L] = wsp.reshape(GM_HEADS // 2, 2 * L, L).astype(BF16)
        lw["b_sp_rows_%d" % L] = jnp.repeat(jnp.transpose(w["b_spatial"][l][:, :L]), GM_HEAD_DIM, axis=1)
    w_out = w["w_out"][l].astype(BF16)
    lw["w_out_a"], lw["w_out_g"] = w_out[:ATT_WIDTH], w_out[ATT_WIDTH:]
    wr_full = jnp.zeros((D, ROUTER_COLS), F32)
    wr_full = wr_full.at[:, :N_GROUPS].set(w["w_router_group"][l])
    wr_full = wr_full.at[:, ROUTER_EXPERT_LANE0:ROUTER_EXPERT_LANE0 + N_EXPERTS].set(w["w_router_expert"][l])
    r_hi, r_lo = _split_bf16(wr_full)
    lw["w_r_a"] = jnp.concatenate([r_hi, r_lo], axis=1)
    lw["w_r_b"] = r_hi
    br = jnp.zeros((1, ROUTER_COLS), F32)
    br = br.at[0, :N_GROUPS].set(w["b_router_group"][l])
    lw["b_r"] = br.at[0, ROUTER_EXPERT_LANE0:ROUTER_EXPERT_LANE0 + N_EXPERTS].set(w["b_router_expert"][l])
    lw["w_gu_e"] = jnp.concatenate([w["w_gate_e"][l], w["w_up_e"][l]], axis=-1).astype(BF16)
    lw["w_d_e"] = w["w_down_e"][l].astype(BF16)
    return lw


def _tiles(seq):
    return min(seq, 512), min(seq, 2048), min(seq, 512)


def _layer_prompt(x, mod, lw):
    B, S, D = x.shape
    tm, tq, tk = _tiles(S)
    pos = jnp.arange(S)
    q_tabs = _rope_tables(pos, lw["g_qnorm"], QK_DIM ** -0.5 * LOG2E)
    k_tabs = _rope_tables(pos, lw["g_knorm"], 1.0)
    ckv, krope, q, gm, k, v = _mix_in(x, mod, lw, q_tabs, k_tabs, tm=tm, chunk_len=GM_CHUNK, emit_kv=True, emit_vrows=False)
    att = _attention(lw["scores_bounded"], q, k, v, tq=tq, tk=tk, q_off=0, kv_valid=S)
    rec, ri, cnt = _out_route(x, att, gm, mod, lw, tm=tm, pack_rows=True)
    cls = ri[:, 0, :].reshape(B * S)
    rank = ri[:, 1, :].reshape(B * S)
    counts = cnt[:N_CLASSES, 0].astype(I32)
    y = _moe_prompt(rec, cls, rank, counts, mod, lw)
    return y, ckv, krope


def _layer_sample(x, mod, past_ckv, past_krope, lw):
    B, S, D = x.shape
    past = past_ckv.shape[1]
    q_tabs = _rope_tables(past + jnp.arange(S), lw["g_qnorm"], QK_DIM ** -0.5 * LOG2E)
    ckv, krope, q, gm, v_rows = _mix_in(x, mod, lw, q_tabs, q_tabs, tm=S, chunk_len=S, emit_kv=False, emit_vrows=True)
    kv_valid = past + S
    kv_pad = -(-kv_valid // LANES) * LANES
    extra = kv_pad - kv_valid
    ckv_all = jnp.concatenate([past_ckv, ckv, jnp.zeros((B, extra, KV_LORA), F32)], axis=1)
    kr_all = jnp.concatenate([past_krope, krope, jnp.zeros((B, extra, ROPE_DIM), F32)], axis=1)
    kr_slot = jnp.pad(kr_all, ((0, 0), (0, 0), (NOPE_DIM, HEAD_PAD - QK_DIM)))
    k_tabs = _rope_tables(jnp.arange(kv_pad), lw["g_knorm"], 1.0)
    k, v = _kv_latent(ckv_all, kr_slot, lw, k_tabs, tr=kv_pad)
    att = _attention(lw["scores_bounded"], q, k, v, tq=S, tk=kv_pad, q_off=past, kv_valid=kv_valid)
    x1, h2, ri, rf, _ = _out_route(x, att, gm, mod, lw, tm=S, pack_rows=False)
    T = B * S
    cls, w_lo, w_hi = ri[:, 0, :].reshape(T), rf[:, 0, :].reshape(T), rf[:, 1, :].reshape(T)
    lo_tab, hi_tab = _pair_tables()
    e_lo, e_hi = jnp.asarray(lo_tab)[cls], jnp.asarray(hi_tab)[cls]
    eids = jnp.arange(N_EXPERTS, dtype=I32)[:, None]
    is_lo, is_hi = eids == e_lo[None, :], eids == e_hi[None, :]
    w_sel = (jnp.where(is_lo, w_lo[None, :], 0.0) + jnp.where(is_hi, w_hi[None, :], 0.0))[:, :, None]
    sel = (is_lo | is_hi).astype(F32)[:, :, None]
    gate_rows = jnp.repeat(mod[:, 5, :], S, axis=0)
    y = _moe_dense(h2.reshape(T, D), x1.reshape(T, D), gate_rows, w_sel, sel, lw)
    return y.reshape(B, S, D), ckv, krope, v_rows


def kernel(x_prompt, x_sample, cache_ckv, cache_krope, c_prompt, c_sample, w_ada, b_ada, g_norm_mix, w_in, g_q_lat, w_uq, g_kv_lat, w_ukv, g_qnorm, g_knorm, g_ln_v, b_ln_v, w_spatial, b_spatial, g_out_attn, g_out_gmlp, w_out, g_norm_ffn, w_router_group, b_router_group, w_router_expert, b_router_expert, w_gate_e, w_up_e, w_down_e):
    w = dict(w_in=w_in, g_norm_mix=g_norm_mix, g_q_lat=g_q_lat, w_uq=w_uq, g_kv_lat=g_kv_lat, w_ukv=w_ukv,
             g_qnorm=g_qnorm, g_knorm=g_knorm, g_ln_v=g_ln_v, b_ln_v=b_ln_v, w_spatial=w_spatial, b_spatial=b_spatial,
             g_out_attn=g_out_attn, g_out_gmlp=g_out_gmlp, w_out=w_out, g_norm_ffn=g_norm_ffn,
             w_router_group=w_router_group, b_router_group=b_router_group, w_router_expert=w_router_expert,
             b_router_expert=b_router_expert, w_gate_e=w_gate_e, w_up_e=w_up_e, w_down_e=w_down_e)
    depth = w_ada.shape[0]
    Bp, Sp, D = x_prompt.shape
    Bs, Ss, _ = x_sample.shape
    assert Sp % GM_CHUNK == 0 and Ss <= GM_CHUNK and Ss % CHUNK == 0 and Bp <= MOD_BATCH_PAD
    c_all = jnp.concatenate([c_prompt, c_sample], axis=0)
    y_p, y_s = x_prompt, x_sample
    outs = [[] for _ in range(5)]
    for l in range(depth):
        lw = _prep_layer(w, l, (GM_CHUNK, Ss))
        mod = _ada_mod(c_all, w_ada[l], b_ada[l]).reshape(Bp + Bs, 6, D)
        y_p, ckv_p, kr_p = _layer_prompt(y_p, mod[:Bp], lw)
        y_s, ckv_s, kr_s, v_s = _layer_sample(y_s, mod[Bp:], cache_ckv[l], cache_krope[l], lw)
        for lst, val in zip(outs, (ckv_p, kr_p, ckv_s, kr_s, v_s)):
            lst.append(val)
    return (y_p, y_s) + tuple(jnp.stack(lst) for lst in outs)
```

```python
import functools

import numpy as np
import jax
import jax.numpy as jnp
from jax import lax
from jax.experimental import pallas as pl
from jax.experimental.pallas import tpu as pltpu

F32 = jnp.float32
BF16 = jnp.bfloat16
I32 = jnp.int32

CHUNK = 64
CHUNK_SHIFT = 6
EPS = 1e-6
MLA_HEADS = 8
Q_LORA = 256
KV_LORA = 128
NOPE_DIM = 64
ROPE_DIM = 32
ROPE_HALF = ROPE_DIM // 2
V_DIM = 64
QK_DIM = NOPE_DIM + ROPE_DIM
ATT_WIDTH = MLA_HEADS * V_DIM
ROPE_BASE = 10000.0
GM_HEADS = 8
GM_HEAD_DIM = 64
GM_WIDTH = GM_HEADS * GM_HEAD_DIM
GM_CHUNK = 128
N_GROUPS = 4
EXPERTS_PER_GROUP = 8
N_EXPERTS = N_GROUPS * EXPERTS_PER_GROUP
D_EXPERT = 256
PAIRS_PER_GROUP = EXPERTS_PER_GROUP * (EXPERTS_PER_GROUP - 1) // 2
N_CLASSES = N_GROUPS * PAIRS_PER_GROUP

LANES = 128
SUBLANES = 8
HEAD_PAD = LANES
PROJ_COLS = 1536
ROUTER_COLS = LANES
ROUTER_EXPERT_LANE0 = SUBLANES
CLASS_ROWS = LANES
MOE_ROWS = 128
MOD_BATCH_PAD = 16
DIAG_ROWS = 256
VMEM_LIMIT = 48 * 1024 * 1024
NEG_BIG = -1e30
LOG2E = 1.4426950408889634
BF16_SLACK = 1.02
SCORE_BOUND = 90.0

assert CHUNK == 1 << CHUNK_SHIFT


def _params(n_axes, vmem=VMEM_LIMIT):
    return pltpu.CompilerParams(dimension_semantics=("arbitrary",) * n_axes, vmem_limit_bytes=vmem)


def _full(shape):
    nd = len(shape)
    return pl.BlockSpec(shape, lambda *_: (0,) * nd)


def _split_bf16(x):
    hi = x.astype(BF16)
    lo = (x - hi.astype(F32)).astype(BF16)
    return hi, lo


def _dot(a, b):
    return jnp.dot(a, b, preferred_element_type=F32)


def _ada_kernel(c_ref, w_ref, b_ref, o_ref):
    a_hi, a_lo = _split_bf16(jax.nn.silu(c_ref[...]))
    w_hi, w_lo = _split_bf16(w_ref[...])
    o_ref[...] = _dot(a_hi, w_hi) + _dot(a_lo, w_hi) + _dot(a_hi, w_lo) + b_ref[...]


def _ada_mod(c, w_ada, b_ada):
    n, d = c.shape
    cols = w_ada.shape[1]
    tn = 1536
    return pl.pallas_call(
        _ada_kernel,
        grid=(cols // tn,),
        in_specs=[_full((n, d)), pl.BlockSpec((d, tn), lambda j: (0, j)), pl.BlockSpec((1, tn), lambda j: (0, j))],
        out_specs=pl.BlockSpec((n, tn), lambda j: (0, j)),
        out_shape=jax.ShapeDtypeStruct((n, cols), F32),
        compiler_params=_params(1),
    )(c, w_ada, b_ada.reshape(1, cols))


def _rms(x, g):
    return x * lax.rsqrt(jnp.mean(x * x, axis=-1, keepdims=True) + EPS) * g


def _head_norm_rope(xh, xh_swapped, tab_c, tab_s):
    ms = jnp.sum(xh * xh, axis=-1, keepdims=True) * (1.0 / QK_DIM)
    return (xh * tab_c + xh_swapped * tab_s) * lax.rsqrt(ms + EPS)


def _keys_values(ckv, kr_slot, gkv, wk, wv, tab_c, tab_s, k_ref, v_ref):
    cb = _rms(ckv, gkv).astype(BF16)
    kall = _dot(cb, wk)
    vall = _dot(cb, wv)
    lane = lax.broadcasted_iota(I32, (1, LANES), 1)
    ones_hi = jnp.where(lane >= V_DIM, 1.0, 0.0)
    ones_lo = 1.0 - ones_hi
    kr_swapped = jnp.where(lane < NOPE_DIM + ROPE_HALF, pltpu.roll(kr_slot, HEAD_PAD - ROPE_HALF, 1),
                           pltpu.roll(kr_slot, ROPE_HALF, 1))
    kr_swapped = jnp.where((lane >= NOPE_DIM) & (lane < QK_DIM), kr_swapped, 0.0)
    for h in range(MLA_HEADS):
        kh = kall[:, h * HEAD_PAD:(h + 1) * HEAD_PAD] + kr_slot
        k_ref[0, h] = _head_norm_rope(kh, kr_swapped, tab_c, tab_s).astype(BF16)
        ones = ones_hi if h % 2 == 0 else ones_lo
        v_ref[0, h] = (vall[:, h * HEAD_PAD:(h + 1) * HEAD_PAD] + ones).astype(BF16)


def _mix_in_kernel(x_ref, mod_ref, gmix_ref, win_ref, gql_ref, wuq_ref, gkv_ref, wk_ref, wv_ref,
                   cq_ref, sq_ref, ck_ref, sk_ref, glnv_ref, blnv_ref, wsp_ref, bsp_ref,
                   ggm_ref, *rest, chunk_len, emit_kv, emit_vrows):
    outs = list(rest[:-1])
    mixed_scr = rest[-1]
    ckv_ref, kr_ref, q_ref, gm_ref = outs[:4]
    outs = outs[4:]
    if emit_kv:
        k_ref, v_ref = outs[:2]
        outs = outs[2:]
    if emit_vrows:
        vrows_ref = outs[0]

    x = x_ref[0]
    tm = x.shape[0]
    shift, scale = mod_ref[0, 0:1, :], mod_ref[0, 1:2, :]
    h = _rms(x, gmix_ref[...]) * (1.0 + scale) + shift
    proj = _dot(h.astype(BF16), win_ref[...])

    q_lat = proj[:, 0:Q_LORA]
    ckv = proj[:, Q_LORA:Q_LORA + KV_LORA]
    kr_blk = proj[:, Q_LORA + KV_LORA:Q_LORA + KV_LORA + LANES]
    ckv_ref[0] = ckv
    kr_ref[0] = kr_blk[:, 0:ROPE_DIM]

    q = _dot(_rms(q_lat, gql_ref[...]).astype(BF16), wuq_ref[...])
    cq, sq = cq_ref[...], sq_ref[...]
    n_q = MLA_HEADS * HEAD_PAD
    for hd in range(MLA_HEADS):
        qh = q[:, hd * HEAD_PAD:(hd + 1) * HEAD_PAD]
        qh_swapped = q[:, n_q + hd * HEAD_PAD:n_q + (hd + 1) * HEAD_PAD]
        q_ref[0, hd] = _head_norm_rope(qh, qh_swapped, cq, sq).astype(BF16)

    if emit_kv:
        lane = lax.broadcasted_iota(I32, (1, LANES), 1)
        kr_slot = jnp.where(lane >= NOPE_DIM, kr_blk, 0.0)
        _keys_values(ckv, kr_slot, gkv_ref[...], wk_ref[...], wv_ref[...], ck_ref[...], sk_ref[...], k_ref, v_ref)

    g_u = proj[:, 512:512 + GM_WIDTH]
    g_v = proj[:, 1024:1024 + GM_WIDTH]
    u = jax.nn.gelu(g_u)
    gv = jax.nn.gelu(g_v)
    mu = jnp.mean(gv, axis=-1, keepdims=True)
    xc = gv - mu
    var = jnp.mean(xc * xc, axis=-1, keepdims=True)
    v_rows = xc * lax.rsqrt(var + EPS) * glnv_ref[...] + blnv_ref[...]
    if emit_vrows:
        vrows_ref[0] = v_rows
    vb = v_rows.astype(BF16)

    L = chunk_len
    t = lax.broadcasted_iota(I32, (2 * L, L), 0)
    s = lax.broadcasted_iota(I32, (2 * L, L), 1)
    t = jnp.where(t >= L, t - L, t)
    allowed = (s >> CHUNK_SHIFT) <= (t >> CHUNK_SHIFT)
    lane = lax.broadcasted_iota(I32, (1, LANES), 1)
    first_head = lane < GM_HEAD_DIM
    for p in range(GM_HEADS // 2):
        w_pair = jnp.where(allowed, wsp_ref[p], jnp.zeros((), BF16))
        for c in range(tm // L):
            vp = vb[c * L:(c + 1) * L, p * LANES:(p + 1) * LANES]
            r = _dot(w_pair, vp)
            mixed = jnp.where(first_head, r[:L], r[L:])
            mixed_scr[c * L:(c + 1) * L, p * LANES:(p + 1) * LANES] = mixed + bsp_ref[:, p * LANES:(p + 1) * LANES]
    gm = u * mixed_scr[...]
    gm_ref[0] = _rms(gm, ggm_ref[...]).astype(BF16)


def _mix_in(x, mod, lw, q_tabs, k_tabs, *, tm, chunk_len, emit_kv, emit_vrows):
    B, S, D = x.shape
    nt = S // tm
    H = MLA_HEADS
    row = lambda b, i: (b, i, 0)
    tab = pl.BlockSpec((tm, LANES), lambda b, i: (i, 0))
    in_specs = [
        pl.BlockSpec((1, tm, D), row),
        pl.BlockSpec((1, 6, D), lambda b, i: (b, 0, 0)),
        _full((1, D)), _full((D, PROJ_COLS)), _full((1, Q_LORA)), _full((Q_LORA, 2 * H * HEAD_PAD)),
        _full((1, KV_LORA)), _full((KV_LORA, H * HEAD_PAD)), _full((KV_LORA, H * HEAD_PAD)),
        tab, tab, tab, tab,
        _full((1, GM_WIDTH)), _full((1, GM_WIDTH)),
        _full((GM_HEADS // 2, 2 * chunk_len, chunk_len)), _full((chunk_len, GM_WIDTH)), _full((1, GM_WIDTH)),
    ]
    out_shape = [
        jax.ShapeDtypeStruct((B, S, KV_LORA), F32),
        jax.ShapeDtypeStruct((B, S, ROPE_DIM), F32),
        jax.ShapeDtypeStruct((B, H, S, HEAD_PAD), BF16),
        jax.ShapeDtypeStruct((B, S, GM_WIDTH), BF16),
    ]
    head_blk = pl.BlockSpec((1, H, tm, HEAD_PAD), lambda b, i: (b, 0, i, 0))
    out_specs = [
        pl.BlockSpec((1, tm, KV_LORA), row),
        pl.BlockSpec((1, tm, ROPE_DIM), row),
        head_blk,
        pl.BlockSpec((1, tm, GM_WIDTH), row),
    ]
    if emit_kv:
        out_shape += [jax.ShapeDtypeStruct((B, H, S, HEAD_PAD), BF16), jax.ShapeDtypeStruct((B, H, S, HEAD_PAD), BF16)]
        out_specs += [head_blk, head_blk]
    if emit_vrows:
        out_shape += [jax.ShapeDtypeStruct((B, S, GM_WIDTH), F32)]
        out_specs += [pl.BlockSpec((1, tm, GM_WIDTH), row)]
    kern = functools.partial(_mix_in_kernel, chunk_len=chunk_len, emit_kv=emit_kv, emit_vrows=emit_vrows)
    return pl.pallas_call(
        kern,
        grid=(B, nt),
        in_specs=in_specs,
        out_specs=out_specs,
        out_shape=out_shape,
        scratch_shapes=[pltpu.VMEM((tm, GM_WIDTH), F32)],
        compiler_params=_params(2),
    )(x, mod, lw["g_norm_mix"], lw["w_in_p"], lw["g_q_lat"], lw["w_uq_p"], lw["g_kv_lat"], lw["w_k_p"], lw["w_v_p"],
      *q_tabs, *k_tabs, lw["g_ln_v"], lw["b_ln_v"],
      lw["w_sp_pairs_%d" % chunk_len], lw["b_sp_rows_%d" % chunk_len], lw["g_out_gmlp"])


def _kv_latent_kernel(ckv_ref, kr_ref, gkv_ref, wk_ref, wv_ref, c_ref, s_ref, k_ref, v_ref):
    _keys_values(ckv_ref[0], kr_ref[0], gkv_ref[...], wk_ref[...], wv_ref[...], c_ref[...], s_ref[...], k_ref, v_ref)


def _kv_latent(ckv_all, kr_slot_all, lw, k_tabs, *, tr):
    B, K, _ = ckv_all.shape
    H = MLA_HEADS
    row = lambda b, i: (b, i, 0)
    tab = pl.BlockSpec((tr, LANES), lambda b, i: (i, 0))
    return pl.pallas_call(
        _kv_latent_kernel,
        grid=(B, K // tr),
        in_specs=[pl.BlockSpec((1, tr, KV_LORA), row), pl.BlockSpec((1, tr, LANES), row),
                  _full((1, KV_LORA)), _full((KV_LORA, H * HEAD_PAD)), _full((KV_LORA, H * HEAD_PAD)),
                  tab, tab],
        out_specs=[pl.BlockSpec((1, H, tr, HEAD_PAD), lambda b, i: (b, 0, i, 0))] * 2,
        out_shape=[jax.ShapeDtypeStruct((B, H, K, HEAD_PAD), BF16)] * 2,
        compiler_params=_params(2),
    )(ckv_all, kr_slot_all, lw["g_kv_lat"], lw["w_k_p"], lw["w_v_p"], *k_tabs)


def _attn_kernel(bounded_ref, q_ref, k_ref, v_ref, o_ref, acc_ref, *, tq, tk, n_q, q_off, kv_valid, split_diag):
    i = pl.program_id(2) if n_q > 1 else 0
    q_first = q_off + i * tq
    vis_first = jnp.minimum(((q_first >> CHUNK_SHIFT) + 1) << CHUNK_SHIFT, kv_valid)
    vis_last = jnp.minimum((((q_first + tq - 1) >> CHUNK_SHIFT) + 1) << CHUNK_SHIFT, kv_valid)
    n_unmasked = vis_first // tk
    n_total = (vis_last + tk - 1) // tk

    q_pos = q_first + lax.broadcasted_iota(I32, (tq, 1), 0)
    limit = jnp.minimum(((q_pos >> CHUNK_SHIFT) + 1) << CHUNK_SHIFT, kv_valid)
    lane = lax.broadcasted_iota(I32, (1, LANES), 1)

    def scores(j, start, masked, r0=0, nr=tq, nk=tk):
        s = lax.dot_general(q_ref[0, j, r0:r0 + nr, :], k_ref[0, j, pl.ds(start, nk), :], (((1,), (1,)), ((), ())),
                            preferred_element_type=F32)
        if masked:
            k_pos = start + lax.broadcasted_iota(I32, (1, nk), 1)
            s = jnp.where(k_pos < limit[r0:r0 + nr], s, NEG_BIG)
        return s

    def plain(start, masked, r0=0, nr=tq, nk=tk, assign=False):
        for j in range(2):
            p = jnp.exp2(scores(j, start, masked, r0, nr, nk)).astype(BF16)
            pv = _dot(p, v_ref[0, j, pl.ds(start, nk), :])
            if assign:
                acc_ref[j, r0:r0 + nr, :] = pv
            else:
                acc_ref[j, r0:r0 + nr, :] += pv

    def online_block(kb, carry, masked):
        start = pl.multiple_of(kb * tk, tk)
        new = []
        for j in range(2):
            m, acc = carry[2 * j:2 * j + 2]
            s = scores(j, start, masked)
            m_new = jnp.maximum(m, jnp.max(s, axis=-1, keepdims=True))
            p = jnp.exp2(s - m_new).astype(BF16)
            acc = jnp.exp2(m - m_new) * acc + _dot(p, v_ref[0, j, pl.ds(start, tk), :])
            new += [m_new, acc]
        return tuple(new)

    def finish(acc0, acc1):
        first = lane < V_DIM
        num = jnp.where(first, acc0, acc1)
        den = pltpu.roll(jnp.where(first, acc1, acc0), V_DIM, 1)
        o_ref[0] = (num / den).astype(BF16)

    @pl.when(bounded_ref[0] == 1)
    def _():
        at = lambda kb: pl.multiple_of(kb * tk, tk)

        def pair(p, c):
            plain(at(2 * p), False)
            plain(at(2 * p + 1), False)
            return c

        def single(kb, c, masked):
            plain(at(kb), masked)
            return c

        if split_diag:
            own = 0 if n_q == 1 else pl.multiple_of(i * tq, tq)
            for r in range(tq // DIAG_ROWS):
                plain(own, True, r * DIAG_ROWS, DIAG_ROWS, (r + 1) * DIAG_ROWS, assign=True)
        else:
            acc_ref[...] = jnp.zeros_like(acc_ref)
        n_pairs = n_unmasked // 2
        lax.fori_loop(0, n_pairs, pair, 0)
        lax.fori_loop(2 * n_pairs, n_unmasked, lambda kb, c: single(kb, c, False), 0)
        if not split_diag:
            lax.fori_loop(n_unmasked, n_total, lambda kb, c: single(kb, c, True), 0)
        finish(acc_ref[0], acc_ref[1])

    @pl.when(bounded_ref[0] != 1)
    def _():
        zeros = jnp.zeros((tq, LANES), F32)
        m0 = jnp.full((tq, 1), NEG_BIG, F32)
        c = lax.fori_loop(0, n_unmasked, lambda kb, c: online_block(kb, c, False), (m0, zeros, m0, zeros))
        c = lax.fori_loop(n_unmasked, n_total, lambda kb, c: online_block(kb, c, True), c)
        finish(c[1], c[3])


def _attention(bounded, q, k, v, *, tq, tk, q_off, kv_valid):
    B, H, Sq, _ = q.shape
    Sk = k.shape[2]
    split_diag = q_off == 0 and kv_valid >= Sq and tq % DIAG_ROWS == 0 and tq % tk == 0
    kern = functools.partial(_attn_kernel, tq=tq, tk=tk, n_q=Sq // tq, q_off=q_off, kv_valid=kv_valid,
                             split_diag=split_diag)
    grid_spec = pltpu.PrefetchScalarGridSpec(
        num_scalar_prefetch=1,
        grid=(B, H // 2, Sq // tq),
        in_specs=[pl.BlockSpec((1, 2, tq, HEAD_PAD), lambda b, hp, i, f: (b, hp, i, 0)),
                  pl.BlockSpec((1, 2, Sk, HEAD_PAD), lambda b, hp, i, f: (b, hp, 0, 0)),
                  pl.BlockSpec((1, 2, Sk, HEAD_PAD), lambda b, hp, i, f: (b, hp, 0, 0))],
        out_specs=pl.BlockSpec((1, tq, LANES), lambda b, hp, i, f: (b, i, hp)),
        scratch_shapes=[pltpu.VMEM((2, tq, LANES), F32)],
    )
    return pl.pallas_call(
        kern,
        grid_spec=grid_spec,
        out_shape=jax.ShapeDtypeStruct((B, Sq, ATT_WIDTH), BF16),
        compiler_params=_params(3),
    )(bounded, q, k, v)


def _scores_bounded(w, l):
    gq = jnp.max(jnp.abs(w["g_qnorm"][l]))
    gk = jnp.max(jnp.abs(w["g_knorm"][l]))
    bound = (QK_DIM ** 0.5) * LOG2E * BF16_SLACK * gq * gk
    return (bound <= SCORE_BOUND).astype(I32).reshape(1)


def _pack_bf16_pairs(h):
    n = h.shape[1] // 2
    hi = pltpu.bitcast(h[:, :n].astype(BF16).astype(F32), jnp.uint32)
    lo = pltpu.bitcast(h[:, n:].astype(BF16).astype(F32), jnp.uint32)
    return pltpu.bitcast(hi | (lo >> 16), F32)


def _unpack_bf16_pairs(words):
    w = pltpu.bitcast(words, jnp.uint32)
    hi = pltpu.bitcast(w & jnp.uint32(0xFFFF0000), F32)
    lo = pltpu.bitcast(w << 16, F32)
    return jnp.concatenate([hi, lo], axis=1).astype(BF16)


def _out_route_kernel(x_ref, att_ref, gm_ref, mod_ref, goa_ref, woa_ref, wog_ref, gffn_ref, wra_ref, wrb_ref, br_ref,
                      *rest, pack_rows):
    carry_scr = rest[-1]
    if pack_rows:
        rec_ref, ri_ref, cnt_ref = rest[:3]
    else:
        x1_ref, h2_ref, ri_ref, rf_ref, cnt_ref = rest[:5]
    first_step = (pl.program_id(0) == 0) & (pl.program_id(1) == 0)

    @pl.when(first_step)
    def _():
        carry_scr[...] = jnp.zeros_like(carry_scr)

    x = x_ref[0]
    tm, D = x.shape
    gate_a = mod_ref[0, 2:3, :]
    shift_m, scale_m = mod_ref[0, 3:4, :], mod_ref[0, 4:5, :]
    att_n = _rms(att_ref[0].astype(F32), goa_ref[...]).astype(BF16)
    mix = _dot(att_n, woa_ref[...]) + _dot(gm_ref[0], wog_ref[...])
    x1 = x + gate_a * mix
    h2 = _rms(x1, gffn_ref[...]) * (1.0 + scale_m) + shift_m
    if pack_rows:
        rec_ref[0, :, 0:D] = x1
        rec_ref[0, :, D:D + D // 2] = _pack_bf16_pairs(h2)
    else:
        x1_ref[0] = x1
        h2_ref[0] = h2.astype(BF16)

    h_hi, h_lo = _split_bf16(h2)
    la = _dot(h_hi, wra_ref[...])
    logits = la[:, :ROUTER_COLS] + la[:, ROUTER_COLS:] + _dot(h_lo, wrb_ref[...]) + br_ref[...]
    lt = logits.T

    g = [lt[r:r + 1] for r in range(N_GROUPS)]
    gmax = jnp.maximum(jnp.maximum(g[0], g[1]), jnp.maximum(g[2], g[3]))
    gsum = sum(jnp.exp(gr - gmax) for gr in g)
    g_prob = 1.0 / gsum
    g_idx = jnp.where(g[0] == gmax, 0.0, jnp.where(g[1] == gmax, 1.0, jnp.where(g[2] == gmax, 2.0, 3.0)))

    e0 = ROUTER_EXPERT_LANE0
    grp = [lt[e0 + EXPERTS_PER_GROUP * r:e0 + EXPERTS_PER_GROUP * (r + 1)] for r in range(N_GROUPS)]
    sel = jnp.where(g_idx == 0.0, grp[0], jnp.where(g_idx == 1.0, grp[1], jnp.where(g_idx == 2.0, grp[2], grp[3])))
    sub = lax.broadcasted_iota(I32, (EXPERTS_PER_GROUP, tm), 0).astype(F32)
    m1 = jnp.max(sel, axis=0, keepdims=True)
    i1 = jnp.min(jnp.where(sel == m1, sub, float(EXPERTS_PER_GROUP)), axis=0, keepdims=True)
    sel2 = jnp.where(sub == i1, -jnp.inf, sel)
    m2 = jnp.max(sel2, axis=0, keepdims=True)
    i2 = jnp.min(jnp.where(sel2 == m2, sub, float(EXPERTS_PER_GROUP)), axis=0, keepdims=True)
    d = jnp.exp(m2 - m1)
    w1 = g_prob / (1.0 + d)
    w2 = g_prob * d / (1.0 + d)
    first_lower = i1 < i2
    lo = jnp.minimum(i1, i2)
    hi = jnp.maximum(i1, i2)
    w_lo = jnp.where(first_lower, w1, w2)
    w_hi = jnp.where(first_lower, w2, w1)
    pair = lo * EXPERTS_PER_GROUP - lo * (lo + 1.0) * 0.5 + hi - lo - 1.0
    cls = g_idx * PAIRS_PER_GROUP + pair

    crow = lax.broadcasted_iota(I32, (CLASS_ROWS, tm), 0).astype(F32)
    onehot = jnp.where(crow == cls, 1.0, 0.0)
    ta = lax.broadcasted_iota(I32, (tm, tm), 0)
    tb = lax.broadcasted_iota(I32, (tm, tm), 1)
    earlier = jnp.where(ta < tb, 1.0, 0.0).astype(BF16)
    before = _dot(onehot.astype(BF16), earlier)
    carry = carry_scr[...]
    rank = jnp.sum(onehot * (before + carry[:, 0:1]), axis=0, keepdims=True)
    carry = carry + jnp.sum(onehot, axis=1, keepdims=True)
    carry_scr[...] = carry
    cnt_ref[...] = carry

    ri_ref[...] = jnp.zeros_like(ri_ref)
    ri_ref[0, 0:1, :] = cls.astype(I32)
    ri_ref[0, 1:2, :] = rank.astype(I32)
    if pack_rows:
        mrow = lax.broadcasted_iota(I32, (LANES, tm), 0)
        batch = pl.program_id(0).astype(F32)
        meta_t = jnp.where(mrow == 0, w_lo, jnp.where(mrow == 1, w_hi, jnp.where(mrow == 2, batch, 0.0)))
        rec_ref[0, :, D + D // 2:] = meta_t.T
    else:
        rf_ref[...] = jnp.zeros_like(rf_ref)
        rf_ref[0, 0:1, :] = w_lo
        rf_ref[0, 1:2, :] = w_hi


def _out_route(x, att, gm, mod, lw, *, tm, pack_rows):
    B, S, D = x.shape
    nt = S // tm
    row = lambda b, i: (b, i, 0)
    tile = lambda b, i: (b * nt + i, 0, 0)
    route_i = (jax.ShapeDtypeStruct((B * nt, SUBLANES, tm), I32), pl.BlockSpec((1, SUBLANES, tm), tile))
    route_f = (jax.ShapeDtypeStruct((B * nt, SUBLANES, tm), F32), pl.BlockSpec((1, SUBLANES, tm), tile))
    counts = (jax.ShapeDtypeStruct((CLASS_ROWS, LANES), F32), _full((CLASS_ROWS, LANES)))
    if pack_rows:
        rec_w = D + D // 2 + LANES
        outs = [(jax.ShapeDtypeStruct((B, S, rec_w), F32), pl.BlockSpec((1, tm, rec_w), row)), route_i, counts]
    else:
        outs = [(jax.ShapeDtypeStruct((B, S, D), F32), pl.BlockSpec((1, tm, D), row)),
                (jax.ShapeDtypeStruct((B, S, D), BF16), pl.BlockSpec((1, tm, D), row)), route_i, route_f, counts]
    out_shape = [o[0] for o in outs]
    out_specs = [o[1] for o in outs]
    return pl.pallas_call(
        functools.partial(_out_route_kernel, pack_rows=pack_rows),
        grid=(B, nt),
        in_specs=[pl.BlockSpec((1, tm, D), row), pl.BlockSpec((1, tm, ATT_WIDTH), row), pl.BlockSpec((1, tm, GM_WIDTH), row),
                  pl.BlockSpec((1, 6, D), lambda b, i: (b, 0, 0)),
                  _full((1, ATT_WIDTH)), _full((ATT_WIDTH, D)), _full((GM_WIDTH, D)), _full((1, D)),
                  _full((D, 2 * ROUTER_COLS)), _full((D, ROUTER_COLS)), _full((1, ROUTER_COLS))],
        out_specs=out_specs,
        out_shape=out_shape,
        scratch_shapes=[pltpu.VMEM((CLASS_ROWS, LANES), F32)],
        compiler_params=_params(2),
    )(x, att, gm, mod, lw["g_out_attn"], lw["w_out_a"], lw["w_out_g"], lw["g_norm_ffn"],
      lw["w_r_a"], lw["w_r_b"], lw["b_r"])


def _swiglu(hb, wgu, wd):
    hid = _dot(hb, wgu)
    act = jax.nn.silu(hid[:, :D_EXPERT]) * hid[:, D_EXPERT:]
    return _dot(act.astype(BF16), wd)


def _moe_pairs_kernel(elo_ref, ehi_ref, nv_ref, dest_ref, rec_hbm, zeros_hbm, gate_ref,
                      wgu_lo_ref, wd_lo_ref, wgu_hi_ref, wd_hi_ref, y_hbm,
                      xbuf0, xbuf1, ybuf0, ybuf1, tok_ref, gsem, ssem, zsem):
    i = pl.program_id(0)
    xbuf, ybuf = (xbuf0, xbuf1), (ybuf0, ybuf1)
    D = ybuf0.shape[-1]
    n_groups_all = MOE_ROWS // SUBLANES

    def gather_start(t, s, g, u):
        pltpu.make_async_copy(rec_hbm.at[pl.ds(t, 1)], xbuf[s].at[g, pl.ds(u, 1)], gsem.at[s]).start()

    def scatter_start(t, s, g, u):
        pltpu.make_async_copy(ybuf[s].at[g, pl.ds(u, 1)], y_hbm.at[pl.ds(t, 1)], ssem.at[s]).start()

    def all_rows(blk, fn):
        base = blk * MOE_ROWS
        for g in range(n_groups_all):
            for u in range(SUBLANES):
                fn(tok_ref[base + g * SUBLANES + u], g, u)

    def valid_rows(blk, fn):
        nv = nv_ref[blk]
        base = blk * MOE_ROWS
        n_groups = nv // SUBLANES

        def group(g, c):
            for u in range(SUBLANES):
                fn(tok_ref[base + g * SUBLANES + u], g, u)
            return c

        def single(r, c):
            fn(tok_ref[base + r], n_groups, r - n_groups * SUBLANES)
            return c

        lax.fori_loop(0, n_groups, group, 0)
        lax.fori_loop(n_groups * SUBLANES, nv, single, 0)

    def gather_wait(s):
        pltpu.make_async_copy(xbuf[s], xbuf[s], gsem.at[s]).wait()

    def scatter_wait(blk, s):
        nv = nv_ref[blk]
        n_groups = nv // SUBLANES
        buf = ybuf[s]

        @pl.when(n_groups > 0)
        def _():
            rows = buf.at[pl.ds(0, n_groups)]
            pltpu.make_async_copy(rows, rows, ssem.at[s]).wait()

        def single(r, c):
            row = buf.at[0, pl.ds(0, 1)]
            pltpu.make_async_copy(row, row, ssem.at[s]).wait()
            return c

        lax.fori_loop(n_groups * SUBLANES, nv, single, 0)

    def experts(s):
        hb = _unpack_bf16_pairs(xbuf[s][:, :, D:D + D // 2].reshape(MOE_ROWS, D // 2))
        meta = xbuf[s][:, :, D + D // 2:].reshape(MOE_ROWS, LANES)
        w_lo, w_hi, bidx = meta[:, 0:1], meta[:, 1:2], meta[:, 2:3]
        blane = lax.broadcasted_iota(I32, (1, 2 * MOD_BATCH_PAD), 1)
        blane = jnp.where(blane >= MOD_BATCH_PAD, blane - MOD_BATCH_PAD, blane).astype(F32)
        onehot = jnp.where(bidx == blane, 1.0, 0.0).astype(BF16)
        gate_m = _dot(onehot, gate_ref[...])
        x = xbuf[s][:, :, 0:D].reshape(MOE_ROWS, D)
        moe = w_lo * _swiglu(hb, wgu_lo_ref[0], wd_lo_ref[0]) + w_hi * _swiglu(hb, wgu_hi_ref[0], wd_hi_ref[0])
        ybuf[s][...] = (x + gate_m * moe).reshape(n_groups_all, SUBLANES, D)

    @pl.when(i == 0)
    def _():
        zero = pltpu.make_async_copy(zeros_hbm, tok_ref, zsem)
        zero.start()
        zero.wait()

        def invert(g, c):
            for u in range(SUBLANES):
                t = g * SUBLANES + u
                tok_ref[dest_ref[t]] = t
            return c
        lax.fori_loop(0, dest_ref.shape[0] // SUBLANES, invert, 0)
        all_rows(0, lambda t, g, u: gather_start(t, 0, g, u))

    prev = jnp.maximum(i - 1, 0)
    has_rows = nv_ref[i] > 0
    prev_full = (i >= 1) & (nv_ref[prev] == MOE_ROWS)
    prev_partial = (i >= 1) & (nv_ref[prev] > 0) & (nv_ref[prev] < MOE_ROWS)

    def step(slot):
        other = 1 - slot

        @pl.when(has_rows)
        def _():
            all_rows(i + 1, lambda t, g, u: gather_start(t, other, g, u))
            gather_wait(slot)

            @pl.when(prev_full)
            def _():
                all_rows(i - 1, lambda t, g, u: scatter_start(t, other, g, u))

            @pl.when(prev_partial)
            def _():
                valid_rows(i - 1, lambda t, g, u: scatter_start(t, other, g, u))

            @pl.when(i >= 2)
            def _():
                scatter_wait(i - 2, slot)
            experts(slot)

        @pl.when(jnp.logical_not(has_rows) & (i >= 1) & (nv_ref[prev] > 0))
        def _():
            gather_wait(slot)
            valid_rows(i - 1, lambda t, g, u: scatter_start(t, other, g, u))

            @pl.when(i >= 2)
            def _():
                scatter_wait(i - 2, slot)
            scatter_wait(i - 1, other)

    for parity in (0, 1):
        pl.when(i % 2 == parity)(functools.partial(step, parity))


def _moe_pairs(rec, blk_elo, blk_ehi, blk_nv, dest, gate_tab, lw):
    T, rec_w = rec.shape
    assert T % SUBLANES == 0
    D = gate_tab.shape[1]
    nb = blk_nv.shape[0]
    wgu, wd = lw["w_gu_e"], lw["w_d_e"]
    grid_spec = pltpu.PrefetchScalarGridSpec(
        num_scalar_prefetch=4,
        grid=(nb,),
        in_specs=[
            pl.BlockSpec(memory_space=pl.ANY),
            pl.BlockSpec(memory_space=pl.ANY),
            pl.BlockSpec(gate_tab.shape, lambda i, *_: (0, 0)),
            pl.BlockSpec((1, D, 2 * D_EXPERT), lambda i, elo, ehi, nv, tok: (elo[i], 0, 0)),
            pl.BlockSpec((1, D_EXPERT, D), lambda i, elo, ehi, nv, tok: (elo[i], 0, 0)),
            pl.BlockSpec((1, D, 2 * D_EXPERT), lambda i, elo, ehi, nv, tok: (ehi[i], 0, 0)),
            pl.BlockSpec((1, D_EXPERT, D), lambda i, elo, ehi, nv, tok: (ehi[i], 0, 0)),
        ],
        out_specs=pl.BlockSpec(memory_space=pl.ANY),
        scratch_shapes=[pltpu.VMEM((MOE_ROWS // SUBLANES, SUBLANES, rec_w), F32),
                        pltpu.VMEM((MOE_ROWS // SUBLANES, SUBLANES, rec_w), F32),
                        pltpu.VMEM((MOE_ROWS // SUBLANES, SUBLANES, D), F32),
                        pltpu.VMEM((MOE_ROWS // SUBLANES, SUBLANES, D), F32),
                        pltpu.SMEM((nb * MOE_ROWS,), I32),
                        pltpu.SemaphoreType.DMA((2,)), pltpu.SemaphoreType.DMA((2,)), pltpu.SemaphoreType.DMA(())],
    )
    return pl.pallas_call(
        _moe_pairs_kernel,
        grid_spec=grid_spec,
        out_shape=jax.ShapeDtypeStruct((T, D), F32),
        compiler_params=_params(1),
    )(blk_elo, blk_ehi, blk_nv, dest, rec, jnp.zeros((nb * MOE_ROWS,), I32), gate_tab, wgu, wd, wgu, wd)


def _small_lookup(table, idx):
    ids = jnp.arange(table.shape[0], dtype=I32)
    return jnp.sum(jnp.where(idx[:, None] == ids[None, :], table[None, :], 0), axis=1)


def _pair_tables():
    lo, hi = [], []
    for g in range(N_GROUPS):
        for a in range(EXPERTS_PER_GROUP):
            for b in range(a + 1, EXPERTS_PER_GROUP):
                lo.append(g * EXPERTS_PER_GROUP + a)
                hi.append(g * EXPERTS_PER_GROUP + b)
    return np.asarray(lo, np.int32), np.asarray(hi, np.int32)


def _moe_prompt(rec, cls, rank, counts, mod, lw):
    B, S, rec_w = rec.shape
    D = mod.shape[-1]
    T = B * S
    nb = T // MOE_ROWS + N_CLASSES + 1
    nblk = (counts + MOE_ROWS - 1) // MOE_ROWS
    blk_end = jnp.cumsum(nblk)
    blk_start = blk_end - nblk
    dest = _small_lookup(blk_start, cls) * MOE_ROWS + rank
    ids = jnp.arange(nb, dtype=I32)
    used = blk_end[-1]
    class_of = lambda blk: jnp.sum((blk_end[None, :] <= blk[:, None]).astype(I32), axis=1)
    blk_cls = jnp.where(ids < used, jnp.minimum(class_of(ids), N_CLASSES - 1), class_of(used[None] - 1))
    first_row = (ids - _small_lookup(blk_start, blk_cls)) * MOE_ROWS
    blk_nv = jnp.where(ids < used, jnp.clip(_small_lookup(counts, blk_cls) - first_row, 0, MOE_ROWS), 0)
    lo_tab, hi_tab = _pair_tables()
    blk_elo = _small_lookup(jnp.asarray(lo_tab), blk_cls)
    blk_ehi = _small_lookup(jnp.asarray(hi_tab), blk_cls)
    gate_hi, gate_lo = _split_bf16(jnp.pad(mod[:, 5, :], ((0, MOD_BATCH_PAD - B), (0, 0))))
    gate_tab = jnp.concatenate([gate_hi, gate_lo], axis=0)
    y = _moe_pairs(rec.reshape(T, rec_w), blk_elo, blk_ehi, blk_nv.astype(I32), dest.astype(I32), gate_tab, lw)
    return y.reshape(B, S, D)


def _moe_dense_kernel(h_ref, x1_ref, gate_ref, w_ref, sel_ref, wgu_ref, wd_ref, y_ref):
    e = pl.program_id(0)

    @pl.when(e == 0)
    def _():
        y_ref[...] = jnp.zeros_like(y_ref)

    ye = _swiglu(h_ref[...], wgu_ref[0], wd_ref[0])
    y_ref[...] += jnp.where(sel_ref[0] > 0.5, w_ref[0] * ye, 0.0)

    @pl.when(e == pl.num_programs(0) - 1)
    def _():
        y_ref[...] = x1_ref[...] + gate_ref[...] * y_ref[...]


def _moe_dense(h2, x1, gate_rows, w_sel, sel, lw):
    T, D = x1.shape
    per_e = lambda e: (e, 0, 0)
    return pl.pallas_call(
        _moe_dense_kernel,
        grid=(N_EXPERTS,),
        in_specs=[_full((T, D)), _full((T, D)), _full((T, D)),
                  pl.BlockSpec((1, T, 1), per_e), pl.BlockSpec((1, T, 1), per_e),
                  pl.BlockSpec((1, D, 2 * D_EXPERT), per_e), pl.BlockSpec((1, D_EXPERT, D), per_e)],
        out_specs=_full((T, D)),
        out_shape=jax.ShapeDtypeStruct((T, D), F32),
        compiler_params=_params(1),
    )(h2, x1, gate_rows, w_sel, sel, lw["w_gu_e"], lw["w_d_e"])


def _rope_tables(pos, gain, scale):
    inv = 1.0 / (ROPE_BASE ** (jnp.arange(ROPE_HALF, dtype=F32) / ROPE_HALF))
    ang = pos.astype(F32)[:, None] * inv[None, :]
    cos, sin = jnp.cos(ang) * scale, jnp.sin(ang) * scale
    n = pos.shape[0]
    z = lambda w: jnp.zeros((n, w), F32)
    pad = HEAD_PAD - QK_DIM
    g_nope, g1, g2 = gain[:NOPE_DIM], gain[NOPE_DIM:NOPE_DIM + ROPE_HALF], gain[NOPE_DIM + ROPE_HALF:]
    tab_c = jnp.concatenate([jnp.broadcast_to(g_nope * scale, (n, NOPE_DIM)), cos * g1, cos * g2, z(pad)], axis=1)
    tab_s = jnp.concatenate([z(NOPE_DIM), -sin * g2, sin * g1, z(pad)], axis=1)
    return tab_c, tab_s


def _prep_layer(w, l, chunk_lens):
    D = w["w_in"].shape[1]
    H = MLA_HEADS
    lw = {}
    row = lambda name: w[name][l].reshape(1, -1)
    for name in ("g_norm_mix", "g_q_lat", "g_kv_lat", "g_ln_v", "b_ln_v", "g_out_attn", "g_out_gmlp", "g_norm_ffn"):
        lw[name] = row(name)
    w_in = w["w_in"][l]
    o1, o2, o3 = Q_LORA, Q_LORA + KV_LORA, Q_LORA + KV_LORA + ROPE_DIM
    o4 = o3 + GM_WIDTH
    z32 = jnp.zeros((D, ROPE_DIM), F32)
    wr = w_in[:, o2:o3]
    lw["w_in_p"] = jnp.concatenate([w_in[:, :o2], wr, z32, wr, z32, w_in[:, o3:o4], w_in[:, o4:]], axis=1).astype(BF16)
    assert lw["w_in_p"].shape[1] == PROJ_COLS
    pad = HEAD_PAD - QK_DIM
    w_uq = w["w_uq"][l].reshape(Q_LORA, H, QK_DIM)
    w_uq_swapped = jnp.concatenate([jnp.zeros((Q_LORA, H, NOPE_DIM), F32), w_uq[:, :, NOPE_DIM + ROPE_HALF:],
                                    w_uq[:, :, NOPE_DIM:NOPE_DIM + ROPE_HALF]], axis=2)
    head_pad = lambda a: jnp.pad(a, ((0, 0), (0, 0), (0, pad))).reshape(Q_LORA, H * HEAD_PAD)
    lw["w_uq_p"] = jnp.concatenate([head_pad(w_uq), head_pad(w_uq_swapped)], axis=1).astype(BF16)
    w_ukv = w["w_ukv"][l].reshape(KV_LORA, H, NOPE_DIM + V_DIM)
    lw["w_k_p"] = jnp.pad(w_ukv[:, :, :NOPE_DIM], ((0, 0), (0, 0), (0, HEAD_PAD - NOPE_DIM))).reshape(KV_LORA, H * HEAD_PAD).astype(BF16)
    w_v = w_ukv[:, :, NOPE_DIM:]
    zeros_v = jnp.zeros_like(w_v)
    odd_head = (jnp.arange(H) % 2 == 1)[None, :, None]
    w_v_lo, w_v_hi = jnp.where(odd_head, zeros_v, w_v), jnp.where(odd_head, w_v, zeros_v)
    lw["w_v_p"] = jnp.concatenate([w_v_lo, w_v_hi], axis=2).reshape(KV_LORA, H * HEAD_PAD).astype(BF16)
    lw["scores_bounded"] = _scores_bounded(w, l)
    lw["g_qnorm"] = w["g_qnorm"][l]
    lw["g_knorm"] = w["g_knorm"][l]
    for L in chunk_lens:
        wsp = w["w_spatial"][l][:, :L, :L]
        lw["w_sp_pairs_%d" % L] = wsp.reshape(GM_HEADS // 2, 2 * L, L).astype(BF16)
        lw["b_sp_rows_%d" % L] = jnp.repeat(jnp.transpose(w["b_spatial"][l][:, :L]), GM_HEAD_DIM, axis=1)
    w_out = w["w_out"][l].astype(BF16)
    lw["w_out_a"], lw["w_out_g"] = w_out[:ATT_WIDTH], w_out[ATT_WIDTH:]
    wr_full = jnp.zeros((D, ROUTER_COLS), F32)
    wr_full = wr_full.at[:, :N_GROUPS].set(w["w_router_group"][l])
    wr_full = wr_full.at[:, ROUTER_EXPERT_LANE0:ROUTER_EXPERT_LANE0 + N_EXPERTS].set(w["w_router_expert"][l])
    r_hi, r_lo = _split_bf16(wr_full)
    lw["w_r_a"] = jnp.concatenate([r_hi, r_lo], axis=1)
    lw["w_r_b"] = r_hi
    br = jnp.zeros((1, ROUTER_COLS), F32)
    br = br.at[0, :N_GROUPS].set(w["b_router_group"][l])
    lw["b_r"] = br.at[0, ROUTER_EXPERT_LANE0:ROUTER_EXPERT_LANE0 + N_EXPERTS].set(w["b_router_expert"][l])
    lw["w_gu_e"] = jnp.concatenate([w["w_gate_e"][l], w["w_up_e"][l]], axis=-1).astype(BF16)
    lw["w_d_e"] = w["w_down_e"][l].astype(BF16)
    return lw


def _tiles(seq):
    return min(seq, 512), min(seq, 2048), min(seq, 512)


def _layer_prompt(x, mod, lw):
    B, S, D = x.shape
    tm, tq, tk = _tiles(S)
    pos = jnp.arange(S)
    q_tabs = _rope_tables(pos, lw["g_qnorm"], QK_DIM ** -0.5 * LOG2E)
    k_tabs = _rope_tables(pos, lw["g_knorm"], 1.0)
    ckv, krope, q, gm, k, v = _mix_in(x, mod, lw, q_tabs, k_tabs, tm=tm, chunk_len=GM_CHUNK, emit_kv=True, emit_vrows=False)
    att = _attention(lw["scores_bounded"], q, k, v, tq=tq, tk=tk, q_off=0, kv_valid=S)
    rec, ri, cnt = _out_route(x, att, gm, mod, lw, tm=tm, pack_rows=True)
    cls = ri[:, 0, :].reshape(B * S)
    rank = ri[:, 1, :].reshape(B * S)
    counts = cnt[:N_CLASSES, 0].astype(I32)
    y = _moe_prompt(rec, cls, rank, counts, mod, lw)
    return y, ckv, krope


def _layer_sample(x, mod, past_ckv, past_krope, lw):
    B, S, D = x.shape
    past = past_ckv.shape[1]
    q_tabs = _rope_tables(past + jnp.arange(S), lw["g_qnorm"], QK_DIM ** -0.5 * LOG2E)
    ckv, krope, q, gm, v_rows = _mix_in(x, mod, lw, q_tabs, q_tabs, tm=S, chunk_len=S, emit_kv=False, emit_vrows=True)
    kv_valid = past + S
    kv_pad = -(-kv_valid // LANES) * LANES
    extra = kv_pad - kv_valid
    ckv_all = jnp.concatenate([past_ckv, ckv, jnp.zeros((B, extra, KV_LORA), F32)], axis=1)
    kr_all = jnp.concatenate([past_krope, krope, jnp.zeros((B, extra, ROPE_DIM), F32)], axis=1)
    kr_slot = jnp.pad(kr_all, ((0, 0), (0, 0), (NOPE_DIM, HEAD_PAD - QK_DIM)))
    k_tabs = _rope_tables(jnp.arange(kv_pad), lw["g_knorm"], 1.0)
    k, v = _kv_latent(ckv_all, kr_slot, lw, k_tabs, tr=kv_pad)
    att = _attention(lw["scores_bounded"], q, k, v, tq=S, tk=kv_pad, q_off=past, kv_valid=kv_valid)
    x1, h2, ri, rf, _ = _out_route(x, att, gm, mod, lw, tm=S, pack_rows=False)
    T = B * S
    cls, w_lo, w_hi = ri[:, 0, :].reshape(T), rf[:, 0, :].reshape(T), rf[:, 1, :].reshape(T)
    lo_tab, hi_tab = _pair_tables()
    e_lo, e_hi = jnp.asarray(lo_tab)[cls], jnp.asarray(hi_tab)[cls]
    eids = jnp.arange(N_EXPERTS, dtype=I32)[:, None]
    is_lo, is_hi = eids == e_lo[None, :], eids == e_hi[None, :]
    w_sel = (jnp.where(is_lo, w_lo[None, :], 0.0) + jnp.where(is_hi, w_hi[None, :], 0.0))[:, :, None]
    sel = (is_lo | is_hi).astype(F32)[:, :, None]
    gate_rows = jnp.repeat(mod[:, 5, :], S, axis=0)
    y = _moe_dense(h2.reshape(T, D), x1.reshape(T, D), gate_rows, w_sel, sel, lw)
    return y.reshape(B, S, D), ckv, krope, v_rows


def kernel(x_prompt, x_sample, cache_ckv, cache_krope, c_prompt, c_sample, w_ada, b_ada, g_norm_mix, w_in, g_q_lat, w_uq, g_kv_lat, w_ukv, g_qnorm, g_knorm, g_ln_v, b_ln_v, w_spatial, b_spatial, g_out_attn, g_out_gmlp, w_out, g_norm_ffn, w_router_group, b_router_group, w_router_expert, b_router_expert, w_gate_e, w_up_e, w_down_e):
    w = dict(w_in=w_in, g_norm_mix=g_norm_mix, g_q_lat=g_q_lat, w_uq=w_uq, g_kv_lat=g_kv_lat, w_ukv=w_ukv,
             g_qnorm=g_qnorm, g_knorm=g_knorm, g_ln_v=g_ln_v, b_ln_v=b_ln_v, w_spatial=w_spatial, b_spatial=b_spatial,
             g_out_attn=g_out_attn, g_out_gmlp=g_out_gmlp, w_out=w_out, g_norm_ffn=g_norm_ffn,
             w_router_group=w_router_group, b_router_group=b_router_group, w_router_expert=w_router_expert,
             b_router_expert=b_router_expert, w_gate_e=w_gate_e, w_up_e=w_up_e, w_down_e=w_down_e)
    depth = w_ada.shape[0]
    Bp, Sp, D = x_prompt.shape
    Bs, Ss, _ = x_sample.shape
    assert Sp % GM_CHUNK == 0 and Ss <= GM_CHUNK and Ss % CHUNK == 0 and Bp <= MOD_BATCH_PAD
    c_all = jnp.concatenate([c_prompt, c_sample], axis=0)
    y_p, y_s = x_prompt, x_sample
    outs = [[] for _ in range(5)]
    for l in range(depth):
        lw = _prep_layer(w, l, (GM_CHUNK, Ss))
        mod = _ada_mod(c_all, w_ada[l], b_ada[l]).reshape(Bp + Bs, 6, D)
        y_p, ckv_p, kr_p = _layer_prompt(y_p, mod[:Bp], lw)
        y_s, ckv_s, kr_s, v_s = _layer_sample(y_s, mod[Bp:], cache_ckv[l], cache_krope[l], lw)
        for lst, val in zip(outs, (ckv_p, kr_p, ckv_s, kr_s, v_s)):
            lst.append(val)
    return (y_p, y_s) + tuple(jnp.stack(lst) for lst in outs)
```

```python
import functools

import numpy as np
import jax
import jax.numpy as jnp
from jax import lax
from jax.experimental import pallas as pl
from jax.experimental.pallas import tpu as pltpu

F32 = jnp.float32
BF16 = jnp.bfloat16
I32 = jnp.int32

CHUNK = 64
CHUNK_SHIFT = 6
EPS = 1e-6
MLA_HEADS = 8
Q_LORA = 256
KV_LORA = 128
NOPE_DIM = 64
ROPE_DIM = 32
ROPE_HALF = ROPE_DIM // 2
V_DIM = 64
QK_DIM = NOPE_DIM + ROPE_DIM
ATT_WIDTH = MLA_HEADS * V_DIM
ROPE_BASE = 10000.0
GM_HEADS = 8
GM_HEAD_DIM = 64
GM_WIDTH = GM_HEADS * GM_HEAD_DIM
GM_CHUNK = 128
N_GROUPS = 4
EXPERTS_PER_GROUP = 8
N_EXPERTS = N_GROUPS * EXPERTS_PER_GROUP
D_EXPERT = 256
PAIRS_PER_GROUP = EXPERTS_PER_GROUP * (EXPERTS_PER_GROUP - 1) // 2
N_CLASSES = N_GROUPS * PAIRS_PER_GROUP

LANES = 128
SUBLANES = 8
HEAD_PAD = LANES
PROJ_COLS = 1536
ROUTER_COLS = LANES
ROUTER_EXPERT_LANE0 = SUBLANES
CLASS_ROWS = LANES
MOE_ROWS = 128
MOD_BATCH_PAD = 16
DIAG_ROWS = 256
COPY_BURST_GROUPS = (0, 3, 6, 9, 12, 14, 16)
VMEM_LIMIT = 48 * 1024 * 1024
NEG_BIG = -1e30
LOG2E = 1.4426950408889634
BF16_SLACK = 1.02
SCORE_BOUND = 90.0

assert CHUNK == 1 << CHUNK_SHIFT


def _params(n_axes, vmem=VMEM_LIMIT):
    return pltpu.CompilerParams(dimension_semantics=("arbitrary",) * n_axes, vmem_limit_bytes=vmem)


def _full(shape):
    nd = len(shape)
    return pl.BlockSpec(shape, lambda *_: (0,) * nd)


def _split_bf16(x):
    hi = x.astype(BF16)
    lo = (x - hi.astype(F32)).astype(BF16)
    return hi, lo


def _dot(a, b):
    return jnp.dot(a, b, preferred_element_type=F32)


def _ada_kernel(c_ref, w_ref, b_ref, o_ref):
    a_hi, a_lo = _split_bf16(jax.nn.silu(c_ref[...]))
    w_hi, w_lo = _split_bf16(w_ref[...])
    o_ref[...] = _dot(a_hi, w_hi) + _dot(a_lo, w_hi) + _dot(a_hi, w_lo) + b_ref[...]


def _ada_mod(c, w_ada, b_ada):
    n, d = c.shape
    cols = w_ada.shape[1]
    tn = 1536
    return pl.pallas_call(
        _ada_kernel,
        grid=(cols // tn,),
        in_specs=[_full((n, d)), pl.BlockSpec((d, tn), lambda j: (0, j)), pl.BlockSpec((1, tn), lambda j: (0, j))],
        out_specs=pl.BlockSpec((n, tn), lambda j: (0, j)),
        out_shape=jax.ShapeDtypeStruct((n, cols), F32),
        compiler_params=_params(1),
    )(c, w_ada, b_ada.reshape(1, cols))


def _rms(x, g):
    return x * lax.rsqrt(jnp.mean(x * x, axis=-1, keepdims=True) + EPS) * g


def _head_norm_rope(xh, xh_swapped, tab_c, tab_s):
    ms = jnp.sum(xh * xh, axis=-1, keepdims=True) * (1.0 / QK_DIM)
    return (xh * tab_c + xh_swapped * tab_s) * lax.rsqrt(ms + EPS)


def _keys_values(ckv, kr_slot, gkv, wk, wv, tab_c, tab_s, k_ref, v_ref):
    cb = _rms(ckv, gkv).astype(BF16)
    kall = _dot(cb, wk)
    vall = _dot(cb, wv)
    lane = lax.broadcasted_iota(I32, (1, LANES), 1)
    ones_hi = jnp.where(lane >= V_DIM, 1.0, 0.0)
    ones_lo = 1.0 - ones_hi
    kr_swapped = jnp.where(lane < NOPE_DIM + ROPE_HALF, pltpu.roll(kr_slot, HEAD_PAD - ROPE_HALF, 1),
                           pltpu.roll(kr_slot, ROPE_HALF, 1))
    kr_swapped = jnp.where((lane >= NOPE_DIM) & (lane < QK_DIM), kr_swapped, 0.0)
    for h in range(MLA_HEADS):
        kh = kall[:, h * HEAD_PAD:(h + 1) * HEAD_PAD] + kr_slot
        k_ref[0, h] = _head_norm_rope(kh, kr_swapped, tab_c, tab_s).astype(BF16)
        ones = ones_hi if h % 2 == 0 else ones_lo
        v_ref[0, h] = (vall[:, h * HEAD_PAD:(h + 1) * HEAD_PAD] + ones).astype(BF16)


def _mix_in_kernel(x_ref, mod_ref, gmix_ref, win_ref, gql_ref, wuq_ref, gkv_ref, wk_ref, wv_ref,
                   cq_ref, sq_ref, ck_ref, sk_ref, glnv_ref, blnv_ref, wsp_ref, bsp_ref,
                   ggm_ref, *rest, chunk_len, emit_kv, emit_vrows):
    outs = list(rest[:-1])
    mixed_scr = rest[-1]
    ckv_ref, kr_ref, q_ref, gm_ref = outs[:4]
    outs = outs[4:]
    if emit_kv:
        k_ref, v_ref = outs[:2]
        outs = outs[2:]
    if emit_vrows:
        vrows_ref = outs[0]

    x = x_ref[0]
    tm = x.shape[0]
    shift, scale = mod_ref[0, 0:1, :], mod_ref[0, 1:2, :]
    h = _rms(x, gmix_ref[...]) * (1.0 + scale) + shift
    proj = _dot(h.astype(BF16), win_ref[...])

    q_lat = proj[:, 0:Q_LORA]
    ckv = proj[:, Q_LORA:Q_LORA + KV_LORA]
    kr_blk = proj[:, Q_LORA + KV_LORA:Q_LORA + KV_LORA + LANES]
    ckv_ref[0] = ckv
    kr_ref[0] = kr_blk[:, 0:ROPE_DIM]

    q = _dot(_rms(q_lat, gql_ref[...]).astype(BF16), wuq_ref[...])
    cq, sq = cq_ref[...], sq_ref[...]
    n_q = MLA_HEADS * HEAD_PAD
    for hd in range(MLA_HEADS):
        qh = q[:, hd * HEAD_PAD:(hd + 1) * HEAD_PAD]
        qh_swapped = q[:, n_q + hd * HEAD_PAD:n_q + (hd + 1) * HEAD_PAD]
        q_ref[0, hd] = _head_norm_rope(qh, qh_swapped, cq, sq).astype(BF16)

    if emit_kv:
        lane = lax.broadcasted_iota(I32, (1, LANES), 1)
        kr_slot = jnp.where(lane >= NOPE_DIM, kr_blk, 0.0)
        _keys_values(ckv, kr_slot, gkv_ref[...], wk_ref[...], wv_ref[...], ck_ref[...], sk_ref[...], k_ref, v_ref)

    g_u = proj[:, 512:512 + GM_WIDTH]
    g_v = proj[:, 1024:1024 + GM_WIDTH]
    u = jax.nn.gelu(g_u)
    gv = jax.nn.gelu(g_v)
    mu = jnp.mean(gv, axis=-1, keepdims=True)
    xc = gv - mu
    var = jnp.mean(xc * xc, axis=-1, keepdims=True)
    v_rows = xc * lax.rsqrt(var + EPS) * glnv_ref[...] + blnv_ref[...]
    if emit_vrows:
        vrows_ref[0] = v_rows
    vb = v_rows.astype(BF16)

    L = chunk_len
    t = lax.broadcasted_iota(I32, (2 * L, L), 0)
    s = lax.broadcasted_iota(I32, (2 * L, L), 1)
    t = jnp.where(t >= L, t - L, t)
    allowed = (s >> CHUNK_SHIFT) <= (t >> CHUNK_SHIFT)
    lane = lax.broadcasted_iota(I32, (1, LANES), 1)
    first_head = lane < GM_HEAD_DIM
    for p in range(GM_HEADS // 2):
        w_pair = jnp.where(allowed, wsp_ref[p], jnp.zeros((), BF16))
        for c in range(tm // L):
            vp = vb[c * L:(c + 1) * L, p * LANES:(p + 1) * LANES]
            r = _dot(w_pair, vp)
            mixed = jnp.where(first_head, r[:L], r[L:])
            mixed_scr[c * L:(c + 1) * L, p * LANES:(p + 1) * LANES] = mixed + bsp_ref[:, p * LANES:(p + 1) * LANES]
    gm = u * mixed_scr[...]
    gm_ref[0] = _rms(gm, ggm_ref[...]).astype(BF16)


def _mix_in(x, mod, lw, q_tabs, k_tabs, *, tm, chunk_len, emit_kv, emit_vrows):
    B, S, D = x.shape
    nt = S // tm
    H = MLA_HEADS
    row = lambda b, i: (b, i, 0)
    tab = pl.BlockSpec((tm, LANES), lambda b, i: (i, 0))
    in_specs = [
        pl.BlockSpec((1, tm, D), row),
        pl.BlockSpec((1, 6, D), lambda b, i: (b, 0, 0)),
        _full((1, D)), _full((D, PROJ_COLS)), _full((1, Q_LORA)), _full((Q_LORA, 2 * H * HEAD_PAD)),
        _full((1, KV_LORA)), _full((KV_LORA, H * HEAD_PAD)), _full((KV_LORA, H * HEAD_PAD)),
        tab, tab, tab, tab,
        _full((1, GM_WIDTH)), _full((1, GM_WIDTH)),
        _full((GM_HEADS // 2, 2 * chunk_len, chunk_len)), _full((chunk_len, GM_WIDTH)), _full((1, GM_WIDTH)),
    ]
    out_shape = [
        jax.ShapeDtypeStruct((B, S, KV_LORA), F32),
        jax.ShapeDtypeStruct((B, S, ROPE_DIM), F32),
        jax.ShapeDtypeStruct((B, H, S, HEAD_PAD), BF16),
        jax.ShapeDtypeStruct((B, S, GM_WIDTH), BF16),
    ]
    head_blk = pl.BlockSpec((1, H, tm, HEAD_PAD), lambda b, i: (b, 0, i, 0))
    out_specs = [
        pl.BlockSpec((1, tm, KV_LORA), row),
        pl.BlockSpec((1, tm, ROPE_DIM), row),
        head_blk,
        pl.BlockSpec((1, tm, GM_WIDTH), row),
    ]
    if emit_kv:
        out_shape += [jax.ShapeDtypeStruct((B, H, S, HEAD_PAD), BF16), jax.ShapeDtypeStruct((B, H, S, HEAD_PAD), BF16)]
        out_specs += [head_blk, head_blk]
    if emit_vrows:
        out_shape += [jax.ShapeDtypeStruct((B, S, GM_WIDTH), F32)]
        out_specs += [pl.BlockSpec((1, tm, GM_WIDTH), row)]
    kern = functools.partial(_mix_in_kernel, chunk_len=chunk_len, emit_kv=emit_kv, emit_vrows=emit_vrows)
    return pl.pallas_call(
        kern,
        grid=(B, nt),
        in_specs=in_specs,
        out_specs=out_specs,
        out_shape=out_shape,
        scratch_shapes=[pltpu.VMEM((tm, GM_WIDTH), F32)],
        compiler_params=_params(2),
    )(x, mod, lw["g_norm_mix"], lw["w_in_p"], lw["g_q_lat"], lw["w_uq_p"], lw["g_kv_lat"], lw["w_k_p"], lw["w_v_p"],
      *q_tabs, *k_tabs, lw["g_ln_v"], lw["b_ln_v"],
      lw["w_sp_pairs_%d" % chunk_len], lw["b_sp_rows_%d" % chunk_len], lw["g_out_gmlp"])


def _kv_latent_kernel(ckv_ref, kr_ref, gkv_ref, wk_ref, wv_ref, c_ref, s_ref, k_ref, v_ref):
    _keys_values(ckv_ref[0], kr_ref[0], gkv_ref[...], wk_ref[...], wv_ref[...], c_ref[...], s_ref[...], k_ref, v_ref)


def _kv_latent(ckv_all, kr_slot_all, lw, k_tabs, *, tr):
    B, K, _ = ckv_all.shape
    H = MLA_HEADS
    row = lambda b, i: (b, i, 0)
    tab = pl.BlockSpec((tr, LANES), lambda b, i: (i, 0))
    return pl.pallas_call(
        _kv_latent_kernel,
        grid=(B, K // tr),
        in_specs=[pl.BlockSpec((1, tr, KV_LORA), row), pl.BlockSpec((1, tr, LANES), row),
                  _full((1, KV_LORA)), _full((KV_LORA, H * HEAD_PAD)), _full((KV_LORA, H * HEAD_PAD)),
                  tab, tab],
        out_specs=[pl.BlockSpec((1, H, tr, HEAD_PAD), lambda b, i: (b, 0, i, 0))] * 2,
        out_shape=[jax.ShapeDtypeStruct((B, H, K, HEAD_PAD), BF16)] * 2,
        compiler_params=_params(2),
    )(ckv_all, kr_slot_all, lw["g_kv_lat"], lw["w_k_p"], lw["w_v_p"], *k_tabs)


def _attn_kernel(bounded_ref, q_ref, k_ref, v_ref, o_ref, acc_ref, *, tq, tk, n_q, q_off, kv_valid, split_diag):
    i = pl.program_id(2) if n_q > 1 else 0
    q_first = q_off + i * tq
    vis_first = jnp.minimum(((q_first >> CHUNK_SHIFT) + 1) << CHUNK_SHIFT, kv_valid)
    vis_last = jnp.minimum((((q_first + tq - 1) >> CHUNK_SHIFT) + 1) << CHUNK_SHIFT, kv_valid)
    n_unmasked = vis_first // tk
    n_total = (vis_last + tk - 1) // tk

    q_pos = q_first + lax.broadcasted_iota(I32, (tq, 1), 0)
    limit = jnp.minimum(((q_pos >> CHUNK_SHIFT) + 1) << CHUNK_SHIFT, kv_valid)
    lane = lax.broadcasted_iota(I32, (1, LANES), 1)

    def scores(j, start, masked, r0=0, nr=tq, nk=tk):
        s = lax.dot_general(q_ref[0, j, r0:r0 + nr, :], k_ref[0, j, pl.ds(start, nk), :], (((1,), (1,)), ((), ())),
                            preferred_element_type=F32)
        if masked:
            k_pos = start + lax.broadcasted_iota(I32, (1, nk), 1)
            s = jnp.where(k_pos < limit[r0:r0 + nr], s, NEG_BIG)
        return s

    def plain(start, masked, r0=0, nr=tq, nk=tk, assign=False):
        for j in range(2):
            p = jnp.exp2(scores(j, start, masked, r0, nr, nk)).astype(BF16)
            pv = _dot(p, v_ref[0, j, pl.ds(start, nk), :])
            if assign:
                acc_ref[j, r0:r0 + nr, :] = pv
            else:
                acc_ref[j, r0:r0 + nr, :] += pv

    def online_block(kb, carry, masked):
        start = pl.multiple_of(kb * tk, tk)
        new = []
        for j in range(2):
            m, acc = carry[2 * j:2 * j + 2]
            s = scores(j, start, masked)
            m_new = jnp.maximum(m, jnp.max(s, axis=-1, keepdims=True))
            p = jnp.exp2(s - m_new).astype(BF16)
            acc = jnp.exp2(m - m_new) * acc + _dot(p, v_ref[0, j, pl.ds(start, tk), :])
            new += [m_new, acc]
        return tuple(new)

    def finish(acc0, acc1):
        first = lane < V_DIM
        num = jnp.where(first, acc0, acc1)
        den = pltpu.roll(jnp.where(first, acc1, acc0), V_DIM, 1)
        o_ref[0] = (num / den).astype(BF16)

    @pl.when(bounded_ref[0] == 1)
    def _():
        at = lambda kb: pl.multiple_of(kb * tk, tk)

        def pair(p, c):
            plain(at(2 * p), False)
            plain(at(2 * p + 1), False)
            return c

        def single(kb, c, masked):
            plain(at(kb), masked)
            return c

        if split_diag:
            own = 0 if n_q == 1 else pl.multiple_of(i * tq, tq)
            for r in range(tq // DIAG_ROWS):
                plain(own, True, r * DIAG_ROWS, DIAG_ROWS, (r + 1) * DIAG_ROWS, assign=True)
        else:
            acc_ref[...] = jnp.zeros_like(acc_ref)
        n_pairs = n_unmasked // 2
        lax.fori_loop(0, n_pairs, pair, 0)
        lax.fori_loop(2 * n_pairs, n_unmasked, lambda kb, c: single(kb, c, False), 0)
        if not split_diag:
            lax.fori_loop(n_unmasked, n_total, lambda kb, c: single(kb, c, True), 0)
        finish(acc_ref[0], acc_ref[1])

    @pl.when(bounded_ref[0] != 1)
    def _():
        zeros = jnp.zeros((tq, LANES), F32)
        m0 = jnp.full((tq, 1), NEG_BIG, F32)
        c = lax.fori_loop(0, n_unmasked, lambda kb, c: online_block(kb, c, False), (m0, zeros, m0, zeros))
        c = lax.fori_loop(n_unmasked, n_total, lambda kb, c: online_block(kb, c, True), c)
        finish(c[1], c[3])


def _attention(bounded, q, k, v, *, tq, tk, q_off, kv_valid):
    B, H, Sq, _ = q.shape
    Sk = k.shape[2]
    split_diag = q_off == 0 and kv_valid >= Sq and tq % DIAG_ROWS == 0 and tq % tk == 0
    kern = functools.partial(_attn_kernel, tq=tq, tk=tk, n_q=Sq // tq, q_off=q_off, kv_valid=kv_valid,
                             split_diag=split_diag)
    grid_spec = pltpu.PrefetchScalarGridSpec(
        num_scalar_prefetch=1,
        grid=(B, H // 2, Sq // tq),
        in_specs=[pl.BlockSpec((1, 2, tq, HEAD_PAD), lambda b, hp, i, f: (b, hp, i, 0)),
                  pl.BlockSpec((1, 2, Sk, HEAD_PAD), lambda b, hp, i, f: (b, hp, 0, 0)),
                  pl.BlockSpec((1, 2, Sk, HEAD_PAD), lambda b, hp, i, f: (b, hp, 0, 0))],
        out_specs=pl.BlockSpec((1, tq, LANES), lambda b, hp, i, f: (b, i, hp)),
        scratch_shapes=[pltpu.VMEM((2, tq, LANES), F32)],
    )
    return pl.pallas_call(
        kern,
        grid_spec=grid_spec,
        out_shape=jax.ShapeDtypeStruct((B, Sq, ATT_WIDTH), BF16),
        compiler_params=_params(3),
    )(bounded, q, k, v)


def _scores_bounded(w, l):
    gq = jnp.max(jnp.abs(w["g_qnorm"][l]))
    gk = jnp.max(jnp.abs(w["g_knorm"][l]))
    bound = (QK_DIM ** 0.5) * LOG2E * BF16_SLACK * gq * gk
    return (bound <= SCORE_BOUND).astype(I32).reshape(1)


def _pack_bf16_pairs(h):
    n = h.shape[1] // 2
    hi = pltpu.bitcast(h[:, :n].astype(BF16).astype(F32), jnp.uint32)
    lo = pltpu.bitcast(h[:, n:].astype(BF16).astype(F32), jnp.uint32)
    return pltpu.bitcast(hi | (lo >> 16), F32)


def _unpack_bf16_pairs(words):
    w = pltpu.bitcast(words, jnp.uint32)
    hi = pltpu.bitcast(w & jnp.uint32(0xFFFF0000), F32)
    lo = pltpu.bitcast(w << 16, F32)
    return jnp.concatenate([hi, lo], axis=1).astype(BF16)


def _out_route_kernel(x_ref, att_ref, gm_ref, mod_ref, goa_ref, woa_ref, wog_ref, gffn_ref, wra_ref, wrb_ref, br_ref,
                      *rest, pack_rows):
    carry_scr = rest[-1]
    if pack_rows:
        rec_ref, ri_ref, cnt_ref = rest[:3]
    else:
        x1_ref, h2_ref, ri_ref, rf_ref, cnt_ref = rest[:5]
    first_step = (pl.program_id(0) == 0) & (pl.program_id(1) == 0)

    @pl.when(first_step)
    def _():
        carry_scr[...] = jnp.zeros_like(carry_scr)

    x = x_ref[0]
    tm, D = x.shape
    gate_a = mod_ref[0, 2:3, :]
    shift_m, scale_m = mod_ref[0, 3:4, :], mod_ref[0, 4:5, :]
    att_n = _rms(att_ref[0].astype(F32), goa_ref[...]).astype(BF16)
    mix = _dot(att_n, woa_ref[...]) + _dot(gm_ref[0], wog_ref[...])
    x1 = x + gate_a * mix
    h2 = _rms(x1, gffn_ref[...]) * (1.0 + scale_m) + shift_m
    if pack_rows:
        rec_ref[0, :, 0:D] = x1
        rec_ref[0, :, D:D + D // 2] = _pack_bf16_pairs(h2)
    else:
        x1_ref[0] = x1
        h2_ref[0] = h2.astype(BF16)

    h_hi, h_lo = _split_bf16(h2)
    la = _dot(h_hi, wra_ref[...])
    logits = la[:, :ROUTER_COLS] + la[:, ROUTER_COLS:] + _dot(h_lo, wrb_ref[...]) + br_ref[...]
    lt = logits.T

    g = [lt[r:r + 1] for r in range(N_GROUPS)]
    gmax = jnp.maximum(jnp.maximum(g[0], g[1]), jnp.maximum(g[2], g[3]))
    gsum = sum(jnp.exp(gr - gmax) for gr in g)
    g_prob = 1.0 / gsum
    g_idx = jnp.where(g[0] == gmax, 0.0, jnp.where(g[1] == gmax, 1.0, jnp.where(g[2] == gmax, 2.0, 3.0)))

    e0 = ROUTER_EXPERT_LANE0
    grp = [lt[e0 + EXPERTS_PER_GROUP * r:e0 + EXPERTS_PER_GROUP * (r + 1)] for r in range(N_GROUPS)]
    sel = jnp.where(g_idx == 0.0, grp[0], jnp.where(g_idx == 1.0, grp[1], jnp.where(g_idx == 2.0, grp[2], grp[3])))
    sub = lax.broadcasted_iota(I32, (EXPERTS_PER_GROUP, tm), 0).astype(F32)
    m1 = jnp.max(sel, axis=0, keepdims=True)
    i1 = jnp.min(jnp.where(sel == m1, sub, float(EXPERTS_PER_GROUP)), axis=0, keepdims=True)
    sel2 = jnp.where(sub == i1, -jnp.inf, sel)
    m2 = jnp.max(sel2, axis=0, keepdims=True)
    i2 = jnp.min(jnp.where(sel2 == m2, sub, float(EXPERTS_PER_GROUP)), axis=0, keepdims=True)
    d = jnp.exp(m2 - m1)
    w1 = g_prob / (1.0 + d)
    w2 = g_prob * d / (1.0 + d)
    first_lower = i1 < i2
    lo = jnp.minimum(i1, i2)
    hi = jnp.maximum(i1, i2)
    w_lo = jnp.where(first_lower, w1, w2)
    w_hi = jnp.where(first_lower, w2, w1)
    pair = lo * EXPERTS_PER_GROUP - lo * (lo + 1.0) * 0.5 + hi - lo - 1.0
    cls = g_idx * PAIRS_PER_GROUP + pair

    crow = lax.broadcasted_iota(I32, (CLASS_ROWS, tm), 0).astype(F32)
    onehot = jnp.where(crow == cls, 1.0, 0.0)
    ta = lax.broadcasted_iota(I32, (tm, tm), 0)
    tb = lax.broadcasted_iota(I32, (tm, tm), 1)
    earlier = jnp.where(ta < tb, 1.0, 0.0).astype(BF16)
    before = _dot(onehot.astype(BF16), earlier)
    carry = carry_scr[...]
    rank = jnp.sum(onehot * (before + carry[:, 0:1]), axis=0, keepdims=True)
    carry = carry + jnp.sum(onehot, axis=1, keepdims=True)
    carry_scr[...] = carry
    cnt_ref[...] = carry

    ri_ref[...] = jnp.zeros_like(ri_ref)
    ri_ref[0, 0:1, :] = cls.astype(I32)
    ri_ref[0, 1:2, :] = rank.astype(I32)
    if pack_rows:
        mrow = lax.broadcasted_iota(I32, (LANES, tm), 0)
        batch = pl.program_id(0).astype(F32)
        meta_t = jnp.where(mrow == 0, w_lo, jnp.where(mrow == 1, w_hi, jnp.where(mrow == 2, batch, 0.0)))
        rec_ref[0, :, D + D // 2:] = meta_t.T
    else:
        rf_ref[...] = jnp.zeros_like(rf_ref)
        rf_ref[0, 0:1, :] = w_lo
        rf_ref[0, 1:2, :] = w_hi


def _out_route(x, att, gm, mod, lw, *, tm, pack_rows):
    B, S, D = x.shape
    nt = S // tm
    row = lambda b, i: (b, i, 0)
    tile = lambda b, i: (b * nt + i, 0, 0)
    route_i = (jax.ShapeDtypeStruct((B * nt, SUBLANES, tm), I32), pl.BlockSpec((1, SUBLANES, tm), tile))
    route_f = (jax.ShapeDtypeStruct((B * nt, SUBLANES, tm), F32), pl.BlockSpec((1, SUBLANES, tm), tile))
    counts = (jax.ShapeDtypeStruct((CLASS_ROWS, LANES), F32), _full((CLASS_ROWS, LANES)))
    if pack_rows:
        rec_w = D + D // 2 + LANES
        outs = [(jax.ShapeDtypeStruct((B, S, rec_w), F32), pl.BlockSpec((1, tm, rec_w), row)), route_i, counts]
    else:
        outs = [(jax.ShapeDtypeStruct((B, S, D), F32), pl.BlockSpec((1, tm, D), row)),
                (jax.ShapeDtypeStruct((B, S, D), BF16), pl.BlockSpec((1, tm, D), row)), route_i, route_f, counts]
    out_shape = [o[0] for o in outs]
    out_specs = [o[1] for o in outs]
    return pl.pallas_call(
        functools.partial(_out_route_kernel, pack_rows=pack_rows),
        grid=(B, nt),
        in_specs=[pl.BlockSpec((1, tm, D), row), pl.BlockSpec((1, tm, ATT_WIDTH), row), pl.BlockSpec((1, tm, GM_WIDTH), row),
                  pl.BlockSpec((1, 6, D), lambda b, i: (b, 0, 0)),
                  _full((1, ATT_WIDTH)), _full((ATT_WIDTH, D)), _full((GM_WIDTH, D)), _full((1, D)),
                  _full((D, 2 * ROUTER_COLS)), _full((D, ROUTER_COLS)), _full((1, ROUTER_COLS))],
        out_specs=out_specs,
        out_shape=out_shape,
        scratch_shapes=[pltpu.VMEM((CLASS_ROWS, LANES), F32)],
        compiler_params=_params(2),
    )(x, att, gm, mod, lw["g_out_attn"], lw["w_out_a"], lw["w_out_g"], lw["g_norm_ffn"],
      lw["w_r_a"], lw["w_r_b"], lw["b_r"])


def _swiglu(hb, wgu, wd):
    hid = _dot(hb, wgu)
    act = jax.nn.silu(hid[:, :D_EXPERT]) * hid[:, D_EXPERT:]
    return _dot(act.astype(BF16), wd)


def _moe_pairs_kernel(elo_ref, ehi_ref, nv_ref, dest_ref, rec_hbm, zeros_hbm, gate_ref,
                      wgu_lo_ref, wd_lo_ref, wgu_hi_ref, wd_hi_ref, y_hbm,
                      xbuf0, xbuf1, xbuf2, ybuf0, ybuf1, ybuf2, tok_ref, gsem, ssem, zsem):
    i = pl.program_id(0)
    xbuf, ybuf = (xbuf0, xbuf1, xbuf2), (ybuf0, ybuf1, ybuf2)
    D = ybuf0.shape[-1]
    n_groups_all = MOE_ROWS // SUBLANES

    def gather_start(t, s, g, u):
        pltpu.make_async_copy(rec_hbm.at[pl.ds(t, 1)], xbuf[s].at[g, pl.ds(u, 1)], gsem.at[s]).start()

    def scatter_start(t, s, g, u):
        pltpu.make_async_copy(ybuf[s].at[g, pl.ds(u, 1)], y_hbm.at[pl.ds(t, 1)], ssem.at[s]).start()

    def some_rows(blk, groups, fn):
        base = blk * MOE_ROWS
        for g in groups:
            for u in range(SUBLANES):
                fn(tok_ref[base + g * SUBLANES + u], g, u)

    def all_rows(blk, fn):
        some_rows(blk, range(n_groups_all), fn)

    def valid_rows(blk, fn):
        nv = nv_ref[blk]
        base = blk * MOE_ROWS
        n_groups = nv // SUBLANES

        def group(g, c):
            for u in range(SUBLANES):
                fn(tok_ref[base + g * SUBLANES + u], g, u)
            return c

        def single(r, c):
            fn(tok_ref[base + r], n_groups, r - n_groups * SUBLANES)
            return c

        lax.fori_loop(0, n_groups, group, 0)
        lax.fori_loop(n_groups * SUBLANES, nv, single, 0)

    def gather_wait(s):
        pltpu.make_async_copy(xbuf[s], xbuf[s], gsem.at[s]).wait()

    def scatter_wait(blk, s):
        nv = nv_ref[blk]
        n_groups = nv // SUBLANES
        buf = ybuf[s]

        @pl.when(n_groups > 0)
        def _():
            rows = buf.at[pl.ds(0, n_groups)]
            pltpu.make_async_copy(rows, rows, ssem.at[s]).wait()

        def single(r, c):
            row = buf.at[0, pl.ds(0, 1)]
            pltpu.make_async_copy(row, row, ssem.at[s]).wait()
            return c

        lax.fori_loop(n_groups * SUBLANES, nv, single, 0)

    def experts(s, start_copies):
        start_copies(0)
        hb = _unpack_bf16_pairs(xbuf[s][:, :, D:D + D // 2].reshape(MOE_ROWS, D // 2))
        meta = xbuf[s][:, :, D + D // 2:].reshape(MOE_ROWS, LANES)
        w_lo, w_hi, bidx = meta[:, 0:1], meta[:, 1:2], meta[:, 2:3]
        blane = lax.broadcasted_iota(I32, (1, 2 * MOD_BATCH_PAD), 1)
        blane = jnp.where(blane >= MOD_BATCH_PAD, blane - MOD_BATCH_PAD, blane).astype(F32)
        onehot = jnp.where(bidx == blane, 1.0, 0.0).astype(BF16)
        gate_m = _dot(onehot, gate_ref[...])
        start_copies(1)
        hid = _dot(hb, wgu_lo_ref[0])
        act_lo = (jax.nn.silu(hid[:, :D_EXPERT]) * hid[:, D_EXPERT:]).astype(BF16)
        start_copies(2)
        moe = w_lo * _dot(act_lo, wd_lo_ref[0])
        start_copies(3)
        hid = _dot(hb, wgu_hi_ref[0])
        act_hi = (jax.nn.silu(hid[:, :D_EXPERT]) * hid[:, D_EXPERT:]).astype(BF16)
        start_copies(4)
        moe = moe + w_hi * _dot(act_hi, wd_hi_ref[0])
        start_copies(5)
        x = xbuf[s][:, :, 0:D].reshape(MOE_ROWS, D)
        ybuf[s][...] = (x + gate_m * moe).reshape(n_groups_all, SUBLANES, D)

    @pl.when(i == 0)
    def _():
        zero = pltpu.make_async_copy(zeros_hbm, tok_ref, zsem)
        zero.start()
        zero.wait()

        def invert(g, c):
            for u in range(SUBLANES):
                t = g * SUBLANES + u
                tok_ref[dest_ref[t]] = t
            return c
        lax.fori_loop(0, dest_ref.shape[0] // SUBLANES, invert, 0)
        all_rows(0, lambda t, g, u: gather_start(t, 0, g, u))
        all_rows(1, lambda t, g, u: gather_start(t, 1, g, u))

    prev = jnp.maximum(i - 1, 0)
    has_rows = nv_ref[i] > 0
    prev_full = (i >= 1) & (nv_ref[prev] == MOE_ROWS)
    prev_partial = (i >= 1) & (nv_ref[prev] > 0) & (nv_ref[prev] < MOE_ROWS)

    def step(slot):
        before, after = (slot + 2) % 3, (slot + 1) % 3

        @pl.when(has_rows)
        def _():
            gather_wait(slot)

            @pl.when(prev_partial)
            def _():
                valid_rows(i - 1, lambda t, g, u: scatter_start(t, before, g, u))

            @pl.when(i >= 3)
            def _():
                scatter_wait(i - 3, slot)

            def start_copies(k):
                groups = range(COPY_BURST_GROUPS[k], COPY_BURST_GROUPS[k + 1])

                @pl.when(prev_full)
                def _():
                    some_rows(i - 1, groups, lambda t, g, u: scatter_start(t, before, g, u))

                @pl.when(has_rows)
                def _():
                    some_rows(i + 2, groups, lambda t, g, u: gather_start(t, before, g, u))

            experts(slot, start_copies)

        @pl.when(jnp.logical_not(has_rows) & (i >= 1) & (nv_ref[prev] > 0))
        def _():
            gather_wait(slot)
            gather_wait(after)

            @pl.when(i >= 3)
            def _():
                scatter_wait(i - 3, slot)
            valid_rows(i - 1, lambda t, g, u: scatter_start(t, before, g, u))

            @pl.when(i >= 2)
            def _():
                scatter_wait(i - 2, after)
            scatter_wait(i - 1, before)

    for parity in range(3):
        pl.when(i % 3 == parity)(functools.partial(step, parity))


def _moe_pairs(rec, blk_elo, blk_ehi, blk_nv, dest, gate_tab, lw):
    T, rec_w = rec.shape
    assert T % SUBLANES == 0
    D = gate_tab.shape[1]
    nb = blk_nv.shape[0]
    wgu, wd = lw["w_gu_e"], lw["w_d_e"]
    grid_spec = pltpu.PrefetchScalarGridSpec(
        num_scalar_prefetch=4,
        grid=(nb,),
        in_specs=[
            pl.BlockSpec(memory_space=pl.ANY),
            pl.BlockSpec(memory_space=pl.ANY),
            pl.BlockSpec(gate_tab.shape, lambda i, *_: (0, 0)),
            pl.BlockSpec((1, D, 2 * D_EXPERT), lambda i, elo, ehi, nv, tok: (elo[i], 0, 0)),
            pl.BlockSpec((1, D_EXPERT, D), lambda i, elo, ehi, nv, tok: (elo[i], 0, 0)),
            pl.BlockSpec((1, D, 2 * D_EXPERT), lambda i, elo, ehi, nv, tok: (ehi[i], 0, 0)),
            pl.BlockSpec((1, D_EXPERT, D), lambda i, elo, ehi, nv, tok: (ehi[i], 0, 0)),
        ],
        out_specs=pl.BlockSpec(memory_space=pl.ANY),
        scratch_shapes=[pltpu.VMEM((MOE_ROWS // SUBLANES, SUBLANES, rec_w), F32)] * 3
                       + [pltpu.VMEM((MOE_ROWS // SUBLANES, SUBLANES, D), F32)] * 3 + [
                        pltpu.SMEM((nb * MOE_ROWS,), I32),
                        pltpu.SemaphoreType.DMA((3,)), pltpu.SemaphoreType.DMA((3,)), pltpu.SemaphoreType.DMA(())],
    )
    return pl.pallas_call(
        _moe_pairs_kernel,
        grid_spec=grid_spec,
        out_shape=jax.ShapeDtypeStruct((T, D), F32),
        compiler_params=_params(1),
    )(blk_elo, blk_ehi, blk_nv, dest, rec, jnp.zeros((nb * MOE_ROWS,), I32), gate_tab, wgu, wd, wgu, wd)


def _small_lookup(table, idx):
    ids = jnp.arange(table.shape[0], dtype=I32)
    return jnp.sum(jnp.where(idx[:, None] == ids[None, :], table[None, :], 0), axis=1)


def _pair_tables():
    lo, hi = [], []
    for g in range(N_GROUPS):
        for a in range(EXPERTS_PER_GROUP):
            for b in range(a + 1, EXPERTS_PER_GROUP):
                lo.append(g * EXPERTS_PER_GROUP + a)
                hi.append(g * EXPERTS_PER_GROUP + b)
    return np.asarray(lo, np.int32), np.asarray(hi, np.int32)


def _moe_prompt(rec, cls, rank, counts, mod, lw):
    B, S, rec_w = rec.shape
    D = mod.shape[-1]
    T = B * S
    nb = T // MOE_ROWS + N_CLASSES + 2
    nblk = (counts + MOE_ROWS - 1) // MOE_ROWS
    blk_end = jnp.cumsum(nblk)
    blk_start = blk_end - nblk
    dest = _small_lookup(blk_start, cls) * MOE_ROWS + rank
    ids = jnp.arange(nb, dtype=I32)
    used = blk_end[-1]
    class_of = lambda blk: jnp.sum((blk_end[None, :] <= blk[:, None]).astype(I32), axis=1)
    blk_cls = jnp.where(ids < used, jnp.minimum(class_of(ids), N_CLASSES - 1), class_of(used[None] - 1))
    first_row = (ids - _small_lookup(blk_start, blk_cls)) * MOE_ROWS
    blk_nv = jnp.where(ids < used, jnp.clip(_small_lookup(counts, blk_cls) - first_row, 0, MOE_ROWS), 0)
    lo_tab, hi_tab = _pair_tables()
    blk_elo = _small_lookup(jnp.asarray(lo_tab), blk_cls)
    blk_ehi = _small_lookup(jnp.asarray(hi_tab), blk_cls)
    gate_hi, gate_lo = _split_bf16(jnp.pad(mod[:, 5, :], ((0, MOD_BATCH_PAD - B), (0, 0))))
    gate_tab = jnp.concatenate([gate_hi, gate_lo], axis=0)
    y = _moe_pairs(rec.reshape(T, rec_w), blk_elo, blk_ehi, blk_nv.astype(I32), dest.astype(I32), gate_tab, lw)
    return y.reshape(B, S, D)


def _moe_dense_kernel(h_ref, x1_ref, gate_ref, w_ref, sel_ref, wgu_ref, wd_ref, y_ref):
    e = pl.program_id(0)

    @pl.when(e == 0)
    def _():
        y_ref[...] = jnp.zeros_like(y_ref)

    ye = _swiglu(h_ref[...], wgu_ref[0], wd_ref[0])
    y_ref[...] += jnp.where(sel_ref[0] > 0.5, w_ref[0] * ye, 0.0)

    @pl.when(e == pl.num_programs(0) - 1)
    def _():
        y_ref[...] = x1_ref[...] + gate_ref[...] * y_ref[...]


def _moe_dense(h2, x1, gate_rows, w_sel, sel, lw):
    T, D = x1.shape
    per_e = lambda e: (e, 0, 0)
    return pl.pallas_call(
        _moe_dense_kernel,
        grid=(N_EXPERTS,),
        in_specs=[_full((T, D)), _full((T, D)), _full((T, D)),
                  pl.BlockSpec((1, T, 1), per_e), pl.BlockSpec((1, T, 1), per_e),
                  pl.BlockSpec((1, D, 2 * D_EXPERT), per_e), pl.BlockSpec((1, D_EXPERT, D), per_e)],
        out_specs=_full((T, D)),
        out_shape=jax.ShapeDtypeStruct((T, D), F32),
        compiler_params=_params(1),
    )(h2, x1, gate_rows, w_sel, sel, lw["w_gu_e"], lw["w_d_e"])


def _rope_tables(pos, gain, scale):
    inv = 1.0 / (ROPE_BASE ** (jnp.arange(ROPE_HALF, dtype=F32) / ROPE_HALF))
    ang = pos.astype(F32)[:, None] * inv[None, :]
    cos, sin = jnp.cos(ang) * scale, jnp.sin(ang) * scale
    n = pos.shape[0]
    z = lambda w: jnp.zeros((n, w), F32)
    pad = HEAD_PAD - QK_DIM
    g_nope, g1, g2 = gain[:NOPE_DIM], gain[NOPE_DIM:NOPE_DIM + ROPE_HALF], gain[NOPE_DIM + ROPE_HALF:]
    tab_c = jnp.concatenate([jnp.broadcast_to(g_nope * scale, (n, NOPE_DIM)), cos * g1, cos * g2, z(pad)], axis=1)
    tab_s = jnp.concatenate([z(NOPE_DIM), -sin * g2, sin * g1, z(pad)], axis=1)
    return tab_c, tab_s


def _prep_layer(w, l, chunk_lens):
    D = w["w_in"].shape[1]
    H = MLA_HEADS
    lw = {}
    row = lambda name: w[name][l].reshape(1, -1)
    for name in ("g_norm_mix", "g_q_lat", "g_kv_lat", "g_ln_v", "b_ln_v", "g_out_attn", "g_out_gmlp", "g_norm_ffn"):
        lw[name] = row(name)
    w_in = w["w_in"][l]
    o1, o2, o3 = Q_LORA, Q_LORA + KV_LORA, Q_LORA + KV_LORA + ROPE_DIM
    o4 = o3 + GM_WIDTH
    z32 = jnp.zeros((D, ROPE_DIM), F32)
    wr = w_in[:, o2:o3]
    lw["w_in_p"] = jnp.concatenate([w_in[:, :o2], wr, z32, wr, z32, w_in[:, o3:o4], w_in[:, o4:]], axis=1).astype(BF16)
    assert lw["w_in_p"].shape[1] == PROJ_COLS
    pad = HEAD_PAD - QK_DIM
    w_uq = w["w_uq"][l].reshape(Q_LORA, H, QK_DIM)
    w_uq_swapped = jnp.concatenate([jnp.zeros((Q_LORA, H, NOPE_DIM), F32), w_uq[:, :, NOPE_DIM + ROPE_HALF:],
                                    w_uq[:, :, NOPE_DIM:NOPE_DIM + ROPE_HALF]], axis=2)
    head_pad = lambda a: jnp.pad(a, ((0, 0), (0, 0), (0, pad))).reshape(Q_LORA, H * HEAD_PAD)
    lw["w_uq_p"] = jnp.concatenate([head_pad(w_uq), head_pad(w_uq_swapped)], axis=1).astype(BF16)
    w_ukv = w["w_ukv"][l].reshape(KV_LORA, H, NOPE_DIM + V_DIM)
    lw["w_k_p"] = jnp.pad(w_ukv[:, :, :NOPE_DIM], ((0, 0), (0, 0), (0, HEAD_PAD - NOPE_DIM))).reshape(KV_LORA, H * HEAD_PAD).astype(BF16)
    w_v = w_ukv[:, :, NOPE_DIM:]
    zeros_v = jnp.zeros_like(w_v)
    odd_head = (jnp.arange(H) % 2 == 1)[None, :, None]
    w_v_lo, w_v_hi = jnp.where(odd_head, zeros_v, w_v), jnp.where(odd_head, w_v, zeros_v)
    lw["w_v_p"] = jnp.concatenate([w_v_lo, w_v_hi], axis=2).reshape(KV_LORA, H * HEAD_PAD).astype(BF16)
    lw["scores_bounded"] = _scores_bounded(w, l)
    lw["g_qnorm"] = w["g_qnorm"][l]
    lw["g_knorm"] = w["g_knorm"][l]
    for L in chunk_lens:
        wsp = w["w_spatial"][l][:, :L, :L]
        lw["w_sp_pairs_%d" % L] = wsp.reshape(GM_HEADS // 2, 2 * L, L).astype(BF16)
        lw["b_sp_rows_%d" % L] = jnp.repeat(jnp.transpose(w["b_spatial"][l][:, :L]), GM_HEAD_DIM, axis=1)
    w_out = w["w_out"][l].astype(BF16)
    lw["w_out_a"], lw["w_out_g"] = w_out[:ATT_WIDTH], w_out[ATT_WIDTH:]
    wr_full = jnp.zeros((D, ROUTER_COLS), F32)
    wr_full = wr_full.at[:, :N_GROUPS].set(w["w_router_group"][l])
    wr_full = wr_full.at[:, ROUTER_EXPERT_LANE0:ROUTER_EXPERT_LANE0 + N_EXPERTS].set(w["w_router_expert"][l])
    r_hi, r_lo = _split_bf16(wr_full)
    lw["w_r_a"] = jnp.concatenate([r_hi, r_lo], axis=1)
    lw["w_r_b"] = r_hi
    br = jnp.zeros((1, ROUTER_COLS), F32)
    br = br.at[0, :N_GROUPS].set(w["b_router_group"][l])
    lw["b_r"] = br.at[0, ROUTER_EXPERT_LANE0:ROUTER_EXPERT_LANE0 + N_EXPERTS].set(w["b_router_expert"][l])
    lw["w_gu_e"] = jnp.concatenate([w["w_gate_e"][l], w["w_up_e"][l]], axis=-1).astype(BF16)
    lw["w_d_e"] = w["w_down_e"][l].astype(BF16)
    return lw


def _tiles(seq):
    return min(seq, 512), min(seq, 2048), min(seq, 512)


def _layer_prompt(x, mod, lw):
    B, S, D = x.shape
    tm, tq, tk = _tiles(S)
    pos = jnp.arange(S)
    q_tabs = _rope_tables(pos, lw["g_qnorm"], QK_DIM ** -0.5 * LOG2E)
    k_tabs = _rope_tables(pos, lw["g_knorm"], 1.0)
    ckv, krope, q, gm, k, v = _mix_in(x, mod, lw, q_tabs, k_tabs, tm=tm, chunk_len=GM_CHUNK, emit_kv=True, emit_vrows=False)
    att = _attention(lw["scores_bounded"], q, k, v, tq=tq, tk=tk, q_off=0, kv_valid=S)
    rec, ri, cnt = _out_route(x, att, gm, mod, lw, tm=tm, pack_rows=True)
    cls = ri[:, 0, :].reshape(B * S)
    rank = ri[:, 1, :].reshape(B * S)
    counts = cnt[:N_CLASSES, 0].astype(I32)
    y = _moe_prompt(rec, cls, rank, counts, mod, lw)
    return y, ckv, krope


def _layer_sample(x, mod, past_ckv, past_krope, lw):
    B, S, D = x.shape
    past = past_ckv.shape[1]
    q_tabs = _rope_tables(past + jnp.arange(S), lw["g_qnorm"], QK_DIM ** -0.5 * LOG2E)
    ckv, krope, q, gm, v_rows = _mix_in(x, mod, lw, q_tabs, q_tabs, tm=S, chunk_len=S, emit_kv=False, emit_vrows=True)
    kv_valid = past + S
    kv_pad = -(-kv_valid // LANES) * LANES
    extra = kv_pad - kv_valid
    ckv_all = jnp.concatenate([past_ckv, ckv, jnp.zeros((B, extra, KV_LORA), F32)], axis=1)
    kr_all = jnp.concatenate([past_krope, krope, jnp.zeros((B, extra, ROPE_DIM), F32)], axis=1)
    kr_slot = jnp.pad(kr_all, ((0, 0), (0, 0), (NOPE_DIM, HEAD_PAD - QK_DIM)))
    k_tabs = _rope_tables(jnp.arange(kv_pad), lw["g_knorm"], 1.0)
    k, v = _kv_latent(ckv_all, kr_slot, lw, k_tabs, tr=kv_pad)
    att = _attention(lw["scores_bounded"], q, k, v, tq=S, tk=kv_pad, q_off=past, kv_valid=kv_valid)
    x1, h2, ri, rf, _ = _out_route(x, att, gm, mod, lw, tm=S, pack_rows=False)
    T = B * S
    cls, w_lo, w_hi = ri[:, 0, :].reshape(T), rf[:, 0, :].reshape(T), rf[:, 1, :].reshape(T)
    lo_tab, hi_tab = _pair_tables()
    e_lo, e_hi = jnp.asarray(lo_tab)[cls], jnp.asarray(hi_tab)[cls]
    eids = jnp.arange(N_EXPERTS, dtype=I32)[:, None]
    is_lo, is_hi = eids == e_lo[None, :], eids == e_hi[None, :]
    w_sel = (jnp.where(is_lo, w_lo[None, :], 0.0) + jnp.where(is_hi, w_hi[None, :], 0.0))[:, :, None]
    sel = (is_lo | is_hi).astype(F32)[:, :, None]
    gate_rows = jnp.repeat(mod[:, 5, :], S, axis=0)
    y = _moe_dense(h2.reshape(T, D), x1.reshape(T, D), gate_rows, w_sel, sel, lw)
    return y.reshape(B, S, D), ckv, krope, v_rows


def kernel(x_prompt, x_sample, cache_ckv, cache_krope, c_prompt, c_sample, w_ada, b_ada, g_norm_mix, w_in, g_q_lat, w_uq, g_kv_lat, w_ukv, g_qnorm, g_knorm, g_ln_v, b_ln_v, w_spatial, b_spatial, g_out_attn, g_out_gmlp, w_out, g_norm_ffn, w_router_group, b_router_group, w_router_expert, b_router_expert, w_gate_e, w_up_e, w_down_e):
    w = dict(w_in=w_in, g_norm_mix=g_norm_mix, g_q_lat=g_q_lat, w_uq=w_uq, g_kv_lat=g_kv_lat, w_ukv=w_ukv,
             g_qnorm=g_qnorm, g_knorm=g_knorm, g_ln_v=g_ln_v, b_ln_v=b_ln_v, w_spatial=w_spatial, b_spatial=b_spatial,
             g_out_attn=g_out_attn, g_out_gmlp=g_out_gmlp, w_out=w_out, g_norm_ffn=g_norm_ffn,
             w_router_group=w_router_group, b_router_group=b_router_group, w_router_expert=w_router_expert,
             b_router_expert=b_router_expert, w_gate_e=w_gate_e, w_up_e=w_up_e, w_down_e=w_down_e)
    depth = w_ada.shape[0]
    Bp, Sp, D = x_prompt.shape
    Bs, Ss, _ = x_sample.shape
    assert Sp % GM_CHUNK == 0 and Ss <= GM_CHUNK and Ss % CHUNK == 0 and Bp <= MOD_BATCH_PAD
    c_all = jnp.concatenate([c_prompt, c_sample], axis=0)
    y_p, y_s = x_prompt, x_sample
    outs = [[] for _ in range(5)]
    for l in range(depth):
        lw = _prep_layer(w, l, (GM_CHUNK, Ss))
        mod = _ada_mod(c_all, w_ada[l], b_ada[l]).reshape(Bp + Bs, 6, D)
        y_p, ckv_p, kr_p = _layer_prompt(y_p, mod[:Bp], lw)
        y_s, ckv_s, kr_s, v_s = _layer_sample(y_s, mod[Bp:], cache_ckv[l], cache_krope[l], lw)
        for lst, val in zip(outs, (ckv_p, kr_p, ckv_s, kr_s, v_s)):
            lst.append(val)
    return (y_p, y_s) + tuple(jnp.stack(lst) for lst in outs)
```

```python
import functools

import numpy as np
import jax
import jax.numpy as jnp
from jax import lax
from jax.experimental import pallas as pl
from jax.experimental.pallas import tpu as pltpu

F32 = jnp.float32
BF16 = jnp.bfloat16
I32 = jnp.int32

CHUNK = 64
CHUNK_SHIFT = 6
EPS = 1e-6
MLA_HEADS = 8
Q_LORA = 256
KV_LORA = 128
NOPE_DIM = 64
ROPE_DIM = 32
ROPE_HALF = ROPE_DIM // 2
V_DIM = 64
QK_DIM = NOPE_DIM + ROPE_DIM
ATT_WIDTH = MLA_HEADS * V_DIM
ROPE_BASE = 10000.0
GM_HEADS = 8
GM_HEAD_DIM = 64
GM_WIDTH = GM_HEADS * GM_HEAD_DIM
GM_CHUNK = 128
N_GROUPS = 4
EXPERTS_PER_GROUP = 8
N_EXPERTS = N_GROUPS * EXPERTS_PER_GROUP
D_EXPERT = 256
PAIRS_PER_GROUP = EXPERTS_PER_GROUP * (EXPERTS_PER_GROUP - 1) // 2
N_CLASSES = N_GROUPS * PAIRS_PER_GROUP

LANES = 128
SUBLANES = 8
HEAD_PAD = LANES
PROJ_COLS = 1536
ROUTER_COLS = LANES
ROUTER_EXPERT_LANE0 = SUBLANES
CLASS_ROWS = LANES
MOE_ROWS = 128
MOD_BATCH_PAD = 16
DIAG_ROWS = 256
COPY_BURST_GROUPS = (0, 3, 6, 9, 12, 14, 16)
VMEM_LIMIT = 48 * 1024 * 1024
NEG_BIG = -1e30
LOG2E = 1.4426950408889634
BF16_SLACK = 1.02
SCORE_BOUND = 90.0

assert CHUNK == 1 << CHUNK_SHIFT


def _params(n_axes, vmem=VMEM_LIMIT):
    return pltpu.CompilerParams(dimension_semantics=("arbitrary",) * n_axes, vmem_limit_bytes=vmem)


def _full(shape):
    nd = len(shape)
    return pl.BlockSpec(shape, lambda *_: (0,) * nd)


def _split_bf16(x):
    hi = x.astype(BF16)
    lo = (x - hi.astype(F32)).astype(BF16)
    return hi, lo


def _dot(a, b):
    return jnp.dot(a, b, preferred_element_type=F32)


def _ada_kernel(c_ref, w_ref, b_ref, o_ref):
    a_hi, a_lo = _split_bf16(jax.nn.silu(c_ref[...]))
    w_hi, w_lo = _split_bf16(w_ref[...])
    o_ref[...] = _dot(a_hi, w_hi) + _dot(a_lo, w_hi) + _dot(a_hi, w_lo) + b_ref[...]


def _ada_mod(c, w_ada, b_ada):
    n, d = c.shape
    cols = w_ada.shape[1]
    tn = 1536
    return pl.pallas_call(
        _ada_kernel,
        grid=(cols // tn,),
        in_specs=[_full((n, d)), pl.BlockSpec((d, tn), lambda j: (0, j)), pl.BlockSpec((1, tn), lambda j: (0, j))],
        out_specs=pl.BlockSpec((n, tn), lambda j: (0, j)),
        out_shape=jax.ShapeDtypeStruct((n, cols), F32),
        compiler_params=_params(1),
    )(c, w_ada, b_ada.reshape(1, cols))


def _rms(x, g):
    return x * lax.rsqrt(jnp.mean(x * x, axis=-1, keepdims=True) + EPS) * g


def _head_norm_rope(xh, xh_swapped, tab_c, tab_s):
    ms = jnp.sum(xh * xh, axis=-1, keepdims=True) * (1.0 / QK_DIM)
    return (xh * tab_c + xh_swapped * tab_s) * lax.rsqrt(ms + EPS)


def _keys_values(ckv, kr_slot, gkv, wk, wv, tab_c, tab_s, k_ref, v_ref):
    cb = _rms(ckv, gkv).astype(BF16)
    kall = _dot(cb, wk)
    vall = _dot(cb, wv)
    lane = lax.broadcasted_iota(I32, (1, LANES), 1)
    ones_hi = jnp.where(lane >= V_DIM, 1.0, 0.0)
    ones_lo = 1.0 - ones_hi
    kr_swapped = jnp.where(lane < NOPE_DIM + ROPE_HALF, pltpu.roll(kr_slot, HEAD_PAD - ROPE_HALF, 1),
                           pltpu.roll(kr_slot, ROPE_HALF, 1))
    kr_swapped = jnp.where((lane >= NOPE_DIM) & (lane < QK_DIM), kr_swapped, 0.0)
    for h in range(MLA_HEADS):
        kh = kall[:, h * HEAD_PAD:(h + 1) * HEAD_PAD] + kr_slot
        k_ref[0, h] = _head_norm_rope(kh, kr_swapped, tab_c, tab_s).astype(BF16)
        ones = ones_hi if h % 2 == 0 else ones_lo
        v_ref[0, h] = (vall[:, h * HEAD_PAD:(h + 1) * HEAD_PAD] + ones).astype(BF16)


def _mix_in_kernel(x_ref, mod_ref, gmix_ref, win_ref, gql_ref, wuq_ref, gkv_ref, wk_ref, wv_ref,
                   cq_ref, sq_ref, ck_ref, sk_ref, glnv_ref, blnv_ref, wsp_ref, bsp_ref,
                   ggm_ref, *rest, chunk_len, emit_kv, emit_vrows):
    outs = list(rest[:-1])
    mixed_scr = rest[-1]
    ckv_ref, kr_ref, q_ref, gm_ref = outs[:4]
    outs = outs[4:]
    if emit_kv:
        k_ref, v_ref = outs[:2]
        outs = outs[2:]
    if emit_vrows:
        vrows_ref = outs[0]

    x = x_ref[0]
    tm = x.shape[0]
    shift, scale = mod_ref[0, 0:1, :], mod_ref[0, 1:2, :]
    h = _rms(x, gmix_ref[...]) * (1.0 + scale) + shift
    proj = _dot(h.astype(BF16), win_ref[...])

    q_lat = proj[:, 0:Q_LORA]
    ckv = proj[:, Q_LORA:Q_LORA + KV_LORA]
    kr_blk = proj[:, Q_LORA + KV_LORA:Q_LORA + KV_LORA + LANES]
    ckv_ref[0] = ckv
    kr_ref[0] = kr_blk[:, 0:ROPE_DIM]

    q = _dot(_rms(q_lat, gql_ref[...]).astype(BF16), wuq_ref[...])
    cq, sq = cq_ref[...], sq_ref[...]
    n_q = MLA_HEADS * HEAD_PAD
    for hd in range(MLA_HEADS):
        qh = q[:, hd * HEAD_PAD:(hd + 1) * HEAD_PAD]
        qh_swapped = q[:, n_q + hd * HEAD_PAD:n_q + (hd + 1) * HEAD_PAD]
        q_ref[0, hd] = _head_norm_rope(qh, qh_swapped, cq, sq).astype(BF16)

    if emit_kv:
        lane = lax.broadcasted_iota(I32, (1, LANES), 1)
        kr_slot = jnp.where(lane >= NOPE_DIM, kr_blk, 0.0)
        _keys_values(ckv, kr_slot, gkv_ref[...], wk_ref[...], wv_ref[...], ck_ref[...], sk_ref[...], k_ref, v_ref)

    g_u = proj[:, 512:512 + GM_WIDTH]
    g_v = proj[:, 1024:1024 + GM_WIDTH]
    u = jax.nn.gelu(g_u)
    gv = jax.nn.gelu(g_v)
    mu = jnp.mean(gv, axis=-1, keepdims=True)
    xc = gv - mu
    var = jnp.mean(xc * xc, axis=-1, keepdims=True)
    v_rows = xc * lax.rsqrt(var + EPS) * glnv_ref[...] + blnv_ref[...]
    if emit_vrows:
        vrows_ref[0] = v_rows
    vb = v_rows.astype(BF16)

    L = chunk_len
    t = lax.broadcasted_iota(I32, (2 * L, L), 0)
    s = lax.broadcasted_iota(I32, (2 * L, L), 1)
    t = jnp.where(t >= L, t - L, t)
    allowed = (s >> CHUNK_SHIFT) <= (t >> CHUNK_SHIFT)
    lane = lax.broadcasted_iota(I32, (1, LANES), 1)
    first_head = lane < GM_HEAD_DIM
    for p in range(GM_HEADS // 2):
        w_pair = jnp.where(allowed, wsp_ref[p], jnp.zeros((), BF16))
        for c in range(tm // L):
            vp = vb[c * L:(c + 1) * L, p * LANES:(p + 1) * LANES]
            r = _dot(w_pair, vp)
            mixed = jnp.where(first_head, r[:L], r[L:])
            mixed_scr[c * L:(c + 1) * L, p * LANES:(p + 1) * LANES] = mixed + bsp_ref[:, p * LANES:(p + 1) * LANES]
    gm = u * mixed_scr[...]
    gm_ref[0] = _rms(gm, ggm_ref[...]).astype(BF16)


def _mix_in(x, mod, lw, q_tabs, k_tabs, *, tm, chunk_len, emit_kv, emit_vrows):
    B, S, D = x.shape
    nt = S // tm
    H = MLA_HEADS
    row = lambda b, i: (b, i, 0)
    tab = pl.BlockSpec((tm, LANES), lambda b, i: (i, 0))
    in_specs = [
        pl.BlockSpec((1, tm, D), row),
        pl.BlockSpec((1, 6, D), lambda b, i: (b, 0, 0)),
        _full((1, D)), _full((D, PROJ_COLS)), _full((1, Q_LORA)), _full((Q_LORA, 2 * H * HEAD_PAD)),
        _full((1, KV_LORA)), _full((KV_LORA, H * HEAD_PAD)), _full((KV_LORA, H * HEAD_PAD)),
        tab, tab, tab, tab,
        _full((1, GM_WIDTH)), _full((1, GM_WIDTH)),
        _full((GM_HEADS // 2, 2 * chunk_len, chunk_len)), _full((chunk_len, GM_WIDTH)), _full((1, GM_WIDTH)),
    ]
    out_shape = [
        jax.ShapeDtypeStruct((B, S, KV_LORA), F32),
        jax.ShapeDtypeStruct((B, S, ROPE_DIM), F32),
        jax.ShapeDtypeStruct((B, H, S, HEAD_PAD), BF16),
        jax.ShapeDtypeStruct((B, S, GM_WIDTH), BF16),
    ]
    head_blk = pl.BlockSpec((1, H, tm, HEAD_PAD), lambda b, i: (b, 0, i, 0))
    out_specs = [
        pl.BlockSpec((1, tm, KV_LORA), row),
        pl.BlockSpec((1, tm, ROPE_DIM), row),
        head_blk,
        pl.BlockSpec((1, tm, GM_WIDTH), row),
    ]
    if emit_kv:
        out_shape += [jax.ShapeDtypeStruct((B, H, S, HEAD_PAD), BF16), jax.ShapeDtypeStruct((B, H, S, HEAD_PAD), BF16)]
        out_specs += [head_blk, head_blk]
    if emit_vrows:
        out_shape += [jax.ShapeDtypeStruct((B, S, GM_WIDTH), F32)]
        out_specs += [pl.BlockSpec((1, tm, GM_WIDTH), row)]
    kern = functools.partial(_mix_in_kernel, chunk_len=chunk_len, emit_kv=emit_kv, emit_vrows=emit_vrows)
    return pl.pallas_call(
        kern,
        grid=(B, nt),
        in_specs=in_specs,
        out_specs=out_specs,
        out_shape=out_shape,
        scratch_shapes=[pltpu.VMEM((tm, GM_WIDTH), F32)],
        compiler_params=_params(2),
    )(x, mod, lw["g_norm_mix"], lw["w_in_p"], lw["g_q_lat"], lw["w_uq_p"], lw["g_kv_lat"], lw["w_k_p"], lw["w_v_p"],
      *q_tabs, *k_tabs, lw["g_ln_v"], lw["b_ln_v"],
      lw["w_sp_pairs_%d" % chunk_len], lw["b_sp_rows_%d" % chunk_len], lw["g_out_gmlp"])


def _kv_latent_kernel(ckv_ref, kr_ref, gkv_ref, wk_ref, wv_ref, c_ref, s_ref, k_ref, v_ref):
    _keys_values(ckv_ref[0], kr_ref[0], gkv_ref[...], wk_ref[...], wv_ref[...], c_ref[...], s_ref[...], k_ref, v_ref)


def _kv_latent(ckv_all, kr_slot_all, lw, k_tabs, *, tr):
    B, K, _ = ckv_all.shape
    H = MLA_HEADS
    row = lambda b, i: (b, i, 0)
    tab = pl.BlockSpec((tr, LANES), lambda b, i: (i, 0))
    return pl.pallas_call(
        _kv_latent_kernel,
        grid=(B, K // tr),
        in_specs=[pl.BlockSpec((1, tr, KV_LORA), row), pl.BlockSpec((1, tr, LANES), row),
                  _full((1, KV_LORA)), _full((KV_LORA, H * HEAD_PAD)), _full((KV_LORA, H * HEAD_PAD)),
                  tab, tab],
        out_specs=[pl.BlockSpec((1, H, tr, HEAD_PAD), lambda b, i: (b, 0, i, 0))] * 2,
        out_shape=[jax.ShapeDtypeStruct((B, H, K, HEAD_PAD), BF16)] * 2,
        compiler_params=_params(2),
    )(ckv_all, kr_slot_all, lw["g_kv_lat"], lw["w_k_p"], lw["w_v_p"], *k_tabs)


def _attn_kernel(bounded_ref, q_ref, k_ref, v_ref, o_ref, acc_ref, *, tq, tk, n_q, q_off, kv_valid, split_diag):
    i = pl.program_id(2) if n_q > 1 else 0
    q_first = q_off + i * tq
    vis_first = jnp.minimum(((q_first >> CHUNK_SHIFT) + 1) << CHUNK_SHIFT, kv_valid)
    vis_last = jnp.minimum((((q_first + tq - 1) >> CHUNK_SHIFT) + 1) << CHUNK_SHIFT, kv_valid)
    n_unmasked = vis_first // tk
    n_total = (vis_last + tk - 1) // tk

    q_pos = q_first + lax.broadcasted_iota(I32, (tq, 1), 0)
    limit = jnp.minimum(((q_pos >> CHUNK_SHIFT) + 1) << CHUNK_SHIFT, kv_valid)
    lane = lax.broadcasted_iota(I32, (1, LANES), 1)

    def scores(j, start, masked, r0=0, nr=tq, nk=tk):
        s = lax.dot_general(q_ref[0, j, r0:r0 + nr, :], k_ref[0, j, pl.ds(start, nk), :], (((1,), (1,)), ((), ())),
                            preferred_element_type=F32)
        if masked:
            k_pos = start + lax.broadcasted_iota(I32, (1, nk), 1)
            s = jnp.where(k_pos < limit[r0:r0 + nr], s, NEG_BIG)
        return s

    def plain(start, masked, r0=0, nr=tq, nk=tk, assign=False):
        for j in range(2):
            p = jnp.exp2(scores(j, start, masked, r0, nr, nk)).astype(BF16)
            pv = _dot(p, v_ref[0, j, pl.ds(start, nk), :])
            if assign:
                acc_ref[j, r0:r0 + nr, :] = pv
            else:
                acc_ref[j, r0:r0 + nr, :] += pv

    def online_block(kb, carry, masked):
        start = pl.multiple_of(kb * tk, tk)
        new = []
        for j in range(2):
            m, acc = carry[2 * j:2 * j + 2]
            s = scores(j, start, masked)
            m_new = jnp.maximum(m, jnp.max(s, axis=-1, keepdims=True))
            p = jnp.exp2(s - m_new).astype(BF16)
            acc = jnp.exp2(m - m_new) * acc + _dot(p, v_ref[0, j, pl.ds(start, tk), :])
            new += [m_new, acc]
        return tuple(new)

    def finish(acc0, acc1):
        first = lane < V_DIM
        num = jnp.where(first, acc0, acc1)
        den = pltpu.roll(jnp.where(first, acc1, acc0), V_DIM, 1)
        o_ref[0] = (num / den).astype(BF16)

    @pl.when(bounded_ref[0] == 1)
    def _():
        at = lambda kb: pl.multiple_of(kb * tk, tk)

        def pair(p, c):
            plain(at(2 * p), False)
            plain(at(2 * p + 1), False)
            return c

        def single(kb, c, masked):
            plain(at(kb), masked)
            return c

        if split_diag:
            own = 0 if n_q == 1 else pl.multiple_of(i * tq, tq)
            for r in range(tq // DIAG_ROWS):
                plain(own, True, r * DIAG_ROWS, DIAG_ROWS, (r + 1) * DIAG_ROWS, assign=True)
        else:
            acc_ref[...] = jnp.zeros_like(acc_ref)
        n_pairs = n_unmasked // 2
        lax.fori_loop(0, n_pairs, pair, 0)
        lax.fori_loop(2 * n_pairs, n_unmasked, lambda kb, c: single(kb, c, False), 0)
        if not split_diag:
            lax.fori_loop(n_unmasked, n_total, lambda kb, c: single(kb, c, True), 0)
        finish(acc_ref[0], acc_ref[1])

    @pl.when(bounded_ref[0] != 1)
    def _():
        zeros = jnp.zeros((tq, LANES), F32)
        m0 = jnp.full((tq, 1), NEG_BIG, F32)
        c = lax.fori_loop(0, n_unmasked, lambda kb, c: online_block(kb, c, False), (m0, zeros, m0, zeros))
        c = lax.fori_loop(n_unmasked, n_total, lambda kb, c: online_block(kb, c, True), c)
        finish(c[1], c[3])


def _attention(bounded, q, k, v, *, tq, tk, q_off, kv_valid):
    B, H, Sq, _ = q.shape
    Sk = k.shape[2]
    split_diag = q_off == 0 and kv_valid >= Sq and tq % DIAG_ROWS == 0 and tq % tk == 0
    kern = functools.partial(_attn_kernel, tq=tq, tk=tk, n_q=Sq // tq, q_off=q_off, kv_valid=kv_valid,
                             split_diag=split_diag)
    grid_spec = pltpu.PrefetchScalarGridSpec(
        num_scalar_prefetch=1,
        grid=(B, H // 2, Sq // tq),
        in_specs=[pl.BlockSpec((1, 2, tq, HEAD_PAD), lambda b, hp, i, f: (b, hp, i, 0)),
                  pl.BlockSpec((1, 2, Sk, HEAD_PAD), lambda b, hp, i, f: (b, hp, 0, 0)),
                  pl.BlockSpec((1, 2, Sk, HEAD_PAD), lambda b, hp, i, f: (b, hp, 0, 0))],
        out_specs=pl.BlockSpec((1, tq, LANES), lambda b, hp, i, f: (b, i, hp)),
        scratch_shapes=[pltpu.VMEM((2, tq, LANES), F32)],
    )
    return pl.pallas_call(
        kern,
        grid_spec=grid_spec,
        out_shape=jax.ShapeDtypeStruct((B, Sq, ATT_WIDTH), BF16),
        compiler_params=_params(3),
    )(bounded, q, k, v)


def _scores_bounded(w, l):
    gq = jnp.max(jnp.abs(w["g_qnorm"][l]))
    gk = jnp.max(jnp.abs(w["g_knorm"][l]))
    bound = (QK_DIM ** 0.5) * LOG2E * BF16_SLACK * gq * gk
    return (bound <= SCORE_BOUND).astype(I32).reshape(1)


def _pack_bf16_pairs(h):
    n = h.shape[1] // 2
    hi = pltpu.bitcast(h[:, :n].astype(BF16).astype(F32), jnp.uint32)
    lo = pltpu.bitcast(h[:, n:].astype(BF16).astype(F32), jnp.uint32)
    return pltpu.bitcast(hi | (lo >> 16), F32)


def _unpack_bf16_pairs(words):
    w = pltpu.bitcast(words, jnp.uint32)
    hi = pltpu.bitcast(w & jnp.uint32(0xFFFF0000), F32)
    lo = pltpu.bitcast(w << 16, F32)
    return jnp.concatenate([hi, lo], axis=1).astype(BF16)


def _out_route_kernel(x_ref, att_ref, gm_ref, mod_ref, goa_ref, woa_ref, wog_ref, gffn_ref, wra_ref, wrb_ref, br_ref,
                      *rest, pack_rows):
    carry_scr = rest[-1]
    if pack_rows:
        rec_ref, ri_ref, cnt_ref = rest[:3]
    else:
        x1_ref, h2_ref, ri_ref, rf_ref, cnt_ref = rest[:5]
    first_step = (pl.program_id(0) == 0) & (pl.program_id(1) == 0)

    @pl.when(first_step)
    def _():
        carry_scr[...] = jnp.zeros_like(carry_scr)

    x = x_ref[0]
    tm, D = x.shape
    gate_a = mod_ref[0, 2:3, :]
    shift_m, scale_m = mod_ref[0, 3:4, :], mod_ref[0, 4:5, :]
    att_n = _rms(att_ref[0].astype(F32), goa_ref[...]).astype(BF16)
    mix = _dot(att_n, woa_ref[...]) + _dot(gm_ref[0], wog_ref[...])
    x1 = x + gate_a * mix
    h2 = _rms(x1, gffn_ref[...]) * (1.0 + scale_m) + shift_m
    if pack_rows:
        rec_ref[0, :, 0:D] = x1
        rec_ref[0, :, D:D + D // 2] = _pack_bf16_pairs(h2)
    else:
        x1_ref[0] = x1
        h2_ref[0] = h2.astype(BF16)

    h_hi, h_lo = _split_bf16(h2)
    la = _dot(h_hi, wra_ref[...])
    logits = la[:, :ROUTER_COLS] + la[:, ROUTER_COLS:] + _dot(h_lo, wrb_ref[...]) + br_ref[...]
    lt = logits.T

    g = [lt[r:r + 1] for r in range(N_GROUPS)]
    gmax = jnp.maximum(jnp.maximum(g[0], g[1]), jnp.maximum(g[2], g[3]))
    gsum = sum(jnp.exp(gr - gmax) for gr in g)
    g_prob = 1.0 / gsum
    g_idx = jnp.where(g[0] == gmax, 0.0, jnp.where(g[1] == gmax, 1.0, jnp.where(g[2] == gmax, 2.0, 3.0)))

    e0 = ROUTER_EXPERT_LANE0
    grp = [lt[e0 + EXPERTS_PER_GROUP * r:e0 + EXPERTS_PER_GROUP * (r + 1)] for r in range(N_GROUPS)]
    sel = jnp.where(g_idx == 0.0, grp[0], jnp.where(g_idx == 1.0, grp[1], jnp.where(g_idx == 2.0, grp[2], grp[3])))
    sub = lax.broadcasted_iota(I32, (EXPERTS_PER_GROUP, tm), 0).astype(F32)
    m1 = jnp.max(sel, axis=0, keepdims=True)
    i1 = jnp.min(jnp.where(sel == m1, sub, float(EXPERTS_PER_GROUP)), axis=0, keepdims=True)
    sel2 = jnp.where(sub == i1, -jnp.inf, sel)
    m2 = jnp.max(sel2, axis=0, keepdims=True)
    i2 = jnp.min(jnp.where(sel2 == m2, sub, float(EXPERTS_PER_GROUP)), axis=0, keepdims=True)
    d = jnp.exp(m2 - m1)
    w1 = g_prob / (1.0 + d)
    w2 = g_prob * d / (1.0 + d)
    first_lower = i1 < i2
    lo = jnp.minimum(i1, i2)
    hi = jnp.maximum(i1, i2)
    w_lo = jnp.where(first_lower, w1, w2)
    w_hi = jnp.where(first_lower, w2, w1)
    pair = lo * EXPERTS_PER_GROUP - lo * (lo + 1.0) * 0.5 + hi - lo - 1.0
    cls = g_idx * PAIRS_PER_GROUP + pair

    crow = lax.broadcasted_iota(I32, (CLASS_ROWS, tm), 0).astype(F32)
    onehot = jnp.where(crow == cls, 1.0, 0.0)
    ta = lax.broadcasted_iota(I32, (tm, tm), 0)
    tb = lax.broadcasted_iota(I32, (tm, tm), 1)
    earlier = jnp.where(ta < tb, 1.0, 0.0).astype(BF16)
    before = _dot(onehot.astype(BF16), earlier)
    carry = carry_scr[...]
    rank = jnp.sum(onehot * (before + carry[:, 0:1]), axis=0, keepdims=True)
    carry = carry + jnp.sum(onehot, axis=1, keepdims=True)
    carry_scr[...] = carry
    cnt_ref[...] = carry

    ri_ref[...] = jnp.zeros_like(ri_ref)
    ri_ref[0, 0:1, :] = cls.astype(I32)
    ri_ref[0, 1:2, :] = rank.astype(I32)
    if pack_rows:
        mrow = lax.broadcasted_iota(I32, (LANES, tm), 0)
        batch = pl.program_id(0).astype(F32)
        meta_t = jnp.where(mrow == 0, w_lo, jnp.where(mrow == 1, w_hi, jnp.where(mrow == 2, batch, 0.0)))
        rec_ref[0, :, D + D // 2:] = meta_t.T
    else:
        rf_ref[...] = jnp.zeros_like(rf_ref)
        rf_ref[0, 0:1, :] = w_lo
        rf_ref[0, 1:2, :] = w_hi


def _out_route(x, att, gm, mod, lw, *, tm, pack_rows):
    B, S, D = x.shape
    nt = S // tm
    row = lambda b, i: (b, i, 0)
    tile = lambda b, i: (b * nt + i, 0, 0)
    route_i = (jax.ShapeDtypeStruct((B * nt, SUBLANES, tm), I32), pl.BlockSpec((1, SUBLANES, tm), tile))
    route_f = (jax.ShapeDtypeStruct((B * nt, SUBLANES, tm), F32), pl.BlockSpec((1, SUBLANES, tm), tile))
    counts = (jax.ShapeDtypeStruct((CLASS_ROWS, LANES), F32), _full((CLASS_ROWS, LANES)))
    if pack_rows:
        rec_w = D + D // 2 + LANES
        outs = [(jax.ShapeDtypeStruct((B, S, rec_w), F32), pl.BlockSpec((1, tm, rec_w), row)), route_i, counts]
    else:
        outs = [(jax.ShapeDtypeStruct((B, S, D), F32), pl.BlockSpec((1, tm, D), row)),
                (jax.ShapeDtypeStruct((B, S, D), BF16), pl.BlockSpec((1, tm, D), row)), route_i, route_f, counts]
    out_shape = [o[0] for o in outs]
    out_specs = [o[1] for o in outs]
    return pl.pallas_call(
        functools.partial(_out_route_kernel, pack_rows=pack_rows),
        grid=(B, nt),
        in_specs=[pl.BlockSpec((1, tm, D), row), pl.BlockSpec((1, tm, ATT_WIDTH), row), pl.BlockSpec((1, tm, GM_WIDTH), row),
                  pl.BlockSpec((1, 6, D), lambda b, i: (b, 0, 0)),
                  _full((1, ATT_WIDTH)), _full((ATT_WIDTH, D)), _full((GM_WIDTH, D)), _full((1, D)),
                  _full((D, 2 * ROUTER_COLS)), _full((D, ROUTER_COLS)), _full((1, ROUTER_COLS))],
        out_specs=out_specs,
        out_shape=out_shape,
        scratch_shapes=[pltpu.VMEM((CLASS_ROWS, LANES), F32)],
        compiler_params=_params(2),
    )(x, att, gm, mod, lw["g_out_attn"], lw["w_out_a"], lw["w_out_g"], lw["g_norm_ffn"],
      lw["w_r_a"], lw["w_r_b"], lw["b_r"])


def _swiglu(hb, wgu, wd):
    hid = _dot(hb, wgu)
    act = jax.nn.silu(hid[:, :D_EXPERT]) * hid[:, D_EXPERT:]
    return _dot(act.astype(BF16), wd)


def _moe_pairs_kernel(elo_ref, ehi_ref, nv_ref, dest_ref, rec_hbm, zeros_hbm, gate_ref,
                      wgu_lo_ref, wd_lo_ref, wgu_hi_ref, wd_hi_ref, y_hbm,
                      xbuf0, xbuf1, xbuf2, ybuf0, ybuf1, ybuf2, tok_ref, gsem, ssem, zsem):
    i = pl.program_id(0)
    xbuf, ybuf = (xbuf0, xbuf1, xbuf2), (ybuf0, ybuf1, ybuf2)
    D = ybuf0.shape[-1]
    n_groups_all = MOE_ROWS // SUBLANES

    def copy_priority(u):
        return u % 2 if isinstance(u, int) else 0

    def gather_start(t, s, g, u):
        copy = pltpu.make_async_copy(rec_hbm.at[pl.ds(t, 1)], xbuf[s].at[g, pl.ds(u, 1)], gsem.at[s])
        copy.start(priority=copy_priority(u))

    def scatter_start(t, s, g, u):
        copy = pltpu.make_async_copy(ybuf[s].at[g, pl.ds(u, 1)], y_hbm.at[pl.ds(t, 1)], ssem.at[s])
        copy.start(priority=copy_priority(u))

    def some_rows(blk, groups, fn):
        base = blk * MOE_ROWS
        for g in groups:
            for u in range(SUBLANES):
                fn(tok_ref[base + g * SUBLANES + u], g, u)

    def all_rows(blk, fn):
        some_rows(blk, range(n_groups_all), fn)

    def valid_rows(blk, fn):
        nv = nv_ref[blk]
        base = blk * MOE_ROWS
        n_groups = nv // SUBLANES

        def group(g, c):
            for u in range(SUBLANES):
                fn(tok_ref[base + g * SUBLANES + u], g, u)
            return c

        def single(r, c):
            fn(tok_ref[base + r], n_groups, r - n_groups * SUBLANES)
            return c

        lax.fori_loop(0, n_groups, group, 0)
        lax.fori_loop(n_groups * SUBLANES, nv, single, 0)

    def gather_wait(s):
        pltpu.make_async_copy(xbuf[s], xbuf[s], gsem.at[s]).wait()

    def scatter_wait(blk, s):
        nv = nv_ref[blk]
        n_groups = nv // SUBLANES
        buf = ybuf[s]

        @pl.when(n_groups > 0)
        def _():
            rows = buf.at[pl.ds(0, n_groups)]
            pltpu.make_async_copy(rows, rows, ssem.at[s]).wait()

        def single(r, c):
            row = buf.at[0, pl.ds(0, 1)]
            pltpu.make_async_copy(row, row, ssem.at[s]).wait()
            return c

        lax.fori_loop(n_groups * SUBLANES, nv, single, 0)

    def experts(s, start_copies):
        start_copies(0)
        hb = _unpack_bf16_pairs(xbuf[s][:, :, D:D + D // 2].reshape(MOE_ROWS, D // 2))
        meta = xbuf[s][:, :, D + D // 2:].reshape(MOE_ROWS, LANES)
        w_lo, w_hi, bidx = meta[:, 0:1], meta[:, 1:2], meta[:, 2:3]
        blane = lax.broadcasted_iota(I32, (1, 2 * MOD_BATCH_PAD), 1)
        blane = jnp.where(blane >= MOD_BATCH_PAD, blane - MOD_BATCH_PAD, blane).astype(F32)
        onehot = jnp.where(bidx == blane, 1.0, 0.0).astype(BF16)
        gate_m = _dot(onehot, gate_ref[...])
        start_copies(1)
        hid = _dot(hb, wgu_lo_ref[0])
        act_lo = (jax.nn.silu(hid[:, :D_EXPERT]) * hid[:, D_EXPERT:]).astype(BF16)
        start_copies(2)
        moe = w_lo * _dot(act_lo, wd_lo_ref[0])
        start_copies(3)
        hid = _dot(hb, wgu_hi_ref[0])
        act_hi = (jax.nn.silu(hid[:, :D_EXPERT]) * hid[:, D_EXPERT:]).astype(BF16)
        start_copies(4)
        moe = moe + w_hi * _dot(act_hi, wd_hi_ref[0])
        start_copies(5)
        x = xbuf[s][:, :, 0:D].reshape(MOE_ROWS, D)
        ybuf[s][...] = (x + gate_m * moe).reshape(n_groups_all, SUBLANES, D)

    @pl.when(i == 0)
    def _():
        zero = pltpu.make_async_copy(zeros_hbm, tok_ref, zsem)
        zero.start()
        zero.wait()

        def invert(g, c):
            for u in range(SUBLANES):
                t = g * SUBLANES + u
                tok_ref[dest_ref[t]] = t
            return c
        lax.fori_loop(0, dest_ref.shape[0] // SUBLANES, invert, 0)
        all_rows(0, lambda t, g, u: gather_start(t, 0, g, u))
        all_rows(1, lambda t, g, u: gather_start(t, 1, g, u))

    prev = jnp.maximum(i - 1, 0)
    has_rows = nv_ref[i] > 0
    prev_full = (i >= 1) & (nv_ref[prev] == MOE_ROWS)
    prev_partial = (i >= 1) & (nv_ref[prev] > 0) & (nv_ref[prev] < MOE_ROWS)

    def step(slot):
        before, after = (slot + 2) % 3, (slot + 1) % 3

        @pl.when(has_rows)
        def _():
            gather_wait(slot)

            @pl.when(prev_partial)
            def _():
                valid_rows(i - 1, lambda t, g, u: scatter_start(t, before, g, u))

            @pl.when(i >= 3)
            def _():
                scatter_wait(i - 3, slot)

            def start_copies(k):
                groups = range(COPY_BURST_GROUPS[k], COPY_BURST_GROUPS[k + 1])

                @pl.when(prev_full)
                def _():
                    some_rows(i - 1, groups, lambda t, g, u: scatter_start(t, before, g, u))

                @pl.when(has_rows)
                def _():
                    some_rows(i + 2, groups, lambda t, g, u: gather_start(t, before, g, u))

            experts(slot, start_copies)

        @pl.when(jnp.logical_not(has_rows) & (i >= 1) & (nv_ref[prev] > 0))
        def _():
            gather_wait(slot)
            gather_wait(after)

            @pl.when(i >= 3)
            def _():
                scatter_wait(i - 3, slot)
            valid_rows(i - 1, lambda t, g, u: scatter_start(t, before, g, u))

            @pl.when(i >= 2)
            def _():
                scatter_wait(i - 2, after)
            scatter_wait(i - 1, before)

    for parity in range(3):
        pl.when(i % 3 == parity)(functools.partial(step, parity))


def _moe_pairs(rec, blk_elo, blk_ehi, blk_nv, dest, gate_tab, lw):
    T, rec_w = rec.shape
    assert T % SUBLANES == 0
    D = gate_tab.shape[1]
    nb = blk_nv.shape[0]
    wgu, wd = lw["w_gu_e"], lw["w_d_e"]
    grid_spec = pltpu.PrefetchScalarGridSpec(
        num_scalar_prefetch=4,
        grid=(nb,),
        in_specs=[
            pl.BlockSpec(memory_space=pl.ANY),
            pl.BlockSpec(memory_space=pl.ANY),
            pl.BlockSpec(gate_tab.shape, lambda i, *_: (0, 0)),
            pl.BlockSpec((1, D, 2 * D_EXPERT), lambda i, elo, ehi, nv, tok: (elo[i], 0, 0)),
            pl.BlockSpec((1, D_EXPERT, D), lambda i, elo, ehi, nv, tok: (elo[i], 0, 0)),
            pl.BlockSpec((1, D, 2 * D_EXPERT), lambda i, elo, ehi, nv, tok: (ehi[i], 0, 0)),
            pl.BlockSpec((1, D_EXPERT, D), lambda i, elo, ehi, nv, tok: (ehi[i], 0, 0)),
        ],
        out_specs=pl.BlockSpec(memory_space=pl.ANY),
        scratch_shapes=[pltpu.VMEM((MOE_ROWS // SUBLANES, SUBLANES, rec_w), F32)] * 3
                       + [pltpu.VMEM((MOE_ROWS // SUBLANES, SUBLANES, D), F32)] * 3 + [
                        pltpu.SMEM((nb * MOE_ROWS,), I32),
                        pltpu.SemaphoreType.DMA((3,)), pltpu.SemaphoreType.DMA((3,)), pltpu.SemaphoreType.DMA(())],
    )
    return pl.pallas_call(
        _moe_pairs_kernel,
        grid_spec=grid_spec,
        out_shape=jax.ShapeDtypeStruct((T, D), F32),
        compiler_params=_params(1),
    )(blk_elo, blk_ehi, blk_nv, dest, rec, jnp.zeros((nb * MOE_ROWS,), I32), gate_tab, wgu, wd, wgu, wd)


def _small_lookup(table, idx):
    ids = jnp.arange(table.shape[0], dtype=I32)
    return jnp.sum(jnp.where(idx[:, None] == ids[None, :], table[None, :], 0), axis=1)


def _pair_tables():
    lo, hi = [], []
    for g in range(N_GROUPS):
        for a in range(EXPERTS_PER_GROUP):
            for b in range(a + 1, EXPERTS_PER_GROUP):
                lo.append(g * EXPERTS_PER_GROUP + a)
                hi.append(g * EXPERTS_PER_GROUP + b)
    return np.asarray(lo, np.int32), np.asarray(hi, np.int32)


def _moe_prompt(rec, cls, rank, counts, mod, lw):
    B, S, rec_w = rec.shape
    D = mod.shape[-1]
    T = B * S
    nb = T // MOE_ROWS + N_CLASSES + 2
    nblk = (counts + MOE_ROWS - 1) // MOE_ROWS
    blk_end = jnp.cumsum(nblk)
    blk_start = blk_end - nblk
    dest = _small_lookup(blk_start, cls) * MOE_ROWS + rank
    ids = jnp.arange(nb, dtype=I32)
    used = blk_end[-1]
    class_of = lambda blk: jnp.sum((blk_end[None, :] <= blk[:, None]).astype(I32), axis=1)
    blk_cls = jnp.where(ids < used, jnp.minimum(class_of(ids), N_CLASSES - 1), class_of(used[None] - 1))
    first_row = (ids - _small_lookup(blk_start, blk_cls)) * MOE_ROWS
    blk_nv = jnp.where(ids < used, jnp.clip(_small_lookup(counts, blk_cls) - first_row, 0, MOE_ROWS), 0)
    lo_tab, hi_tab = _pair_tables()
    blk_elo = _small_lookup(jnp.asarray(lo_tab), blk_cls)
    blk_ehi = _small_lookup(jnp.asarray(hi_tab), blk_cls)
    gate_hi, gate_lo = _split_bf16(jnp.pad(mod[:, 5, :], ((0, MOD_BATCH_PAD - B), (0, 0))))
    gate_tab = jnp.concatenate([gate_hi, gate_lo], axis=0)
    y = _moe_pairs(rec.reshape(T, rec_w), blk_elo, blk_ehi, blk_nv.astype(I32), dest.astype(I32), gate_tab, lw)
    return y.reshape(B, S, D)


def _moe_dense_kernel(h_ref, x1_ref, gate_ref, w_ref, sel_ref, wgu_ref, wd_ref, y_ref):
    e = pl.program_id(0)

    @pl.when(e == 0)
    def _():
        y_ref[...] = jnp.zeros_like(y_ref)

    ye = _swiglu(h_ref[...], wgu_ref[0], wd_ref[0])
    y_ref[...] += jnp.where(sel_ref[0] > 0.5, w_ref[0] * ye, 0.0)

    @pl.when(e == pl.num_programs(0) - 1)
    def _():
        y_ref[...] = x1_ref[...] + gate_ref[...] * y_ref[...]


def _moe_dense(h2, x1, gate_rows, w_sel, sel, lw):
    T, D = x1.shape
    per_e = lambda e: (e, 0, 0)
    return pl.pallas_call(
        _moe_dense_kernel,
        grid=(N_EXPERTS,),
        in_specs=[_full((T, D)), _full((T, D)), _full((T, D)),
                  pl.BlockSpec((1, T, 1), per_e), pl.BlockSpec((1, T, 1), per_e),
                  pl.BlockSpec((1, D, 2 * D_EXPERT), per_e), pl.BlockSpec((1, D_EXPERT, D), per_e)],
        out_specs=_full((T, D)),
        out_shape=jax.ShapeDtypeStruct((T, D), F32),
        compiler_params=_params(1),
    )(h2, x1, gate_rows, w_sel, sel, lw["w_gu_e"], lw["w_d_e"])


def _rope_tables(pos, gain, scale):
    inv = 1.0 / (ROPE_BASE ** (jnp.arange(ROPE_HALF, dtype=F32) / ROPE_HALF))
    ang = pos.astype(F32)[:, None] * inv[None, :]
    cos, sin = jnp.cos(ang) * scale, jnp.sin(ang) * scale
    n = pos.shape[0]
    z = lambda w: jnp.zeros((n, w), F32)
    pad = HEAD_PAD - QK_DIM
    g_nope, g1, g2 = gain[:NOPE_DIM], gain[NOPE_DIM:NOPE_DIM + ROPE_HALF], gain[NOPE_DIM + ROPE_HALF:]
    tab_c = jnp.concatenate([jnp.broadcast_to(g_nope * scale, (n, NOPE_DIM)), cos * g1, cos * g2, z(pad)], axis=1)
    tab_s = jnp.concatenate([z(NOPE_DIM), -sin * g2, sin * g1, z(pad)], axis=1)
    return tab_c, tab_s


def _prep_layer(w, l, chunk_lens):
    D = w["w_in"].shape[1]
    H = MLA_HEADS
    lw = {}
    row = lambda name: w[name][l].reshape(1, -1)
    for name in ("g_norm_mix", "g_q_lat", "g_kv_lat", "g_ln_v", "b_ln_v", "g_out_attn", "g_out_gmlp", "g_norm_ffn"):
        lw[name] = row(name)
    w_in = w["w_in"][l]
    o1, o2, o3 = Q_LORA, Q_LORA + KV_LORA, Q_LORA + KV_LORA + ROPE_DIM
    o4 = o3 + GM_WIDTH
    z32 = jnp.zeros((D, ROPE_DIM), F32)
    wr = w_in[:, o2:o3]
    lw["w_in_p"] = jnp.concatenate([w_in[:, :o2], wr, z32, wr, z32, w_in[:, o3:o4], w_in[:, o4:]], axis=1).astype(BF16)
    assert lw["w_in_p"].shape[1] == PROJ_COLS
    pad = HEAD_PAD - QK_DIM
    w_uq = w["w_uq"][l].reshape(Q_LORA, H, QK_DIM)
    w_uq_swapped = jnp.concatenate([jnp.zeros((Q_LORA, H, NOPE_DIM), F32), w_uq[:, :, NOPE_DIM + ROPE_HALF:],
                                    w_uq[:, :, NOPE_DIM:NOPE_DIM + ROPE_HALF]], axis=2)
    head_pad = lambda a: jnp.pad(a, ((0, 0), (0, 0), (0, pad))).reshape(Q_LORA, H * HEAD_PAD)
    lw["w_uq_p"] = jnp.concatenate([head_pad(w_uq), head_pad(w_uq_swapped)], axis=1).astype(BF16)
    w_ukv = w["w_ukv"][l].reshape(KV_LORA, H, NOPE_DIM + V_DIM)
    lw["w_k_p"] = jnp.pad(w_ukv[:, :, :NOPE_DIM], ((0, 0), (0, 0), (0, HEAD_PAD - NOPE_DIM))).reshape(KV_LORA, H * HEAD_PAD).astype(BF16)
    w_v = w_ukv[:, :, NOPE_DIM:]
    zeros_v = jnp.zeros_like(w_v)
    odd_head = (jnp.arange(H) % 2 == 1)[None, :, None]
    w_v_lo, w_v_hi = jnp.where(odd_head, zeros_v, w_v), jnp.where(odd_head, w_v, zeros_v)
    lw["w_v_p"] = jnp.concatenate([w_v_lo, w_v_hi], axis=2).reshape(KV_LORA, H * HEAD_PAD).astype(BF16)
    lw["scores_bounded"] = _scores_bounded(w, l)
    lw["g_qnorm"] = w["g_qnorm"][l]
    lw["g_knorm"] = w["g_knorm"][l]
    for L in chunk_lens:
        wsp = w["w_spatial"][l][:, :L, :L]
        lw["w_sp_pairs_%d" % L] = wsp.reshape(GM_HEADS // 2, 2 * L, L).astype(BF16)
        lw["b_sp_rows_%d" % L] = jnp.repeat(jnp.transpose(w["b_spatial"][l][:, :L]), GM_HEAD_DIM, axis=1)
    w_out = w["w_out"][l].astype(BF16)
    lw["w_out_a"], lw["w_out_g"] = w_out[:ATT_WIDTH], w_out[ATT_WIDTH:]
    wr_full = jnp.zeros((D, ROUTER_COLS), F32)
    wr_full = wr_full.at[:, :N_GROUPS].set(w["w_router_group"][l])
    wr_full = wr_full.at[:, ROUTER_EXPERT_LANE0:ROUTER_EXPERT_LANE0 + N_EXPERTS].set(w["w_router_expert"][l])
    r_hi, r_lo = _split_bf16(wr_full)
    lw["w_r_a"] = jnp.concatenate([r_hi, r_lo], axis=1)
    lw["w_r_b"] = r_hi
    br = jnp.zeros((1, ROUTER_COLS), F32)
    br = br.at[0, :N_GROUPS].set(w["b_router_group"][l])
    lw["b_r"] = br.at[0, ROUTER_EXPERT_LANE0:ROUTER_EXPERT_LANE0 + N_EXPERTS].set(w["b_router_expert"][l])
    lw["w_gu_e"] = jnp.concatenate([w["w_gate_e"][l], w["w_up_e"][l]], axis=-1).astype(BF16)
    lw["w_d_e"] = w["w_down_e"][l].astype(BF16)
    return lw


def _tiles(seq):
    return min(seq, 512), min(seq, 4096), min(seq, 512)


def _layer_prompt(x, mod, lw):
    B, S, D = x.shape
    tm, tq, tk = _tiles(S)
    pos = jnp.arange(S)
    q_tabs = _rope_tables(pos, lw["g_qnorm"], QK_DIM ** -0.5 * LOG2E)
    k_tabs = _rope_tables(pos, lw["g_knorm"], 1.0)
    ckv, krope, q, gm, k, v = _mix_in(x, mod, lw, q_tabs, k_tabs, tm=tm, chunk_len=GM_CHUNK, emit_kv=True, emit_vrows=False)
    att = _attention(lw["scores_bounded"], q, k, v, tq=tq, tk=tk, q_off=0, kv_valid=S)
    rec, ri, cnt = _out_route(x, att, gm, mod, lw, tm=tm, pack_rows=True)
    cls = ri[:, 0, :].reshape(B * S)
    rank = ri[:, 1, :].reshape(B * S)
    counts = cnt[:N_CLASSES, 0].astype(I32)
    y = _moe_prompt(rec, cls, rank, counts, mod, lw)
    return y, ckv, krope


def _layer_sample(x, mod, past_ckv, past_krope, lw):
    B, S, D = x.shape
    past = past_ckv.shape[1]
    q_tabs = _rope_tables(past + jnp.arange(S), lw["g_qnorm"], QK_DIM ** -0.5 * LOG2E)
    ckv, krope, q, gm, v_rows = _mix_in(x, mod, lw, q_tabs, q_tabs, tm=S, chunk_len=S, emit_kv=False, emit_vrows=True)
    kv_valid = past + S
    kv_pad = -(-kv_valid // LANES) * LANES
    extra = kv_pad - kv_valid
    ckv_all = jnp.concatenate([past_ckv, ckv, jnp.zeros((B, extra, KV_LORA), F32)], axis=1)
    kr_all = jnp.concatenate([past_krope, krope, jnp.zeros((B, extra, ROPE_DIM), F32)], axis=1)
    kr_slot = jnp.pad(kr_all, ((0, 0), (0, 0), (NOPE_DIM, HEAD_PAD - QK_DIM)))
    k_tabs = _rope_tables(jnp.arange(kv_pad), lw["g_knorm"], 1.0)
    k, v = _kv_latent(ckv_all, kr_slot, lw, k_tabs, tr=kv_pad)
    att = _attention(lw["scores_bounded"], q, k, v, tq=S, tk=kv_pad, q_off=past, kv_valid=kv_valid)
    x1, h2, ri, rf, _ = _out_route(x, att, gm, mod, lw, tm=S, pack_rows=False)
    T = B * S
    cls, w_lo, w_hi = ri[:, 0, :].reshape(T), rf[:, 0, :].reshape(T), rf[:, 1, :].reshape(T)
    lo_tab, hi_tab = _pair_tables()
    e_lo, e_hi = jnp.asarray(lo_tab)[cls], jnp.asarray(hi_tab)[cls]
    eids = jnp.arange(N_EXPERTS, dtype=I32)[:, None]
    is_lo, is_hi = eids == e_lo[None, :], eids == e_hi[None, :]
    w_sel = (jnp.where(is_lo, w_lo[None, :], 0.0) + jnp.where(is_hi, w_hi[None, :], 0.0))[:, :, None]
    sel = (is_lo | is_hi).astype(F32)[:, :, None]
    gate_rows = jnp.repeat(mod[:, 5, :], S, axis=0)
    y = _moe_dense(h2.reshape(T, D), x1.reshape(T, D), gate_rows, w_sel, sel, lw)
    return y.reshape(B, S, D), ckv, krope, v_rows


def kernel(x_prompt, x_sample, cache_ckv, cache_krope, c_prompt, c_sample, w_ada, b_ada, g_norm_mix, w_in, g_q_lat, w_uq, g_kv_lat, w_ukv, g_qnorm, g_knorm, g_ln_v, b_ln_v, w_spatial, b_spatial, g_out_attn, g_out_gmlp, w_out, g_norm_ffn, w_router_group, b_router_group, w_router_expert, b_router_expert, w_gate_e, w_up_e, w_down_e):
    w = dict(w_in=w_in, g_norm_mix=g_norm_mix, g_q_lat=g_q_lat, w_uq=w_uq, g_kv_lat=g_kv_lat, w_ukv=w_ukv,
             g_qnorm=g_qnorm, g_knorm=g_knorm, g_ln_v=g_ln_v, b_ln_v=b_ln_v, w_spatial=w_spatial, b_spatial=b_spatial,
             g_out_attn=g_out_attn, g_out_gmlp=g_out_gmlp, w_out=w_out, g_norm_ffn=g_norm_ffn,
             w_router_group=w_router_group, b_router_group=b_router_group, w_router_expert=w_router_expert,
             b_router_expert=b_router_expert, w_gate_e=w_gate_e, w_up_e=w_up_e, w_down_e=w_down_e)
    depth = w_ada.shape[0]
    Bp, Sp, D = x_prompt.shape
    Bs, Ss, _ = x_sample.shape
    assert Sp % GM_CHUNK == 0 and Ss <= GM_CHUNK and Ss % CHUNK == 0 and Bp <= MOD_BATCH_PAD
    c_all = jnp.concatenate([c_prompt, c_sample], axis=0)
    y_p, y_s = x_prompt, x_sample
    outs = [[] for _ in range(5)]
    for l in range(depth):
        lw = _prep_layer(w, l, (GM_CHUNK, Ss))
        mod = _ada_mod(c_all, w_ada[l], b_ada[l]).reshape(Bp + Bs, 6, D)
        y_p, ckv_p, kr_p = _layer_prompt(y_p, mod[:Bp], lw)
        y_s, ckv_s, kr_s, v_s = _layer_sample(y_s, mod[Bp:], cache_ckv[l], cache_krope[l], lw)
        for lst, val in zip(outs, (ckv_p, kr_p, ckv_s, kr_s, v_s)):
            lst.append(val)
    return (y_p, y_s) + tuple(jnp.stack(lst) for lst in outs)
```
